```python
import math
import jax, jax.numpy as jnp
from jax import lax
import numpy as np

D_MODEL = 1024
BATCH = 16
SEQ = 2048
DEPTH = 4

GRID_W = 64
CTX_LEN = 256
Q_BLOCK = 128
ROPE_THETA = 10000.0
EPS = 1e-6

NA_HEADS = 4
NA_DH = 64
NA_KH = 8
NA_KW = 16
HY_W = 256
HY_SHORT = 3
HY_BANDS = 16
HY_EMB = 1 + 2 * HY_BANDS
HY_FFN = 64
HY_FAST_DECAY = 0.3
HY_SLOW_DECAY = 1.5
HY_TARGET = 1e-2
GQA_HEADS = 4
GQA_KV = 2
GQA_DH = 64
DIFF_HEADS = 4
DIFF_DH = 32
DIFF_DV = 64
N_BRANCH = 4
BRANCH_W = 256
PEER_HEADS = 8
PEER_NKEYS = 128
PEER_DK = 128
PEER_TOPK = 16
PEER_EXPERTS = PEER_NKEYS * PEER_NKEYS
PEER_CHUNK = 128

NA_W = NA_HEADS * NA_DH
GQA_QW = GQA_HEADS * GQA_DH
GQA_KVW = GQA_KV * GQA_DH
DIFF_QW = DIFF_HEADS * 2 * DIFF_DH
DIFF_VW = DIFF_HEADS * DIFF_DV
_SPLITS = (NA_W, NA_W, NA_W, 3 * HY_W, GQA_QW, GQA_KVW, GQA_KVW, DIFF_QW, DIFF_QW, DIFF_VW, N_BRANCH * D_MODEL)
SPLIT_IDX = tuple(int(s) for s in np.cumsum(_SPLITS)[:-1])
IN_W = int(sum(_SPLITS))

kernel_name = 'hybrid_diffusion_na_hyena_gqa_diff_peer'


def rmsnorm(x, g):
    x32 = x.astype(jnp.float32)
    y = x32 * lax.rsqrt(jnp.mean(x32 * x32, axis=-1, keepdims=True) + EPS)
    return (y * g.astype(jnp.float32)).astype(x.dtype)


def modulate(x, shift, scale):
    return x * (1 + scale) + shift


def grid_positions(n):
    t = jnp.arange(n)
    return (t // GRID_W).astype(jnp.float32), (t % GRID_W).astype(jnp.float32)


def rope_1d(x, pos):
    half = x.shape[-1] // 2
    freqs = ROPE_THETA ** (-jnp.arange(half, dtype=jnp.float32) / half)
    ang = pos[:, None] * freqs[None, :]
    cos, sin = jnp.cos(ang), jnp.sin(ang)
    x32 = x.astype(jnp.float32)
    x1, x2 = x32[..., :half], x32[..., half:]
    return jnp.concatenate([x1 * cos - x2 * sin, x1 * sin + x2 * cos], axis=-1).astype(x.dtype)


def rope_2d(x, rows, cols):
    h = x.shape[-1] // 2
    return jnp.concatenate([rope_1d(x[..., :h], rows), rope_1d(x[..., h:], cols)], axis=-1)


def to_heads(t, h):
    b, n, _ = t.shape
    return t.reshape(b, n, h, -1).transpose(0, 2, 1, 3)


def from_heads(t):
    b, h, n, d = t.shape
    return t.transpose(0, 2, 1, 3).reshape(b, n, h * d)


def group_q(q):
    b, h, n, d = q.shape
    return q.reshape(b, GQA_KV, h // GQA_KV, n, d)


def diff_heads(p):
    b, n, _ = p.shape
    return p.reshape(b, n, DIFF_HEADS, 2, DIFF_DH).transpose(0, 2, 3, 1, 4)


def attend(q, k, v):
    b, hk, g, n, dh = q.shape
    nb = n // Q_BLOCK
    k32 = k.astype(jnp.float32)
    v32 = v.astype(jnp.float32)
    scale = dh ** -0.5
    qb = jnp.moveaxis(q.reshape(b, hk, g, nb, Q_BLOCK, dh), 3, 0)

    def block(qblk):
        s = jnp.einsum('bhgqd,bhmd->bhgqm', qblk.astype(jnp.float32), k32) * scale
        p = jax.nn.softmax(s, axis=-1)
        return jnp.einsum('bhgqm,bhmd->bhgqd', p, v32)

    o = lax.map(block, qb)
    return jnp.moveaxis(o, 0, 3).reshape(b, hk, g, n, -1).astype(q.dtype)


def neighborhood_attention(q, k, v, kc, vc, rpb):
    b, h, n, dh = q.shape
    nrow = n // GRID_W
    kh = min(NA_KH, nrow)
    scale = dh ** -0.5
    f32 = jnp.float32
    qg = q.astype(f32).reshape(b, h, nrow, GRID_W, dh)
    kg = k.astype(f32).reshape(b, h, nrow, GRID_W, dh)
    vg = v.astype(f32).reshape(b, h, nrow, GRID_W, dh)
    kc32 = kc.astype(f32)
    vc32 = vc.astype(f32)
    rpb32 = rpb.astype(f32)
    nctx = kc.shape[2]
    col = jnp.arange(GRID_W)
    col_idx = jnp.clip(col - NA_KW // 2, 0, GRID_W - NA_KW)[:, None] + jnp.arange(NA_KW)[None, :]
    dc = col_idx - col[:, None] + NA_KW - 1

    def row(r):
        rs = jnp.clip(r - kh // 2, 0, nrow - kh)
        qr = lax.dynamic_index_in_dim(qg, r, axis=2, keepdims=False)
        kb = lax.dynamic_slice_in_dim(kg, rs, kh, axis=2)[:, :, :, col_idx]
        vb = lax.dynamic_slice_in_dim(vg, rs, kh, axis=2)[:, :, :, col_idx]
        dr = rs + jnp.arange(kh) - r + NA_KH - 1
        bias = rpb32[:, dr[:, None, None], dc[None, :, :]]
        s_loc = jnp.einsum('bhwd,bhiwjd->bhwij', qr, kb) * scale + jnp.transpose(bias, (0, 2, 1, 3))[None]
        s_loc = s_loc.reshape(b, h, GRID_W, kh * NA_KW)
        s_ctx = jnp.einsum('bhwd,bhcd->bhwc', qr, kc32) * scale
        p = jax.nn.softmax(jnp.concatenate([s_ctx, s_loc], axis=-1), axis=-1)
        o = jnp.einsum('bhwc,bhcd->bhwd', p[..., :nctx], vc32)
        p_loc = p[..., nctx:].reshape(b, h, GRID_W, kh, NA_KW)
        return o + jnp.einsum('bhwij,bhiwjd->bhwd', p_loc, vb)

    out = lax.map(row, jnp.arange(nrow))
    return jnp.transpose(out, (1, 2, 0, 3, 4)).reshape(b, h, n, dh).astype(q.dtype)


def hyena_filters(n, w1, b1, w2, b2, w3, b3, w4, freq):
    f = lambda a: a.astype(jnp.float32)
    t = jnp.linspace(0.0, 1.0, n, dtype=jnp.float32)[:, None]
    w = (2.0 * math.pi / n) * jnp.arange(n, dtype=jnp.float32)[:, None]
    bands = jnp.linspace(1e-4, HY_BANDS - 1, HY_BANDS, dtype=jnp.float32)[None, :]
    z = jnp.concatenate([t, jnp.cos(w * bands), jnp.sin(w * bands)], axis=-1)
    hh = jnp.sin(f(freq[0]) * (z @ f(w1) + f(b1)))
    hh = jnp.sin(f(freq[1]) * (hh @ f(w2) + f(b2)))
    hh = jnp.sin(f(freq[2]) * (hh @ f(w3) + f(b3)))
    hh = hh @ f(w4)
    deltas = jnp.linspace(math.log(HY_TARGET) / HY_SLOW_DECAY, math.log(HY_TARGET) / HY_FAST_DECAY, HY_W, dtype=jnp.float32)
    deltas = jnp.tile(jnp.abs(deltas), 2)
    hh = hh * jnp.exp(-t * deltas[None, :])
    hh = hh / jnp.sum(jnp.abs(hh), axis=0, keepdims=True)
    return hh[:, :HY_W], hh[:, HY_W:]


def fft_conv_bidir(u, hf, hb):
    n = u.shape[1]
    m = 2 * n
    u32 = u.astype(jnp.float32)
    yf = jnp.fft.irfft(jnp.fft.rfft(u32, n=m, axis=1) * jnp.fft.rfft(hf, n=m, axis=0)[None], n=m, axis=1)[:, :n]
    yb = jnp.fft.irfft(jnp.fft.rfft(u32[:, ::-1], n=m, axis=1) * jnp.fft.rfft(hb, n=m, axis=0)[None], n=m, axis=1)[:, :n][:, ::-1]
    return yf + yb


def short_conv(u, w, b):
    up = jnp.pad(u, ((0, 0), (1, 1), (0, 0)))
    return up[:, :-2] * w[0] + up[:, 1:-1] * w[1] + up[:, 2:] * w[2] + b


def hyena(u, w_short, b_short, filt, bias):
    uc = short_conv(u, w_short, b_short)
    x0, x1, v = jnp.split(uc, 3, axis=-1)
    z = x1 * v
    y = fft_conv_bidir(z, filt[0], filt[1]) + z.astype(jnp.float32) * bias.astype(jnp.float32)
    return (x0.astype(jnp.float32) * y).astype(u.dtype)


def diff_attention(q, k, v, lam, lam_init, g):
    a1 = attend(q[:, :, 0:1], k[:, :, 0], v)[:, :, 0]
    a2 = attend(q[:, :, 1:2], k[:, :, 1], v)[:, :, 0]
    o = a1.astype(jnp.float32) - lam * a2.astype(jnp.float32)
    return (rmsnorm(o, g) * (1.0 - lam_init)).astype(v.dtype)


def merge_branches(ys, gate_logits, wb, wo):
    y = jnp.stack(ys, axis=2)
    proj = jnp.einsum('bnkw,kwd->bnkd', y, wb)
    b, n, _ = gate_logits.shape
    g = jax.nn.sigmoid(gate_logits.reshape(b, n, N_BRANCH, -1))
    return jnp.sum(g * proj, axis=2) @ wo


def peer(x, wq, keys, u_tab, v_tab):
    t, d = x.shape
    f32 = jnp.float32
    q = (x @ wq).astype(f32).reshape(t, PEER_HEADS, 2, PEER_DK // 2)
    s = jnp.einsum('thpd,hpnd->thpn', q, keys.astype(f32))
    sv, si = lax.top_k(s, PEER_TOPK)
    cand = (sv[:, :, 0, :, None] + sv[:, :, 1, None, :]).reshape(t, PEER_HEADS, PEER_TOPK * PEER_TOPK)
    cidx = (si[:, :, 0, :, None] * PEER_NKEYS + si[:, :, 1, None, :]).reshape(t, PEER_HEADS, PEER_TOPK * PEER_TOPK)
    best, pos = lax.top_k(cand, PEER_TOPK)
    eidx = jnp.take_along_axis(cidx, pos, axis=-1)
    g = jax.nn.softmax(best, axis=-1)
    nc = t // PEER_CHUNK
    xs = x.reshape(nc, PEER_CHUNK, d)
    es = eidx.reshape(nc, PEER_CHUNK, PEER_HEADS * PEER_TOPK)
    gs = g.reshape(nc, PEER_CHUNK, PEER_HEADS * PEER_TOPK)

    def chunk(a):
        xc, ec, gc = a
        act = jax.nn.gelu(jnp.einsum('tkd,td->tk', u_tab[ec].astype(f32), xc.astype(f32)), approximate=False)
        return jnp.einsum('tk,tkd->td', gc * act, v_tab[ec].astype(f32))

    out = lax.map(chunk, (xs, es, gs)).reshape(t, d)
    return out.astype(x.dtype)


def setup_inputs(seed: int = 0) -> dict:
    key = jax.random.key(seed)
    ks = iter(jax.random.split(key, 48))
    D = D_MODEL

    def nrm(shape, scale):
        return jax.random.normal(next(ks), shape, jnp.float32) * scale

    return {
        'x': nrm((BATCH, SEQ, D), 1.0),
        'c': nrm((BATCH, D), 1.0),
        'ctx': nrm((BATCH, CTX_LEN, D), 1.0),
        'c_ctx': nrm((D,), 1.0),
        'w_mod': nrm((DEPTH, D, 6 * D), 0.2 * D ** -0.5),
        'b_mod': nrm((DEPTH, 6 * D), 0.02),
        'norm1_g': 1.0 + nrm((DEPTH, D), 0.02),
        'norm2_g': 1.0 + nrm((DEPTH, D), 0.02),
        'w_in': nrm((DEPTH, D, IN_W), D ** -0.5),
        'na_rpb': nrm((DEPTH, NA_HEADS, 2 * NA_KH - 1, 2 * NA_KW - 1), 0.1),
        'hy_short_w': nrm((DEPTH, HY_SHORT, 3 * HY_W), HY_SHORT ** -0.5),
        'hy_short_b': nrm((DEPTH, 3 * HY_W), 0.02),
        'hy_w1': nrm((DEPTH, HY_EMB, HY_FFN), HY_EMB ** -0.5),
        'hy_b1': nrm((DEPTH, HY_FFN), 0.02),
        'hy_w2': nrm((DEPTH, HY_FFN, HY_FFN), HY_FFN ** -0.5),
        'hy_b2': nrm((DEPTH, HY_FFN), 0.02),
        'hy_w3': nrm((DEPTH, HY_FFN, HY_FFN), HY_FFN ** -0.5),
        'hy_b3': nrm((DEPTH, HY_FFN), 0.02),
        'hy_w4': nrm((DEPTH, HY_FFN, 2 * HY_W), HY_FFN ** -0.5),
        'hy_freq': 1.0 + nrm((DEPTH, 3, HY_FFN), 0.01),
        'hy_bias': nrm((DEPTH, HY_W), 1.0),
        'gqa_qn': 1.0 + nrm((DEPTH, GQA_DH), 0.02),
        'gqa_kn': 1.0 + nrm((DEPTH, GQA_DH), 0.02),
        'diff_lq1': nrm((DEPTH, DIFF_DH), 0.1),
        'diff_lk1': nrm((DEPTH, DIFF_DH), 0.1),
        'diff_lq2': nrm((DEPTH, DIFF_DH), 0.1),
        'diff_lk2': nrm((DEPTH, DIFF_DH), 0.1),
        'diff_subln': 1.0 + nrm((DEPTH, DIFF_DV), 0.02),
        'w_branch': nrm((DEPTH, N_BRANCH, BRANCH_W, D), BRANCH_W ** -0.5),
        'w_out': nrm((DEPTH, D, D), D ** -0.5),
        'peer_wq': nrm((DEPTH, D, PEER_HEADS * PEER_DK), D ** -0.5),
        'peer_keys': nrm((DEPTH, PEER_HEADS, 2, PEER_NKEYS, PEER_DK // 2), (PEER_DK // 2) ** -0.5),
        'peer_u': nrm((DEPTH, PEER_EXPERTS, D), D ** -0.5),
        'peer_v': nrm((DEPTH, PEER_EXPERTS, D), 0.5),
        'final_g': 1.0 + nrm((D,), 0.02),
    }


def reference(x, c, ctx, c_ctx, w_mod, b_mod, norm1_g, norm2_g, w_in, na_rpb, hy_short_w, hy_short_b, hy_w1, hy_b1, hy_w2, hy_b2, hy_w3, hy_b3, hy_w4, hy_freq, hy_bias, gqa_qn, gqa_kn, diff_lq1, diff_lk1, diff_lq2, diff_lk2, diff_subln, w_branch, w_out, peer_wq, peer_keys, peer_u, peer_v, final_g):
    f32 = jnp.float32
    B, L, D = x.shape
    C = ctx.shape[1]
    rows, cols = grid_positions(L)
    h_lat, h_ctx = x, ctx
    for l in range(DEPTH):
        need_ctx = l < DEPTH - 1
        lam_init = 0.8 - 0.6 * math.exp(-0.3 * l)
        mod_lat = jnp.split((jax.nn.silu(c) @ w_mod[l] + b_mod[l])[:, None, :], 6, axis=-1)
        mod_ctx = jnp.split((jax.nn.silu(c_ctx) @ w_mod[l] + b_mod[l])[None, None, :], 6, axis=-1)

        n_lat = modulate(rmsnorm(h_lat, norm1_g[l]), mod_lat[0], mod_lat[1])
        n_ctx = modulate(rmsnorm(h_ctx, norm1_g[l]), mod_ctx[0], mod_ctx[1])
        p_lat = jnp.split(n_lat @ w_in[l], SPLIT_IDX, axis=-1)
        p_ctx = jnp.split(n_ctx @ w_in[l], SPLIT_IDX, axis=-1)

        qa, ka, va = [to_heads(t, NA_HEADS) for t in p_lat[0:3]]
        qac, kac, vac = [to_heads(t, NA_HEADS) for t in p_ctx[0:3]]
        ya_lat = from_heads(neighborhood_attention(qa, ka, va, kac, vac, na_rpb[l]))

        filt_w = (hy_w1[l], hy_b1[l], hy_w2[l], hy_b2[l], hy_w3[l], hy_b3[l], hy_w4[l], hy_freq[l])
        yb_lat = hyena(p_lat[3], hy_short_w[l], hy_short_b[l], hyena_filters(L, *filt_w), hy_bias[l])

        qg = rope_2d(rmsnorm(to_heads(p_lat[4], GQA_HEADS), gqa_qn[l]), rows, cols)
        kg = rope_2d(rmsnorm(to_heads(p_lat[5], GQA_KV), gqa_kn[l]), rows, cols)
        vg = to_heads(p_lat[6], GQA_KV)
        qgc = rmsnorm(to_heads(p_ctx[4], GQA_HEADS), gqa_qn[l])
        kgc = rmsnorm(to_heads(p_ctx[5], GQA_KV), gqa_kn[l])
        vgc = to_heads(p_ctx[6], GQA_KV)
        yc_lat = from_heads(attend(group_q(qg), jnp.concatenate([kgc, kg], axis=2), jnp.concatenate([vgc, vg], axis=2)).reshape(B, GQA_HEADS, L, GQA_DH))

        qd = rope_2d(diff_heads(p_lat[7]), rows, cols)
        kd = rope_2d(diff_heads(p_lat[8]), rows, cols)
        vd = to_heads(p_lat[9], DIFF_HEADS)
        qdc = diff_heads(p_ctx[7])
        kdc = diff_heads(p_ctx[8])
        vdc = to_heads(p_ctx[9], DIFF_HEADS)
        lam = (jnp.exp(jnp.sum(diff_lq1[l].astype(f32) * diff_lk1[l].astype(f32)))
               - jnp.exp(jnp.sum(diff_lq2[l].astype(f32) * diff_lk2[l].astype(f32))) + lam_init)
        yd_lat = from_heads(diff_attention(qd, jnp.concatenate([kdc, kd], axis=3), jnp.concatenate([vdc, vd], axis=2), lam, lam_init, diff_subln[l]))

        h_lat = h_lat + mod_lat[2] * merge_branches([ya_lat, yb_lat, yc_lat, yd_lat], p_lat[10], w_branch[l], w_out[l])

        if need_ctx:
            ya_ctx = from_heads(attend(qac[:, :, None], kac, vac)[:, :, 0])
            yb_ctx = hyena(p_ctx[3], hy_short_w[l], hy_short_b[l], hyena_filters(C, *filt_w), hy_bias[l])
            yc_ctx = from_heads(attend(group_q(qgc), kgc, vgc).reshape(B, GQA_HEADS, C, GQA_DH))
            yd_ctx = from_heads(diff_attention(qdc, kdc, vdc, lam, lam_init, diff_subln[l]))
            h_ctx = h_ctx + mod_ctx[2] * merge_branches([ya_ctx, yb_ctx, yc_ctx, yd_ctx], p_ctx[10], w_branch[l], w_out[l])

        m_lat = modulate(rmsnorm(h_lat, norm2_g[l]), mod_lat[3], mod_lat[4]).reshape(B * L, D)
        if need_ctx:
            m_ctx = modulate(rmsnorm(h_ctx, norm2_g[l]), mod_ctx[3], mod_ctx[4]).reshape(B * C, D)
            o = peer(jnp.concatenate([m_ctx, m_lat], axis=0), peer_wq[l], peer_keys[l], peer_u[l], peer_v[l])
            h_ctx = h_ctx + mod_ctx[5] * o[:B * C].reshape(B, C, D)
            o_lat = o[B * C:]
        else:
            o_lat = peer(m_lat, peer_wq[l], peer_keys[l], peer_u[l], peer_v[l])
        h_lat = h_lat + mod_lat[5] * o_lat.reshape(B, L, D)
    return rmsnorm(h_lat, final_g)
```

```python
import functools
import math

import numpy as np
import jax
import jax.numpy as jnp
from jax import lax
from jax.experimental import pallas as pl
from jax.experimental.pallas import tpu as pltpu

F32 = jnp.float32
BF16 = jnp.bfloat16
I32 = jnp.int32
HIGHEST = lax.Precision.HIGHEST

EPS = 1e-6
GRID_W = 64
ROPE_THETA = 10000.0
NA_HEADS, NA_DH, NA_KH, NA_KW = 4, 64, 8, 16
HY_W, HY_BANDS, HY_FFN = 256, 16, 64
HY_FAST_DECAY, HY_SLOW_DECAY, HY_TARGET = 0.3, 1.5, 1e-2
GQA_HEADS, GQA_KV, GQA_DH = 4, 2, 64
DIFF_HEADS, DIFF_DH, DIFF_DV = 4, 32, 64
N_BRANCH, BRANCH_W = 4, 256
PEER_HEADS, PEER_NKEYS, PEER_DK, PEER_TOPK = 8, 128, 128, 16

LANE = 128
ROW_TILE = 256
VMEM_LIMIT = 56 * 1024 * 1024
NEG = -1e30
MIX_W = 3072
GATE_W = N_BRANCH * 1024
COL_TILE = 1024


def _cparams(sem):
    return pltpu.CompilerParams(dimension_semantics=sem, vmem_limit_bytes=VMEM_LIMIT)


def _mod_kernel(c_ref, w_ref, b_ref, o_ref):
    cv = c_ref[...]
    a = cv * jax.nn.sigmoid(cv)
    o_ref[...] = jnp.dot(a, w_ref[...], preferred_element_type=F32, precision=HIGHEST) + b_ref[...]


def _modulation(cvec, w_mod, b_mod):
    depth, d, n = w_mod.shape
    r = cvec.shape[0]
    tn = 1024
    return pl.pallas_call(
        _mod_kernel,
        grid=(depth, n // tn),
        in_specs=[pl.BlockSpec((r, d), lambda l, j: (0, 0)),
                  pl.BlockSpec((None, d, tn), lambda l, j: (l, 0, j)),
                  pl.BlockSpec((None, 1, tn), lambda l, j: (l, 0, j))],
        out_specs=pl.BlockSpec((None, r, tn), lambda l, j: (l, 0, j)),
        out_shape=jax.ShapeDtypeStruct((depth, r, n), F32),
        compiler_params=_cparams(("arbitrary", "arbitrary")),
        name="modulation",
    )(cvec, w_mod, b_mod.reshape(depth, 1, n))


def _norm_modulate(h_ref, mod_ref, g_ref, srow, row0, n_ctx):
    x = h_ref[...]
    y = x * lax.rsqrt(jnp.mean(x * x, axis=-1, keepdims=True) + EPS) * g_ref[...]
    row = row0 + lax.broadcasted_iota(I32, x.shape, 0)
    is_ctx = row < n_ctx
    shift = jnp.where(is_ctx, mod_ref[0, srow:srow + 1, :], mod_ref[1, srow:srow + 1, :])
    scale = jnp.where(is_ctx, mod_ref[0, srow + 1:srow + 2, :], mod_ref[1, srow + 1:srow + 2, :])
    return y * (1.0 + scale) + shift


def _in_proj_kernel(h_ref, mod_ref, g_ref, w_ref, mix_ref, gate_ref, nb_ref, *, tm, n_ctx, n_mix):
    i, j = pl.program_id(1), pl.program_id(2)

    @pl.when(j == 0)
    def _():
        nb_ref[...] = _norm_modulate(h_ref, mod_ref, g_ref, 0, i * tm, n_ctx).astype(BF16)

    res = jnp.dot(nb_ref[...], w_ref[...], preferred_element_type=F32)

    @pl.when(j < n_mix)
    def _():
        mix_ref[...] = res

    @pl.when(j >= n_mix)
    def _():
        gate_ref[...] = res


def _in_proj(h, modl, g, w, n_ctx):
    b, s, d = h.shape
    tm, tn = 768, COL_TILE
    n_mix = MIX_W // tn
    n_tot = (MIX_W + GATE_W) // tn
    return pl.pallas_call(
        functools.partial(_in_proj_kernel, tm=tm, n_ctx=n_ctx, n_mix=n_mix),
        grid=(b, s // tm, n_tot),
        in_specs=[pl.BlockSpec((None, tm, d), lambda bi, i, j: (bi, i, 0)),
                  pl.BlockSpec((None, 2, 8, d), lambda bi, i, j: (bi, 0, 0, 0)),
                  pl.BlockSpec((1, d), lambda bi, i, j: (0, 0)),
                  pl.BlockSpec((d, tn), lambda bi, i, j: (0, j))],
        out_specs=[pl.BlockSpec((None, tm, tn), lambda bi, i, j: (bi, i, jnp.minimum(j, n_mix - 1))),
                   pl.BlockSpec((None, tm, tn), lambda bi, i, j: (bi, i, jnp.maximum(j - n_mix, 0)))],
        out_shape=[jax.ShapeDtypeStruct((b, s, MIX_W), F32), jax.ShapeDtypeStruct((b, s, GATE_W), F32)],
        scratch_shapes=[pltpu.VMEM((tm, d), BF16)],
        compiler_params=_cparams(("arbitrary", "arbitrary", "arbitrary")),
        name="in_proj",
    )(h, modl, g, w)


def _peer_q_kernel(h_ref, mod_ref, g_ref, w_ref, q_ref, nb_ref, *, tm, n_ctx):
    i = pl.program_id(1)
    nb = _norm_modulate(h_ref, mod_ref, g_ref, 3, i * tm, n_ctx).astype(BF16)
    nb_ref[...] = nb
    q_ref[...] = jnp.dot(nb, w_ref[...], preferred_element_type=F32)


def _peer_q(h, modl, g, wq, n_ctx):
    b, s, d = h.shape
    tm = 768
    n = wq.shape[1]
    return pl.pallas_call(
        functools.partial(_peer_q_kernel, tm=tm, n_ctx=n_ctx),
        grid=(b, s // tm),
        in_specs=[pl.BlockSpec((None, tm, d), lambda bi, i: (bi, i, 0)),
                  pl.BlockSpec((None, 2, 8, d), lambda bi, i: (bi, 0, 0, 0)),
                  pl.BlockSpec((1, d), lambda bi, i: (0, 0)),
                  pl.BlockSpec((d, n), lambda bi, i: (0, 0))],
        out_specs=[pl.BlockSpec((None, tm, n), lambda bi, i: (bi, i, 0)),
                   pl.BlockSpec((None, tm, d), lambda bi, i: (bi, i, 0))],
        out_shape=[jax.ShapeDtypeStruct((b, s, n), F32), jax.ShapeDtypeStruct((b, s, d), BF16)],
        compiler_params=_cparams(("arbitrary", "arbitrary")),
        name="peer_q",
    )(h, modl, g, wq)


def _lane_group(shape, width):
    return lax.broadcasted_iota(I32, shape, 1) >> int(math.log2(width))


def _head_mean_sq(x, n_heads, dh):
    hid = _lane_group(x.shape, dh)
    x2 = x * x
    ms = jnp.zeros_like(x)
    for hh in range(n_heads):
        m = hid == hh
        s = jnp.sum(jnp.where(m, x2, 0.0), axis=-1, keepdims=True)
        ms = jnp.where(m, s, ms)
    return ms * (1.0 / dh)


def _rope(x, cos, sin_signed, qs):
    w = x.shape[-1]
    lane = lax.broadcasted_iota(I32, x.shape, 1)
    lo = (lane & (2 * qs - 1)) < qs
    partner = jnp.where(lo, pltpu.roll(x, w - qs, 1), pltpu.roll(x, qs, 1))
    return x * cos + partner * sin_signed


def _prep_kernel(na_ref, gqa_ref, dq_ref, dk_ref, dv_ref, cg_ref, sg_ref, cd_ref, sd_ref, qn_ref, kn_ref, o_ref):
    na = na_ref[...]
    o_ref[:, 0:256] = (na[:, 0:256] * (NA_DH ** -0.5)).astype(BF16)
    o_ref[:, 256:768] = na[:, 256:768].astype(BF16)
    gq = gqa_ref[:, 0:256]
    gk = gqa_ref[:, 256:384]
    cg, sg = cg_ref[...], sg_ref[...]
    gq = gq * lax.rsqrt(_head_mean_sq(gq, GQA_HEADS, GQA_DH) + EPS) * qn_ref[...]
    gk = gk * lax.rsqrt(_head_mean_sq(gk, GQA_KV, GQA_DH) + EPS) * kn_ref[...]
    o_ref[:, 768:1024] = (_rope(gq, cg, sg, GQA_DH // 4) * (GQA_DH ** -0.5)).astype(BF16)
    o_ref[:, 1024:1152] = _rope(gk, cg[:, 0:128], sg[:, 0:128], GQA_DH // 4).astype(BF16)
    o_ref[:, 1152:1280] = gqa_ref[:, 384:512].astype(BF16)
    cd, sd = cd_ref[...], sd_ref[...]
    o_ref[:, 1280:1536] = (_rope(dq_ref[...], cd, sd, DIFF_DH // 4) * (DIFF_DH ** -0.5)).astype(BF16)
    o_ref[:, 1536:1792] = _rope(dk_ref[...], cd, sd, DIFF_DH // 4).astype(BF16)
    o_ref[:, 1792:2048] = dv_ref[...].astype(BF16)


def _prep(mix, cg, sg, cd, sd, qn, kn):
    b, s, _ = mix.shape
    tm = ROW_TILE
    tab = pl.BlockSpec((tm, 256), lambda bi, i: (i, 0))
    return pl.pallas_call(
        _prep_kernel,
        grid=(b, s // tm),
        in_specs=[pl.BlockSpec((None, tm, 768), lambda bi, i: (bi, i, 0)),
                  pl.BlockSpec((None, tm, 512), lambda bi, i: (bi, i, 3)),
                  pl.BlockSpec((None, tm, 256), lambda bi, i: (bi, i, 8)),
                  pl.BlockSpec((None, tm, 256), lambda bi, i: (bi, i, 9)),
                  pl.BlockSpec((None, tm, 256), lambda bi, i: (bi, i, 10)),
                  tab, tab, tab, tab,
                  pl.BlockSpec((1, 256), lambda bi, i: (0, 0)),
                  pl.BlockSpec((1, 128), lambda bi, i: (0, 0))],
        out_specs=pl.BlockSpec((None, tm, 2048), lambda bi, i: (bi, i, 0)),
        out_shape=jax.ShapeDtypeStruct((b, s, 2048), BF16),
        compiler_params=_cparams(("arbitrary", "arbitrary")),
        name="qk_prep",
    )(mix, mix, mix, mix, mix, cg, sg, cd, sd, qn, kn)


def _rope_tables(n_ctx, n_lat, dh, width):
    qs = dh // 4
    t = np.arange(n_lat)
    rows, cols = (t // GRID_W).astype(np.float64), (t % GRID_W).astype(np.float64)
    lane = np.arange(width) % dh
    part = lane // (2 * qs)
    u = lane % (2 * qs)
    f = u % qs
    lo = u < qs
    freqs = ROPE_THETA ** (-f.astype(np.float64) / qs)
    pos = np.where(part[None, :] == 0, rows[:, None], cols[:, None])
    ang = (pos.astype(np.float32) * freqs.astype(np.float32)[None, :]).astype(np.float32)
    cos = np.cos(ang.astype(np.float64))
    sin = np.sin(ang.astype(np.float64)) * np.where(lo, -1.0, 1.0)[None, :]
    cos = np.concatenate([np.ones((n_ctx, width)), cos], axis=0)
    sin = np.concatenate([np.zeros((n_ctx, width)), sin], axis=0)
    return jnp.asarray(cos, F32), jnp.asarray(sin, F32)


def _softmax_rows(s):
    m = jnp.max(s, axis=-1, keepdims=True)
    p = jnp.exp(s - m)
    return p, jnp.sum(p, axis=-1, keepdims=True)


def _dot_nt(a, b):
    return lax.dot_general(a, b, (((1,), (1,)), ((), ())), preferred_element_type=F32)


def _gqa_kernel(q_ref, k_ref, v_ref, o_ref, *, n_ctx):
    i = pl.program_id(1)

    def run(nk):
        k = k_ref[0:nk, :]
        v = v_ref[0:nk, :]
        q = q_ref[...]
        tq = q.shape[0]
        lane = lax.broadcasted_iota(I32, (tq, LANE), 1)
        grp = _lane_group((tq, LANE), GQA_DH)
        outs = []
        for g in range(GQA_KV):
            mask = grp == g
            ql = jnp.concatenate([jnp.where(mask, q[:, 0:128], 0), jnp.where(mask, q[:, 128:256], 0)], axis=0)
            p, l = _softmax_rows(_dot_nt(ql, k))
            outs.append(jnp.dot(p.astype(BF16), v, preferred_element_type=F32) / l)
        for half in range(2):
            sel = jnp.where(lane < GQA_DH, outs[0][half * tq:(half + 1) * tq], outs[1][half * tq:(half + 1) * tq])
            o_ref[:, half * 128:(half + 1) * 128] = sel.astype(o_ref.dtype)

    @pl.when(i == 0)
    def _():
        run(n_ctx)

    @pl.when(i > 0)
    def _():
        run(k_ref.shape[0])


def _gqa_attention(pb, n_ctx):
    b, s, _ = pb.shape
    tq = ROW_TILE
    return pl.pallas_call(
        functools.partial(_gqa_kernel, n_ctx=n_ctx),
        grid=(b, s // tq),
        in_specs=[pl.BlockSpec((None, tq, 256), lambda bi, i: (bi, i, 3)),
                  pl.BlockSpec((None, s, 128), lambda bi, i: (bi, 0, 8)),
                  pl.BlockSpec((None, s, 128), lambda bi, i: (bi, 0, 9))],
        out_specs=pl.BlockSpec((None, tq, 256), lambda bi, i: (bi, i, 0)),
        out_shape=jax.ShapeDtypeStruct((b, s, 256), BF16),
        compiler_params=_cparams(("arbitrary", "arbitrary")),
        name="gqa_attention",
    )(pb, pb, pb)


def _diff_kernel(q_ref, k_ref, v_ref, par_ref, g_ref, o_ref, *, n_ctx):
    i = pl.program_id(1)

    def run(nk):
        k = k_ref[0:nk, :]
        v = v_ref[0:nk, :]
        q = q_ref[...]
        tq = q.shape[0]
        lam = par_ref[0:1, 0:1]
        comp = _lane_group(q.shape, DIFF_DH)
        head = _lane_group(q.shape, DIFF_DV)
        acc = jnp.zeros(q.shape, F32)
        for hh in range(DIFF_HEADS):
            ql = jnp.concatenate([jnp.where(comp == 2 * hh, q, 0), jnp.where(comp == 2 * hh + 1, q, 0)], axis=0)
            p, l = _softmax_rows(_dot_nt(ql, k))
            p = p / l
            pc = (p[0:tq] - lam * p[tq:2 * tq]).astype(BF16)
            o = jnp.dot(pc, v, preferred_element_type=F32)
            acc = jnp.where(head == hh, o, acc)
        y = acc * lax.rsqrt(_head_mean_sq(acc, DIFF_HEADS, DIFF_DV) + EPS) * g_ref[...]
        o_ref[...] = (y * par_ref[1:2, 0:1]).astype(o_ref.dtype)

    @pl.when(i == 0)
    def _():
        run(n_ctx)

    @pl.when(i > 0)
    def _():
        run(k_ref.shape[0])


def _diff_attention(pb, par, subln, n_ctx):
    b, s, _ = pb.shape
    tq = ROW_TILE
    return pl.pallas_call(
        functools.partial(_diff_kernel, n_ctx=n_ctx),
        grid=(b, s // tq),
        in_specs=[pl.BlockSpec((None, tq, 256), lambda bi, i: (bi, i, 5)),
                  pl.BlockSpec((None, s, 256), lambda bi, i: (bi, 0, 6)),
                  pl.BlockSpec((None, s, 256), lambda bi, i: (bi, 0, 7)),
                  pl.BlockSpec((8, 128), lambda bi, i: (0, 0)),
                  pl.BlockSpec((1, 256), lambda bi, i: (0, 0))],
        out_specs=pl.BlockSpec((None, tq, 256), lambda bi, i: (bi, i, 0)),
        out_shape=jax.ShapeDtypeStruct((b, s, 256), BF16),
        compiler_params=_cparams(("arbitrary", "arbitrary")),
        name="diff_attention",
    )(pb, pb, pb, par, subln)


def _na_kernel(q_ref, k_ref, v_ref, bias_ref, o_ref, *, n_ctx, n_rows):
    i = pl.program_id(1)
    tq = q_ref.shape[0]

    def stack_heads(q):
        head = _lane_group(q.shape, NA_DH)
        return jnp.concatenate([jnp.where(head == hh, q, 0) for hh in range(NA_HEADS)], axis=0)

    def unstack_heads(o, rows):
        head = _lane_group((rows, 256), NA_DH)
        out = jnp.zeros((rows, 256), F32)
        for hh in range(NA_HEADS):
            out = jnp.where(head == hh, o[hh * rows:(hh + 1) * rows], out)
        return out

    @pl.when(i == 0)
    def _():
        kc = k_ref[0:n_ctx, :]
        vc = v_ref[0:n_ctx, :]
        p, l = _softmax_rows(_dot_nt(stack_heads(q_ref[...]), kc))
        o = jnp.dot(p.astype(BF16), vc, preferred_element_type=F32) / l
        o_ref[...] = unstack_heads(o, tq).astype(o_ref.dtype)

    @pl.when(i > 0)
    def _():
        kc = k_ref[0:n_ctx, :]
        vc = v_ref[0:n_ctx, :]
        for rr in range(tq // GRID_W):
            r = (i - 1) * (tq // GRID_W) + rr
            rs = jnp.clip(r - NA_KH // 2, 0, n_rows - NA_KH)
            start = pl.multiple_of(n_ctx + rs * GRID_W, GRID_W)
            kcat = jnp.concatenate([kc, k_ref[pl.ds(start, NA_KH * GRID_W), :]], axis=0)
            vcat = jnp.concatenate([vc, v_ref[pl.ds(start, NA_KH * GRID_W), :]], axis=0)
            q = q_ref[rr * GRID_W:(rr + 1) * GRID_W, :]
            s = _dot_nt(stack_heads(q), kcat) + bias_ref[r - rs]
            p, l = _softmax_rows(s)
            o = jnp.dot(p.astype(BF16), vcat, preferred_element_type=F32) / l
            o_ref[rr * GRID_W:(rr + 1) * GRID_W, :] = unstack_heads(o, GRID_W).astype(o_ref.dtype)


def _na_attention(pb, bias, n_ctx):
    b, s, _ = pb.shape
    tq = ROW_TILE
    n_rows = (s - n_ctx) // GRID_W
    return pl.pallas_call(
        functools.partial(_na_kernel, n_ctx=n_ctx, n_rows=n_rows),
        grid=(b, s // tq),
        in_specs=[pl.BlockSpec((None, tq, 256), lambda bi, i: (bi, i, 0)),
                  pl.BlockSpec((None, s, 256), lambda bi, i: (bi, 0, 1)),
                  pl.BlockSpec((None, s, 256), lambda bi, i: (bi, 0, 2)),
                  pl.BlockSpec(bias.shape, lambda bi, i: (0, 0, 0))],
        out_specs=pl.BlockSpec((None, tq, 256), lambda bi, i: (bi, i, 0)),
        out_shape=jax.ShapeDtypeStruct((b, s, 256), BF16),
        compiler_params=_cparams(("arbitrary", "arbitrary")),
        name="na_attention",
    )(pb, pb, pb, bias)


def _na_bias_table(rpb, n_ctx):
    o = np.arange(NA_KH)[:, None, None, None]
    w = np.arange(GRID_W)[None, :, None, None]
    ki = np.arange(NA_KH)[None, None, :, None]
    kc = np.arange(GRID_W)[None, None, None, :]
    cs = np.clip(w - NA_KW // 2, 0, GRID_W - NA_KW)
    inwin = np.broadcast_to((kc >= cs) & (kc < cs + NA_KW), (NA_KH, GRID_W, NA_KH, GRID_W))
    dr = np.broadcast_to(ki - o + NA_KH - 1, inwin.shape)
    dc = np.broadcast_to(np.clip(kc - w + NA_KW - 1, 0, 2 * NA_KW - 2), inwin.shape)
    vals = rpb.astype(F32)[:, dr, dc]
    vals = jnp.where(jnp.asarray(inwin)[None], vals, NEG)
    vals = jnp.transpose(vals, (1, 0, 2, 3, 4)).reshape(NA_KH, NA_HEADS * GRID_W, NA_KH * GRID_W)
    return jnp.concatenate([jnp.zeros((NA_KH, NA_HEADS * GRID_W, n_ctx), F32), vals], axis=-1)


def _hy_pre_kernel(u_ref, w_ref, b_ref, x0_ref, z_ref, *, bounds, tm):
    s = u_ref.shape[0]
    w0, w1, w2, bias = w_ref[0:1, :], w_ref[1:2, :], w_ref[2:3, :], b_ref[...]
    rid = lax.broadcasted_iota(I32, (tm, u_ref.shape[1]), 0)
    for c0 in range(0, s, tm):
        u = u_ref[c0:c0 + tm, :]
        prev = pltpu.roll(u, 1, 0)
        nxt = pltpu.roll(u, tm - 1, 0)
        first = jnp.zeros_like(w0) if c0 in bounds else u_ref[c0 - 1:c0, :]
        last = jnp.zeros_like(w0) if (c0 + tm) in bounds else u_ref[c0 + tm:c0 + tm + 1, :]
        prev = jnp.where(rid == 0, first, prev)
        nxt = jnp.where(rid == tm - 1, last, nxt)
        uc = prev * w0 + u * w1 + nxt * w2 + bias
        x0_ref[c0:c0 + tm, :] = uc[:, 0:HY_W]
        z_ref[c0:c0 + tm, :] = uc[:, HY_W:2 * HY_W] * uc[:, 2 * HY_W:3 * HY_W]


def _hy_pre(mix, w_short, b_short, n_ctx):
    b, s, _ = mix.shape
    return pl.pallas_call(
        functools.partial(_hy_pre_kernel, bounds=(0, n_ctx, s), tm=ROW_TILE),
        grid=(b,),
        in_specs=[pl.BlockSpec((None, s, 3 * HY_W), lambda bi: (bi, 0, 1)),
                  pl.BlockSpec((3, 3 * HY_W), lambda bi: (0, 0)),
                  pl.BlockSpec((1, 3 * HY_W), lambda bi: (0, 0))],
        out_specs=[pl.BlockSpec((None, s, HY_W), lambda bi: (bi, 0, 0)),
                   pl.BlockSpec((None, s, HY_W), lambda bi: (bi, 0, 0))],
        out_shape=[jax.ShapeDtypeStruct((b, s, HY_W), F32), jax.ShapeDtypeStruct((b, s, HY_W), F32)],
        compiler_params=_cparams(("arbitrary",)),
        name="hyena_pre",
    )(mix, w_short, b_short)


def _hy_filter_kernel(z_ref, w1_ref, b1_ref, w2_ref, b2_ref, w3_ref, b3_ref, w4_ref, f_ref, dec_ref, o_ref):
    def lin(a, w_ref, b_ref):
        return jnp.dot(a, w_ref[...], preferred_element_type=F32, precision=HIGHEST) + b_ref[...]

    hh = jnp.sin(f_ref[0:1, :] * lin(z_ref[...], w1_ref, b1_ref))
    hh = jnp.sin(f_ref[1:2, :] * lin(hh, w2_ref, b2_ref))
    hh = jnp.sin(f_ref[2:3, :] * lin(hh, w3_ref, b3_ref))
    hh = jnp.dot(hh, w4_ref[...], preferred_element_type=F32, precision=HIGHEST) * dec_ref[...]
    o_ref[...] = hh / jnp.sum(jnp.abs(hh), axis=0, keepdims=True)


def _pad_to(a, shape):
    return jnp.pad(a, [(0, t - s) for s, t in zip(a.shape, shape)])


def _hy_filters(n, w1, b1, w2, b2, w3, b3, w4, freq):
    depth = w1.shape[0]
    t = np.linspace(0.0, 1.0, n, dtype=np.float32)[:, None]
    w = np.float32(2.0 * math.pi / n) * np.arange(n, dtype=np.float32)[:, None]
    bands = np.linspace(1e-4, HY_BANDS - 1, HY_BANDS, dtype=np.float32)[None, :]
    z = np.concatenate([t, np.cos(w * bands), np.sin(w * bands)], axis=-1).astype(np.float32)
    z = np.pad(z, ((0, 0), (0, LANE - z.shape[1])))
    deltas = np.linspace(math.log(HY_TARGET) / HY_SLOW_DECAY, math.log(HY_TARGET) / HY_FAST_DECAY, HY_W, dtype=np.float32)
    deltas = np.tile(np.abs(deltas), 2)
    decay = np.exp(-t * deltas[None, :]).astype(np.float32)
    p = LANE
    args = (jnp.asarray(z), _pad_to(w1, (depth, p, p)), _pad_to(b1[:, None, :], (depth, 1, p)),
            _pad_to(w2, (depth, p, p)), _pad_to(b2[:, None, :], (depth, 1, p)),
            _pad_to(w3, (depth, p, p)), _pad_to(b3[:, None, :], (depth, 1, p)),
            _pad_to(w4, (depth, p, 2 * HY_W)), _pad_to(freq, (depth, 8, p)), jnp.asarray(decay))
    per_layer = lambda shp: pl.BlockSpec((None,) + shp, lambda l: (l,) + (0,) * len(shp))
    const = lambda shp: pl.BlockSpec(shp, lambda l: (0,) * len(shp))
    return pl.pallas_call(
        _hy_filter_kernel,
        grid=(depth,),
        in_specs=[const((n, p)), per_layer((p, p)), per_layer((1, p)), per_layer((p, p)), per_layer((1, p)),
                  per_layer((p, p)), per_layer((1, p)), per_layer((p, 2 * HY_W)), per_layer((8, p)),
                  const((n, 2 * HY_W))],
        out_specs=per_layer((n, 2 * HY_W)),
        out_shape=jax.ShapeDtypeStruct((depth, n, 2 * HY_W), F32),
        compiler_params=_cparams(("arbitrary",)),
        name="hyena_filters",
    )(*args)


def _conv_taps(filt):
    hf, hb = filt[..., :HY_W], filt[..., HY_W:]
    taps = jnp.concatenate([hb[:, :0:-1], (hf[:, 0:1] + hb[:, 0:1]), hf[:, 1:], jnp.zeros_like(hf[:, 0:1])], axis=1)
    return jnp.transpose(taps, (0, 2, 1))


def _hy_conv_kernel(z_ref, k_ref, o_ref, *, nblk, nb):
    cg = z_ref.shape[0]
    n2 = k_ref.shape[1]
    ncols = (2 * nblk - 1) * LANE

    def body(ci, _):
        krow = k_ref[pl.ds(ci, 1), :]
        kb = jnp.broadcast_to(krow, (LANE, n2))
        big = pltpu.roll(kb, n2 - (LANE - 1), 1, stride=1, stride_axis=0)[:, :ncols].astype(BF16)
        o_ref[ci] = jnp.zeros(o_ref.shape[1:], F32)
        for m in range(-(nblk - 1), nblk):
            km = big[:, (m + nblk - 1) * LANE:(m + nblk) * LANE]
            cnt = (nblk - abs(m)) * nb
            src = 0 if m >= 0 else -m * nb
            dst = m * nb if m >= 0 else 0
            zin = z_ref[ci, src:src + cnt, :]
            o_ref[ci, dst:dst + cnt, :] += jnp.dot(zin, km, preferred_element_type=F32)
        return 0

    lax.fori_loop(0, cg, body, 0)


def _hy_conv(zr, taps, nblk, nb):
    c, r, _ = zr.shape
    cg = 8
    return pl.pallas_call(
        functools.partial(_hy_conv_kernel, nblk=nblk, nb=nb),
        grid=(c // cg,),
        in_specs=[pl.BlockSpec((cg, r, LANE), lambda i: (i, 0, 0)),
                  pl.BlockSpec((cg, taps.shape[1]), lambda i: (i, 0))],
        out_specs=pl.BlockSpec((cg, r, LANE), lambda i: (i, 0, 0)),
        out_shape=jax.ShapeDtypeStruct((c, r, LANE), F32),
        compiler_params=_cparams(("arbitrary",)),
        name="hyena_conv",
    )(zr, taps)


def _hy_long_conv(z, taps):
    b, n, c = z.shape
    nblk = n // LANE
    zr = jnp.transpose(z.astype(BF16).reshape(b, nblk, LANE, c), (3, 1, 0, 2)).reshape(c, nblk * b, LANE)
    y = _hy_conv(zr, taps, nblk, b)
    return jnp.transpose(y.reshape(c, nblk, b, LANE), (2, 1, 3, 0)).reshape(b, n, c)


def _merge_kernel(h_ref, mod_ref, ya_ref, x0_ref, z_ref, yconv_ref, hb_ref, yc_ref, yd_ref, gate_ref,
                  wb_ref, wo_ref, o_ref, *, n_ctx, tm):
    i = pl.program_id(1)
    yb = (x0_ref[...] * (yconv_ref[...] + z_ref[...] * hb_ref[...])).astype(BF16)
    ys = (ya_ref[...], yb, yc_ref[...], yd_ref[...])
    acc = jnp.zeros(h_ref.shape, F32)
    d = h_ref.shape[1]
    for k in range(N_BRANCH):
        proj = jnp.dot(ys[k], wb_ref[k], preferred_element_type=F32)
        acc = acc + jax.nn.sigmoid(gate_ref[:, k * d:(k + 1) * d]) * proj
    out = jnp.dot(acc.astype(BF16), wo_ref[...], preferred_element_type=F32)
    row = i * tm + lax.broadcasted_iota(I32, out.shape, 0)
    gate = jnp.where(row < n_ctx, mod_ref[0, 2:3, :], mod_ref[1, 2:3, :])
    o_ref[...] = h_ref[...] + gate * out


def _merge(h, modl, ya, x0, z, yconv, hbias, yc, yd, gates, wb, wo, n_ctx):
    b, s, d = h.shape
    tm = ROW_TILE
    row = lambda w: pl.BlockSpec((None, tm, w), lambda bi, i: (bi, i, 0))
    return pl.pallas_call(
        functools.partial(_merge_kernel, n_ctx=n_ctx, tm=tm),
        grid=(b, s // tm),
        in_specs=[row(d), pl.BlockSpec((None, 2, 8, d), lambda bi, i: (bi, 0, 0, 0)),
                  row(256), row(256), row(256), row(256), pl.BlockSpec((1, 256), lambda bi, i: (0, 0)),
                  row(256), row(256), row(GATE_W),
                  pl.BlockSpec(wb.shape, lambda bi, i: (0, 0, 0)),
                  pl.BlockSpec(wo.shape, lambda bi, i: (0, 0))],
        out_specs=row(d),
        out_shape=jax.ShapeDtypeStruct((b, s, d), F32),
        compiler_params=_cparams(("arbitrary", "arbitrary")),
        name="merge",
    )(h, modl, ya, x0, z, yconv, hbias, yc, yd, gates, wb, wo)


def _topk_rows(s, k):
    r = s.shape[0]
    rid = lax.broadcasted_iota(I32, s.shape, 0)
    vals, idxs = [], []
    for _ in range(k):
        m = jnp.max(s, axis=0, keepdims=True)
        ix = jnp.min(jnp.where(s == m, rid, r), axis=0, keepdims=True)
        vals.append(m)
        idxs.append(ix)
        s = jnp.where(rid == ix, -jnp.inf, s)
    return jnp.concatenate(vals, axis=0), jnp.concatenate(idxs, axis=0)


def _select_rows(table, pos):
    out = jnp.zeros(pos.shape, table.dtype)
    for a in range(table.shape[0]):
        out = jnp.where(pos == a, table[a:a + 1, :], out)
    return out


def _peer_route_kernel(q_ref, keys_ref, i_ref, j_ref, g_ref):
    st = _dot_nt(keys_ref[...], q_ref[...].astype(BF16))
    sv1, si1 = _topk_rows(st[0:PEER_NKEYS], PEER_TOPK)
    sv2, si2 = _topk_rows(st[PEER_NKEYS:2 * PEER_NKEYS], PEER_TOPK)
    cand = jnp.concatenate([sv1[a:a + 1, :] + sv2 for a in range(PEER_TOPK)], axis=0)
    best, pos = _topk_rows(cand, PEER_TOPK)
    i_ref[...] = _select_rows(si1, pos >> int(math.log2(PEER_TOPK)))
    j_ref[...] = _select_rows(si2, pos & (PEER_TOPK - 1))
    e = jnp.exp(best - jnp.max(best, axis=0, keepdims=True))
    g_ref[...] = e / jnp.sum(e, axis=0, keepdims=True)


def _peer_route(q, keys_blk):
    t, _ = q.shape
    tt = 256
    out = pl.BlockSpec((None, PEER_TOPK, tt), lambda ti, hh: (hh, 0, ti))
    shp = (PEER_HEADS, PEER_TOPK, t)
    return pl.pallas_call(
        _peer_route_kernel,
        grid=(t // tt, PEER_HEADS),
        in_specs=[pl.BlockSpec((tt, PEER_DK), lambda ti, hh: (ti, hh)),
                  pl.BlockSpec((None, 2 * PEER_NKEYS, PEER_DK), lambda ti, hh: (hh, 0, 0))],
        out_specs=[out, out, out],
        out_shape=[jax.ShapeDtypeStruct(shp, I32), jax.ShapeDtypeStruct(shp, I32), jax.ShapeDtypeStruct(shp, F32)],
        compiler_params=_cparams(("arbitrary", "arbitrary")),
        name="peer_route",
    )(q, keys_blk)


def _peer_expert_kernel(h_ref, mod_ref, x_ref, i_ref, j_ref, g_ref, ut_ref, v_ref, o_ref, gs_ref, acc_ref,
                        *, n_ctx, tm):
    ti, e = pl.program_id(1), pl.program_id(2)
    n_e = pl.num_programs(2)
    te = ut_ref.shape[1]
    nk = PEER_NKEYS

    @pl.when(e == 0)
    def _():
        sub = lax.broadcasted_iota(I32, (nk, nk), 0)

        def body(t, _):
            irow = i_ref[pl.ds(t, 1), :]
            jrow = j_ref[pl.ds(t, 1), :]
            grow = g_ref[pl.ds(t, 1), :]
            at = jnp.where(sub == irow, 1.0, 0.0).astype(BF16)
            bt = jnp.where(sub == jrow, grow, 0.0)
            bhi = bt.astype(BF16)
            blo = (bt - bhi.astype(F32)).astype(BF16)
            gt = _dot_nt(jnp.concatenate([at, at], axis=1), jnp.concatenate([bhi, blo], axis=1))
            gs_ref[pl.ds(pl.multiple_of(t * nk, nk), nk), :] = gt
            return 0

        lax.fori_loop(0, tm, body, 0)
        acc_ref[...] = jnp.zeros(acc_ref.shape, F32)

    hid = jnp.dot(x_ref[...], ut_ref[...], preferred_element_type=F32)
    cols = []
    for ii in range(te // nk):
        gi = gs_ref[pl.ds(e * (te // nk) + ii, tm, stride=nk), :]
        a = hid[:, ii * nk:(ii + 1) * nk]
        act = 0.5 * a * (1.0 + lax.erf(a * (2.0 ** -0.5)))
        cols.append((gi * act).astype(BF16))
    acc_ref[...] += jnp.dot(jnp.concatenate(cols, axis=1), v_ref[...], preferred_element_type=F32)

    @pl.when(e == n_e - 1)
    def _():
        row = ti * tm + lax.broadcasted_iota(I32, acc_ref.shape, 0)
        gate = jnp.where(row < n_ctx, mod_ref[0, 5:6, :], mod_ref[1, 5:6, :])
        o_ref[...] = h_ref[...] + gate * acc_ref[...]


def _peer_experts(h, modl, xb, i_idx, j_idx, g, ut, v, n_ctx):
    b, s, d = h.shape
    tm, te = ROW_TILE, 2048
    n_exp = ut.shape[1]
    slots = PEER_HEADS * PEER_TOPK
    row = lambda w: pl.BlockSpec((None, tm, w), lambda bi, i, e: (bi, i, 0))
    return pl.pallas_call(
        functools.partial(_peer_expert_kernel, n_ctx=n_ctx, tm=tm),
        grid=(b, s // tm, n_exp // te),
        in_specs=[row(d), pl.BlockSpec((None, 2, 8, d), lambda bi, i, e: (bi, 0, 0, 0)),
                  row(d), row(slots), row(slots), row(slots),
                  pl.BlockSpec((d, te), lambda bi, i, e: (0, e)),
                  pl.BlockSpec((te, d), lambda bi, i, e: (e, 0))],
        out_specs=row(d),
        out_shape=jax.ShapeDtypeStruct((b, s, d), F32),
        scratch_shapes=[pltpu.VMEM((tm * PEER_NKEYS, PEER_NKEYS), F32), pltpu.VMEM((tm, d), F32)],
        compiler_params=_cparams(("arbitrary", "arbitrary", "arbitrary")),
        name="peer_experts",
    )(h, modl, xb, i_idx, j_idx, g, ut, v)


def _final_norm_kernel(x_ref, g_ref, o_ref):
    x = x_ref[...]
    o_ref[...] = x * lax.rsqrt(jnp.mean(x * x, axis=-1, keepdims=True) + EPS) * g_ref[...]


def _final_norm(h, g, n_ctx):
    b, s, d = h.shape
    tm = ROW_TILE
    n_lat = s - n_ctx
    off = n_ctx // tm
    return pl.pallas_call(
        _final_norm_kernel,
        grid=(b, n_lat // tm),
        in_specs=[pl.BlockSpec((None, tm, d), lambda bi, i: (bi, i + off, 0)),
                  pl.BlockSpec((1, d), lambda bi, i: (0, 0))],
        out_specs=pl.BlockSpec((None, tm, d), lambda bi, i: (bi, i, 0)),
        out_shape=jax.ShapeDtypeStruct((b, n_lat, d), F32),
        compiler_params=_cparams(("arbitrary", "arbitrary")),
        name="final_norm",
    )(h, g)


def _prep_w_in(w_in):
    depth, d, _ = w_in.shape
    mixw = w_in[:, :, :2816]
    gq = mixw[:, :, 1536:1792].reshape(depth, d, GQA_HEADS, GQA_DH)[:, :, (0, 2, 1, 3), :].reshape(depth, d, 256)
    mixw = jnp.concatenate([mixw[:, :, :1536], gq, mixw[:, :, 1792:], jnp.zeros((depth, d, MIX_W - 2816), w_in.dtype)], axis=-1)
    return jnp.concatenate([mixw, w_in[:, :, 2816:]], axis=-1).astype(BF16)


def _prep_peer_keys(keys):
    depth = keys.shape[0]
    half = PEER_DK // 2
    z = jnp.zeros((depth, PEER_HEADS, PEER_NKEYS, half), keys.dtype)
    top = jnp.concatenate([keys[:, :, 0], z], axis=-1)
    bot = jnp.concatenate([z, keys[:, :, 1]], axis=-1)
    return jnp.concatenate([top, bot], axis=2).astype(BF16)


def kernel(x, c, ctx, c_ctx, w_mod, b_mod, norm1_g, norm2_g, w_in, na_rpb, hy_short_w, hy_short_b, hy_w1, hy_b1, hy_w2, hy_b2, hy_w3, hy_b3, hy_w4, hy_freq, hy_bias, gqa_qn, gqa_kn, diff_lq1, diff_lk1, diff_lq2, diff_lk2, diff_subln, w_branch, w_out, peer_wq, peer_keys, peer_u, peer_v, final_g):
    B, L, D = x.shape
    C = ctx.shape[1]
    S = C + L
    depth = w_mod.shape[0]
    assert C == ROW_TILE and L % ROW_TILE == 0 and D == 1024 and L % GRID_W == 0

    h = jnp.concatenate([ctx, x], axis=1)

    r = -(-(B + 1) // 8) * 8
    cvec = jnp.zeros((r, D), F32).at[:B].set(c).at[B].set(c_ctx)
    modall = _modulation(cvec, w_mod, b_mod)
    mod_lat = modall[:, :B].reshape(depth, B, 1, 6, D)
    mod_ctx = jnp.broadcast_to(modall[:, B].reshape(depth, 1, 1, 6, D), (depth, B, 1, 6, D))
    mods = jnp.pad(jnp.concatenate([mod_ctx, mod_lat], axis=2), ((0, 0), (0, 0), (0, 0), (0, 2), (0, 0)))

    fargs = (hy_w1, hy_b1, hy_w2, hy_b2, hy_w3, hy_b3, hy_w4, hy_freq)
    taps_lat = _conv_taps(_hy_filters(L, *fargs))
    taps_ctx = _conv_taps(_hy_filters(C, *fargs))

    cg, sg = _rope_tables(C, L, GQA_DH, 256)
    cd, sd = _rope_tables(C, L, DIFF_DH, 256)

    lam_init = jnp.asarray([0.8 - 0.6 * math.exp(-0.3 * l) for l in range(depth)], F32)
    lam = (jnp.exp(jnp.sum(diff_lq1.astype(F32) * diff_lk1.astype(F32), axis=-1))
           - jnp.exp(jnp.sum(diff_lq2.astype(F32) * diff_lk2.astype(F32), axis=-1)) + lam_init)
    diff_par = jnp.zeros((depth, 8, LANE), F32).at[:, 0, :].set(lam[:, None]).at[:, 1, :].set(1.0 - lam_init[:, None])

    wb = w_branch.astype(BF16)
    wb = wb.at[:, 2].set(wb[:, 2].reshape(depth, GQA_HEADS, GQA_DH, D)[:, (0, 2, 1, 3)].reshape(depth, BRANCH_W, D))

    xs = dict(
        mods=mods, n1=norm1_g[:, None, :], n2=norm2_g[:, None, :], w_in=_prep_w_in(w_in),
        rpb=na_rpb, sw=hy_short_w, sb=hy_short_b[:, None, :], hbias=hy_bias[:, None, :],
        taps_lat=taps_lat, taps_ctx=taps_ctx,
        qn=jnp.tile(gqa_qn, (1, GQA_HEADS))[:, None, :], kn=jnp.tile(gqa_kn, (1, GQA_KV))[:, None, :],
        diff_par=diff_par, subln=jnp.tile(diff_subln, (1, DIFF_HEADS))[:, None, :],
        wb=wb, wo=w_out.astype(BF16), wq=peer_wq.astype(BF16), keys=_prep_peer_keys(peer_keys),
        u=peer_u, v=peer_v,
    )

    def layer(h, p):
        mix, gates = _in_proj(h, p["mods"], p["n1"], p["w_in"], C)
        pb = _prep(mix, cg, sg, cd, sd, p["qn"], p["kn"])
        ya = _na_attention(pb, _na_bias_table(p["rpb"], C), C)
        yc = _gqa_attention(pb, C)
        yd = _diff_attention(pb, p["diff_par"], p["subln"], C)
        x0, z = _hy_pre(mix, p["sw"], p["sb"], C)
        yconv = jnp.concatenate([_hy_long_conv(z[:, :C], p["taps_ctx"]), _hy_long_conv(z[:, C:], p["taps_lat"])], axis=1)
        h = _merge(h, p["mods"], ya, x0, z, yconv, p["hbias"], yc, yd, gates, p["wb"], p["wo"], C)

        q, xb = _peer_q(h, p["mods"], p["n2"], p["wq"], C)
        i_idx, j_idx, g = _peer_route(q.reshape(B * S, -1), p["keys"])
        slots = PEER_HEADS * PEER_TOPK
        to_rows = lambda a: jnp.transpose(a, (2, 0, 1)).reshape(B, S, slots)
        ut = jnp.transpose(p["u"].astype(BF16))
        h = _peer_experts(h, p["mods"], xb, to_rows(i_idx), to_rows(j_idx), to_rows(g), ut, p["v"].astype(BF16), C)
        return h, None

    h, _ = lax.scan(layer, h, xs)
    return _final_norm(h, final_g[None, :], C)
```

```python
import functools
import math

import numpy as np
import jax
import jax.numpy as jnp
from jax import lax
from jax.experimental import pallas as pl
from jax.experimental.pallas import tpu as pltpu

F32 = jnp.float32
BF16 = jnp.bfloat16
I32 = jnp.int32
HIGHEST = lax.Precision.HIGHEST

EPS = 1e-6
GRID_W = 64
ROPE_THETA = 10000.0
NA_HEADS, NA_DH, NA_KH, NA_KW = 4, 64, 8, 16
HY_W, HY_BANDS, HY_FFN = 256, 16, 64
HY_FAST_DECAY, HY_SLOW_DECAY, HY_TARGET = 0.3, 1.5, 1e-2
GQA_HEADS, GQA_KV, GQA_DH = 4, 2, 64
DIFF_HEADS, DIFF_DH, DIFF_DV = 4, 32, 64
N_BRANCH, BRANCH_W = 4, 256
PEER_HEADS, PEER_NKEYS, PEER_DK, PEER_TOPK = 8, 128, 128, 16

LANE = 128
ROW_TILE = 256
VMEM_LIMIT = 56 * 1024 * 1024
NEG = -1e30
MIX_W = 3072
GATE_W = N_BRANCH * 1024
COL_TILE = 1024


def _cparams(sem):
    return pltpu.CompilerParams(dimension_semantics=sem, vmem_limit_bytes=VMEM_LIMIT)


def _mod_kernel(c_ref, w_ref, b_ref, o_ref):
    cv = c_ref[...]
    a = cv * jax.nn.sigmoid(cv)
    o_ref[...] = jnp.dot(a, w_ref[...], preferred_element_type=F32, precision=HIGHEST) + b_ref[...]


def _modulation(cvec, w_mod, b_mod):
    depth, d, n = w_mod.shape
    r = cvec.shape[0]
    tn = 1024
    return pl.pallas_call(
        _mod_kernel,
        grid=(depth, n // tn),
        in_specs=[pl.BlockSpec((r, d), lambda l, j: (0, 0)),
                  pl.BlockSpec((None, d, tn), lambda l, j: (l, 0, j)),
                  pl.BlockSpec((None, 1, tn), lambda l, j: (l, 0, j))],
        out_specs=pl.BlockSpec((None, r, tn), lambda l, j: (l, 0, j)),
        out_shape=jax.ShapeDtypeStruct((depth, r, n), F32),
        compiler_params=_cparams(("arbitrary", "arbitrary")),
        name="modulation",
    )(cvec, w_mod, b_mod.reshape(depth, 1, n))


def _norm_modulate(h_ref, mod_ref, g_ref, srow, row0, n_ctx):
    x = h_ref[...]
    y = x * lax.rsqrt(jnp.mean(x * x, axis=-1, keepdims=True) + EPS) * g_ref[...]
    row = row0 + lax.broadcasted_iota(I32, x.shape, 0)
    is_ctx = row < n_ctx
    shift = jnp.where(is_ctx, mod_ref[0, srow:srow + 1, :], mod_ref[1, srow:srow + 1, :])
    scale = jnp.where(is_ctx, mod_ref[0, srow + 1:srow + 2, :], mod_ref[1, srow + 1:srow + 2, :])
    return y * (1.0 + scale) + shift


def _in_proj_kernel(h_ref, mod_ref, g_ref, w_ref, mix_ref, gate_ref, nb_ref, *, tm, n_ctx, n_mix):
    i, j = pl.program_id(1), pl.program_id(2)

    @pl.when(j == 0)
    def _():
        nb_ref[...] = _norm_modulate(h_ref, mod_ref, g_ref, 0, i * tm, n_ctx).astype(BF16)

    res = jnp.dot(nb_ref[...], w_ref[...], preferred_element_type=F32)

    @pl.when(j < n_mix)
    def _():
        mix_ref[...] = res

    @pl.when(j >= n_mix)
    def _():
        gate_ref[...] = res


def _in_proj(h, modl, g, w, n_ctx):
    b, s, d = h.shape
    tm, tn = 768, COL_TILE
    n_mix = MIX_W // tn
    n_tot = (MIX_W + GATE_W) // tn
    return pl.pallas_call(
        functools.partial(_in_proj_kernel, tm=tm, n_ctx=n_ctx, n_mix=n_mix),
        grid=(b, s // tm, n_tot),
        in_specs=[pl.BlockSpec((None, tm, d), lambda bi, i, j: (bi, i, 0)),
                  pl.BlockSpec((None, 2, 8, d), lambda bi, i, j: (bi, 0, 0, 0)),
                  pl.BlockSpec((1, d), lambda bi, i, j: (0, 0)),
                  pl.BlockSpec((d, tn), lambda bi, i, j: (0, j))],
        out_specs=[pl.BlockSpec((None, tm, tn), lambda bi, i, j: (bi, i, jnp.minimum(j, n_mix - 1))),
                   pl.BlockSpec((None, tm, tn), lambda bi, i, j: (bi, i, jnp.maximum(j - n_mix, 0)))],
        out_shape=[jax.ShapeDtypeStruct((b, s, MIX_W), F32), jax.ShapeDtypeStruct((b, s, GATE_W), F32)],
        scratch_shapes=[pltpu.VMEM((tm, d), BF16)],
        compiler_params=_cparams(("arbitrary", "arbitrary", "arbitrary")),
        name="in_proj",
    )(h, modl, g, w)


def _peer_q_kernel(h_ref, mod_ref, g_ref, w_ref, q_ref, nb_ref, *, tm, n_ctx):
    i = pl.program_id(1)
    nb = _norm_modulate(h_ref, mod_ref, g_ref, 3, i * tm, n_ctx).astype(BF16)
    nb_ref[...] = nb
    q_ref[...] = jnp.dot(nb, w_ref[...], preferred_element_type=F32)


def _peer_q(h, modl, g, wq, n_ctx):
    b, s, d = h.shape
    tm = 768
    n = wq.shape[1]
    return pl.pallas_call(
        functools.partial(_peer_q_kernel, tm=tm, n_ctx=n_ctx),
        grid=(b, s // tm),
        in_specs=[pl.BlockSpec((None, tm, d), lambda bi, i: (bi, i, 0)),
                  pl.BlockSpec((None, 2, 8, d), lambda bi, i: (bi, 0, 0, 0)),
                  pl.BlockSpec((1, d), lambda bi, i: (0, 0)),
                  pl.BlockSpec((d, n), lambda bi, i: (0, 0))],
        out_specs=[pl.BlockSpec((None, tm, n), lambda bi, i: (bi, i, 0)),
                   pl.BlockSpec((None, tm, d), lambda bi, i: (bi, i, 0))],
        out_shape=[jax.ShapeDtypeStruct((b, s, n), F32), jax.ShapeDtypeStruct((b, s, d), BF16)],
        compiler_params=_cparams(("arbitrary", "arbitrary")),
        name="peer_q",
    )(h, modl, g, wq)


def _lane_group(shape, width):
    return lax.broadcasted_iota(I32, shape, 1) >> int(math.log2(width))


def _head_mean_sq(x, n_heads, dh):
    hid = _lane_group(x.shape, dh)
    x2 = x * x
    ms = jnp.zeros_like(x)
    for hh in range(n_heads):
        m = hid == hh
        s = jnp.sum(jnp.where(m, x2, 0.0), axis=-1, keepdims=True)
        ms = jnp.where(m, s, ms)
    return ms * (1.0 / dh)


def _rope(x, cos, sin_signed, qs):
    w = x.shape[-1]
    lane = lax.broadcasted_iota(I32, x.shape, 1)
    lo = (lane & (2 * qs - 1)) < qs
    partner = jnp.where(lo, pltpu.roll(x, w - qs, 1), pltpu.roll(x, qs, 1))
    return x * cos + partner * sin_signed


def _prep_kernel(na_ref, gqa_ref, dq_ref, dk_ref, dv_ref, cg_ref, sg_ref, cd_ref, sd_ref, qn_ref, kn_ref, o_ref):
    na = na_ref[...]
    o_ref[:, 0:256] = (na[:, 0:256] * (NA_DH ** -0.5)).astype(BF16)
    o_ref[:, 256:768] = na[:, 256:768].astype(BF16)
    gq = gqa_ref[:, 0:256]
    gk = gqa_ref[:, 256:384]
    cg, sg = cg_ref[...], sg_ref[...]
    gq = gq * lax.rsqrt(_head_mean_sq(gq, GQA_HEADS, GQA_DH) + EPS) * qn_ref[...]
    gk = gk * lax.rsqrt(_head_mean_sq(gk, GQA_KV, GQA_DH) + EPS) * kn_ref[...]
    o_ref[:, 768:1024] = (_rope(gq, cg, sg, GQA_DH // 4) * (GQA_DH ** -0.5)).astype(BF16)
    o_ref[:, 1024:1152] = _rope(gk, cg[:, 0:128], sg[:, 0:128], GQA_DH // 4).astype(BF16)
    o_ref[:, 1152:1280] = gqa_ref[:, 384:512].astype(BF16)
    cd, sd = cd_ref[...], sd_ref[...]
    o_ref[:, 1280:1536] = (_rope(dq_ref[...], cd, sd, DIFF_DH // 4) * (DIFF_DH ** -0.5)).astype(BF16)
    o_ref[:, 1536:1792] = _rope(dk_ref[...], cd, sd, DIFF_DH // 4).astype(BF16)
    o_ref[:, 1792:2048] = dv_ref[...].astype(BF16)


def _prep(mix, cg, sg, cd, sd, qn, kn):
    b, s, _ = mix.shape
    tm = ROW_TILE
    tab = pl.BlockSpec((tm, 256), lambda bi, i: (i, 0))
    return pl.pallas_call(
        _prep_kernel,
        grid=(b, s // tm),
        in_specs=[pl.BlockSpec((None, tm, 768), lambda bi, i: (bi, i, 0)),
                  pl.BlockSpec((None, tm, 512), lambda bi, i: (bi, i, 3)),
                  pl.BlockSpec((None, tm, 256), lambda bi, i: (bi, i, 8)),
                  pl.BlockSpec((None, tm, 256), lambda bi, i: (bi, i, 9)),
                  pl.BlockSpec((None, tm, 256), lambda bi, i: (bi, i, 10)),
                  tab, tab, tab, tab,
                  pl.BlockSpec((1, 256), lambda bi, i: (0, 0)),
                  pl.BlockSpec((1, 128), lambda bi, i: (0, 0))],
        out_specs=pl.BlockSpec((None, tm, 2048), lambda bi, i: (bi, i, 0)),
        out_shape=jax.ShapeDtypeStruct((b, s, 2048), BF16),
        compiler_params=_cparams(("arbitrary", "arbitrary")),
        name="qk_prep",
    )(mix, mix, mix, mix, mix, cg, sg, cd, sd, qn, kn)


def _rope_tables(n_ctx, n_lat, dh, width):
    qs = dh // 4
    t = np.arange(n_lat)
    rows, cols = (t // GRID_W).astype(np.float64), (t % GRID_W).astype(np.float64)
    lane = np.arange(width) % dh
    part = lane // (2 * qs)
    u = lane % (2 * qs)
    f = u % qs
    lo = u < qs
    freqs = ROPE_THETA ** (-f.astype(np.float64) / qs)
    pos = np.where(part[None, :] == 0, rows[:, None], cols[:, None])
    ang = (pos.astype(np.float32) * freqs.astype(np.float32)[None, :]).astype(np.float32)
    cos = np.cos(ang.astype(np.float64))
    sin = np.sin(ang.astype(np.float64)) * np.where(lo, -1.0, 1.0)[None, :]
    cos = np.concatenate([np.ones((n_ctx, width)), cos], axis=0)
    sin = np.concatenate([np.zeros((n_ctx, width)), sin], axis=0)
    return jnp.asarray(cos, F32), jnp.asarray(sin, F32)


def _softmax_rows(s):
    m = jnp.max(s, axis=-1, keepdims=True)
    p = jnp.exp(s - m)
    return p, jnp.sum(p, axis=-1, keepdims=True)


def _dot_nt(a, b):
    return lax.dot_general(a, b, (((1,), (1,)), ((), ())), preferred_element_type=F32)


def _gqa_kernel(q_ref, k_ref, v_ref, o_ref, *, n_ctx):
    i = pl.program_id(1)

    def run(nk):
        k = k_ref[0:nk, :]
        v = v_ref[0:nk, :]
        q = q_ref[...]
        tq = q.shape[0]
        lane = lax.broadcasted_iota(I32, (tq, LANE), 1)
        grp = _lane_group((tq, LANE), GQA_DH)
        outs = []
        for g in range(GQA_KV):
            mask = grp == g
            ql = jnp.concatenate([jnp.where(mask, q[:, 0:128], 0), jnp.where(mask, q[:, 128:256], 0)], axis=0)
            p, l = _softmax_rows(_dot_nt(ql, k))
            outs.append(jnp.dot(p.astype(BF16), v, preferred_element_type=F32) / l)
        for half in range(2):
            sel = jnp.where(lane < GQA_DH, outs[0][half * tq:(half + 1) * tq], outs[1][half * tq:(half + 1) * tq])
            o_ref[:, half * 128:(half + 1) * 128] = sel.astype(o_ref.dtype)

    @pl.when(i == 0)
    def _():
        run(n_ctx)

    @pl.when(i > 0)
    def _():
        run(k_ref.shape[0])


def _gqa_attention(pb, n_ctx):
    b, s, _ = pb.shape
    tq = ROW_TILE
    return pl.pallas_call(
        functools.partial(_gqa_kernel, n_ctx=n_ctx),
        grid=(b, s // tq),
        in_specs=[pl.BlockSpec((None, tq, 256), lambda bi, i: (bi, i, 3)),
                  pl.BlockSpec((None, s, 128), lambda bi, i: (bi, 0, 8)),
                  pl.BlockSpec((None, s, 128), lambda bi, i: (bi, 0, 9))],
        out_specs=pl.BlockSpec((None, tq, 256), lambda bi, i: (bi, i, 0)),
        out_shape=jax.ShapeDtypeStruct((b, s, 256), BF16),
        compiler_params=_cparams(("arbitrary", "arbitrary")),
        name="gqa_attention",
    )(pb, pb, pb)


def _diff_kernel(q_ref, k_ref, v_ref, par_ref, g_ref, o_ref, *, n_ctx):
    i = pl.program_id(1)

    def run(nk):
        k = k_ref[0:nk, :]
        v = v_ref[0:nk, :]
        q = q_ref[...]
        tq = q.shape[0]
        lam = par_ref[0:1, 0:1]
        comp = _lane_group(q.shape, DIFF_DH)
        head = _lane_group(q.shape, DIFF_DV)
        acc = jnp.zeros(q.shape, F32)
        for hh in range(DIFF_HEADS):
            ql = jnp.concatenate([jnp.where(comp == 2 * hh, q, 0), jnp.where(comp == 2 * hh + 1, q, 0)], axis=0)
            p, l = _softmax_rows(_dot_nt(ql, k))
            p = p / l
            pc = (p[0:tq] - lam * p[tq:2 * tq]).astype(BF16)
            o = jnp.dot(pc, v, preferred_element_type=F32)
            acc = jnp.where(head == hh, o, acc)
        y = acc * lax.rsqrt(_head_mean_sq(acc, DIFF_HEADS, DIFF_DV) + EPS) * g_ref[...]
        o_ref[...] = (y * par_ref[1:2, 0:1]).astype(o_ref.dtype)

    @pl.when(i == 0)
    def _():
        run(n_ctx)

    @pl.when(i > 0)
    def _():
        run(k_ref.shape[0])


def _diff_attention(pb, par, subln, n_ctx):
    b, s, _ = pb.shape
    tq = ROW_TILE
    return pl.pallas_call(
        functools.partial(_diff_kernel, n_ctx=n_ctx),
        grid=(b, s // tq),
        in_specs=[pl.BlockSpec((None, tq, 256), lambda bi, i: (bi, i, 5)),
                  pl.BlockSpec((None, s, 256), lambda bi, i: (bi, 0, 6)),
                  pl.BlockSpec((None, s, 256), lambda bi, i: (bi, 0, 7)),
                  pl.BlockSpec((8, 128), lambda bi, i: (0, 0)),
                  pl.BlockSpec((1, 256), lambda bi, i: (0, 0))],
        out_specs=pl.BlockSpec((None, tq, 256), lambda bi, i: (bi, i, 0)),
        out_shape=jax.ShapeDtypeStruct((b, s, 256), BF16),
        compiler_params=_cparams(("arbitrary", "arbitrary")),
        name="diff_attention",
    )(pb, pb, pb, par, subln)


def _na_kernel(q_ref, k_ref, v_ref, bias_ref, o_ref, *, n_ctx, n_rows):
    i = pl.program_id(1)
    tq = q_ref.shape[0]

    def stack_heads(q):
        head = _lane_group(q.shape, NA_DH)
        return jnp.concatenate([jnp.where(head == hh, q, 0) for hh in range(NA_HEADS)], axis=0)

    def unstack_heads(o, rows):
        head = _lane_group((rows, 256), NA_DH)
        out = jnp.zeros((rows, 256), F32)
        for hh in range(NA_HEADS):
            out = jnp.where(head == hh, o[hh * rows:(hh + 1) * rows], out)
        return out

    @pl.when(i == 0)
    def _():
        kc = k_ref[0:n_ctx, :]
        vc = v_ref[0:n_ctx, :]
        p, l = _softmax_rows(_dot_nt(stack_heads(q_ref[...]), kc))
        o = jnp.dot(p.astype(BF16), vc, preferred_element_type=F32) / l
        o_ref[...] = unstack_heads(o, tq).astype(o_ref.dtype)

    @pl.when(i > 0)
    def _():
        kc = k_ref[0:n_ctx, :]
        vc = v_ref[0:n_ctx, :]
        for rr in range(tq // GRID_W):
            r = (i - 1) * (tq // GRID_W) + rr
            rs = jnp.clip(r - NA_KH // 2, 0, n_rows - NA_KH)
            start = pl.multiple_of(n_ctx + rs * GRID_W, GRID_W)
            kcat = jnp.concatenate([kc, k_ref[pl.ds(start, NA_KH * GRID_W), :]], axis=0)
            vcat = jnp.concatenate([vc, v_ref[pl.ds(start, NA_KH * GRID_W), :]], axis=0)
            q = q_ref[rr * GRID_W:(rr + 1) * GRID_W, :]
            s = _dot_nt(stack_heads(q), kcat) + bias_ref[r - rs]
            p, l = _softmax_rows(s)
            o = jnp.dot(p.astype(BF16), vcat, preferred_element_type=F32) / l
            o_ref[rr * GRID_W:(rr + 1) * GRID_W, :] = unstack_heads(o, GRID_W).astype(o_ref.dtype)


def _na_attention(pb, bias, n_ctx):
    b, s, _ = pb.shape
    tq = ROW_TILE
    n_rows = (s - n_ctx) // GRID_W
    return pl.pallas_call(
        functools.partial(_na_kernel, n_ctx=n_ctx, n_rows=n_rows),
        grid=(b, s // tq),
        in_specs=[pl.BlockSpec((None, tq, 256), lambda bi, i: (bi, i, 0)),
                  pl.BlockSpec((None, s, 256), lambda bi, i: (bi, 0, 1)),
                  pl.BlockSpec((None, s, 256), lambda bi, i: (bi, 0, 2)),
                  pl.BlockSpec(bias.shape, lambda bi, i: (0, 0, 0))],
        out_specs=pl.BlockSpec((None, tq, 256), lambda bi, i: (bi, i, 0)),
        out_shape=jax.ShapeDtypeStruct((b, s, 256), BF16),
        compiler_params=_cparams(("arbitrary", "arbitrary")),
        name="na_attention",
    )(pb, pb, pb, bias)


def _na_bias_table(rpb, n_ctx):
    depth = rpb.shape[0]
    pad = GRID_W - NA_KW
    padded = jnp.pad(rpb.astype(F32), ((0, 0), (0, 0), (0, 0), (pad, pad)))
    cols = jnp.stack([padded[..., GRID_W - 1 - w:2 * GRID_W - 1 - w] for w in range(GRID_W)], axis=3)
    vals = jnp.stack([cols[:, :, NA_KH - 1 - o:2 * NA_KH - 1 - o] for o in range(NA_KH)], axis=2)
    vals = jnp.transpose(vals, (0, 2, 1, 4, 3, 5))
    w = np.arange(GRID_W)[:, None, None]
    kc = np.arange(GRID_W)[None, None, :]
    cs = np.clip(w - NA_KW // 2, 0, GRID_W - NA_KW)
    inwin = np.broadcast_to((kc >= cs) & (kc < cs + NA_KW), (GRID_W, NA_KH, GRID_W))
    vals = jnp.where(jnp.asarray(inwin), vals, NEG).reshape(depth, NA_KH, NA_HEADS * GRID_W, NA_KH * GRID_W)
    return jnp.concatenate([jnp.zeros((depth, NA_KH, NA_HEADS * GRID_W, n_ctx), F32), vals], axis=-1)


def _hy_pre_kernel(u_ref, w_ref, b_ref, x0_ref, z_ref, *, bounds, tm):
    s = u_ref.shape[0]
    w0, w1, w2, bias = w_ref[0:1, :], w_ref[1:2, :], w_ref[2:3, :], b_ref[...]
    rid = lax.broadcasted_iota(I32, (tm, u_ref.shape[1]), 0)
    for c0 in range(0, s, tm):
        u = u_ref[c0:c0 + tm, :]
        prev = pltpu.roll(u, 1, 0)
        nxt = pltpu.roll(u, tm - 1, 0)
        first = jnp.zeros_like(w0) if c0 in bounds else u_ref[c0 - 1:c0, :]
        last = jnp.zeros_like(w0) if (c0 + tm) in bounds else u_ref[c0 + tm:c0 + tm + 1, :]
        prev = jnp.where(rid == 0, first, prev)
        nxt = jnp.where(rid == tm - 1, last, nxt)
        uc = prev * w0 + u * w1 + nxt * w2 + bias
        x0_ref[c0:c0 + tm, :] = uc[:, 0:HY_W]
        z_ref[c0:c0 + tm, :] = uc[:, HY_W:2 * HY_W] * uc[:, 2 * HY_W:3 * HY_W]


def _hy_pre(mix, w_short, b_short, n_ctx):
    b, s, _ = mix.shape
    return pl.pallas_call(
        functools.partial(_hy_pre_kernel, bounds=(0, n_ctx, s), tm=ROW_TILE),
        grid=(b,),
        in_specs=[pl.BlockSpec((None, s, 3 * HY_W), lambda bi: (bi, 0, 1)),
                  pl.BlockSpec((3, 3 * HY_W), lambda bi: (0, 0)),
                  pl.BlockSpec((1, 3 * HY_W), lambda bi: (0, 0))],
        out_specs=[pl.BlockSpec((None, s, HY_W), lambda bi: (bi, 0, 0)),
                   pl.BlockSpec((None, s, HY_W), lambda bi: (bi, 0, 0))],
        out_shape=[jax.ShapeDtypeStruct((b, s, HY_W), F32), jax.ShapeDtypeStruct((b, s, HY_W), F32)],
        compiler_params=_cparams(("arbitrary",)),
        name="hyena_pre",
    )(mix, w_short, b_short)


def _hy_filter_kernel(z_ref, w1_ref, b1_ref, w2_ref, b2_ref, w3_ref, b3_ref, w4_ref, f_ref, dec_ref, o_ref):
    def lin(a, w_ref, b_ref):
        return jnp.dot(a, w_ref[...], preferred_element_type=F32, precision=HIGHEST) + b_ref[...]

    hh = jnp.sin(f_ref[0:1, :] * lin(z_ref[...], w1_ref, b1_ref))
    hh = jnp.sin(f_ref[1:2, :] * lin(hh, w2_ref, b2_ref))
    hh = jnp.sin(f_ref[2:3, :] * lin(hh, w3_ref, b3_ref))
    hh = jnp.dot(hh, w4_ref[...], preferred_element_type=F32, precision=HIGHEST) * dec_ref[...]
    o_ref[...] = hh / jnp.sum(jnp.abs(hh), axis=0, keepdims=True)


def _pad_to(a, shape):
    return jnp.pad(a, [(0, t - s) for s, t in zip(a.shape, shape)])


def _hy_filters(n, w1, b1, w2, b2, w3, b3, w4, freq):
    depth = w1.shape[0]
    t = np.linspace(0.0, 1.0, n, dtype=np.float32)[:, None]
    w = np.float32(2.0 * math.pi / n) * np.arange(n, dtype=np.float32)[:, None]
    bands = np.linspace(1e-4, HY_BANDS - 1, HY_BANDS, dtype=np.float32)[None, :]
    z = np.concatenate([t, np.cos(w * bands), np.sin(w * bands)], axis=-1).astype(np.float32)
    z = np.pad(z, ((0, 0), (0, LANE - z.shape[1])))
    deltas = np.linspace(math.log(HY_TARGET) / HY_SLOW_DECAY, math.log(HY_TARGET) / HY_FAST_DECAY, HY_W, dtype=np.float32)
    deltas = np.tile(np.abs(deltas), 2)
    decay = np.exp(-t * deltas[None, :]).astype(np.float32)
    p = LANE
    args = (jnp.asarray(z), _pad_to(w1, (depth, p, p)), _pad_to(b1[:, None, :], (depth, 1, p)),
            _pad_to(w2, (depth, p, p)), _pad_to(b2[:, None, :], (depth, 1, p)),
            _pad_to(w3, (depth, p, p)), _pad_to(b3[:, None, :], (depth, 1, p)),
            _pad_to(w4, (depth, p, 2 * HY_W)), _pad_to(freq, (depth, 8, p)), jnp.asarray(decay))
    per_layer = lambda shp: pl.BlockSpec((None,) + shp, lambda l: (l,) + (0,) * len(shp))
    const = lambda shp: pl.BlockSpec(shp, lambda l: (0,) * len(shp))
    return pl.pallas_call(
        _hy_filter_kernel,
        grid=(depth,),
        in_specs=[const((n, p)), per_layer((p, p)), per_layer((1, p)), per_layer((p, p)), per_layer((1, p)),
                  per_layer((p, p)), per_layer((1, p)), per_layer((p, 2 * HY_W)), per_layer((8, p)),
                  const((n, 2 * HY_W))],
        out_specs=per_layer((n, 2 * HY_W)),
        out_shape=jax.ShapeDtypeStruct((depth, n, 2 * HY_W), F32),
        compiler_params=_cparams(("arbitrary",)),
        name="hyena_filters",
    )(*args)


def _conv_taps(filt):
    hf, hb = filt[..., :HY_W], filt[..., HY_W:]
    taps = jnp.concatenate([hb[:, :0:-1], (hf[:, 0:1] + hb[:, 0:1]), hf[:, 1:], jnp.zeros_like(hf[:, 0:1])], axis=1)
    return jnp.transpose(taps, (0, 2, 1))


def _hy_conv_kernel(z_ref, k_ref, o_ref, *, nblk, nb):
    cg = z_ref.shape[0]
    n2 = k_ref.shape[1]
    ncols = (2 * nblk - 1) * LANE

    def body(ci, _):
        krow = k_ref[pl.ds(ci, 1), :]
        kb = jnp.broadcast_to(krow, (LANE, n2))
        big = pltpu.roll(kb, n2 - (LANE - 1), 1, stride=1, stride_axis=0)[:, :ncols].astype(BF16)
        o_ref[ci] = jnp.zeros(o_ref.shape[1:], F32)
        for m in range(-(nblk - 1), nblk):
            km = big[:, (m + nblk - 1) * LANE:(m + nblk) * LANE]
            cnt = (nblk - abs(m)) * nb
            src = 0 if m >= 0 else -m * nb
            dst = m * nb if m >= 0 else 0
            zin = z_ref[ci, src:src + cnt, :]
            o_ref[ci, dst:dst + cnt, :] += jnp.dot(zin, km, preferred_element_type=F32)
        return 0

    lax.fori_loop(0, cg, body, 0)


def _hy_conv(zr, taps, nblk, nb):
    c, r, _ = zr.shape
    cg = 8
    return pl.pallas_call(
        functools.partial(_hy_conv_kernel, nblk=nblk, nb=nb),
        grid=(c // cg,),
        in_specs=[pl.BlockSpec((cg, r, LANE), lambda i: (i, 0, 0)),
                  pl.BlockSpec((cg, taps.shape[1]), lambda i: (i, 0))],
        out_specs=pl.BlockSpec((cg, r, LANE), lambda i: (i, 0, 0)),
        out_shape=jax.ShapeDtypeStruct((c, r, LANE), F32),
        compiler_params=_cparams(("arbitrary",)),
        name="hyena_conv",
    )(zr, taps)


def _hy_long_conv(z, taps):
    b, n, c = z.shape
    nblk = n // LANE
    zr = jnp.transpose(z.astype(BF16).reshape(b, nblk, LANE, c), (3, 1, 0, 2)).reshape(c, nblk * b, LANE)
    y = _hy_conv(zr, taps, nblk, b)
    return jnp.transpose(y.reshape(c, nblk, b, LANE), (2, 1, 3, 0)).reshape(b, n, c)


def _merge_kernel(h_ref, mod_ref, ya_ref, x0_ref, z_ref, yconv_ref, hb_ref, yc_ref, yd_ref, gate_ref,
                  wb_ref, wo_ref, o_ref, *, n_ctx, tm):
    i = pl.program_id(1)
    yb = (x0_ref[...] * (yconv_ref[...] + z_ref[...] * hb_ref[...])).astype(BF16)
    ys = (ya_ref[...], yb, yc_ref[...], yd_ref[...])
    acc = jnp.zeros(h_ref.shape, F32)
    d = h_ref.shape[1]
    for k in range(N_BRANCH):
        proj = jnp.dot(ys[k], wb_ref[k], preferred_element_type=F32)
        acc = acc + jax.nn.sigmoid(gate_ref[:, k * d:(k + 1) * d]) * proj
    out = jnp.dot(acc.astype(BF16), wo_ref[...], preferred_element_type=F32)
    row = i * tm + lax.broadcasted_iota(I32, out.shape, 0)
    gate = jnp.where(row < n_ctx, mod_ref[0, 2:3, :], mod_ref[1, 2:3, :])
    o_ref[...] = h_ref[...] + gate * out


def _merge(h, modl, ya, x0, z, yconv, hbias, yc, yd, gates, wb, wo, n_ctx):
    b, s, d = h.shape
    tm = ROW_TILE
    row = lambda w: pl.BlockSpec((None, tm, w), lambda bi, i: (bi, i, 0))
    return pl.pallas_call(
        functools.partial(_merge_kernel, n_ctx=n_ctx, tm=tm),
        grid=(b, s // tm),
        in_specs=[row(d), pl.BlockSpec((None, 2, 8, d), lambda bi, i: (bi, 0, 0, 0)),
                  row(256), row(256), row(256), row(256), pl.BlockSpec((1, 256), lambda bi, i: (0, 0)),
                  row(256), row(256), row(GATE_W),
                  pl.BlockSpec(wb.shape, lambda bi, i: (0, 0, 0)),
                  pl.BlockSpec(wo.shape, lambda bi, i: (0, 0))],
        out_specs=row(d),
        out_shape=jax.ShapeDtypeStruct((b, s, d), F32),
        compiler_params=_cparams(("arbitrary", "arbitrary")),
        name="merge",
    )(h, modl, ya, x0, z, yconv, hbias, yc, yd, gates, wb, wo)


def _topk_rows(s, label, k):
    vals, labs = [], []
    for _ in range(k):
        m = jnp.max(s, axis=0, keepdims=True)
        lb = jnp.min(jnp.where(s == m, label, float(2 ** 20)), axis=0, keepdims=True)
        vals.append(m)
        labs.append(lb)
        s = jnp.where(label == lb, -jnp.inf, s)
    return jnp.concatenate(vals, axis=0), jnp.concatenate(labs, axis=0).astype(I32)


_CAND_FIXED_A = ((0, 0), (0, 8), (1, 0), (2, 0), (3, 0))
_CAND_FIXED_B = ((0, 0), (0, 8), (1, 0), (2, 0))


def _select_rows(table, pos):
    out = jnp.zeros(pos.shape, table.dtype)
    for a in range(table.shape[0]):
        out = jnp.where(pos == a, table[a:a + 1, :], out)
    return out


def _peer_route_kernel(q_ref, keys_ref, i_ref, j_ref, g_ref):
    st = _dot_nt(keys_ref[...], q_ref[...].astype(BF16))
    t = st.shape[1]
    key_id = lax.broadcasted_iota(I32, (PEER_NKEYS, t), 0).astype(F32)
    sv1, si1 = _topk_rows(st[0:PEER_NKEYS], key_id, PEER_TOPK)
    sv2, si2 = _topk_rows(st[PEER_NKEYS:2 * PEER_NKEYS], key_id, PEER_TOPK)
    r8 = lax.broadcasted_iota(I32, (8, t), 0)
    cands, labels = [], []
    for a, b0 in _CAND_FIXED_A:
        cands.append(sv1[a:a + 1, :] + sv2[b0:b0 + 8, :])
        labels.append((a * PEER_TOPK + b0 + r8).astype(F32))
    for b, a0 in _CAND_FIXED_B:
        c = sv1[a0:a0 + 8, :] + sv2[b:b + 1, :]
        cands.append(jnp.where(r8 < 4, -jnp.inf, c) if a0 == 0 else c)
        labels.append(((a0 + r8) * PEER_TOPK + b).astype(F32))
    best, pos = _topk_rows(jnp.concatenate(cands, axis=0), jnp.concatenate(labels, axis=0), PEER_TOPK)
    i_ref[...] = _select_rows(si1, pos >> int(math.log2(PEER_TOPK)))
    j_ref[...] = _select_rows(si2, pos & (PEER_TOPK - 1))
    e = jnp.exp(best - jnp.max(best, axis=0, keepdims=True))
    g_ref[...] = e / jnp.sum(e, axis=0, keepdims=True)


def _peer_route(q, keys_blk):
    t, _ = q.shape
    tt = 256
    out = pl.BlockSpec((None, PEER_TOPK, tt), lambda ti, hh: (hh, 0, ti))
    shp = (PEER_HEADS, PEER_TOPK, t)
    return pl.pallas_call(
        _peer_route_kernel,
        grid=(t // tt, PEER_HEADS),
        in_specs=[pl.BlockSpec((tt, PEER_DK), lambda ti, hh: (ti, hh)),
                  pl.BlockSpec((None, 2 * PEER_NKEYS, PEER_DK), lambda ti, hh: (hh, 0, 0))],
        out_specs=[out, out, out],
        out_shape=[jax.ShapeDtypeStruct(shp, I32), jax.ShapeDtypeStruct(shp, I32), jax.ShapeDtypeStruct(shp, F32)],
        compiler_params=_cparams(("arbitrary", "arbitrary")),
        name="peer_route",
    )(q, keys_blk)


GS_PITCH = PEER_NKEYS + 1
GS_UNROLL = 8


def _peer_expert_kernel(h_ref, mod_ref, x_ref, i_ref, j_ref, g_ref, ut_ref, v_ref, o_ref, gs_ref, acc_ref,
                        *, n_ctx, tm):
    ti, e = pl.program_id(1), pl.program_id(2)
    n_e = pl.num_programs(2)
    te = ut_ref.shape[1]
    nk = PEER_NKEYS

    @pl.when(e == 0)
    def _():
        sub = lax.broadcasted_iota(I32, (nk, nk), 0)

        def build(t):
            irow = i_ref[pl.ds(t, 1), :]
            jrow = j_ref[pl.ds(t, 1), :]
            grow = g_ref[pl.ds(t, 1), :]
            at = jnp.where(sub == irow, 1.0, 0.0).astype(BF16)
            bt = jnp.where(sub == jrow, grow, 0.0)
            bhi = bt.astype(BF16)
            blo = (bt - bhi.astype(F32)).astype(BF16)
            gt = _dot_nt(jnp.concatenate([at, at], axis=1), jnp.concatenate([bhi, blo], axis=1))
            gs_ref[pl.ds(t * GS_PITCH, nk), :] = gt

        def body(tb, _):
            for u in range(GS_UNROLL):
                build(tb * GS_UNROLL + u)
            return 0

        lax.fori_loop(0, tm // GS_UNROLL, body, 0)

    hid = jnp.dot(x_ref[...], ut_ref[...], preferred_element_type=F32)
    cols = []
    for ii in range(te // nk):
        gi = gs_ref[pl.ds(e * (te // nk) + ii, tm, stride=GS_PITCH), :]
        a = hid[:, ii * nk:(ii + 1) * nk]
        act = 0.5 * a * (1.0 + lax.erf(a * (2.0 ** -0.5)))
        cols.append((gi * act).astype(BF16))
    contrib = jnp.dot(jnp.concatenate(cols, axis=1), v_ref[...], preferred_element_type=F32)

    @pl.when(e == 0)
    def _():
        acc_ref[...] = contrib

    @pl.when(e > 0)
    def _():
        acc_ref[...] += contrib

    @pl.when(e == n_e - 1)
    def _():
        row = ti * tm + lax.broadcasted_iota(I32, acc_ref.shape, 0)
        gate = jnp.where(row < n_ctx, mod_ref[0, 5:6, :], mod_ref[1, 5:6, :])
        o_ref[...] = h_ref[...] + gate * acc_ref[...]


def _peer_experts(h, modl, xb, i_idx, j_idx, g, ut, v, n_ctx):
    b, s, d = h.shape
    tm, te = ROW_TILE, 2048
    n_exp = ut.shape[1]
    slots = PEER_HEADS * PEER_TOPK
    row = lambda w: pl.BlockSpec((None, tm, w), lambda bi, i, e: (bi, i, 0))
    return pl.pallas_call(
        functools.partial(_peer_expert_kernel, n_ctx=n_ctx, tm=tm),
        grid=(b, s // tm, n_exp // te),
        in_specs=[row(d), pl.BlockSpec((None, 2, 8, d), lambda bi, i, e: (bi, 0, 0, 0)),
                  row(d), row(slots), row(slots), row(slots),
                  pl.BlockSpec((d, te), lambda bi, i, e: (0, e)),
                  pl.BlockSpec((te, d), lambda bi, i, e: (e, 0))],
        out_specs=row(d),
        out_shape=jax.ShapeDtypeStruct((b, s, d), F32),
        scratch_shapes=[pltpu.VMEM((tm * GS_PITCH, PEER_NKEYS), F32), pltpu.VMEM((tm, d), F32)],
        compiler_params=_cparams(("arbitrary", "arbitrary", "arbitrary")),
        name="peer_experts",
    )(h, modl, xb, i_idx, j_idx, g, ut, v)


def _final_norm_kernel(x_ref, g_ref, o_ref):
    x = x_ref[...]
    o_ref[...] = x * lax.rsqrt(jnp.mean(x * x, axis=-1, keepdims=True) + EPS) * g_ref[...]


def _final_norm(h, g, n_ctx):
    b, s, d = h.shape
    tm = ROW_TILE
    n_lat = s - n_ctx
    off = n_ctx // tm
    return pl.pallas_call(
        _final_norm_kernel,
        grid=(b, n_lat // tm),
        in_specs=[pl.BlockSpec((None, tm, d), lambda bi, i: (bi, i + off, 0)),
                  pl.BlockSpec((1, d), lambda bi, i: (0, 0))],
        out_specs=pl.BlockSpec((None, tm, d), lambda bi, i: (bi, i, 0)),
        out_shape=jax.ShapeDtypeStruct((b, n_lat, d), F32),
        compiler_params=_cparams(("arbitrary", "arbitrary")),
        name="final_norm",
    )(h, g)


def _prep_w_in(w_in):
    depth, d, _ = w_in.shape
    mixw = w_in[:, :, :2816]
    gq = mixw[:, :, 1536:1792].reshape(depth, d, GQA_HEADS, GQA_DH)[:, :, (0, 2, 1, 3), :].reshape(depth, d, 256)
    mixw = jnp.concatenate([mixw[:, :, :1536], gq, mixw[:, :, 1792:], jnp.zeros((depth, d, MIX_W - 2816), w_in.dtype)], axis=-1)
    return jnp.concatenate([mixw, w_in[:, :, 2816:]], axis=-1).astype(BF16)


def _prep_peer_keys(keys):
    depth = keys.shape[0]
    half = PEER_DK // 2
    z = jnp.zeros((depth, PEER_HEADS, PEER_NKEYS, half), keys.dtype)
    top = jnp.concatenate([keys[:, :, 0], z], axis=-1)
    bot = jnp.concatenate([z, keys[:, :, 1]], axis=-1)
    return jnp.concatenate([top, bot], axis=2).astype(BF16)


def kernel(x, c, ctx, c_ctx, w_mod, b_mod, norm1_g, norm2_g, w_in, na_rpb, hy_short_w, hy_short_b, hy_w1, hy_b1, hy_w2, hy_b2, hy_w3, hy_b3, hy_w4, hy_freq, hy_bias, gqa_qn, gqa_kn, diff_lq1, diff_lk1, diff_lq2, diff_lk2, diff_subln, w_branch, w_out, peer_wq, peer_keys, peer_u, peer_v, final_g):
    B, L, D = x.shape
    C = ctx.shape[1]
    S = C + L
    depth = w_mod.shape[0]
    assert C == ROW_TILE and L % ROW_TILE == 0 and D == 1024 and L % GRID_W == 0

    h = jnp.concatenate([ctx, x], axis=1)

    r = -(-(B + 1) // 8) * 8
    cvec = jnp.zeros((r, D), F32).at[:B].set(c).at[B].set(c_ctx)
    modall = _modulation(cvec, w_mod, b_mod)
    mod_lat = modall[:, :B].reshape(depth, B, 1, 6, D)
    mod_ctx = jnp.broadcast_to(modall[:, B].reshape(depth, 1, 1, 6, D), (depth, B, 1, 6, D))
    mods = jnp.pad(jnp.concatenate([mod_ctx, mod_lat], axis=2), ((0, 0), (0, 0), (0, 0), (0, 2), (0, 0)))

    fargs = (hy_w1, hy_b1, hy_w2, hy_b2, hy_w3, hy_b3, hy_w4, hy_freq)
    taps_lat = _conv_taps(_hy_filters(L, *fargs))
    taps_ctx = _conv_taps(_hy_filters(C, *fargs))

    cg, sg = _rope_tables(C, L, GQA_DH, 256)
    cd, sd = _rope_tables(C, L, DIFF_DH, 256)

    lam_init = jnp.asarray([0.8 - 0.6 * math.exp(-0.3 * l) for l in range(depth)], F32)
    lam = (jnp.exp(jnp.sum(diff_lq1.astype(F32) * diff_lk1.astype(F32), axis=-1))
           - jnp.exp(jnp.sum(diff_lq2.astype(F32) * diff_lk2.astype(F32), axis=-1)) + lam_init)
    diff_par = jnp.zeros((depth, 8, LANE), F32).at[:, 0, :].set(lam[:, None]).at[:, 1, :].set(1.0 - lam_init[:, None])

    wb = w_branch.astype(BF16)
    wb = wb.at[:, 2].set(wb[:, 2].reshape(depth, GQA_HEADS, GQA_DH, D)[:, (0, 2, 1, 3)].reshape(depth, BRANCH_W, D))

    xs = dict(
        mods=mods, n1=norm1_g[:, None, :], n2=norm2_g[:, None, :], w_in=_prep_w_in(w_in),
        bias=_na_bias_table(na_rpb, C), sw=hy_short_w, sb=hy_short_b[:, None, :], hbias=hy_bias[:, None, :],
        taps_lat=taps_lat, taps_ctx=taps_ctx,
        qn=jnp.tile(gqa_qn, (1, GQA_HEADS))[:, None, :], kn=jnp.tile(gqa_kn, (1, GQA_KV))[:, None, :],
        diff_par=diff_par, subln=jnp.tile(diff_subln, (1, DIFF_HEADS))[:, None, :],
        wb=wb, wo=w_out.astype(BF16), wq=peer_wq.astype(BF16), keys=_prep_peer_keys(peer_keys),
        u=peer_u, v=peer_v,
    )

    def layer(h, p):
        mix, gates = _in_proj(h, p["mods"], p["n1"], p["w_in"], C)
        pb = _prep(mix, cg, sg, cd, sd, p["qn"], p["kn"])
        ya = _na_attention(pb, p["bias"], C)
        yc = _gqa_attention(pb, C)
        yd = _diff_attention(pb, p["diff_par"], p["subln"], C)
        x0, z = _hy_pre(mix, p["sw"], p["sb"], C)
        yconv = jnp.concatenate([_hy_long_conv(z[:, :C], p["taps_ctx"]), _hy_long_conv(z[:, C:], p["taps_lat"])], axis=1)
        h = _merge(h, p["mods"], ya, x0, z, yconv, p["hbias"], yc, yd, gates, p["wb"], p["wo"], C)

        q, xb = _peer_q(h, p["mods"], p["n2"], p["wq"], C)
        i_idx, j_idx, g = _peer_route(q.reshape(B * S, -1), p["keys"])
        slots = PEER_HEADS * PEER_TOPK
        to_rows = lambda a: jnp.transpose(a, (2, 0, 1)).reshape(B, S, slots)
        ut = jnp.transpose(p["u"].astype(BF16))
        h = _peer_experts(h, p["mods"], xb, to_rows(i_idx), to_rows(j_idx), to_rows(g), ut, p["v"].astype(BF16), C)
        return h, None

    h, _ = lax.scan(layer, h, xs)
    return _final_norm(h, final_g[None, :], C)
```

```python
import functools
import math

import numpy as np
import jax
import jax.numpy as jnp
from jax import lax
from jax.experimental import pallas as pl
from jax.experimental.pallas import tpu as pltpu

F32 = jnp.float32
BF16 = jnp.bfloat16
I32 = jnp.int32
HIGHEST = lax.Precision.HIGHEST

EPS = 1e-6
GRID_W = 64
ROPE_THETA = 10000.0
NA_HEADS, NA_DH, NA_KH, NA_KW = 4, 64, 8, 16
HY_W, HY_BANDS, HY_FFN = 256, 16, 64
HY_FAST_DECAY, HY_SLOW_DECAY, HY_TARGET = 0.3, 1.5, 1e-2
GQA_HEADS, GQA_KV, GQA_DH = 4, 2, 64
DIFF_HEADS, DIFF_DH, DIFF_DV = 4, 32, 64
N_BRANCH, BRANCH_W = 4, 256
PEER_HEADS, PEER_NKEYS, PEER_DK, PEER_TOPK = 8, 128, 128, 16

LANE = 128
ROW_TILE = 256
VMEM_LIMIT = 56 * 1024 * 1024
NEG = -1e30
MIX_W = 3072
GATE_W = N_BRANCH * 1024
COL_TILE = 1024


def _cparams(sem):
    return pltpu.CompilerParams(dimension_semantics=sem, vmem_limit_bytes=VMEM_LIMIT)


def _mod_kernel(c_ref, w_ref, b_ref, o_ref):
    cv = c_ref[...]
    a = cv * jax.nn.sigmoid(cv)
    o_ref[...] = jnp.dot(a, w_ref[...], preferred_element_type=F32, precision=HIGHEST) + b_ref[...]


def _modulation(cvec, w_mod, b_mod):
    depth, d, n = w_mod.shape
    r = cvec.shape[0]
    tn = 1024
    return pl.pallas_call(
        _mod_kernel,
        grid=(depth, n // tn),
        in_specs=[pl.BlockSpec((r, d), lambda l, j: (0, 0)),
                  pl.BlockSpec((None, d, tn), lambda l, j: (l, 0, j)),
                  pl.BlockSpec((None, 1, tn), lambda l, j: (l, 0, j))],
        out_specs=pl.BlockSpec((None, r, tn), lambda l, j: (l, 0, j)),
        out_shape=jax.ShapeDtypeStruct((depth, r, n), F32),
        compiler_params=_cparams(("arbitrary", "arbitrary")),
        name="modulation",
    )(cvec, w_mod, b_mod.reshape(depth, 1, n))


def _norm_modulate(h_ref, mod_ref, g_ref, srow, row0, n_ctx):
    x = h_ref[...]
    y = x * lax.rsqrt(jnp.mean(x * x, axis=-1, keepdims=True) + EPS) * g_ref[...]
    row = row0 + lax.broadcasted_iota(I32, x.shape, 0)
    is_ctx = row < n_ctx
    shift = jnp.where(is_ctx, mod_ref[0, srow:srow + 1, :], mod_ref[1, srow:srow + 1, :])
    scale = jnp.where(is_ctx, mod_ref[0, srow + 1:srow + 2, :], mod_ref[1, srow + 1:srow + 2, :])
    return y * (1.0 + scale) + shift


def _in_proj_kernel(h_ref, mod_ref, g_ref, w_ref, mix_ref, gate_ref, nb_ref, *, tm, n_ctx, n_mix):
    i, j = pl.program_id(1), pl.program_id(2)

    @pl.when(j == 0)
    def _():
        nb_ref[...] = _norm_modulate(h_ref, mod_ref, g_ref, 0, i * tm, n_ctx).astype(BF16)

    res = jnp.dot(nb_ref[...], w_ref[...], preferred_element_type=F32)

    @pl.when(j < n_mix)
    def _():
        mix_ref[...] = res

    @pl.when(j >= n_mix)
    def _():
        gate_ref[...] = res


def _in_proj(h, modl, g, w, n_ctx):
    b, s, d = h.shape
    tm, tn = 768, COL_TILE
    n_mix = MIX_W // tn
    n_tot = (MIX_W + GATE_W) // tn
    return pl.pallas_call(
        functools.partial(_in_proj_kernel, tm=tm, n_ctx=n_ctx, n_mix=n_mix),
        grid=(b, s // tm, n_tot),
        in_specs=[pl.BlockSpec((None, tm, d), lambda bi, i, j: (bi, i, 0)),
                  pl.BlockSpec((None, 2, 8, d), lambda bi, i, j: (bi, 0, 0, 0)),
                  pl.BlockSpec((1, d), lambda bi, i, j: (0, 0)),
                  pl.BlockSpec((d, tn), lambda bi, i, j: (0, j))],
        out_specs=[pl.BlockSpec((None, tm, tn), lambda bi, i, j: (bi, i, jnp.minimum(j, n_mix - 1))),
                   pl.BlockSpec((None, tm, tn), lambda bi, i, j: (bi, i, jnp.maximum(j - n_mix, 0)))],
        out_shape=[jax.ShapeDtypeStruct((b, s, MIX_W), F32), jax.ShapeDtypeStruct((b, s, GATE_W), F32)],
        scratch_shapes=[pltpu.VMEM((tm, d), BF16)],
        compiler_params=_cparams(("arbitrary", "arbitrary", "arbitrary")),
        name="in_proj",
    )(h, modl, g, w)


def _peer_q_kernel(h_ref, mod_ref, g_ref, w_ref, q_ref, nb_ref, *, tm, n_ctx):
    i = pl.program_id(1)
    nb = _norm_modulate(h_ref, mod_ref, g_ref, 3, i * tm, n_ctx).astype(BF16)
    nb_ref[...] = nb
    q_ref[...] = jnp.dot(nb, w_ref[...], preferred_element_type=F32)


def _peer_q(h, modl, g, wq, n_ctx):
    b, s, d = h.shape
    tm = 768
    n = wq.shape[1]
    return pl.pallas_call(
        functools.partial(_peer_q_kernel, tm=tm, n_ctx=n_ctx),
        grid=(b, s // tm),
        in_specs=[pl.BlockSpec((None, tm, d), lambda bi, i: (bi, i, 0)),
                  pl.BlockSpec((None, 2, 8, d), lambda bi, i: (bi, 0, 0, 0)),
                  pl.BlockSpec((1, d), lambda bi, i: (0, 0)),
                  pl.BlockSpec((d, n), lambda bi, i: (0, 0))],
        out_specs=[pl.BlockSpec((None, tm, n), lambda bi, i: (bi, i, 0)),
                   pl.BlockSpec((None, tm, d), lambda bi, i: (bi, i, 0))],
        out_shape=[jax.ShapeDtypeStruct((b, s, n), F32), jax.ShapeDtypeStruct((b, s, d), BF16)],
        compiler_params=_cparams(("arbitrary", "arbitrary")),
        name="peer_q",
    )(h, modl, g, wq)


def _lane_group(shape, width):
    return lax.broadcasted_iota(I32, shape, 1) >> int(math.log2(width))


def _head_mean_sq(x, n_heads, dh):
    hid = _lane_group(x.shape, dh)
    x2 = x * x
    ms = jnp.zeros_like(x)
    for hh in range(n_heads):
        m = hid == hh
        s = jnp.sum(jnp.where(m, x2, 0.0), axis=-1, keepdims=True)
        ms = jnp.where(m, s, ms)
    return ms * (1.0 / dh)


def _rope(x, cos, sin_signed, qs):
    w = x.shape[-1]
    lane = lax.broadcasted_iota(I32, x.shape, 1)
    lo = (lane & (2 * qs - 1)) < qs
    partner = jnp.where(lo, pltpu.roll(x, w - qs, 1), pltpu.roll(x, qs, 1))
    return x * cos + partner * sin_signed


def _prep_kernel(na_ref, gqa_ref, dq_ref, dk_ref, dv_ref, cg_ref, sg_ref, cd_ref, sd_ref, qn_ref, kn_ref, o_ref):
    na = na_ref[...]
    o_ref[:, 0:256] = (na[:, 0:256] * (NA_DH ** -0.5)).astype(BF16)
    o_ref[:, 256:768] = na[:, 256:768].astype(BF16)
    gq = gqa_ref[:, 0:256]
    gk = gqa_ref[:, 256:384]
    cg, sg = cg_ref[...], sg_ref[...]
    gq = gq * lax.rsqrt(_head_mean_sq(gq, GQA_HEADS, GQA_DH) + EPS) * qn_ref[...]
    gk = gk * lax.rsqrt(_head_mean_sq(gk, GQA_KV, GQA_DH) + EPS) * kn_ref[...]
    o_ref[:, 768:1024] = (_rope(gq, cg, sg, GQA_DH // 4) * (GQA_DH ** -0.5)).astype(BF16)
    o_ref[:, 1024:1152] = _rope(gk, cg[:, 0:128], sg[:, 0:128], GQA_DH // 4).astype(BF16)
    o_ref[:, 1152:1280] = gqa_ref[:, 384:512].astype(BF16)
    cd, sd = cd_ref[...], sd_ref[...]
    o_ref[:, 1280:1536] = (_rope(dq_ref[...], cd, sd, DIFF_DH // 4) * (DIFF_DH ** -0.5)).astype(BF16)
    o_ref[:, 1536:1792] = _rope(dk_ref[...], cd, sd, DIFF_DH // 4).astype(BF16)
    o_ref[:, 1792:2048] = dv_ref[...].astype(BF16)


def _prep(mix, cg, sg, cd, sd, qn, kn):
    b, s, _ = mix.shape
    tm = ROW_TILE
    tab = pl.BlockSpec((tm, 256), lambda bi, i: (i, 0))
    return pl.pallas_call(
        _prep_kernel,
        grid=(b, s // tm),
        in_specs=[pl.BlockSpec((None, tm, 768), lambda bi, i: (bi, i, 0)),
                  pl.BlockSpec((None, tm, 512), lambda bi, i: (bi, i, 3)),
                  pl.BlockSpec((None, tm, 256), lambda bi, i: (bi, i, 8)),
                  pl.BlockSpec((None, tm, 256), lambda bi, i: (bi, i, 9)),
                  pl.BlockSpec((None, tm, 256), lambda bi, i: (bi, i, 10)),
                  tab, tab, tab, tab,
                  pl.BlockSpec((1, 256), lambda bi, i: (0, 0)),
                  pl.BlockSpec((1, 128), lambda bi, i: (0, 0))],
        out_specs=pl.BlockSpec((None, tm, 2048), lambda bi, i: (bi, i, 0)),
        out_shape=jax.ShapeDtypeStruct((b, s, 2048), BF16),
        compiler_params=_cparams(("arbitrary", "arbitrary")),
        name="qk_prep",
    )(mix, mix, mix, mix, mix, cg, sg, cd, sd, qn, kn)


def _rope_tables(n_ctx, n_lat, dh, width):
    qs = dh // 4
    t = np.arange(n_lat)
    rows, cols = (t // GRID_W).astype(np.float64), (t % GRID_W).astype(np.float64)
    lane = np.arange(width) % dh
    part = lane // (2 * qs)
    u = lane % (2 * qs)
    f = u % qs
    lo = u < qs
    freqs = ROPE_THETA ** (-f.astype(np.float64) / qs)
    pos = np.where(part[None, :] == 0, rows[:, None], cols[:, None])
    ang = (pos.astype(np.float32) * freqs.astype(np.float32)[None, :]).astype(np.float32)
    cos = np.cos(ang.astype(np.float64))
    sin = np.sin(ang.astype(np.float64)) * np.where(lo, -1.0, 1.0)[None, :]
    cos = np.concatenate([np.ones((n_ctx, width)), cos], axis=0)
    sin = np.concatenate([np.zeros((n_ctx, width)), sin], axis=0)
    return jnp.asarray(cos, F32), jnp.asarray(sin, F32)


def _softmax_rows(s):
    m = jnp.max(s, axis=-1, keepdims=True)
    p = jnp.exp(s - m)
    return p, jnp.sum(p, axis=-1, keepdims=True)


def _dot_nt(a, b):
    return lax.dot_general(a, b, (((1,), (1,)), ((), ())), preferred_element_type=F32)


def _gqa_kernel(q_ref, k_ref, v_ref, o_ref, *, n_ctx):
    i = pl.program_id(1)

    def run(nk):
        k = k_ref[0:nk, :]
        v = v_ref[0:nk, :]
        q = q_ref[...]
        tq = q.shape[0]
        lane = lax.broadcasted_iota(I32, (tq, LANE), 1)
        grp = _lane_group((tq, LANE), GQA_DH)
        outs = []
        for g in range(GQA_KV):
            mask = grp == g
            ql = jnp.concatenate([jnp.where(mask, q[:, 0:128], 0), jnp.where(mask, q[:, 128:256], 0)], axis=0)
            p, l = _softmax_rows(_dot_nt(ql, k))
            outs.append(jnp.dot(p.astype(BF16), v, preferred_element_type=F32) / l)
        for half in range(2):
            sel = jnp.where(lane < GQA_DH, outs[0][half * tq:(half + 1) * tq], outs[1][half * tq:(half + 1) * tq])
            o_ref[:, half * 128:(half + 1) * 128] = sel.astype(o_ref.dtype)

    @pl.when(i == 0)
    def _():
        run(n_ctx)

    @pl.when(i > 0)
    def _():
        run(k_ref.shape[0])


def _gqa_attention(pb, n_ctx):
    b, s, _ = pb.shape
    tq = ROW_TILE
    return pl.pallas_call(
        functools.partial(_gqa_kernel, n_ctx=n_ctx),
        grid=(b, s // tq),
        in_specs=[pl.BlockSpec((None, tq, 256), lambda bi, i: (bi, i, 3)),
                  pl.BlockSpec((None, s, 128), lambda bi, i: (bi, 0, 8)),
                  pl.BlockSpec((None, s, 128), lambda bi, i: (bi, 0, 9))],
        out_specs=pl.BlockSpec((None, tq, 256), lambda bi, i: (bi, i, 0)),
        out_shape=jax.ShapeDtypeStruct((b, s, 256), BF16),
        compiler_params=_cparams(("arbitrary", "arbitrary")),
        name="gqa_attention",
    )(pb, pb, pb)


def _diff_kernel(q_ref, k_ref, v_ref, par_ref, g_ref, o_ref, *, n_ctx):
    i = pl.program_id(1)

    def run(nk):
        k = k_ref[0:nk, :]
        v = v_ref[0:nk, :]
        q = q_ref[...]
        tq = q.shape[0]
        lam = par_ref[0:1, 0:1]
        comp = _lane_group(q.shape, DIFF_DH)
        head = _lane_group(q.shape, DIFF_DV)
        acc = jnp.zeros(q.shape, F32)
        for hh in range(DIFF_HEADS):
            ql = jnp.concatenate([jnp.where(comp == 2 * hh, q, 0), jnp.where(comp == 2 * hh + 1, q, 0)], axis=0)
            p, l = _softmax_rows(_dot_nt(ql, k))
            p = p / l
            pc = (p[0:tq] - lam * p[tq:2 * tq]).astype(BF16)
            o = jnp.dot(pc, v, preferred_element_type=F32)
            acc = jnp.where(head == hh, o, acc)
        y = acc * lax.rsqrt(_head_mean_sq(acc, DIFF_HEADS, DIFF_DV) + EPS) * g_ref[...]
        o_ref[...] = (y * par_ref[1:2, 0:1]).astype(o_ref.dtype)

    @pl.when(i == 0)
    def _():
        run(n_ctx)

    @pl.when(i > 0)
    def _():
        run(k_ref.shape[0])


def _diff_attention(pb, par, subln, n_ctx):
    b, s, _ = pb.shape
    tq = ROW_TILE
    return pl.pallas_call(
        functools.partial(_diff_kernel, n_ctx=n_ctx),
        grid=(b, s // tq),
        in_specs=[pl.BlockSpec((None, tq, 256), lambda bi, i: (bi, i, 5)),
                  pl.BlockSpec((None, s, 256), lambda bi, i: (bi, 0, 6)),
                  pl.BlockSpec((None, s, 256), lambda bi, i: (bi, 0, 7)),
                  pl.BlockSpec((8, 128), lambda bi, i: (0, 0)),
                  pl.BlockSpec((1, 256), lambda bi, i: (0, 0))],
        out_specs=pl.BlockSpec((None, tq, 256), lambda bi, i: (bi, i, 0)),
        out_shape=jax.ShapeDtypeStruct((b, s, 256), BF16),
        compiler_params=_cparams(("arbitrary", "arbitrary")),
        name="diff_attention",
    )(pb, pb, pb, par, subln)


def _na_kernel(q_ref, k_ref, v_ref, bias_ref, o_ref, *, n_ctx, n_rows):
    i = pl.program_id(1)
    tq = q_ref.shape[0]

    def stack_heads(q):
        head = _lane_group(q.shape, NA_DH)
        return jnp.concatenate([jnp.where(head == hh, q, 0) for hh in range(NA_HEADS)], axis=0)

    def unstack_heads(o, rows):
        head = _lane_group((rows, 256), NA_DH)
        out = jnp.zeros((rows, 256), F32)
        for hh in range(NA_HEADS):
            out = jnp.where(head == hh, o[hh * rows:(hh + 1) * rows], out)
        return out

    @pl.when(i == 0)
    def _():
        kc = k_ref[0:n_ctx, :]
        vc = v_ref[0:n_ctx, :]
        p, l = _softmax_rows(_dot_nt(stack_heads(q_ref[...]), kc))
        o = jnp.dot(p.astype(BF16), vc, preferred_element_type=F32) / l
        o_ref[...] = unstack_heads(o, tq).astype(o_ref.dtype)

    @pl.when(i > 0)
    def _():
        kc = k_ref[0:n_ctx, :]
        vc = v_ref[0:n_ctx, :]
        for rr in range(tq // GRID_W):
            r = (i - 1) * (tq // GRID_W) + rr
            rs = jnp.clip(r - NA_KH // 2, 0, n_rows - NA_KH)
            start = pl.multiple_of(n_ctx + rs * GRID_W, GRID_W)
            kcat = jnp.concatenate([kc, k_ref[pl.ds(start, NA_KH * GRID_W), :]], axis=0)
            vcat = jnp.concatenate([vc, v_ref[pl.ds(start, NA_KH * GRID_W), :]], axis=0)
            q = q_ref[rr * GRID_W:(rr + 1) * GRID_W, :]
            s = _dot_nt(stack_heads(q), kcat) + bias_ref[r - rs]
            p, l = _softmax_rows(s)
            o = jnp.dot(p.astype(BF16), vcat, preferred_element_type=F32) / l
            o_ref[rr * GRID_W:(rr + 1) * GRID_W, :] = unstack_heads(o, GRID_W).astype(o_ref.dtype)


def _na_attention(pb, bias, n_ctx):
    b, s, _ = pb.shape
    tq = ROW_TILE
    n_rows = (s - n_ctx) // GRID_W
    return pl.pallas_call(
        functools.partial(_na_kernel, n_ctx=n_ctx, n_rows=n_rows),
        grid=(b, s // tq),
        in_specs=[pl.BlockSpec((None, tq, 256), lambda bi, i: (bi, i, 0)),
                  pl.BlockSpec((None, s, 256), lambda bi, i: (bi, 0, 1)),
                  pl.BlockSpec((None, s, 256), lambda bi, i: (bi, 0, 2)),
                  pl.BlockSpec(bias.shape, lambda bi, i: (0, 0, 0))],
        out_specs=pl.BlockSpec((None, tq, 256), lambda bi, i: (bi, i, 0)),
        out_shape=jax.ShapeDtypeStruct((b, s, 256), BF16),
        compiler_params=_cparams(("arbitrary", "arbitrary")),
        name="na_attention",
    )(pb, pb, pb, bias)


def _na_bias_table(rpb, n_ctx):
    depth = rpb.shape[0]
    pad = GRID_W - NA_KW
    padded = jnp.pad(rpb.astype(F32), ((0, 0), (0, 0), (0, 0), (pad, pad)))
    cols = jnp.stack([padded[..., GRID_W - 1 - w:2 * GRID_W - 1 - w] for w in range(GRID_W)], axis=3)
    vals = jnp.stack([cols[:, :, NA_KH - 1 - o:2 * NA_KH - 1 - o] for o in range(NA_KH)], axis=2)
    vals = jnp.transpose(vals, (0, 2, 1, 4, 3, 5))
    w = np.arange(GRID_W)[:, None, None]
    kc = np.arange(GRID_W)[None, None, :]
    cs = np.clip(w - NA_KW // 2, 0, GRID_W - NA_KW)
    inwin = np.broadcast_to((kc >= cs) & (kc < cs + NA_KW), (GRID_W, NA_KH, GRID_W))
    vals = jnp.where(jnp.asarray(inwin), vals, NEG).reshape(depth, NA_KH, NA_HEADS * GRID_W, NA_KH * GRID_W)
    return jnp.concatenate([jnp.zeros((depth, NA_KH, NA_HEADS * GRID_W, n_ctx), F32), vals], axis=-1)


def _hy_pre_kernel(u_ref, w_ref, b_ref, x0_ref, z_ref, *, bounds, tm):
    s = u_ref.shape[0]
    w0, w1, w2, bias = w_ref[0:1, :], w_ref[1:2, :], w_ref[2:3, :], b_ref[...]
    rid = lax.broadcasted_iota(I32, (tm, u_ref.shape[1]), 0)
    for c0 in range(0, s, tm):
        u = u_ref[c0:c0 + tm, :]
        prev = pltpu.roll(u, 1, 0)
        nxt = pltpu.roll(u, tm - 1, 0)
        first = jnp.zeros_like(w0) if c0 in bounds else u_ref[c0 - 1:c0, :]
        last = jnp.zeros_like(w0) if (c0 + tm) in bounds else u_ref[c0 + tm:c0 + tm + 1, :]
        prev = jnp.where(rid == 0, first, prev)
        nxt = jnp.where(rid == tm - 1, last, nxt)
        uc = prev * w0 + u * w1 + nxt * w2 + bias
        x0_ref[c0:c0 + tm, :] = uc[:, 0:HY_W]
        z_ref[c0:c0 + tm, :] = uc[:, HY_W:2 * HY_W] * uc[:, 2 * HY_W:3 * HY_W]


def _hy_pre(mix, w_short, b_short, n_ctx):
    b, s, _ = mix.shape
    return pl.pallas_call(
        functools.partial(_hy_pre_kernel, bounds=(0, n_ctx, s), tm=ROW_TILE),
        grid=(b,),
        in_specs=[pl.BlockSpec((None, s, 3 * HY_W), lambda bi: (bi, 0, 1)),
                  pl.BlockSpec((3, 3 * HY_W), lambda bi: (0, 0)),
                  pl.BlockSpec((1, 3 * HY_W), lambda bi: (0, 0))],
        out_specs=[pl.BlockSpec((None, s, HY_W), lambda bi: (bi, 0, 0)),
                   pl.BlockSpec((None, s, HY_W), lambda bi: (bi, 0, 0))],
        out_shape=[jax.ShapeDtypeStruct((b, s, HY_W), F32), jax.ShapeDtypeStruct((b, s, HY_W), F32)],
        compiler_params=_cparams(("arbitrary",)),
        name="hyena_pre",
    )(mix, w_short, b_short)


def _hy_filter_kernel(z_ref, w1_ref, b1_ref, w2_ref, b2_ref, w3_ref, b3_ref, w4_ref, f_ref, dec_ref, o_ref):
    def lin(a, w_ref, b_ref):
        return jnp.dot(a, w_ref[...], preferred_element_type=F32, precision=HIGHEST) + b_ref[...]

    hh = jnp.sin(f_ref[0:1, :] * lin(z_ref[...], w1_ref, b1_ref))
    hh = jnp.sin(f_ref[1:2, :] * lin(hh, w2_ref, b2_ref))
    hh = jnp.sin(f_ref[2:3, :] * lin(hh, w3_ref, b3_ref))
    hh = jnp.dot(hh, w4_ref[...], preferred_element_type=F32, precision=HIGHEST) * dec_ref[...]
    o_ref[...] = hh / jnp.sum(jnp.abs(hh), axis=0, keepdims=True)


def _pad_to(a, shape):
    return jnp.pad(a, [(0, t - s) for s, t in zip(a.shape, shape)])


def _hy_filters(n, w1, b1, w2, b2, w3, b3, w4, freq):
    depth = w1.shape[0]
    t = np.linspace(0.0, 1.0, n, dtype=np.float32)[:, None]
    w = np.float32(2.0 * math.pi / n) * np.arange(n, dtype=np.float32)[:, None]
    bands = np.linspace(1e-4, HY_BANDS - 1, HY_BANDS, dtype=np.float32)[None, :]
    z = np.concatenate([t, np.cos(w * bands), np.sin(w * bands)], axis=-1).astype(np.float32)
    z = np.pad(z, ((0, 0), (0, LANE - z.shape[1])))
    deltas = np.linspace(math.log(HY_TARGET) / HY_SLOW_DECAY, math.log(HY_TARGET) / HY_FAST_DECAY, HY_W, dtype=np.float32)
    deltas = np.tile(np.abs(deltas), 2)
    decay = np.exp(-t * deltas[None, :]).astype(np.float32)
    p = LANE
    args = (jnp.asarray(z), _pad_to(w1, (depth, p, p)), _pad_to(b1[:, None, :], (depth, 1, p)),
            _pad_to(w2, (depth, p, p)), _pad_to(b2[:, None, :], (depth, 1, p)),
            _pad_to(w3, (depth, p, p)), _pad_to(b3[:, None, :], (depth, 1, p)),
            _pad_to(w4, (depth, p, 2 * HY_W)), _pad_to(freq, (depth, 8, p)), jnp.asarray(decay))
    per_layer = lambda shp: pl.BlockSpec((None,) + shp, lambda l: (l,) + (0,) * len(shp))
    const = lambda shp: pl.BlockSpec(shp, lambda l: (0,) * len(shp))
    return pl.pallas_call(
        _hy_filter_kernel,
        grid=(depth,),
        in_specs=[const((n, p)), per_layer((p, p)), per_layer((1, p)), per_layer((p, p)), per_layer((1, p)),
                  per_layer((p, p)), per_layer((1, p)), per_layer((p, 2 * HY_W)), per_layer((8, p)),
                  const((n, 2 * HY_W))],
        out_specs=per_layer((n, 2 * HY_W)),
        out_shape=jax.ShapeDtypeStruct((depth, n, 2 * HY_W), F32),
        compiler_params=_cparams(("arbitrary",)),
        name="hyena_filters",
    )(*args)


def _conv_taps(filt):
    hf, hb = filt[..., :HY_W], filt[..., HY_W:]
    taps = jnp.concatenate([hb[:, :0:-1], (hf[:, 0:1] + hb[:, 0:1]), hf[:, 1:], jnp.zeros_like(hf[:, 0:1])], axis=1)
    return jnp.transpose(taps, (0, 2, 1))


def _hy_conv_kernel(z_ref, k_ref, o_ref, *, nblk, nb):
    cg = z_ref.shape[0]
    n2 = k_ref.shape[1]
    ncols = (2 * nblk - 1) * LANE

    def body(ci, _):
        krow = k_ref[pl.ds(ci, 1), :]
        kb = jnp.broadcast_to(krow, (LANE, n2))
        big = pltpu.roll(kb, n2 - (LANE - 1), 1, stride=1, stride_axis=0)[:, :ncols].astype(BF16)
        o_ref[ci] = jnp.zeros(o_ref.shape[1:], F32)
        for m in range(-(nblk - 1), nblk):
            km = big[:, (m + nblk - 1) * LANE:(m + nblk) * LANE]
            cnt = (nblk - abs(m)) * nb
            src = 0 if m >= 0 else -m * nb
            dst = m * nb if m >= 0 else 0
            zin = z_ref[ci, src:src + cnt, :]
            o_ref[ci, dst:dst + cnt, :] += jnp.dot(zin, km, preferred_element_type=F32)
        return 0

    lax.fori_loop(0, cg, body, 0)


def _hy_conv(zr, taps, nblk, nb):
    c, r, _ = zr.shape
    cg = 8
    return pl.pallas_call(
        functools.partial(_hy_conv_kernel, nblk=nblk, nb=nb),
        grid=(c // cg,),
        in_specs=[pl.BlockSpec((cg, r, LANE), lambda i: (i, 0, 0)),
                  pl.BlockSpec((cg, taps.shape[1]), lambda i: (i, 0))],
        out_specs=pl.BlockSpec((cg, r, LANE), lambda i: (i, 0, 0)),
        out_shape=jax.ShapeDtypeStruct((c, r, LANE), F32),
        compiler_params=_cparams(("arbitrary",)),
        name="hyena_conv",
    )(zr, taps)


def _hy_long_conv(z, taps):
    b, n, c = z.shape
    nblk = n // LANE
    zr = jnp.transpose(z.astype(BF16).reshape(b, nblk, LANE, c), (3, 1, 0, 2)).reshape(c, nblk * b, LANE)
    y = _hy_conv(zr, taps, nblk, b)
    return jnp.transpose(y.reshape(c, nblk, b, LANE), (2, 1, 3, 0)).reshape(b, n, c)


def _merge_kernel(h_ref, mod_ref, ya_ref, x0_ref, z_ref, yconv_ref, hb_ref, yc_ref, yd_ref, gate_ref,
                  wb_ref, wo_ref, o_ref, *, n_ctx, tm):
    i = pl.program_id(1)
    yb = (x0_ref[...] * (yconv_ref[...] + z_ref[...] * hb_ref[...])).astype(BF16)
    ys = (ya_ref[...], yb, yc_ref[...], yd_ref[...])
    acc = jnp.zeros(h_ref.shape, F32)
    d = h_ref.shape[1]
    for k in range(N_BRANCH):
        proj = jnp.dot(ys[k], wb_ref[k], preferred_element_type=F32)
        acc = acc + jax.nn.sigmoid(gate_ref[:, k * d:(k + 1) * d]) * proj
    out = jnp.dot(acc.astype(BF16), wo_ref[...], preferred_element_type=F32)
    row = i * tm + lax.broadcasted_iota(I32, out.shape, 0)
    gate = jnp.where(row < n_ctx, mod_ref[0, 2:3, :], mod_ref[1, 2:3, :])
    o_ref[...] = h_ref[...] + gate * out


def _merge(h, modl, ya, x0, z, yconv, hbias, yc, yd, gates, wb, wo, n_ctx):
    b, s, d = h.shape
    tm = ROW_TILE
    row = lambda w: pl.BlockSpec((None, tm, w), lambda bi, i: (bi, i, 0))
    return pl.pallas_call(
        functools.partial(_merge_kernel, n_ctx=n_ctx, tm=tm),
        grid=(b, s // tm),
        in_specs=[row(d), pl.BlockSpec((None, 2, 8, d), lambda bi, i: (bi, 0, 0, 0)),
                  row(256), row(256), row(256), row(256), pl.BlockSpec((1, 256), lambda bi, i: (0, 0)),
                  row(256), row(256), row(GATE_W),
                  pl.BlockSpec(wb.shape, lambda bi, i: (0, 0, 0)),
                  pl.BlockSpec(wo.shape, lambda bi, i: (0, 0))],
        out_specs=row(d),
        out_shape=jax.ShapeDtypeStruct((b, s, d), F32),
        compiler_params=_cparams(("arbitrary", "arbitrary")),
        name="merge",
    )(h, modl, ya, x0, z, yconv, hbias, yc, yd, gates, wb, wo)


def _topk_rows(s, label, k):
    vals, labs = [], []
    for _ in range(k):
        m = jnp.max(s, axis=0, keepdims=True)
        lb = jnp.min(jnp.where(s == m, label, float(2 ** 20)), axis=0, keepdims=True)
        vals.append(m)
        labs.append(lb)
        s = jnp.where(label == lb, -jnp.inf, s)
    return jnp.concatenate(vals, axis=0), jnp.concatenate(labs, axis=0).astype(I32)


_CAND_FIXED_A = ((0, 0), (0, 8), (1, 0), (2, 0), (3, 0))
_CAND_FIXED_B = ((0, 0), (0, 8), (1, 0), (2, 0))


def _select_rows(table, pos):
    out = jnp.zeros(pos.shape, table.dtype)
    for a in range(table.shape[0]):
        out = jnp.where(pos == a, table[a:a + 1, :], out)
    return out


def _peer_route_kernel(q_ref, keys_ref, i_ref, j_ref, g_ref):
    st = _dot_nt(keys_ref[...], q_ref[...].astype(BF16))
    t = st.shape[1]
    key_id = lax.broadcasted_iota(I32, (PEER_NKEYS, t), 0).astype(F32)
    sv1, si1 = _topk_rows(st[0:PEER_NKEYS], key_id, PEER_TOPK)
    sv2, si2 = _topk_rows(st[PEER_NKEYS:2 * PEER_NKEYS], key_id, PEER_TOPK)
    r8 = lax.broadcasted_iota(I32, (8, t), 0)
    cands, labels = [], []
    for a, b0 in _CAND_FIXED_A:
        cands.append(sv1[a:a + 1, :] + sv2[b0:b0 + 8, :])
        labels.append((a * PEER_TOPK + b0 + r8).astype(F32))
    for b, a0 in _CAND_FIXED_B:
        c = sv1[a0:a0 + 8, :] + sv2[b:b + 1, :]
        cands.append(jnp.where(r8 < 4, -jnp.inf, c) if a0 == 0 else c)
        labels.append(((a0 + r8) * PEER_TOPK + b).astype(F32))
    best, pos = _topk_rows(jnp.concatenate(cands, axis=0), jnp.concatenate(labels, axis=0), PEER_TOPK)
    i_ref[...] = _select_rows(si1, pos >> int(math.log2(PEER_TOPK)))
    j_ref[...] = _select_rows(si2, pos & (PEER_TOPK - 1))
    e = jnp.exp(best - jnp.max(best, axis=0, keepdims=True))
    g_ref[...] = e / jnp.sum(e, axis=0, keepdims=True)


def _peer_route(q, keys_blk):
    t, _ = q.shape
    tt = 256
    out = pl.BlockSpec((None, PEER_TOPK, tt), lambda ti, hh: (hh, 0, ti))
    shp = (PEER_HEADS, PEER_TOPK, t)
    return pl.pallas_call(
        _peer_route_kernel,
        grid=(t // tt, PEER_HEADS),
        in_specs=[pl.BlockSpec((tt, PEER_DK), lambda ti, hh: (ti, hh)),
                  pl.BlockSpec((None, 2 * PEER_NKEYS, PEER_DK), lambda ti, hh: (hh, 0, 0))],
        out_specs=[out, out, out],
        out_shape=[jax.ShapeDtypeStruct(shp, I32), jax.ShapeDtypeStruct(shp, I32), jax.ShapeDtypeStruct(shp, F32)],
        compiler_params=_cparams(("arbitrary", "arbitrary")),
        name="peer_route",
    )(q, keys_blk)


GS_HALF = PEER_NKEYS // 2
GS_PITCH = GS_HALF + 1
GS_UNROLL = 8
PEER_TM = 576
PEER_TE = 1024
U32 = jnp.uint32


def _peer_expert_kernel(h_ref, mod_ref, x_ref, i_ref, j_ref, g_ref, ut_ref, v_ref, o_ref, gs_ref, acc_ref,
                        *, n_ctx, tm):
    ti, e = pl.program_id(1), pl.program_id(2)
    n_e = pl.num_programs(2)
    te = ut_ref.shape[1]
    nk = PEER_NKEYS

    @pl.when(e == 0)
    def _():
        sub = lax.broadcasted_iota(I32, (nk, nk), 0)

        def build(t):
            irow = i_ref[pl.ds(t, 1), :]
            jrow = j_ref[pl.ds(t, 1), :]
            grow = g_ref[pl.ds(t, 1), :]
            at = jnp.where(sub == irow, 1.0, 0.0).astype(BF16)
            bt = jnp.where(sub == jrow, grow, 0.0).astype(BF16)
            gt = _dot_nt(at, bt).astype(BF16).astype(F32)
            lo = pltpu.bitcast(gt[0:GS_HALF], U32) >> 16
            hi = pltpu.bitcast(gt[GS_HALF:nk], U32)
            gs_ref[pl.ds(t * GS_PITCH, GS_HALF), :] = hi | lo

        def body(tb, _):
            for u in range(GS_UNROLL):
                build(tb * GS_UNROLL + u)
            return 0

        lax.fori_loop(0, tm // GS_UNROLL, body, 0)

    hid = jnp.dot(x_ref[...], ut_ref[...], preferred_element_type=F32)
    nb = te // nk // 2

    def gelu(a):
        return 0.5 * a * (1.0 + lax.erf(a * (2.0 ** -0.5)))

    first, second = [], []
    for k in range(nb):
        word = gs_ref[pl.ds(e * nb + k, tm, stride=GS_PITCH), :]
        g_lo = pltpu.bitcast(word << 16, F32)
        g_hi = pltpu.bitcast(word & jnp.uint32(0xFFFF0000), F32)
        first.append((g_lo * gelu(hid[:, k * nk:(k + 1) * nk])).astype(BF16))
        second.append((g_hi * gelu(hid[:, (nb + k) * nk:(nb + k + 1) * nk])).astype(BF16))
    contrib = jnp.dot(jnp.concatenate(first + second, axis=1), v_ref[...], preferred_element_type=F32)

    @pl.when(e == 0)
    def _():
        acc_ref[...] = contrib

    @pl.when(e > 0)
    def _():
        acc_ref[...] += contrib

    @pl.when(e == n_e - 1)
    def _():
        row = ti * tm + lax.broadcasted_iota(I32, acc_ref.shape, 0)
        gate = jnp.where(row < n_ctx, mod_ref[0, 5:6, :], mod_ref[1, 5:6, :])
        o_ref[...] = h_ref[...] + gate * acc_ref[...]


def _peer_experts(h, modl, xb, i_idx, j_idx, g, ut, v, n_ctx):
    b, s, d = h.shape
    tm, te = PEER_TM, PEER_TE
    n_exp = ut.shape[1]
    slots = PEER_HEADS * PEER_TOPK
    row = lambda w: pl.BlockSpec((None, tm, w), lambda bi, i, e: (bi, i, 0))
    return pl.pallas_call(
        functools.partial(_peer_expert_kernel, n_ctx=n_ctx, tm=tm),
        grid=(b, s // tm, n_exp // te),
        in_specs=[row(d), pl.BlockSpec((None, 2, 8, d), lambda bi, i, e: (bi, 0, 0, 0)),
                  row(d), row(slots), row(slots), row(slots),
                  pl.BlockSpec((d, te), lambda bi, i, e: (0, e)),
                  pl.BlockSpec((te, d), lambda bi, i, e: (e, 0))],
        out_specs=row(d),
        out_shape=jax.ShapeDtypeStruct((b, s, d), F32),
        scratch_shapes=[pltpu.VMEM((tm * GS_PITCH, PEER_NKEYS), U32), pltpu.VMEM((tm, d), F32)],
        compiler_params=_cparams(("arbitrary", "arbitrary", "arbitrary")),
        name="peer_experts",
    )(h, modl, xb, i_idx, j_idx, g, ut, v)


def _final_norm_kernel(x_ref, g_ref, o_ref):
    x = x_ref[...]
    o_ref[...] = x * lax.rsqrt(jnp.mean(x * x, axis=-1, keepdims=True) + EPS) * g_ref[...]


def _final_norm(h, g, n_ctx):
    b, s, d = h.shape
    tm = ROW_TILE
    n_lat = s - n_ctx
    off = n_ctx // tm
    return pl.pallas_call(
        _final_norm_kernel,
        grid=(b, n_lat // tm),
        in_specs=[pl.BlockSpec((None, tm, d), lambda bi, i: (bi, i + off, 0)),
                  pl.BlockSpec((1, d), lambda bi, i: (0, 0))],
        out_specs=pl.BlockSpec((None, tm, d), lambda bi, i: (bi, i, 0)),
        out_shape=jax.ShapeDtypeStruct((b, n_lat, d), F32),
        compiler_params=_cparams(("arbitrary", "arbitrary")),
        name="final_norm",
    )(h, g)


def _prep_w_in(w_in):
    depth, d, _ = w_in.shape
    mixw = w_in[:, :, :2816]
    gq = mixw[:, :, 1536:1792].reshape(depth, d, GQA_HEADS, GQA_DH)[:, :, (0, 2, 1, 3), :].reshape(depth, d, 256)
    mixw = jnp.concatenate([mixw[:, :, :1536], gq, mixw[:, :, 1792:], jnp.zeros((depth, d, MIX_W - 2816), w_in.dtype)], axis=-1)
    return jnp.concatenate([mixw, w_in[:, :, 2816:]], axis=-1).astype(BF16)


def _peer_chunk_order(w):
    n_exp, d = w.shape
    nb = PEER_TE // PEER_NKEYS // 2
    return w.reshape(2, GS_HALF // nb, nb, PEER_NKEYS, d).transpose(1, 0, 2, 3, 4).reshape(n_exp, d)


def _prep_peer_keys(keys):
    depth = keys.shape[0]
    half = PEER_DK // 2
    z = jnp.zeros((depth, PEER_HEADS, PEER_NKEYS, half), keys.dtype)
    top = jnp.concatenate([keys[:, :, 0], z], axis=-1)
    bot = jnp.concatenate([z, keys[:, :, 1]], axis=-1)
    return jnp.concatenate([top, bot], axis=2).astype(BF16)


def kernel(x, c, ctx, c_ctx, w_mod, b_mod, norm1_g, norm2_g, w_in, na_rpb, hy_short_w, hy_short_b, hy_w1, hy_b1, hy_w2, hy_b2, hy_w3, hy_b3, hy_w4, hy_freq, hy_bias, gqa_qn, gqa_kn, diff_lq1, diff_lk1, diff_lq2, diff_lk2, diff_subln, w_branch, w_out, peer_wq, peer_keys, peer_u, peer_v, final_g):
    B, L, D = x.shape
    C = ctx.shape[1]
    S = C + L
    depth = w_mod.shape[0]
    assert C == ROW_TILE and L % ROW_TILE == 0 and D == 1024 and L % GRID_W == 0

    h = jnp.concatenate([ctx, x], axis=1)

    r = -(-(B + 1) // 8) * 8
    cvec = jnp.zeros((r, D), F32).at[:B].set(c).at[B].set(c_ctx)
    modall = _modulation(cvec, w_mod, b_mod)
    mod_lat = modall[:, :B].reshape(depth, B, 1, 6, D)
    mod_ctx = jnp.broadcast_to(modall[:, B].reshape(depth, 1, 1, 6, D), (depth, B, 1, 6, D))
    mods = jnp.pad(jnp.concatenate([mod_ctx, mod_lat], axis=2), ((0, 0), (0, 0), (0, 0), (0, 2), (0, 0)))

    fargs = (hy_w1, hy_b1, hy_w2, hy_b2, hy_w3, hy_b3, hy_w4, hy_freq)
    taps_lat = _conv_taps(_hy_filters(L, *fargs))
    taps_ctx = _conv_taps(_hy_filters(C, *fargs))

    cg, sg = _rope_tables(C, L, GQA_DH, 256)
    cd, sd = _rope_tables(C, L, DIFF_DH, 256)

    lam_init = jnp.asarray([0.8 - 0.6 * math.exp(-0.3 * l) for l in range(depth)], F32)
    lam = (jnp.exp(jnp.sum(diff_lq1.astype(F32) * diff_lk1.astype(F32), axis=-1))
           - jnp.exp(jnp.sum(diff_lq2.astype(F32) * diff_lk2.astype(F32), axis=-1)) + lam_init)
    diff_par = jnp.zeros((depth, 8, LANE), F32).at[:, 0, :].set(lam[:, None]).at[:, 1, :].set(1.0 - lam_init[:, None])

    wb = w_branch.astype(BF16)
    wb = wb.at[:, 2].set(wb[:, 2].reshape(depth, GQA_HEADS, GQA_DH, D)[:, (0, 2, 1, 3)].reshape(depth, BRANCH_W, D))

    xs = dict(
        mods=mods, n1=norm1_g[:, None, :], n2=norm2_g[:, None, :], w_in=_prep_w_in(w_in),
        bias=_na_bias_table(na_rpb, C), sw=hy_short_w, sb=hy_short_b[:, None, :], hbias=hy_bias[:, None, :],
        taps_lat=taps_lat, taps_ctx=taps_ctx,
        qn=jnp.tile(gqa_qn, (1, GQA_HEADS))[:, None, :], kn=jnp.tile(gqa_kn, (1, GQA_KV))[:, None, :],
        diff_par=diff_par, subln=jnp.tile(diff_subln, (1, DIFF_HEADS))[:, None, :],
        wb=wb, wo=w_out.astype(BF16), wq=peer_wq.astype(BF16), keys=_prep_peer_keys(peer_keys),
        u=peer_u, v=peer_v,
    )

    def layer(h, p):
        mix, gates = _in_proj(h, p["mods"], p["n1"], p["w_in"], C)
        pb = _prep(mix, cg, sg, cd, sd, p["qn"], p["kn"])
        ya = _na_attention(pb, p["bias"], C)
        yc = _gqa_attention(pb, C)
        yd = _diff_attention(pb, p["diff_par"], p["subln"], C)
        x0, z = _hy_pre(mix, p["sw"], p["sb"], C)
        yconv = jnp.concatenate([_hy_long_conv(z[:, :C], p["taps_ctx"]), _hy_long_conv(z[:, C:], p["taps_lat"])], axis=1)
        h = _merge(h, p["mods"], ya, x0, z, yconv, p["hbias"], yc, yd, gates, p["wb"], p["wo"], C)

        q, xb = _peer_q(h, p["mods"], p["n2"], p["wq"], C)
        i_idx, j_idx, g = _peer_route(q.reshape(B * S, -1), p["keys"])
        slots = PEER_HEADS * PEER_TOPK
        to_rows = lambda a: jnp.transpose(a, (2, 0, 1)).reshape(B, S, slots)
        ut = jnp.transpose(_peer_chunk_order(p["u"].astype(BF16)))
        h = _peer_experts(h, p["mods"], xb, to_rows(i_idx), to_rows(j_idx), to_rows(g), ut,
                          _peer_chunk_order(p["v"].astype(BF16)), C)
        return h, None

    h, _ = lax.scan(layer, h, xs)
    return _final_norm(h, final_g[None, :], C)
```

```python
import functools
import math

import numpy as np
import jax
import jax.numpy as jnp
from jax import lax
from jax.experimental import pallas as pl
from jax.experimental.pallas import tpu as pltpu

F32 = jnp.float32
BF16 = jnp.bfloat16
I32 = jnp.int32
HIGHEST = lax.Precision.HIGHEST

EPS = 1e-6
GRID_W = 64
ROPE_THETA = 10000.0
NA_HEADS, NA_DH, NA_KH, NA_KW = 4, 64, 8, 16
HY_W, HY_BANDS, HY_FFN = 256, 16, 64
HY_FAST_DECAY, HY_SLOW_DECAY, HY_TARGET = 0.3, 1.5, 1e-2
GQA_HEADS, GQA_KV, GQA_DH = 4, 2, 64
DIFF_HEADS, DIFF_DH, DIFF_DV = 4, 32, 64
N_BRANCH, BRANCH_W = 4, 256
PEER_HEADS, PEER_NKEYS, PEER_DK, PEER_TOPK = 8, 128, 128, 16

LANE = 128
ROW_TILE = 256
VMEM_LIMIT = 56 * 1024 * 1024
NEG = -1e30
LOG2E = math.log2(math.e)
MIX_W = 3072
GATE_W = N_BRANCH * 1024
COL_TILE = 1024


def _cparams(sem):
    return pltpu.CompilerParams(dimension_semantics=sem, vmem_limit_bytes=VMEM_LIMIT)


def _mod_kernel(c_ref, w_ref, b_ref, o_ref):
    cv = c_ref[...]
    a = cv * jax.nn.sigmoid(cv)
    o_ref[...] = jnp.dot(a, w_ref[...], preferred_element_type=F32, precision=HIGHEST) + b_ref[...]


def _modulation(cvec, w_mod, b_mod):
    depth, d, n = w_mod.shape
    r = cvec.shape[0]
    tn = 1024
    return pl.pallas_call(
        _mod_kernel,
        grid=(depth, n // tn),
        in_specs=[pl.BlockSpec((r, d), lambda l, j: (0, 0)),
                  pl.BlockSpec((None, d, tn), lambda l, j: (l, 0, j)),
                  pl.BlockSpec((None, 1, tn), lambda l, j: (l, 0, j))],
        out_specs=pl.BlockSpec((None, r, tn), lambda l, j: (l, 0, j)),
        out_shape=jax.ShapeDtypeStruct((depth, r, n), F32),
        compiler_params=_cparams(("arbitrary", "arbitrary")),
        name="modulation",
    )(cvec, w_mod, b_mod.reshape(depth, 1, n))


def _norm_modulate(h_ref, mod_ref, g_ref, srow, row0, n_ctx):
    x = h_ref[...]
    y = x * lax.rsqrt(jnp.mean(x * x, axis=-1, keepdims=True) + EPS) * g_ref[...]
    row = row0 + lax.broadcasted_iota(I32, x.shape, 0)
    is_ctx = row < n_ctx
    shift = jnp.where(is_ctx, mod_ref[0, srow:srow + 1, :], mod_ref[1, srow:srow + 1, :])
    scale = jnp.where(is_ctx, mod_ref[0, srow + 1:srow + 2, :], mod_ref[1, srow + 1:srow + 2, :])
    return y * (1.0 + scale) + shift


def _in_proj_kernel(h_ref, mod_ref, g_ref, w_ref, mix_ref, gate_ref, nb_ref, *, tm, n_ctx, n_mix):
    i, j = pl.program_id(1), pl.program_id(2)

    @pl.when(j == 0)
    def _():
        nb_ref[...] = _norm_modulate(h_ref, mod_ref, g_ref, 0, i * tm, n_ctx).astype(BF16)

    res = jnp.dot(nb_ref[...], w_ref[...], preferred_element_type=F32)

    @pl.when(j < n_mix)
    def _():
        mix_ref[...] = res

    @pl.when(j >= n_mix)
    def _():
        gate_ref[...] = res.astype(gate_ref.dtype)


def _in_proj(h, modl, g, w, n_ctx):
    b, s, d = h.shape
    tm, tn = 768, COL_TILE
    n_mix = MIX_W // tn
    n_tot = (MIX_W + GATE_W) // tn
    return pl.pallas_call(
        functools.partial(_in_proj_kernel, tm=tm, n_ctx=n_ctx, n_mix=n_mix),
        grid=(b, s // tm, n_tot),
        in_specs=[pl.BlockSpec((None, tm, d), lambda bi, i, j: (bi, i, 0)),
                  pl.BlockSpec((None, 2, 8, d), lambda bi, i, j: (bi, 0, 0, 0)),
                  pl.BlockSpec((1, d), lambda bi, i, j: (0, 0)),
                  pl.BlockSpec((d, tn), lambda bi, i, j: (0, j))],
        out_specs=[pl.BlockSpec((None, tm, tn), lambda bi, i, j: (bi, i, jnp.minimum(j, n_mix - 1))),
                   pl.BlockSpec((None, tm, tn), lambda bi, i, j: (bi, i, jnp.maximum(j - n_mix, 0)))],
        out_shape=[jax.ShapeDtypeStruct((b, s, MIX_W), F32), jax.ShapeDtypeStruct((b, s, GATE_W), BF16)],
        scratch_shapes=[pltpu.VMEM((tm, d), BF16)],
        compiler_params=_cparams(("arbitrary", "arbitrary", "arbitrary")),
        name="in_proj",
    )(h, modl, g, w)


def _peer_q_kernel(h_ref, mod_ref, g_ref, w_ref, q_ref, nb_ref, *, tm, n_ctx):
    i = pl.program_id(1)
    nb = _norm_modulate(h_ref, mod_ref, g_ref, 3, i * tm, n_ctx).astype(BF16)
    nb_ref[...] = nb
    q_ref[...] = jnp.dot(nb, w_ref[...], preferred_element_type=F32)


def _peer_q(h, modl, g, wq, n_ctx):
    b, s, d = h.shape
    tm = 768
    n = wq.shape[1]
    return pl.pallas_call(
        functools.partial(_peer_q_kernel, tm=tm, n_ctx=n_ctx),
        grid=(b, s // tm),
        in_specs=[pl.BlockSpec((None, tm, d), lambda bi, i: (bi, i, 0)),
                  pl.BlockSpec((None, 2, 8, d), lambda bi, i: (bi, 0, 0, 0)),
                  pl.BlockSpec((1, d), lambda bi, i: (0, 0)),
                  pl.BlockSpec((d, n), lambda bi, i: (0, 0))],
        out_specs=[pl.BlockSpec((None, tm, n), lambda bi, i: (bi, i, 0)),
                   pl.BlockSpec((None, tm, d), lambda bi, i: (bi, i, 0))],
        out_shape=[jax.ShapeDtypeStruct((b, s, n), F32), jax.ShapeDtypeStruct((b, s, d), BF16)],
        compiler_params=_cparams(("arbitrary", "arbitrary")),
        name="peer_q",
    )(h, modl, g, wq)


def _lane_group(shape, width):
    return lax.broadcasted_iota(I32, shape, 1) >> int(math.log2(width))


def _head_mean_sq(x, n_heads, dh):
    hid = _lane_group(x.shape, dh)
    x2 = x * x
    ms = jnp.zeros_like(x)
    for hh in range(n_heads):
        m = hid == hh
        s = jnp.sum(jnp.where(m, x2, 0.0), axis=-1, keepdims=True)
        ms = jnp.where(m, s, ms)
    return ms * (1.0 / dh)


def _rope(x, cos, sin_signed, qs):
    w = x.shape[-1]
    lane = lax.broadcasted_iota(I32, x.shape, 1)
    lo = (lane & (2 * qs - 1)) < qs
    partner = jnp.where(lo, pltpu.roll(x, w - qs, 1), pltpu.roll(x, qs, 1))
    return x * cos + partner * sin_signed


def _prep_kernel(na_ref, gqa_ref, dq_ref, dk_ref, dv_ref, cg_ref, sg_ref, cd_ref, sd_ref, qn_ref, kn_ref, o_ref):
    na = na_ref[...]
    o_ref[:, 0:256] = (na[:, 0:256] * (LOG2E * NA_DH ** -0.5)).astype(BF16)
    o_ref[:, 256:768] = na[:, 256:768].astype(BF16)
    gq = gqa_ref[:, 0:256]
    gk = gqa_ref[:, 256:384]
    cg, sg = cg_ref[...], sg_ref[...]
    gq = gq * lax.rsqrt(_head_mean_sq(gq, GQA_HEADS, GQA_DH) + EPS) * qn_ref[...]
    gk = gk * lax.rsqrt(_head_mean_sq(gk, GQA_KV, GQA_DH) + EPS) * kn_ref[...]
    o_ref[:, 768:1024] = (_rope(gq, cg, sg, GQA_DH // 4) * (LOG2E * GQA_DH ** -0.5)).astype(BF16)
    o_ref[:, 1024:1152] = _rope(gk, cg[:, 0:128], sg[:, 0:128], GQA_DH // 4).astype(BF16)
    o_ref[:, 1152:1280] = gqa_ref[:, 384:512].astype(BF16)
    cd, sd = cd_ref[...], sd_ref[...]
    o_ref[:, 1280:1536] = (_rope(dq_ref[...], cd, sd, DIFF_DH // 4) * (LOG2E * DIFF_DH ** -0.5)).astype(BF16)
    o_ref[:, 1536:1792] = _rope(dk_ref[...], cd, sd, DIFF_DH // 4).astype(BF16)
    o_ref[:, 1792:2048] = dv_ref[...].astype(BF16)


def _prep(mix, cg, sg, cd, sd, qn, kn):
    b, s, _ = mix.shape
    tm = ROW_TILE
    tab = pl.BlockSpec((tm, 256), lambda bi, i: (i, 0))
    return pl.pallas_call(
        _prep_kernel,
        grid=(b, s // tm),
        in_specs=[pl.BlockSpec((None, tm, 768), lambda bi, i: (bi, i, 0)),
                  pl.BlockSpec((None, tm, 512), lambda bi, i: (bi, i, 3)),
                  pl.BlockSpec((None, tm, 256), lambda bi, i: (bi, i, 8)),
                  pl.BlockSpec((None, tm, 256), lambda bi, i: (bi, i, 9)),
                  pl.BlockSpec((None, tm, 256), lambda bi, i: (bi, i, 10)),
                  tab, tab, tab, tab,
                  pl.BlockSpec((1, 256), lambda bi, i: (0, 0)),
                  pl.BlockSpec((1, 128), lambda bi, i: (0, 0))],
        out_specs=pl.BlockSpec((None, tm, 2048), lambda bi, i: (bi, i, 0)),
        out_shape=jax.ShapeDtypeStruct((b, s, 2048), BF16),
        compiler_params=_cparams(("arbitrary", "arbitrary")),
        name="qk_prep",
    )(mix, mix, mix, mix, mix, cg, sg, cd, sd, qn, kn)


def _rope_tables(n_ctx, n_lat, dh, width):
    qs = dh // 4
    t = np.arange(n_lat)
    rows, cols = (t // GRID_W).astype(np.float64), (t % GRID_W).astype(np.float64)
    lane = np.arange(width) % dh
    part = lane // (2 * qs)
    u = lane % (2 * qs)
    f = u % qs
    lo = u < qs
    freqs = ROPE_THETA ** (-f.astype(np.float64) / qs)
    pos = np.where(part[None, :] == 0, rows[:, None], cols[:, None])
    ang = (pos.astype(np.float32) * freqs.astype(np.float32)[None, :]).astype(np.float32)
    cos = np.cos(ang.astype(np.float64))
    sin = np.sin(ang.astype(np.float64)) * np.where(lo, -1.0, 1.0)[None, :]
    cos = np.concatenate([np.ones((n_ctx, width)), cos], axis=0)
    sin = np.concatenate([np.zeros((n_ctx, width)), sin], axis=0)
    return jnp.asarray(cos, F32), jnp.asarray(sin, F32)


def _softmax_rows(s):
    m = jnp.max(s, axis=-1, keepdims=True)
    p = jnp.exp2(s - m)
    return p, jnp.sum(p, axis=-1, keepdims=True)


def _dot_nt(a, b):
    return lax.dot_general(a, b, (((1,), (1,)), ((), ())), preferred_element_type=F32)


def _gqa_kernel(q_ref, k_ref, v_ref, o_ref, *, n_ctx):
    i = pl.program_id(1)

    def run(nk):
        k = k_ref[0:nk, :]
        v = v_ref[0:nk, :]
        q = q_ref[...]
        tq = q.shape[0]
        lane = lax.broadcasted_iota(I32, (tq, LANE), 1)
        grp = _lane_group((tq, LANE), GQA_DH)
        outs = []
        for g in range(GQA_KV):
            mask = grp == g
            ql = jnp.concatenate([jnp.where(mask, q[:, 0:128], 0), jnp.where(mask, q[:, 128:256], 0)], axis=0)
            p, l = _softmax_rows(_dot_nt(ql, k))
            outs.append(jnp.dot(p.astype(BF16), v, preferred_element_type=F32) / l)
        for half in range(2):
            sel = jnp.where(lane < GQA_DH, outs[0][half * tq:(half + 1) * tq], outs[1][half * tq:(half + 1) * tq])
            o_ref[:, half * 128:(half + 1) * 128] = sel.astype(o_ref.dtype)

    @pl.when(i == 0)
    def _():
        run(n_ctx)

    @pl.when(i > 0)
    def _():
        run(k_ref.shape[0])


def _gqa_attention(pb, n_ctx):
    b, s, _ = pb.shape
    tq = ROW_TILE
    return pl.pallas_call(
        functools.partial(_gqa_kernel, n_ctx=n_ctx),
        grid=(b, s // tq),
        in_specs=[pl.BlockSpec((None, tq, 256), lambda bi, i: (bi, i, 3)),
                  pl.BlockSpec((None, s, 128), lambda bi, i: (bi, 0, 8)),
                  pl.BlockSpec((None, s, 128), lambda bi, i: (bi, 0, 9))],
        out_specs=pl.BlockSpec((None, tq, 256), lambda bi, i: (bi, i, 0)),
        out_shape=jax.ShapeDtypeStruct((b, s, 256), BF16),
        compiler_params=_cparams(("arbitrary", "arbitrary")),
        name="gqa_attention",
    )(pb, pb, pb)


def _diff_kernel(q_ref, k_ref, v_ref, par_ref, g_ref, o_ref, *, n_ctx):
    i = pl.program_id(1)

    def run(nk):
        k = k_ref[0:nk, :]
        v = v_ref[0:nk, :]
        q = q_ref[...]
        tq = q.shape[0]
        lam = par_ref[0:1, 0:1]
        comp = _lane_group(q.shape, DIFF_DH)
        head = _lane_group(q.shape, DIFF_DV)
        acc = jnp.zeros(q.shape, F32)
        for hh in range(DIFF_HEADS):
            ql = jnp.concatenate([jnp.where(comp == 2 * hh, q, 0), jnp.where(comp == 2 * hh + 1, q, 0)], axis=0)
            p, l = _softmax_rows(_dot_nt(ql, k))
            inv = 1.0 / l
            pc = (p[0:tq] * inv[0:tq] - p[tq:2 * tq] * (lam * inv[tq:2 * tq])).astype(BF16)
            o = jnp.dot(pc, v, preferred_element_type=F32)
            acc = jnp.where(head == hh, o, acc)
        y = acc * lax.rsqrt(_head_mean_sq(acc, DIFF_HEADS, DIFF_DV) + EPS) * g_ref[...]
        o_ref[...] = (y * par_ref[1:2, 0:1]).astype(o_ref.dtype)

    @pl.when(i == 0)
    def _():
        run(n_ctx)

    @pl.when(i > 0)
    def _():
        run(k_ref.shape[0])


def _diff_attention(pb, par, subln, n_ctx):
    b, s, _ = pb.shape
    tq = ROW_TILE
    return pl.pallas_call(
        functools.partial(_diff_kernel, n_ctx=n_ctx),
        grid=(b, s // tq),
        in_specs=[pl.BlockSpec((None, tq, 256), lambda bi, i: (bi, i, 5)),
                  pl.BlockSpec((None, s, 256), lambda bi, i: (bi, 0, 6)),
                  pl.BlockSpec((None, s, 256), lambda bi, i: (bi, 0, 7)),
                  pl.BlockSpec((8, 128), lambda bi, i: (0, 0)),
                  pl.BlockSpec((1, 256), lambda bi, i: (0, 0))],
        out_specs=pl.BlockSpec((None, tq, 256), lambda bi, i: (bi, i, 0)),
        out_shape=jax.ShapeDtypeStruct((b, s, 256), BF16),
        compiler_params=_cparams(("arbitrary", "arbitrary")),
        name="diff_attention",
    )(pb, pb, pb, par, subln)


def _na_kernel(q_ref, k_ref, v_ref, bias_ref, o_ref, *, n_ctx, n_rows):
    i = pl.program_id(1)
    tq = q_ref.shape[0]

    def stack_heads(q):
        head = _lane_group(q.shape, NA_DH)
        return jnp.concatenate([jnp.where(head == hh, q, 0) for hh in range(NA_HEADS)], axis=0)

    def unstack_heads(o, rows):
        head = _lane_group((rows, 256), NA_DH)
        out = jnp.zeros((rows, 256), F32)
        for hh in range(NA_HEADS):
            out = jnp.where(head == hh, o[hh * rows:(hh + 1) * rows], out)
        return out

    @pl.when(i == 0)
    def _():
        kc = k_ref[0:n_ctx, :]
        vc = v_ref[0:n_ctx, :]
        p, l = _softmax_rows(_dot_nt(stack_heads(q_ref[...]), kc))
        o = jnp.dot(p.astype(BF16), vc, preferred_element_type=F32) / l
        o_ref[...] = unstack_heads(o, tq).astype(o_ref.dtype)

    @pl.when(i > 0)
    def _():
        kc = k_ref[0:n_ctx, :]
        vc = v_ref[0:n_ctx, :]
        for rr in range(tq // GRID_W):
            r = (i - 1) * (tq // GRID_W) + rr
            rs = jnp.clip(r - NA_KH // 2, 0, n_rows - NA_KH)
            start = pl.multiple_of(n_ctx + rs * GRID_W, GRID_W)
            kcat = jnp.concatenate([kc, k_ref[pl.ds(start, NA_KH * GRID_W), :]], axis=0)
            vcat = jnp.concatenate([vc, v_ref[pl.ds(start, NA_KH * GRID_W), :]], axis=0)
            q = q_ref[rr * GRID_W:(rr + 1) * GRID_W, :]
            s = _dot_nt(stack_heads(q), kcat) + bias_ref[r - rs]
            p, l = _softmax_rows(s)
            o = jnp.dot(p.astype(BF16), vcat, preferred_element_type=F32) / l
            o_ref[rr * GRID_W:(rr + 1) * GRID_W, :] = unstack_heads(o, GRID_W).astype(o_ref.dtype)


def _na_attention(pb, bias, n_ctx):
    b, s, _ = pb.shape
    tq = ROW_TILE
    n_rows = (s - n_ctx) // GRID_W
    return pl.pallas_call(
        functools.partial(_na_kernel, n_ctx=n_ctx, n_rows=n_rows),
        grid=(b, s // tq),
        in_specs=[pl.BlockSpec((None, tq, 256), lambda bi, i: (bi, i, 0)),
                  pl.BlockSpec((None, s, 256), lambda bi, i: (bi, 0, 1)),
                  pl.BlockSpec((None, s, 256), lambda bi, i: (bi, 0, 2)),
                  pl.BlockSpec(bias.shape, lambda bi, i: (0, 0, 0))],
        out_specs=pl.BlockSpec((None, tq, 256), lambda bi, i: (bi, i, 0)),
        out_shape=jax.ShapeDtypeStruct((b, s, 256), BF16),
        compiler_params=_cparams(("arbitrary", "arbitrary")),
        name="na_attention",
    )(pb, pb, pb, bias)


def _na_bias_table(rpb, n_ctx):
    depth = rpb.shape[0]
    pad = GRID_W - NA_KW
    padded = jnp.pad(rpb.astype(F32), ((0, 0), (0, 0), (0, 0), (pad, pad)))
    cols = jnp.stack([padded[..., GRID_W - 1 - w:2 * GRID_W - 1 - w] for w in range(GRID_W)], axis=3)
    vals = jnp.stack([cols[:, :, NA_KH - 1 - o:2 * NA_KH - 1 - o] for o in range(NA_KH)], axis=2)
    vals = jnp.transpose(vals, (0, 2, 1, 4, 3, 5))
    w = np.arange(GRID_W)[:, None, None]
    kc = np.arange(GRID_W)[None, None, :]
    cs = np.clip(w - NA_KW // 2, 0, GRID_W - NA_KW)
    inwin = np.broadcast_to((kc >= cs) & (kc < cs + NA_KW), (GRID_W, NA_KH, GRID_W))
    vals = jnp.where(jnp.asarray(inwin), vals * LOG2E, NEG).reshape(depth, NA_KH, NA_HEADS * GRID_W, NA_KH * GRID_W)
    return jnp.concatenate([jnp.zeros((depth, NA_KH, NA_HEADS * GRID_W, n_ctx), F32), vals], axis=-1)


def _hy_pre_kernel(u_ref, w_ref, b_ref, x0_ref, z_ref, *, bounds, tm):
    s = u_ref.shape[0]
    w0, w1, w2, bias = w_ref[0:1, :], w_ref[1:2, :], w_ref[2:3, :], b_ref[...]
    rid = lax.broadcasted_iota(I32, (tm, u_ref.shape[1]), 0)
    for c0 in range(0, s, tm):
        u = u_ref[c0:c0 + tm, :]
        prev = pltpu.roll(u, 1, 0)
        nxt = pltpu.roll(u, tm - 1, 0)
        first = jnp.zeros_like(w0) if c0 in bounds else u_ref[c0 - 1:c0, :]
        last = jnp.zeros_like(w0) if (c0 + tm) in bounds else u_ref[c0 + tm:c0 + tm + 1, :]
        prev = jnp.where(rid == 0, first, prev)
        nxt = jnp.where(rid == tm - 1, last, nxt)
        uc = prev * w0 + u * w1 + nxt * w2 + bias
        x0_ref[c0:c0 + tm, :] = uc[:, 0:HY_W]
        z_ref[c0:c0 + tm, :] = uc[:, HY_W:2 * HY_W] * uc[:, 2 * HY_W:3 * HY_W]


def _hy_pre(mix, w_short, b_short, n_ctx):
    b, s, _ = mix.shape
    return pl.pallas_call(
        functools.partial(_hy_pre_kernel, bounds=(0, n_ctx, s), tm=ROW_TILE),
        grid=(b,),
        in_specs=[pl.BlockSpec((None, s, 3 * HY_W), lambda bi: (bi, 0, 1)),
                  pl.BlockSpec((3, 3 * HY_W), lambda bi: (0, 0)),
                  pl.BlockSpec((1, 3 * HY_W), lambda bi: (0, 0))],
        out_specs=[pl.BlockSpec((None, s, HY_W), lambda bi: (bi, 0, 0)),
                   pl.BlockSpec((None, s, HY_W), lambda bi: (bi, 0, 0))],
        out_shape=[jax.ShapeDtypeStruct((b, s, HY_W), F32), jax.ShapeDtypeStruct((b, s, HY_W), F32)],
        compiler_params=_cparams(("arbitrary",)),
        name="hyena_pre",
    )(mix, w_short, b_short)


def _hy_filter_kernel(z_ref, w1_ref, b1_ref, w2_ref, b2_ref, w3_ref, b3_ref, w4_ref, f_ref, dec_ref, o_ref):
    def lin(a, w_ref, b_ref):
        return jnp.dot(a, w_ref[...], preferred_element_type=F32, precision=HIGHEST) + b_ref[...]

    hh = jnp.sin(f_ref[0:1, :] * lin(z_ref[...], w1_ref, b1_ref))
    hh = jnp.sin(f_ref[1:2, :] * lin(hh, w2_ref, b2_ref))
    hh = jnp.sin(f_ref[2:3, :] * lin(hh, w3_ref, b3_ref))
    hh = jnp.dot(hh, w4_ref[...], preferred_element_type=F32, precision=HIGHEST) * dec_ref[...]
    o_ref[...] = hh / jnp.sum(jnp.abs(hh), axis=0, keepdims=True)


def _pad_to(a, shape):
    return jnp.pad(a, [(0, t - s) for s, t in zip(a.shape, shape)])


def _hy_filters(n, w1, b1, w2, b2, w3, b3, w4, freq):
    depth = w1.shape[0]
    t = np.linspace(0.0, 1.0, n, dtype=np.float32)[:, None]
    w = np.float32(2.0 * math.pi / n) * np.arange(n, dtype=np.float32)[:, None]
    bands = np.linspace(1e-4, HY_BANDS - 1, HY_BANDS, dtype=np.float32)[None, :]
    z = np.concatenate([t, np.cos(w * bands), np.sin(w * bands)], axis=-1).astype(np.float32)
    z = np.pad(z, ((0, 0), (0, LANE - z.shape[1])))
    deltas = np.linspace(math.log(HY_TARGET) / HY_SLOW_DECAY, math.log(HY_TARGET) / HY_FAST_DECAY, HY_W, dtype=np.float32)
    deltas = np.tile(np.abs(deltas), 2)
    decay = np.exp(-t * deltas[None, :]).astype(np.float32)
    p = LANE
    args = (jnp.asarray(z), _pad_to(w1, (depth, p, p)), _pad_to(b1[:, None, :], (depth, 1, p)),
            _pad_to(w2, (depth, p, p)), _pad_to(b2[:, None, :], (depth, 1, p)),
            _pad_to(w3, (depth, p, p)), _pad_to(b3[:, None, :], (depth, 1, p)),
            _pad_to(w4, (depth, p, 2 * HY_W)), _pad_to(freq, (depth, 8, p)), jnp.asarray(decay))
    per_layer = lambda shp: pl.BlockSpec((None,) + shp, lambda l: (l,) + (0,) * len(shp))
    const = lambda shp: pl.BlockSpec(shp, lambda l: (0,) * len(shp))
    return pl.pallas_call(
        _hy_filter_kernel,
        grid=(depth,),
        in_specs=[const((n, p)), per_layer((p, p)), per_layer((1, p)), per_layer((p, p)), per_layer((1, p)),
                  per_layer((p, p)), per_layer((1, p)), per_layer((p, 2 * HY_W)), per_layer((8, p)),
                  const((n, 2 * HY_W))],
        out_specs=per_layer((n, 2 * HY_W)),
        out_shape=jax.ShapeDtypeStruct((depth, n, 2 * HY_W), F32),
        compiler_params=_cparams(("arbitrary",)),
        name="hyena_filters",
    )(*args)


def _conv_taps(filt):
    hf, hb = filt[..., :HY_W], filt[..., HY_W:]
    taps = jnp.concatenate([hb[:, :0:-1], (hf[:, 0:1] + hb[:, 0:1]), hf[:, 1:], jnp.zeros_like(hf[:, 0:1])], axis=1)
    return jnp.transpose(taps, (0, 2, 1))


def _hy_conv_kernel(z_ref, k_ref, o_ref, *, nblk, nb):
    cg = z_ref.shape[0]
    n2 = k_ref.shape[1]
    ncols = (2 * nblk - 1) * LANE

    def body(ci, _):
        krow = k_ref[pl.ds(ci, 1), :]
        kb = jnp.broadcast_to(krow, (LANE, n2))
        big = pltpu.roll(kb, n2 - (LANE - 1), 1, stride=1, stride_axis=0)[:, :ncols].astype(BF16)
        o_ref[ci] = jnp.zeros(o_ref.shape[1:], F32)
        for m in range(-(nblk - 1), nblk):
            km = big[:, (m + nblk - 1) * LANE:(m + nblk) * LANE]
            cnt = (nblk - abs(m)) * nb
            src = 0 if m >= 0 else -m * nb
            dst = m * nb if m >= 0 else 0
            zin = z_ref[ci, src:src + cnt, :]
            o_ref[ci, dst:dst + cnt, :] += jnp.dot(zin, km, preferred_element_type=F32)
        return 0

    lax.fori_loop(0, cg, body, 0)


def _hy_conv(zr, taps, nblk, nb):
    c, r, _ = zr.shape
    cg = 8
    return pl.pallas_call(
        functools.partial(_hy_conv_kernel, nblk=nblk, nb=nb),
        grid=(c // cg,),
        in_specs=[pl.BlockSpec((cg, r, LANE), lambda i: (i, 0, 0)),
                  pl.BlockSpec((cg, taps.shape[1]), lambda i: (i, 0))],
        out_specs=pl.BlockSpec((cg, r, LANE), lambda i: (i, 0, 0)),
        out_shape=jax.ShapeDtypeStruct((c, r, LANE), F32),
        compiler_params=_cparams(("arbitrary",)),
        name="hyena_conv",
    )(zr, taps)


def _hy_long_conv(z, taps):
    b, n, c = z.shape
    nblk = n // LANE
    zr = jnp.transpose(z.astype(BF16).reshape(b, nblk, LANE, c), (3, 1, 0, 2)).reshape(c, nblk * b, LANE)
    y = _hy_conv(zr, taps, nblk, b)
    return jnp.transpose(y.reshape(c, nblk, b, LANE), (2, 1, 3, 0)).reshape(b, n, c)


def _merge_kernel(h_ref, mod_ref, ya_ref, x0_ref, z_ref, yconv_ref, hb_ref, yc_ref, yd_ref, gate_ref,
                  wb_ref, wo_ref, o_ref, *, n_ctx, tm):
    i = pl.program_id(1)
    yb = (x0_ref[...] * (yconv_ref[...] + z_ref[...] * hb_ref[...])).astype(BF16)
    ys = (ya_ref[...], yb, yc_ref[...], yd_ref[...])
    acc = jnp.zeros(h_ref.shape, F32)
    d = h_ref.shape[1]
    for k in range(N_BRANCH):
        proj = jnp.dot(ys[k], wb_ref[k], preferred_element_type=F32)
        acc = acc + jax.nn.sigmoid(gate_ref[:, k * d:(k + 1) * d].astype(F32)) * proj
    out = jnp.dot(acc.astype(BF16), wo_ref[...], preferred_element_type=F32)
    row = i * tm + lax.broadcasted_iota(I32, out.shape, 0)
    gate = jnp.where(row < n_ctx, mod_ref[0, 2:3, :], mod_ref[1, 2:3, :])
    o_ref[...] = h_ref[...] + gate * out


def _merge(h, modl, ya, x0, z, yconv, hbias, yc, yd, gates, wb, wo, n_ctx):
    b, s, d = h.shape
    tm = ROW_TILE
    row = lambda w: pl.BlockSpec((None, tm, w), lambda bi, i: (bi, i, 0))
    return pl.pallas_call(
        functools.partial(_merge_kernel, n_ctx=n_ctx, tm=tm),
        grid=(b, s // tm),
        in_specs=[row(d), pl.BlockSpec((None, 2, 8, d), lambda bi, i: (bi, 0, 0, 0)),
                  row(256), row(256), row(256), row(256), pl.BlockSpec((1, 256), lambda bi, i: (0, 0)),
                  row(256), row(256), row(GATE_W),
                  pl.BlockSpec(wb.shape, lambda bi, i: (0, 0, 0)),
                  pl.BlockSpec(wo.shape, lambda bi, i: (0, 0))],
        out_specs=row(d),
        out_shape=jax.ShapeDtypeStruct((b, s, d), F32),
        compiler_params=_cparams(("arbitrary", "arbitrary")),
        name="merge",
    )(h, modl, ya, x0, z, yconv, hbias, yc, yd, gates, wb, wo)


def _topk_rows(s, label, k):
    vals, labs = [], []
    for _ in range(k):
        m = jnp.max(s, axis=0, keepdims=True)
        lb = jnp.min(jnp.where(s == m, label, float(2 ** 20)), axis=0, keepdims=True)
        vals.append(m)
        labs.append(lb)
        s = jnp.where(label == lb, -jnp.inf, s)
    return jnp.concatenate(vals, axis=0), jnp.concatenate(labs, axis=0).astype(I32)


_CAND_FIXED_A = ((0, 0), (0, 8), (1, 0), (2, 0), (3, 0))
_CAND_FIXED_B = ((0, 0), (0, 8), (1, 0), (2, 0))


def _select_rows(table, pos):
    out = jnp.zeros(pos.shape, table.dtype)
    for a in range(table.shape[0]):
        out = jnp.where(pos == a, table[a:a + 1, :], out)
    return out


def _peer_route_kernel(q_ref, keys_ref, i_ref, j_ref, g_ref):
    st = _dot_nt(keys_ref[...], q_ref[...].astype(BF16))
    t = st.shape[1]
    key_id = lax.broadcasted_iota(I32, (PEER_NKEYS, t), 0).astype(F32)
    sv1, si1 = _topk_rows(st[0:PEER_NKEYS], key_id, PEER_TOPK)
    sv2, si2 = _topk_rows(st[PEER_NKEYS:2 * PEER_NKEYS], key_id, PEER_TOPK)
    r8 = lax.broadcasted_iota(I32, (8, t), 0)
    cands, labels = [], []
    for a, b0 in _CAND_FIXED_A:
        cands.append(sv1[a:a + 1, :] + sv2[b0:b0 + 8, :])
        labels.append((a * PEER_TOPK + b0 + r8).astype(F32))
    for b, a0 in _CAND_FIXED_B:
        c = sv1[a0:a0 + 8, :] + sv2[b:b + 1, :]
        cands.append(jnp.where(r8 < 4, -jnp.inf, c) if a0 == 0 else c)
        labels.append(((a0 + r8) * PEER_TOPK + b).astype(F32))
    best, pos = _topk_rows(jnp.concatenate(cands, axis=0), jnp.concatenate(labels, axis=0), PEER_TOPK)
    i_ref[...] = _select_rows(si1, pos >> int(math.log2(PEER_TOPK)))
    j_ref[...] = _select_rows(si2, pos & (PEER_TOPK - 1))
    e = jnp.exp(best - jnp.max(best, axis=0, keepdims=True))
    g_ref[...] = e / jnp.sum(e, axis=0, keepdims=True)


def _peer_route(q, keys_blk):
    t, _ = q.shape
    tt = 256
    out = pl.BlockSpec((None, PEER_TOPK, tt), lambda ti, hh: (hh, 0, ti))
    shp = (PEER_HEADS, PEER_TOPK, t)
    return pl.pallas_call(
        _peer_route_kernel,
        grid=(t // tt, PEER_HEADS),
        in_specs=[pl.BlockSpec((tt, PEER_DK), lambda ti, hh: (ti, hh)),
                  pl.BlockSpec((None, 2 * PEER_NKEYS, PEER_DK), lambda ti, hh: (hh, 0, 0))],
        out_specs=[out, out, out],
        out_shape=[jax.ShapeDtypeStruct(shp, I32), jax.ShapeDtypeStruct(shp, I32), jax.ShapeDtypeStruct(shp, F32)],
        compiler_params=_cparams(("arbitrary", "arbitrary")),
        name="peer_route",
    )(q, keys_blk)


GS_HALF = PEER_NKEYS // 2
GS_PITCH = GS_HALF + 1
GS_UNROLL = 8
PEER_TM = 576
PEER_TE = 1024
U32 = jnp.uint32


def _peer_expert_kernel(h_ref, mod_ref, x_ref, i_ref, j_ref, g_ref, ut_ref, v_ref, o_ref, gs_ref, acc_ref, m_ref,
                        *, n_ctx, tm):
    ti, e = pl.program_id(1), pl.program_id(2)
    n_e = pl.num_programs(2)
    te = ut_ref.shape[1]
    nk = PEER_NKEYS

    @pl.when(e == 0)
    def _():
        sub = lax.broadcasted_iota(I32, (nk, nk), 0)

        def build(t):
            irow = i_ref[pl.ds(t, 1), :]
            jrow = j_ref[pl.ds(t, 1), :]
            grow = g_ref[pl.ds(t, 1), :]
            at = jnp.where(sub == irow, 1.0, 0.0).astype(BF16)
            bt = jnp.where(sub == jrow, grow, 0.0).astype(BF16)
            gt = _dot_nt(at, bt).astype(BF16).astype(F32)
            lo = pltpu.bitcast(gt[0:GS_HALF], U32) >> 16
            hi = pltpu.bitcast(gt[GS_HALF:nk], U32)
            gs_ref[pl.ds(t * GS_PITCH, GS_HALF), :] = hi | lo

        def body(tb, _):
            for u in range(GS_UNROLL):
                build(tb * GS_UNROLL + u)
            return 0

        lax.fori_loop(0, tm // GS_UNROLL, body, 0)

    nb = te // nk // 2

    def gelu(a):
        return 0.5 * a * (1.0 + lax.erf(a * (2.0 ** -0.5)))

    def up_stage():
        hid = jnp.dot(x_ref[...], ut_ref[...], preferred_element_type=F32)
        slot = e % 2
        for k in range(nb):
            word = gs_ref[pl.ds(e * nb + k, tm, stride=GS_PITCH), :]
            g_lo = pltpu.bitcast(word << 16, F32)
            g_hi = pltpu.bitcast(word & jnp.uint32(0xFFFF0000), F32)
            m_ref[slot, :, k * nk:(k + 1) * nk] = (g_lo * gelu(hid[:, k * nk:(k + 1) * nk])).astype(BF16)
            m_ref[slot, :, (nb + k) * nk:(nb + k + 1) * nk] = (
                g_hi * gelu(hid[:, (nb + k) * nk:(nb + k + 1) * nk])).astype(BF16)

    def down_stage():
        return jnp.dot(m_ref[(e + 1) % 2], v_ref[...], preferred_element_type=F32)

    @pl.when(e == 0)
    def _():
        up_stage()

    @pl.when(e == 1)
    def _():
        up_stage()
        acc_ref[...] = down_stage()

    @pl.when((e > 1) & (e < n_e - 1))
    def _():
        up_stage()
        acc_ref[...] += down_stage()

    @pl.when(e == n_e - 1)
    def _():
        row = ti * tm + lax.broadcasted_iota(I32, acc_ref.shape, 0)
        gate = jnp.where(row < n_ctx, mod_ref[0, 5:6, :], mod_ref[1, 5:6, :])
        o_ref[...] = h_ref[...] + gate * (acc_ref[...] + down_stage())


def _peer_experts(h, modl, xb, i_idx, j_idx, g, ut, v, n_ctx):
    b, s, d = h.shape
    tm, te = PEER_TM, PEER_TE
    n_chunks = ut.shape[1] // te
    slots = PEER_HEADS * PEER_TOPK
    row = lambda w: pl.BlockSpec((None, tm, w), lambda bi, i, e: (bi, i, 0))
    return pl.pallas_call(
        functools.partial(_peer_expert_kernel, n_ctx=n_ctx, tm=tm),
        grid=(b, s // tm, n_chunks + 1),
        in_specs=[row(d), pl.BlockSpec((None, 2, 8, d), lambda bi, i, e: (bi, 0, 0, 0)),
                  row(d), row(slots), row(slots), row(slots),
                  pl.BlockSpec((d, te), lambda bi, i, e: (0, jnp.minimum(e, n_chunks - 1))),
                  pl.BlockSpec((te, d), lambda bi, i, e: (jnp.maximum(e - 1, 0), 0))],
        out_specs=row(d),
        out_shape=jax.ShapeDtypeStruct((b, s, d), F32),
        scratch_shapes=[pltpu.VMEM((tm * GS_PITCH, PEER_NKEYS), U32), pltpu.VMEM((tm, d), F32),
                        pltpu.VMEM((2, tm, te), BF16)],
        compiler_params=_cparams(("arbitrary", "arbitrary", "arbitrary")),
        name="peer_experts",
    )(h, modl, xb, i_idx, j_idx, g, ut, v)


def _final_norm_kernel(x_ref, g_ref, o_ref):
    x = x_ref[...]
    o_ref[...] = x * lax.rsqrt(jnp.mean(x * x, axis=-1, keepdims=True) + EPS) * g_ref[...]


def _final_norm(h, g, n_ctx):
    b, s, d = h.shape
    tm = ROW_TILE
    n_lat = s - n_ctx
    off = n_ctx // tm
    return pl.pallas_call(
        _final_norm_kernel,
        grid=(b, n_lat // tm),
        in_specs=[pl.BlockSpec((None, tm, d), lambda bi, i: (bi, i + off, 0)),
                  pl.BlockSpec((1, d), lambda bi, i: (0, 0))],
        out_specs=pl.BlockSpec((None, tm, d), lambda bi, i: (bi, i, 0)),
        out_shape=jax.ShapeDtypeStruct((b, n_lat, d), F32),
        compiler_params=_cparams(("arbitrary", "arbitrary")),
        name="final_norm",
    )(h, g)


def _prep_w_in(w_in):
    depth, d, _ = w_in.shape
    mixw = w_in[:, :, :2816]
    gq = mixw[:, :, 1536:1792].reshape(depth, d, GQA_HEADS, GQA_DH)[:, :, (0, 2, 1, 3), :].reshape(depth, d, 256)
    mixw = jnp.concatenate([mixw[:, :, :1536], gq, mixw[:, :, 1792:], jnp.zeros((depth, d, MIX_W - 2816), w_in.dtype)], axis=-1)
    return jnp.concatenate([mixw, w_in[:, :, 2816:]], axis=-1).astype(BF16)


def _peer_chunk_order(w):
    n_exp, d = w.shape
    nb = PEER_TE // PEER_NKEYS // 2
    return w.reshape(2, GS_HALF // nb, nb, PEER_NKEYS, d).transpose(1, 0, 2, 3, 4).reshape(n_exp, d)


def _prep_peer_keys(keys):
    depth = keys.shape[0]
    half = PEER_DK // 2
    z = jnp.zeros((depth, PEER_HEADS, PEER_NKEYS, half), keys.dtype)
    top = jnp.concatenate([keys[:, :, 0], z], axis=-1)
    bot = jnp.concatenate([z, keys[:, :, 1]], axis=-1)
    return jnp.concatenate([top, bot], axis=2).astype(BF16)


def kernel(x, c, ctx, c_ctx, w_mod, b_mod, norm1_g, norm2_g, w_in, na_rpb, hy_short_w, hy_short_b, hy_w1, hy_b1, hy_w2, hy_b2, hy_w3, hy_b3, hy_w4, hy_freq, hy_bias, gqa_qn, gqa_kn, diff_lq1, diff_lk1, diff_lq2, diff_lk2, diff_subln, w_branch, w_out, peer_wq, peer_keys, peer_u, peer_v, final_g):
    B, L, D = x.shape
    C = ctx.shape[1]
    S = C + L
    depth = w_mod.shape[0]
    assert C == ROW_TILE and L % ROW_TILE == 0 and D == 1024 and L % GRID_W == 0

    h = jnp.concatenate([ctx, x], axis=1)

    r = -(-(B + 1) // 8) * 8
    cvec = jnp.zeros((r, D), F32).at[:B].set(c).at[B].set(c_ctx)
    modall = _modulation(cvec, w_mod, b_mod)
    mod_lat = modall[:, :B].reshape(depth, B, 1, 6, D)
    mod_ctx = jnp.broadcast_to(modall[:, B].reshape(depth, 1, 1, 6, D), (depth, B, 1, 6, D))
    mods = jnp.pad(jnp.concatenate([mod_ctx, mod_lat], axis=2), ((0, 0), (0, 0), (0, 0), (0, 2), (0, 0)))

    fargs = (hy_w1, hy_b1, hy_w2, hy_b2, hy_w3, hy_b3, hy_w4, hy_freq)
    taps_lat = _conv_taps(_hy_filters(L, *fargs))
    taps_ctx = _conv_taps(_hy_filters(C, *fargs))

    cg, sg = _rope_tables(C, L, GQA_DH, 256)
    cd, sd = _rope_tables(C, L, DIFF_DH, 256)

    lam_init = jnp.asarray([0.8 - 0.6 * math.exp(-0.3 * l) for l in range(depth)], F32)
    lam = (jnp.exp(jnp.sum(diff_lq1.astype(F32) * diff_lk1.astype(F32), axis=-1))
           - jnp.exp(jnp.sum(diff_lq2.astype(F32) * diff_lk2.astype(F32), axis=-1)) + lam_init)
    diff_par = jnp.zeros((depth, 8, LANE), F32).at[:, 0, :].set(lam[:, None]).at[:, 1, :].set(1.0 - lam_init[:, None])

    wb = w_branch.astype(BF16)
    wb = wb.at[:, 2].set(wb[:, 2].reshape(depth, GQA_HEADS, GQA_DH, D)[:, (0, 2, 1, 3)].reshape(depth, BRANCH_W, D))

    xs = dict(
        mods=mods, n1=norm1_g[:, None, :], n2=norm2_g[:, None, :], w_in=_prep_w_in(w_in),
        bias=_na_bias_table(na_rpb, C), sw=hy_short_w, sb=hy_short_b[:, None, :], hbias=hy_bias[:, None, :],
        taps_lat=taps_lat, taps_ctx=taps_ctx,
        qn=jnp.tile(gqa_qn, (1, GQA_HEADS))[:, None, :], kn=jnp.tile(gqa_kn, (1, GQA_KV))[:, None, :],
        diff_par=diff_par, subln=jnp.tile(diff_subln, (1, DIFF_HEADS))[:, None, :],
        wb=wb, wo=w_out.astype(BF16), wq=peer_wq.astype(BF16), keys=_prep_peer_keys(peer_keys),
        u=peer_u, v=peer_v,
    )

    def layer(h, p):
        mix, gates = _in_proj(h, p["mods"], p["n1"], p["w_in"], C)
        pb = _prep(mix, cg, sg, cd, sd, p["qn"], p["kn"])
        ya = _na_attention(pb, p["bias"], C)
        yc = _gqa_attention(pb, C)
        yd = _diff_attention(pb, p["diff_par"], p["subln"], C)
        x0, z = _hy_pre(mix, p["sw"], p["sb"], C)
        yconv = jnp.concatenate([_hy_long_conv(z[:, :C], p["taps_ctx"]), _hy_long_conv(z[:, C:], p["taps_lat"])], axis=1)
        h = _merge(h, p["mods"], ya, x0, z, yconv, p["hbias"], yc, yd, gates, p["wb"], p["wo"], C)

        q, xb = _peer_q(h, p["mods"], p["n2"], p["wq"], C)
        i_idx, j_idx, g = _peer_route(q.reshape(B * S, -1), p["keys"])
        slots = PEER_HEADS * PEER_TOPK
        to_rows = lambda a: jnp.transpose(a, (2, 0, 1)).reshape(B, S, slots)
        ut = jnp.transpose(_peer_chunk_order(p["u"].astype(BF16)))
        h = _peer_experts(h, p["mods"], xb, to_rows(i_idx), to_rows(j_idx), to_rows(g), ut,
                          _peer_chunk_order(p["v"].astype(BF16)), C)
        return h, None

    h, _ = lax.scan(layer, h, xs)
    return _final_norm(h, final_g[None, :], C)
```

```python
import functools
import math

import numpy as np
import jax
import jax.numpy as jnp
from jax import lax
from jax.experimental import pallas as pl
from jax.experimental.pallas import tpu as pltpu

F32 = jnp.float32
BF16 = jnp.bfloat16
I32 = jnp.int32
HIGHEST = lax.Precision.HIGHEST

EPS = 1e-6
GRID_W = 64
ROPE_THETA = 10000.0
NA_HEADS, NA_DH, NA_KH, NA_KW = 4, 64, 8, 16
HY_W, HY_BANDS, HY_FFN = 256, 16, 64
HY_FAST_DECAY, HY_SLOW_DECAY, HY_TARGET = 0.3, 1.5, 1e-2
GQA_HEADS, GQA_KV, GQA_DH = 4, 2, 64
DIFF_HEADS, DIFF_DH, DIFF_DV = 4, 32, 64
N_BRANCH, BRANCH_W = 4, 256
PEER_HEADS, PEER_NKEYS, PEER_DK, PEER_TOPK = 8, 128, 128, 16

LANE = 128
ROW_TILE = 256
VMEM_LIMIT = 56 * 1024 * 1024
NEG = -1e30
LOG2E = math.log2(math.e)
MIX_W = 3072
GATE_W = N_BRANCH * 1024
COL_TILE = 1024


def _cparams(sem):
    return pltpu.CompilerParams(dimension_semantics=sem, vmem_limit_bytes=VMEM_LIMIT)


def _mod_kernel(c_ref, w_ref, b_ref, o_ref):
    cv = c_ref[...]
    a = cv * jax.nn.sigmoid(cv)
    o_ref[...] = jnp.dot(a, w_ref[...], preferred_element_type=F32, precision=HIGHEST) + b_ref[...]


def _modulation(cvec, w_mod, b_mod):
    depth, d, n = w_mod.shape
    r = cvec.shape[0]
    tn = 1024
    return pl.pallas_call(
        _mod_kernel,
        grid=(depth, n // tn),
        in_specs=[pl.BlockSpec((r, d), lambda l, j: (0, 0)),
                  pl.BlockSpec((None, d, tn), lambda l, j: (l, 0, j)),
                  pl.BlockSpec((None, 1, tn), lambda l, j: (l, 0, j))],
        out_specs=pl.BlockSpec((None, r, tn), lambda l, j: (l, 0, j)),
        out_shape=jax.ShapeDtypeStruct((depth, r, n), F32),
        compiler_params=_cparams(("arbitrary", "arbitrary")),
        name="modulation",
    )(cvec, w_mod, b_mod.reshape(depth, 1, n))


def _norm_modulate(h_ref, mod_ref, g_ref, srow, row0, n_ctx):
    x = h_ref[...]
    y = x * lax.rsqrt(jnp.mean(x * x, axis=-1, keepdims=True) + EPS) * g_ref[...]
    row = row0 + lax.broadcasted_iota(I32, x.shape, 0)
    is_ctx = row < n_ctx
    shift = jnp.where(is_ctx, mod_ref[0, srow:srow + 1, :], mod_ref[1, srow:srow + 1, :])
    scale = jnp.where(is_ctx, mod_ref[0, srow + 1:srow + 2, :], mod_ref[1, srow + 1:srow + 2, :])
    return y * (1.0 + scale) + shift


def _in_proj_kernel(h_ref, mod_ref, g_ref, w_ref, mix_ref, gate_ref, nb_ref, *, tm, n_ctx, n_mix):
    i, j = pl.program_id(1), pl.program_id(2)

    @pl.when(j == 0)
    def _():
        nb_ref[...] = _norm_modulate(h_ref, mod_ref, g_ref, 0, i * tm, n_ctx).astype(BF16)

    res = jnp.dot(nb_ref[...], w_ref[...], preferred_element_type=F32)

    @pl.when(j < n_mix)
    def _():
        mix_ref[...] = res

    @pl.when(j >= n_mix)
    def _():
        gate_ref[...] = res.astype(gate_ref.dtype)


def _in_proj(h, modl, g, w, n_ctx):
    b, s, d = h.shape
    tm, tn = 768, COL_TILE
    n_mix = MIX_W // tn
    n_tot = (MIX_W + GATE_W) // tn
    return pl.pallas_call(
        functools.partial(_in_proj_kernel, tm=tm, n_ctx=n_ctx, n_mix=n_mix),
        grid=(b, s // tm, n_tot),
        in_specs=[pl.BlockSpec((None, tm, d), lambda bi, i, j: (bi, i, 0)),
                  pl.BlockSpec((None, 2, 8, d), lambda bi, i, j: (bi, 0, 0, 0)),
                  pl.BlockSpec((1, d), lambda bi, i, j: (0, 0)),
                  pl.BlockSpec((d, tn), lambda bi, i, j: (0, j))],
        out_specs=[pl.BlockSpec((None, tm, tn), lambda bi, i, j: (bi, i, jnp.minimum(j, n_mix - 1))),
                   pl.BlockSpec((None, tm, tn), lambda bi, i, j: (bi, i, jnp.maximum(j - n_mix, 0)))],
        out_shape=[jax.ShapeDtypeStruct((b, s, MIX_W), F32), jax.ShapeDtypeStruct((b, s, GATE_W), BF16)],
        scratch_shapes=[pltpu.VMEM((tm, d), BF16)],
        compiler_params=_cparams(("arbitrary", "arbitrary", "arbitrary")),
        name="in_proj",
    )(h, modl, g, w)


def _peer_q_kernel(h_ref, mod_ref, g_ref, w_ref, q_ref, nb_ref, *, tm, n_ctx):
    i = pl.program_id(1)
    nb = _norm_modulate(h_ref, mod_ref, g_ref, 3, i * tm, n_ctx).astype(BF16)
    nb_ref[...] = nb
    q_ref[...] = jnp.dot(nb, w_ref[...], preferred_element_type=F32)


def _peer_q(h, modl, g, wq, n_ctx):
    b, s, d = h.shape
    tm = 768
    n = wq.shape[1]
    return pl.pallas_call(
        functools.partial(_peer_q_kernel, tm=tm, n_ctx=n_ctx),
        grid=(b, s // tm),
        in_specs=[pl.BlockSpec((None, tm, d), lambda bi, i: (bi, i, 0)),
                  pl.BlockSpec((None, 2, 8, d), lambda bi, i: (bi, 0, 0, 0)),
                  pl.BlockSpec((1, d), lambda bi, i: (0, 0)),
                  pl.BlockSpec((d, n), lambda bi, i: (0, 0))],
        out_specs=[pl.BlockSpec((None, tm, n), lambda bi, i: (bi, i, 0)),
                   pl.BlockSpec((None, tm, d), lambda bi, i: (bi, i, 0))],
        out_shape=[jax.ShapeDtypeStruct((b, s, n), F32), jax.ShapeDtypeStruct((b, s, d), BF16)],
        compiler_params=_cparams(("arbitrary", "arbitrary")),
        name="peer_q",
    )(h, modl, g, wq)


def _lane_group(shape, width):
    return lax.broadcasted_iota(I32, shape, 1) >> int(math.log2(width))


def _head_mean_sq(x, n_heads, dh):
    hid = _lane_group(x.shape, dh)
    x2 = x * x
    ms = jnp.zeros_like(x)
    for hh in range(n_heads):
        m = hid == hh
        s = jnp.sum(jnp.where(m, x2, 0.0), axis=-1, keepdims=True)
        ms = jnp.where(m, s, ms)
    return ms * (1.0 / dh)


def _rope(x, cos, sin_signed, qs):
    w = x.shape[-1]
    lane = lax.broadcasted_iota(I32, x.shape, 1)
    lo = (lane & (2 * qs - 1)) < qs
    partner = jnp.where(lo, pltpu.roll(x, w - qs, 1), pltpu.roll(x, qs, 1))
    return x * cos + partner * sin_signed


def _prep_kernel(na_ref, gqa_ref, dq_ref, dk_ref, dv_ref, cg_ref, sg_ref, cd_ref, sd_ref, qn_ref, kn_ref, o_ref):
    na = na_ref[...]
    o_ref[:, 0:256] = (na[:, 0:256] * (LOG2E * NA_DH ** -0.5)).astype(BF16)
    o_ref[:, 256:768] = na[:, 256:768].astype(BF16)
    gq = gqa_ref[:, 0:256]
    gk = gqa_ref[:, 256:384]
    cg, sg = cg_ref[...], sg_ref[...]
    gq = gq * lax.rsqrt(_head_mean_sq(gq, GQA_HEADS, GQA_DH) + EPS) * qn_ref[...]
    gk = gk * lax.rsqrt(_head_mean_sq(gk, GQA_KV, GQA_DH) + EPS) * kn_ref[...]
    o_ref[:, 768:1024] = (_rope(gq, cg, sg, GQA_DH // 4) * (LOG2E * GQA_DH ** -0.5)).astype(BF16)
    o_ref[:, 1024:1152] = _rope(gk, cg[:, 0:128], sg[:, 0:128], GQA_DH // 4).astype(BF16)
    o_ref[:, 1152:1280] = gqa_ref[:, 384:512].astype(BF16)
    cd, sd = cd_ref[...], sd_ref[...]
    o_ref[:, 1280:1536] = (_rope(dq_ref[...], cd, sd, DIFF_DH // 4) * (LOG2E * DIFF_DH ** -0.5)).astype(BF16)
    o_ref[:, 1536:1792] = _rope(dk_ref[...], cd, sd, DIFF_DH // 4).astype(BF16)
    o_ref[:, 1792:2048] = dv_ref[...].astype(BF16)


def _prep(mix, cg, sg, cd, sd, qn, kn):
    b, s, _ = mix.shape
    tm = ROW_TILE
    tab = pl.BlockSpec((tm, 256), lambda bi, i: (i, 0))
    return pl.pallas_call(
        _prep_kernel,
        grid=(b, s // tm),
        in_specs=[pl.BlockSpec((None, tm, 768), lambda bi, i: (bi, i, 0)),
                  pl.BlockSpec((None, tm, 512), lambda bi, i: (bi, i, 3)),
                  pl.BlockSpec((None, tm, 256), lambda bi, i: (bi, i, 8)),
                  pl.BlockSpec((None, tm, 256), lambda bi, i: (bi, i, 9)),
                  pl.BlockSpec((None, tm, 256), lambda bi, i: (bi, i, 10)),
                  tab, tab, tab, tab,
                  pl.BlockSpec((1, 256), lambda bi, i: (0, 0)),
                  pl.BlockSpec((1, 128), lambda bi, i: (0, 0))],
        out_specs=pl.BlockSpec((None, tm, 2048), lambda bi, i: (bi, i, 0)),
        out_shape=jax.ShapeDtypeStruct((b, s, 2048), BF16),
        compiler_params=_cparams(("arbitrary", "arbitrary")),
        name="qk_prep",
    )(mix, mix, mix, mix, mix, cg, sg, cd, sd, qn, kn)


def _rope_tables(n_ctx, n_lat, dh, width):
    qs = dh // 4
    t = np.arange(n_lat)
    rows, cols = (t // GRID_W).astype(np.float64), (t % GRID_W).astype(np.float64)
    lane = np.arange(width) % dh
    part = lane // (2 * qs)
    u = lane % (2 * qs)
    f = u % qs
    lo = u < qs
    freqs = ROPE_THETA ** (-f.astype(np.float64) / qs)
    pos = np.where(part[None, :] == 0, rows[:, None], cols[:, None])
    ang = (pos.astype(np.float32) * freqs.astype(np.float32)[None, :]).astype(np.float32)
    cos = np.cos(ang.astype(np.float64))
    sin = np.sin(ang.astype(np.float64)) * np.where(lo, -1.0, 1.0)[None, :]
    cos = np.concatenate([np.ones((n_ctx, width)), cos], axis=0)
    sin = np.concatenate([np.zeros((n_ctx, width)), sin], axis=0)
    return jnp.asarray(cos, F32), jnp.asarray(sin, F32)


def _softmax_rows(s):
    m = jnp.max(s, axis=-1, keepdims=True)
    p = jnp.exp2(s - m)
    return p, jnp.sum(p, axis=-1, keepdims=True)


def _dot_nt(a, b):
    return lax.dot_general(a, b, (((1,), (1,)), ((), ())), preferred_element_type=F32)


def _gqa_kernel(q_ref, k_ref, v_ref, o_ref, *, n_ctx):
    i = pl.program_id(1)

    def run(nk):
        k = k_ref[0:nk, :]
        v = v_ref[0:nk, :]
        q = q_ref[...]
        tq = q.shape[0]
        grp = _lane_group((tq, LANE), GQA_DH)

        def scores(g):
            mask = grp == g
            ql = jnp.concatenate([jnp.where(mask, q[:, 0:128], 0), jnp.where(mask, q[:, 128:256], 0)], axis=0)
            return _dot_nt(ql, k)

        outs = []
        s_next = scores(0)
        for g in range(GQA_KV):
            s = s_next
            if g + 1 < GQA_KV:
                s_next = scores(g + 1)
            p, l = _softmax_rows(s)
            outs.append(jnp.dot(p.astype(BF16), v, preferred_element_type=F32) / l)
        for half in range(2):
            sel = jnp.where(grp == 0, outs[0][half * tq:(half + 1) * tq], outs[1][half * tq:(half + 1) * tq])
            o_ref[:, half * 128:(half + 1) * 128] = sel.astype(o_ref.dtype)

    @pl.when(i == 0)
    def _():
        run(n_ctx)

    @pl.when(i > 0)
    def _():
        run(k_ref.shape[0])


def _gqa_attention(pb, n_ctx):
    b, s, _ = pb.shape
    tq = ROW_TILE
    return pl.pallas_call(
        functools.partial(_gqa_kernel, n_ctx=n_ctx),
        grid=(b, s // tq),
        in_specs=[pl.BlockSpec((None, tq, 256), lambda bi, i: (bi, i, 3)),
                  pl.BlockSpec((None, s, 128), lambda bi, i: (bi, 0, 8)),
                  pl.BlockSpec((None, s, 128), lambda bi, i: (bi, 0, 9))],
        out_specs=pl.BlockSpec((None, tq, 256), lambda bi, i: (bi, i, 0)),
        out_shape=jax.ShapeDtypeStruct((b, s, 256), BF16),
        compiler_params=_cparams(("arbitrary", "arbitrary")),
        name="gqa_attention",
    )(pb, pb, pb)


def _diff_kernel(q_ref, k_ref, v_ref, par_ref, g_ref, o_ref, *, n_ctx):
    i = pl.program_id(1)

    def run(nk):
        k = k_ref[0:nk, :]
        v = v_ref[0:nk, :]
        q = q_ref[...]
        tq = q.shape[0]
        lam = par_ref[0:1, 0:1]
        comp = _lane_group(q.shape, DIFF_DH)
        head = _lane_group(q.shape, DIFF_DV)
        acc = jnp.zeros(q.shape, F32)

        def scores(hh):
            ql = jnp.concatenate([jnp.where(comp == 2 * hh, q, 0), jnp.where(comp == 2 * hh + 1, q, 0)], axis=0)
            return _dot_nt(ql, k)

        s_next = scores(0)
        for hh in range(DIFF_HEADS):
            s = s_next
            if hh + 1 < DIFF_HEADS:
                s_next = scores(hh + 1)
            p, l = _softmax_rows(s)
            inv = 1.0 / l
            pc = (p[0:tq] * inv[0:tq] - p[tq:2 * tq] * (lam * inv[tq:2 * tq])).astype(BF16)
            o = jnp.dot(pc, v, preferred_element_type=F32)
            acc = jnp.where(head == hh, o, acc)
        y = acc * lax.rsqrt(_head_mean_sq(acc, DIFF_HEADS, DIFF_DV) + EPS) * g_ref[...]
        o_ref[...] = (y * par_ref[1:2, 0:1]).astype(o_ref.dtype)

    @pl.when(i == 0)
    def _():
        run(n_ctx)

    @pl.when(i > 0)
    def _():
        run(k_ref.shape[0])


def _diff_attention(pb, par, subln, n_ctx):
    b, s, _ = pb.shape
    tq = ROW_TILE
    return pl.pallas_call(
        functools.partial(_diff_kernel, n_ctx=n_ctx),
        grid=(b, s // tq),
        in_specs=[pl.BlockSpec((None, tq, 256), lambda bi, i: (bi, i, 5)),
                  pl.BlockSpec((None, s, 256), lambda bi, i: (bi, 0, 6)),
                  pl.BlockSpec((None, s, 256), lambda bi, i: (bi, 0, 7)),
                  pl.BlockSpec((8, 128), lambda bi, i: (0, 0)),
                  pl.BlockSpec((1, 256), lambda bi, i: (0, 0))],
        out_specs=pl.BlockSpec((None, tq, 256), lambda bi, i: (bi, i, 0)),
        out_shape=jax.ShapeDtypeStruct((b, s, 256), BF16),
        compiler_params=_cparams(("arbitrary", "arbitrary")),
        name="diff_attention",
    )(pb, pb, pb, par, subln)


def _na_kernel(q_ref, k_ref, v_ref, bias_ref, o_ref, *, n_ctx, n_rows):
    i = pl.program_id(1)
    tq = q_ref.shape[0]

    def stack_heads(q):
        head = _lane_group(q.shape, NA_DH)
        return jnp.concatenate([jnp.where(head == hh, q, 0) for hh in range(NA_HEADS)], axis=0)

    def unstack_heads(o, rows):
        head = _lane_group((rows, 256), NA_DH)
        out = jnp.zeros((rows, 256), F32)
        for hh in range(NA_HEADS):
            out = jnp.where(head == hh, o[hh * rows:(hh + 1) * rows], out)
        return out

    @pl.when(i == 0)
    def _():
        kc = k_ref[0:n_ctx, :]
        vc = v_ref[0:n_ctx, :]
        p, l = _softmax_rows(_dot_nt(stack_heads(q_ref[...]), kc))
        o = jnp.dot(p.astype(BF16), vc, preferred_element_type=F32) / l
        o_ref[...] = unstack_heads(o, tq).astype(o_ref.dtype)

    @pl.when(i > 0)
    def _():
        kc = k_ref[0:n_ctx, :]
        vc = v_ref[0:n_ctx, :]
        def scores(rr):
            r = (i - 1) * (tq // GRID_W) + rr
            rs = jnp.clip(r - NA_KH // 2, 0, n_rows - NA_KH)
            start = pl.multiple_of(n_ctx + rs * GRID_W, GRID_W)
            kcat = jnp.concatenate([kc, k_ref[pl.ds(start, NA_KH * GRID_W), :]], axis=0)
            q = q_ref[rr * GRID_W:(rr + 1) * GRID_W, :]
            return _dot_nt(stack_heads(q), kcat) + bias_ref[r - rs], start

        nxt = scores(0)
        for rr in range(tq // GRID_W):
            s, start = nxt
            if rr + 1 < tq // GRID_W:
                nxt = scores(rr + 1)
            vcat = jnp.concatenate([vc, v_ref[pl.ds(start, NA_KH * GRID_W), :]], axis=0)
            p, l = _softmax_rows(s)
            o = jnp.dot(p.astype(BF16), vcat, preferred_element_type=F32) / l
            o_ref[rr * GRID_W:(rr + 1) * GRID_W, :] = unstack_heads(o, GRID_W).astype(o_ref.dtype)


def _na_attention(pb, bias, n_ctx):
    b, s, _ = pb.shape
    tq = ROW_TILE
    n_rows = (s - n_ctx) // GRID_W
    return pl.pallas_call(
        functools.partial(_na_kernel, n_ctx=n_ctx, n_rows=n_rows),
        grid=(b, s // tq),
        in_specs=[pl.BlockSpec((None, tq, 256), lambda bi, i: (bi, i, 0)),
                  pl.BlockSpec((None, s, 256), lambda bi, i: (bi, 0, 1)),
                  pl.BlockSpec((None, s, 256), lambda bi, i: (bi, 0, 2)),
                  pl.BlockSpec(bias.shape, lambda bi, i: (0, 0, 0))],
        out_specs=pl.BlockSpec((None, tq, 256), lambda bi, i: (bi, i, 0)),
        out_shape=jax.ShapeDtypeStruct((b, s, 256), BF16),
        compiler_params=_cparams(("arbitrary", "arbitrary")),
        name="na_attention",
    )(pb, pb, pb, bias)


def _na_bias_table(rpb, n_ctx):
    depth = rpb.shape[0]
    pad = GRID_W - NA_KW
    padded = jnp.pad(rpb.astype(F32), ((0, 0), (0, 0), (0, 0), (pad, pad)))
    cols = jnp.stack([padded[..., GRID_W - 1 - w:2 * GRID_W - 1 - w] for w in range(GRID_W)], axis=3)
    vals = jnp.stack([cols[:, :, NA_KH - 1 - o:2 * NA_KH - 1 - o] for o in range(NA_KH)], axis=2)
    vals = jnp.transpose(vals, (0, 2, 1, 4, 3, 5))
    w = np.arange(GRID_W)[:, None, None]
    kc = np.arange(GRID_W)[None, None, :]
    cs = np.clip(w - NA_KW // 2, 0, GRID_W - NA_KW)
    inwin = np.broadcast_to((kc >= cs) & (kc < cs + NA_KW), (GRID_W, NA_KH, GRID_W))
    vals = jnp.where(jnp.asarray(inwin), vals * LOG2E, NEG).reshape(depth, NA_KH, NA_HEADS * GRID_W, NA_KH * GRID_W)
    return jnp.concatenate([jnp.zeros((depth, NA_KH, NA_HEADS * GRID_W, n_ctx), F32), vals], axis=-1)


def _hy_pre_kernel(u_ref, w_ref, b_ref, x0_ref, z_ref, *, bounds, tm):
    s = u_ref.shape[0]
    w0, w1, w2, bias = w_ref[0:1, :], w_ref[1:2, :], w_ref[2:3, :], b_ref[...]
    rid = lax.broadcasted_iota(I32, (tm, u_ref.shape[1]), 0)
    for c0 in range(0, s, tm):
        u = u_ref[c0:c0 + tm, :]
        prev = pltpu.roll(u, 1, 0)
        nxt = pltpu.roll(u, tm - 1, 0)
        first = jnp.zeros_like(w0) if c0 in bounds else u_ref[c0 - 1:c0, :]
        last = jnp.zeros_like(w0) if (c0 + tm) in bounds else u_ref[c0 + tm:c0 + tm + 1, :]
        prev = jnp.where(rid == 0, first, prev)
        nxt = jnp.where(rid == tm - 1, last, nxt)
        uc = prev * w0 + u * w1 + nxt * w2 + bias
        x0_ref[c0:c0 + tm, :] = uc[:, 0:HY_W]
        z_ref[c0:c0 + tm, :] = uc[:, HY_W:2 * HY_W] * uc[:, 2 * HY_W:3 * HY_W]


def _hy_pre(mix, w_short, b_short, n_ctx):
    b, s, _ = mix.shape
    return pl.pallas_call(
        functools.partial(_hy_pre_kernel, bounds=(0, n_ctx, s), tm=ROW_TILE),
        grid=(b,),
        in_specs=[pl.BlockSpec((None, s, 3 * HY_W), lambda bi: (bi, 0, 1)),
                  pl.BlockSpec((3, 3 * HY_W), lambda bi: (0, 0)),
                  pl.BlockSpec((1, 3 * HY_W), lambda bi: (0, 0))],
        out_specs=[pl.BlockSpec((None, s, HY_W), lambda bi: (bi, 0, 0)),
                   pl.BlockSpec((None, s, HY_W), lambda bi: (bi, 0, 0))],
        out_shape=[jax.ShapeDtypeStruct((b, s, HY_W), F32), jax.ShapeDtypeStruct((b, s, HY_W), F32)],
        compiler_params=_cparams(("arbitrary",)),
        name="hyena_pre",
    )(mix, w_short, b_short)


def _hy_filter_kernel(z_ref, w1_ref, b1_ref, w2_ref, b2_ref, w3_ref, b3_ref, w4_ref, f_ref, dec_ref, o_ref):
    def lin(a, w_ref, b_ref):
        return jnp.dot(a, w_ref[...], preferred_element_type=F32, precision=HIGHEST) + b_ref[...]

    hh = jnp.sin(f_ref[0:1, :] * lin(z_ref[...], w1_ref, b1_ref))
    hh = jnp.sin(f_ref[1:2, :] * lin(hh, w2_ref, b2_ref))
    hh = jnp.sin(f_ref[2:3, :] * lin(hh, w3_ref, b3_ref))
    hh = jnp.dot(hh, w4_ref[...], preferred_element_type=F32, precision=HIGHEST) * dec_ref[...]
    o_ref[...] = hh / jnp.sum(jnp.abs(hh), axis=0, keepdims=True)


def _pad_to(a, shape):
    return jnp.pad(a, [(0, t - s) for s, t in zip(a.shape, shape)])


def _hy_filters(n, w1, b1, w2, b2, w3, b3, w4, freq):
    depth = w1.shape[0]
    t = np.linspace(0.0, 1.0, n, dtype=np.float32)[:, None]
    w = np.float32(2.0 * math.pi / n) * np.arange(n, dtype=np.float32)[:, None]
    bands = np.linspace(1e-4, HY_BANDS - 1, HY_BANDS, dtype=np.float32)[None, :]
    z = np.concatenate([t, np.cos(w * bands), np.sin(w * bands)], axis=-1).astype(np.float32)
    z = np.pad(z, ((0, 0), (0, LANE - z.shape[1])))
    deltas = np.linspace(math.log(HY_TARGET) / HY_SLOW_DECAY, math.log(HY_TARGET) / HY_FAST_DECAY, HY_W, dtype=np.float32)
    deltas = np.tile(np.abs(deltas), 2)
    decay = np.exp(-t * deltas[None, :]).astype(np.float32)
    p = LANE
    args = (jnp.asarray(z), _pad_to(w1, (depth, p, p)), _pad_to(b1[:, None, :], (depth, 1, p)),
            _pad_to(w2, (depth, p, p)), _pad_to(b2[:, None, :], (depth, 1, p)),
            _pad_to(w3, (depth, p, p)), _pad_to(b3[:, None, :], (depth, 1, p)),
            _pad_to(w4, (depth, p, 2 * HY_W)), _pad_to(freq, (depth, 8, p)), jnp.asarray(decay))
    per_layer = lambda shp: pl.BlockSpec((None,) + shp, lambda l: (l,) + (0,) * len(shp))
    const = lambda shp: pl.BlockSpec(shp, lambda l: (0,) * len(shp))
    return pl.pallas_call(
        _hy_filter_kernel,
        grid=(depth,),
        in_specs=[const((n, p)), per_layer((p, p)), per_layer((1, p)), per_layer((p, p)), per_layer((1, p)),
                  per_layer((p, p)), per_layer((1, p)), per_layer((p, 2 * HY_W)), per_layer((8, p)),
                  const((n, 2 * HY_W))],
        out_specs=per_layer((n, 2 * HY_W)),
        out_shape=jax.ShapeDtypeStruct((depth, n, 2 * HY_W), F32),
        compiler_params=_cparams(("arbitrary",)),
        name="hyena_filters",
    )(*args)


def _conv_taps(filt):
    hf, hb = filt[..., :HY_W], filt[..., HY_W:]
    taps = jnp.concatenate([hb[:, :0:-1], (hf[:, 0:1] + hb[:, 0:1]), hf[:, 1:], jnp.zeros_like(hf[:, 0:1])], axis=1)
    return jnp.transpose(taps, (0, 2, 1))


def _hy_conv_kernel(z_ref, k_ref, o_ref, *, nblk, nb):
    cg = z_ref.shape[0]
    n2 = k_ref.shape[1]
    ncols = (2 * nblk - 1) * LANE

    def body(ci, _):
        krow = k_ref[pl.ds(ci, 1), :]
        kb = jnp.broadcast_to(krow, (LANE, n2))
        big = pltpu.roll(kb, n2 - (LANE - 1), 1, stride=1, stride_axis=0)[:, :ncols].astype(BF16)
        o_ref[ci] = jnp.zeros(o_ref.shape[1:], F32)
        for m in range(-(nblk - 1), nblk):
            km = big[:, (m + nblk - 1) * LANE:(m + nblk) * LANE]
            cnt = (nblk - abs(m)) * nb
            src = 0 if m >= 0 else -m * nb
            dst = m * nb if m >= 0 else 0
            zin = z_ref[ci, src:src + cnt, :]
            o_ref[ci, dst:dst + cnt, :] += jnp.dot(zin, km, preferred_element_type=F32)
        return 0

    lax.fori_loop(0, cg, body, 0)


def _hy_conv(zr, taps, nblk, nb):
    c, r, _ = zr.shape
    cg = 8
    return pl.pallas_call(
        functools.partial(_hy_conv_kernel, nblk=nblk, nb=nb),
        grid=(c // cg,),
        in_specs=[pl.BlockSpec((cg, r, LANE), lambda i: (i, 0, 0)),
                  pl.BlockSpec((cg, taps.shape[1]), lambda i: (i, 0))],
        out_specs=pl.BlockSpec((cg, r, LANE), lambda i: (i, 0, 0)),
        out_shape=jax.ShapeDtypeStruct((c, r, LANE), F32),
        compiler_params=_cparams(("arbitrary",)),
        name="hyena_conv",
    )(zr, taps)


def _hy_long_conv(z, taps):
    b, n, c = z.shape
    nblk = n // LANE
    zr = jnp.transpose(z.astype(BF16).reshape(b, nblk, LANE, c), (3, 1, 0, 2)).reshape(c, nblk * b, LANE)
    y = _hy_conv(zr, taps, nblk, b)
    return jnp.transpose(y.reshape(c, nblk, b, LANE), (2, 1, 3, 0)).reshape(b, n, c)


def _merge_kernel(h_ref, mod_ref, ya_ref, x0_ref, z_ref, yconv_ref, hb_ref, yc_ref, yd_ref, gate_ref,
                  wb_ref, wo_ref, o_ref, *, n_ctx, tm):
    i = pl.program_id(1)
    yb = (x0_ref[...] * (yconv_ref[...] + z_ref[...] * hb_ref[...])).astype(BF16)
    ys = (ya_ref[...], yb, yc_ref[...], yd_ref[...])
    acc = jnp.zeros(h_ref.shape, F32)
    d = h_ref.shape[1]
    for k in range(N_BRANCH):
        proj = jnp.dot(ys[k], wb_ref[k], preferred_element_type=F32)
        acc = acc + jax.nn.sigmoid(gate_ref[:, k * d:(k + 1) * d].astype(F32)) * proj
    out = jnp.dot(acc.astype(BF16), wo_ref[...], preferred_element_type=F32)
    row = i * tm + lax.broadcasted_iota(I32, out.shape, 0)
    gate = jnp.where(row < n_ctx, mod_ref[0, 2:3, :], mod_ref[1, 2:3, :])
    o_ref[...] = h_ref[...] + gate * out


def _merge(h, modl, ya, x0, z, yconv, hbias, yc, yd, gates, wb, wo, n_ctx):
    b, s, d = h.shape
    tm = ROW_TILE
    row = lambda w: pl.BlockSpec((None, tm, w), lambda bi, i: (bi, i, 0))
    return pl.pallas_call(
        functools.partial(_merge_kernel, n_ctx=n_ctx, tm=tm),
        grid=(b, s // tm),
        in_specs=[row(d), pl.BlockSpec((None, 2, 8, d), lambda bi, i: (bi, 0, 0, 0)),
                  row(256), row(256), row(256), row(256), pl.BlockSpec((1, 256), lambda bi, i: (0, 0)),
                  row(256), row(256), row(GATE_W),
                  pl.BlockSpec(wb.shape, lambda bi, i: (0, 0, 0)),
                  pl.BlockSpec(wo.shape, lambda bi, i: (0, 0))],
        out_specs=row(d),
        out_shape=jax.ShapeDtypeStruct((b, s, d), F32),
        compiler_params=_cparams(("arbitrary", "arbitrary")),
        name="merge",
    )(h, modl, ya, x0, z, yconv, hbias, yc, yd, gates, wb, wo)


def _topk_rows(s, label, k):
    vals, labs = [], []
    for _ in range(k):
        m = jnp.max(s, axis=0, keepdims=True)
        lb = jnp.min(jnp.where(s == m, label, float(2 ** 20)), axis=0, keepdims=True)
        vals.append(m)
        labs.append(lb)
        s = jnp.where(label == lb, -jnp.inf, s)
    return jnp.concatenate(vals, axis=0), jnp.concatenate(labs, axis=0).astype(I32)


_CAND_FIXED_A = ((0, 0), (0, 8), (1, 0), (2, 0), (3, 0))
_CAND_FIXED_B = ((0, 0), (0, 8), (1, 0), (2, 0))


def _select_rows(table, pos):
    out = jnp.zeros(pos.shape, table.dtype)
    for a in range(table.shape[0]):
        out = jnp.where(pos == a, table[a:a + 1, :], out)
    return out


def _peer_route_kernel(q_ref, keys_ref, i_ref, j_ref, g_ref):
    st = _dot_nt(keys_ref[...], q_ref[...].astype(BF16))
    t = st.shape[1]
    key_id = lax.broadcasted_iota(I32, (PEER_NKEYS, t), 0).astype(F32)
    sv1, si1 = _topk_rows(st[0:PEER_NKEYS], key_id, PEER_TOPK)
    sv2, si2 = _topk_rows(st[PEER_NKEYS:2 * PEER_NKEYS], key_id, PEER_TOPK)
    r8 = lax.broadcasted_iota(I32, (8, t), 0)
    cands, labels = [], []
    for a, b0 in _CAND_FIXED_A:
        cands.append(sv1[a:a + 1, :] + sv2[b0:b0 + 8, :])
        labels.append((a * PEER_TOPK + b0 + r8).astype(F32))
    for b, a0 in _CAND_FIXED_B:
        c = sv1[a0:a0 + 8, :] + sv2[b:b + 1, :]
        cands.append(jnp.where(r8 < 4, -jnp.inf, c) if a0 == 0 else c)
        labels.append(((a0 + r8) * PEER_TOPK + b).astype(F32))
    best, pos = _topk_rows(jnp.concatenate(cands, axis=0), jnp.concatenate(labels, axis=0), PEER_TOPK)
    i_ref[...] = _select_rows(si1, pos >> int(math.log2(PEER_TOPK)))
    j_ref[...] = _select_rows(si2, pos & (PEER_TOPK - 1))
    e = jnp.exp(best - jnp.max(best, axis=0, keepdims=True))
    g_ref[...] = e / jnp.sum(e, axis=0, keepdims=True)


def _peer_route(q, keys_blk):
    t, _ = q.shape
    tt = 256
    out = pl.BlockSpec((None, PEER_TOPK, tt), lambda ti, hh: (hh, 0, ti))
    shp = (PEER_HEADS, PEER_TOPK, t)
    return pl.pallas_call(
        _peer_route_kernel,
        grid=(t // tt, PEER_HEADS),
        in_specs=[pl.BlockSpec((tt, PEER_DK), lambda ti, hh: (ti, hh)),
                  pl.BlockSpec((None, 2 * PEER_NKEYS, PEER_DK), lambda ti, hh: (hh, 0, 0))],
        out_specs=[out, out, out],
        out_shape=[jax.ShapeDtypeStruct(shp, I32), jax.ShapeDtypeStruct(shp, I32), jax.ShapeDtypeStruct(shp, F32)],
        compiler_params=_cparams(("arbitrary", "arbitrary")),
        name="peer_route",
    )(q, keys_blk)


GS_HALF = PEER_NKEYS // 2
GS_PITCH = GS_HALF + 1
GS_UNROLL = 16
PEER_TM = 576
PEER_TE = 1024
U32 = jnp.uint32


def _peer_expert_kernel(h_ref, mod_ref, x_ref, i_ref, j_ref, g_ref, ut_ref, v_ref, o_ref, gs_ref, acc_ref,
                        *, n_ctx, tm):
    ti, e = pl.program_id(1), pl.program_id(2)
    n_e = pl.num_programs(2)
    te = ut_ref.shape[1]
    nk = PEER_NKEYS

    @pl.when(e == 0)
    def _():
        sub = lax.broadcasted_iota(I32, (nk, nk), 0)

        def build(t):
            irow = i_ref[pl.ds(t, 1), :]
            jrow = j_ref[pl.ds(t, 1), :]
            grow = g_ref[pl.ds(t, 1), :]
            at = jnp.where(sub == irow, 1.0, 0.0).astype(BF16)
            bt = jnp.where(sub == jrow, grow, 0.0).astype(BF16)
            gt = _dot_nt(at, bt).astype(BF16).astype(F32)
            lo = pltpu.bitcast(gt[0:GS_HALF], U32) >> 16
            hi = pltpu.bitcast(gt[GS_HALF:nk], U32)
            gs_ref[pl.ds(t * GS_PITCH, GS_HALF), :] = hi | lo

        def body(tb, _):
            for u in range(GS_UNROLL):
                build(tb * GS_UNROLL + u)
            return 0

        lax.fori_loop(0, tm // GS_UNROLL, body, 0)

    nb = te // nk // 2

    def gelu(a):
        return 0.5 * a * (1.0 + lax.erf(a * (2.0 ** -0.5)))

    hid = jnp.dot(x_ref[...], ut_ref[...], preferred_element_type=F32)
    first, second = [], []
    for k in range(nb):
        word = gs_ref[pl.ds(e * nb + k, tm, stride=GS_PITCH), :]
        g_lo = pltpu.bitcast(word << 16, F32)
        g_hi = pltpu.bitcast(word & jnp.uint32(0xFFFF0000), F32)
        first.append((g_lo * gelu(hid[:, k * nk:(k + 1) * nk])).astype(BF16))
        second.append((g_hi * gelu(hid[:, (nb + k) * nk:(nb + k + 1) * nk])).astype(BF16))
    contrib = jnp.dot(jnp.concatenate(first + second, axis=1), v_ref[...], preferred_element_type=F32)

    @pl.when(e == 0)
    def _():
        acc_ref[...] = contrib

    @pl.when(e > 0)
    def _():
        acc_ref[...] += contrib

    @pl.when(e == n_e - 1)
    def _():
        row = ti * tm + lax.broadcasted_iota(I32, acc_ref.shape, 0)
        gate = jnp.where(row < n_ctx, mod_ref[0, 5:6, :], mod_ref[1, 5:6, :])
        o_ref[...] = h_ref[...] + gate * acc_ref[...]


def _peer_experts(h, modl, xb, i_idx, j_idx, g, ut, v, n_ctx):
    b, s, d = h.shape
    tm, te = PEER_TM, PEER_TE
    n_chunks = ut.shape[0]
    slots = PEER_HEADS * PEER_TOPK
    row = lambda w: pl.BlockSpec((None, tm, w), lambda bi, i, e: (bi, i, 0))
    return pl.pallas_call(
        functools.partial(_peer_expert_kernel, n_ctx=n_ctx, tm=tm),
        grid=(b, s // tm, n_chunks),
        in_specs=[row(d), pl.BlockSpec((None, 2, 8, d), lambda bi, i, e: (bi, 0, 0, 0)),
                  row(d), row(slots), row(slots), row(slots),
                  pl.BlockSpec((None, d, te), lambda bi, i, e: (e, 0, 0)),
                  pl.BlockSpec((te, d), lambda bi, i, e: (e, 0))],
        out_specs=row(d),
        out_shape=jax.ShapeDtypeStruct((b, s, d), F32),
        scratch_shapes=[pltpu.VMEM((tm * GS_PITCH, PEER_NKEYS), U32), pltpu.VMEM((tm, d), F32)],
        compiler_params=_cparams(("arbitrary", "arbitrary", "arbitrary")),
        name="peer_experts",
    )(h, modl, xb, i_idx, j_idx, g, ut, v)


def _final_norm_kernel(x_ref, g_ref, o_ref):
    x = x_ref[...]
    o_ref[...] = x * lax.rsqrt(jnp.mean(x * x, axis=-1, keepdims=True) + EPS) * g_ref[...]


def _final_norm(h, g, n_ctx):
    b, s, d = h.shape
    tm = ROW_TILE
    n_lat = s - n_ctx
    off = n_ctx // tm
    return pl.pallas_call(
        _final_norm_kernel,
        grid=(b, n_lat // tm),
        in_specs=[pl.BlockSpec((None, tm, d), lambda bi, i: (bi, i + off, 0)),
                  pl.BlockSpec((1, d), lambda bi, i: (0, 0))],
        out_specs=pl.BlockSpec((None, tm, d), lambda bi, i: (bi, i, 0)),
        out_shape=jax.ShapeDtypeStruct((b, n_lat, d), F32),
        compiler_params=_cparams(("arbitrary", "arbitrary")),
        name="final_norm",
    )(h, g)


def _prep_w_in(w_in):
    depth, d, _ = w_in.shape
    mixw = w_in[:, :, :2816]
    gq = mixw[:, :, 1536:1792].reshape(depth, d, GQA_HEADS, GQA_DH)[:, :, (0, 2, 1, 3), :].reshape(depth, d, 256)
    mixw = jnp.concatenate([mixw[:, :, :1536], gq, mixw[:, :, 1792:], jnp.zeros((depth, d, MIX_W - 2816), w_in.dtype)], axis=-1)
    return jnp.concatenate([mixw, w_in[:, :, 2816:]], axis=-1).astype(BF16)


def _peer_chunk_order(w):
    n_exp, d = w.shape
    nb = PEER_TE // PEER_NKEYS // 2
    return w.reshape(2, GS_HALF // nb, nb, PEER_NKEYS, d).transpose(1, 0, 2, 3, 4).reshape(n_exp, d)


def _prep_peer_keys(keys):
    depth = keys.shape[0]
    half = PEER_DK // 2
    z = jnp.zeros((depth, PEER_HEADS, PEER_NKEYS, half), keys.dtype)
    top = jnp.concatenate([keys[:, :, 0], z], axis=-1)
    bot = jnp.concatenate([z, keys[:, :, 1]], axis=-1)
    return jnp.concatenate([top, bot], axis=2).astype(BF16)


def kernel(x, c, ctx, c_ctx, w_mod, b_mod, norm1_g, norm2_g, w_in, na_rpb, hy_short_w, hy_short_b, hy_w1, hy_b1, hy_w2, hy_b2, hy_w3, hy_b3, hy_w4, hy_freq, hy_bias, gqa_qn, gqa_kn, diff_lq1, diff_lk1, diff_lq2, diff_lk2, diff_subln, w_branch, w_out, peer_wq, peer_keys, peer_u, peer_v, final_g):
    B, L, D = x.shape
    C = ctx.shape[1]
    S = C + L
    depth = w_mod.shape[0]
    assert C == ROW_TILE and L % ROW_TILE == 0 and D == 1024 and L % GRID_W == 0

    h = jnp.concatenate([ctx, x], axis=1)

    r = -(-(B + 1) // 8) * 8
    cvec = jnp.zeros((r, D), F32).at[:B].set(c).at[B].set(c_ctx)
    modall = _modulation(cvec, w_mod, b_mod)
    mod_lat = modall[:, :B].reshape(depth, B, 1, 6, D)
    mod_ctx = jnp.broadcast_to(modall[:, B].reshape(depth, 1, 1, 6, D), (depth, B, 1, 6, D))
    mods = jnp.pad(jnp.concatenate([mod_ctx, mod_lat], axis=2), ((0, 0), (0, 0), (0, 0), (0, 2), (0, 0)))

    fargs = (hy_w1, hy_b1, hy_w2, hy_b2, hy_w3, hy_b3, hy_w4, hy_freq)
    taps_lat = _conv_taps(_hy_filters(L, *fargs))
    taps_ctx = _conv_taps(_hy_filters(C, *fargs))

    cg, sg = _rope_tables(C, L, GQA_DH, 256)
    cd, sd = _rope_tables(C, L, DIFF_DH, 256)

    lam_init = jnp.asarray([0.8 - 0.6 * math.exp(-0.3 * l) for l in range(depth)], F32)
    lam = (jnp.exp(jnp.sum(diff_lq1.astype(F32) * diff_lk1.astype(F32), axis=-1))
           - jnp.exp(jnp.sum(diff_lq2.astype(F32) * diff_lk2.astype(F32), axis=-1)) + lam_init)
    diff_par = jnp.zeros((depth, 8, LANE), F32).at[:, 0, :].set(lam[:, None]).at[:, 1, :].set(1.0 - lam_init[:, None])

    wb = w_branch.astype(BF16)
    wb = wb.at[:, 2].set(wb[:, 2].reshape(depth, GQA_HEADS, GQA_DH, D)[:, (0, 2, 1, 3)].reshape(depth, BRANCH_W, D))

    xs = dict(
        mods=mods, n1=norm1_g[:, None, :], n2=norm2_g[:, None, :], w_in=_prep_w_in(w_in),
        bias=_na_bias_table(na_rpb, C), sw=hy_short_w, sb=hy_short_b[:, None, :], hbias=hy_bias[:, None, :],
        taps_lat=taps_lat, taps_ctx=taps_ctx,
        qn=jnp.tile(gqa_qn, (1, GQA_HEADS))[:, None, :], kn=jnp.tile(gqa_kn, (1, GQA_KV))[:, None, :],
        diff_par=diff_par, subln=jnp.tile(diff_subln, (1, DIFF_HEADS))[:, None, :],
        wb=wb, wo=w_out.astype(BF16), wq=peer_wq.astype(BF16), keys=_prep_peer_keys(peer_keys),
        u=peer_u, v=peer_v,
    )

    def layer(h, p):
        mix, gates = _in_proj(h, p["mods"], p["n1"], p["w_in"], C)
        pb = _prep(mix, cg, sg, cd, sd, p["qn"], p["kn"])
        ya = _na_attention(pb, p["bias"], C)
        yc = _gqa_attention(pb, C)
        yd = _diff_attention(pb, p["diff_par"], p["subln"], C)
        x0, z = _hy_pre(mix, p["sw"], p["sb"], C)
        yconv = jnp.concatenate([_hy_long_conv(z[:, :C], p["taps_ctx"]), _hy_long_conv(z[:, C:], p["taps_lat"])], axis=1)
        h = _merge(h, p["mods"], ya, x0, z, yconv, p["hbias"], yc, yd, gates, p["wb"], p["wo"], C)

        q, xb = _peer_q(h, p["mods"], p["n2"], p["wq"], C)
        i_idx, j_idx, g = _peer_route(q.reshape(B * S, -1), p["keys"])
        slots = PEER_HEADS * PEER_TOPK
        to_rows = lambda a: jnp.transpose(a, (2, 0, 1)).reshape(B, S, slots)
        ut = jnp.transpose(_peer_chunk_order(p["u"].astype(BF16)).reshape(-1, PEER_TE, D), (0, 2, 1))
        h = _peer_experts(h, p["mods"], xb, to_rows(i_idx), to_rows(j_idx), to_rows(g), ut,
                          _peer_chunk_order(p["v"].astype(BF16)), C)
        return h, None

    h, _ = lax.scan(layer, h, xs)
    return _final_norm(h, final_g[None, :], C)
```

```python
import functools
import math

import numpy as np
import jax
import jax.numpy as jnp
from jax import lax
from jax.experimental import pallas as pl
from jax.experimental.pallas import tpu as pltpu

F32 = jnp.float32
BF16 = jnp.bfloat16
I32 = jnp.int32
HIGHEST = lax.Precision.HIGHEST

EPS = 1e-6
GRID_W = 64
ROPE_THETA = 10000.0
NA_HEADS, NA_DH, NA_KH, NA_KW = 4, 64, 8, 16
HY_W, HY_BANDS, HY_FFN = 256, 16, 64
HY_FAST_DECAY, HY_SLOW_DECAY, HY_TARGET = 0.3, 1.5, 1e-2
GQA_HEADS, GQA_KV, GQA_DH = 4, 2, 64
DIFF_HEADS, DIFF_DH, DIFF_DV = 4, 32, 64
N_BRANCH, BRANCH_W = 4, 256
PEER_HEADS, PEER_NKEYS, PEER_DK, PEER_TOPK = 8, 128, 128, 16

LANE = 128
ROW_TILE = 256
VMEM_LIMIT = 56 * 1024 * 1024
NEG = -1e30
LOG2E = math.log2(math.e)
MIX_W = 3072
GATE_W = N_BRANCH * 1024
COL_TILE = 1024


def _cparams(sem):
    return pltpu.CompilerParams(dimension_semantics=sem, vmem_limit_bytes=VMEM_LIMIT)


def _mod_kernel(c_ref, w_ref, b_ref, o_ref):
    cv = c_ref[...]
    a = cv * jax.nn.sigmoid(cv)
    o_ref[...] = jnp.dot(a, w_ref[...], preferred_element_type=F32, precision=HIGHEST) + b_ref[...]


def _modulation(cvec, w_mod, b_mod):
    depth, d, n = w_mod.shape
    r = cvec.shape[0]
    tn = 1024
    return pl.pallas_call(
        _mod_kernel,
        grid=(depth, n // tn),
        in_specs=[pl.BlockSpec((r, d), lambda l, j: (0, 0)),
                  pl.BlockSpec((None, d, tn), lambda l, j: (l, 0, j)),
                  pl.BlockSpec((None, 1, tn), lambda l, j: (l, 0, j))],
        out_specs=pl.BlockSpec((None, r, tn), lambda l, j: (l, 0, j)),
        out_shape=jax.ShapeDtypeStruct((depth, r, n), F32),
        compiler_params=_cparams(("arbitrary", "arbitrary")),
        name="modulation",
    )(cvec, w_mod, b_mod.reshape(depth, 1, n))


def _norm_modulate(h_ref, mod_ref, g_ref, srow, row0, n_ctx):
    x = h_ref[...]
    y = x * lax.rsqrt(jnp.mean(x * x, axis=-1, keepdims=True) + EPS) * g_ref[...]
    row = row0 + lax.broadcasted_iota(I32, x.shape, 0)
    is_ctx = row < n_ctx
    shift = jnp.where(is_ctx, mod_ref[0, srow:srow + 1, :], mod_ref[1, srow:srow + 1, :])
    scale = jnp.where(is_ctx, mod_ref[0, srow + 1:srow + 2, :], mod_ref[1, srow + 1:srow + 2, :])
    return y * (1.0 + scale) + shift


def _in_proj_kernel(h_ref, mod_ref, g_ref, w_ref, mix_ref, gate_ref, nb_ref, *, tm, n_ctx, n_mix):
    i, j = pl.program_id(1), pl.program_id(2)

    @pl.when(j == 0)
    def _():
        nb_ref[...] = _norm_modulate(h_ref, mod_ref, g_ref, 0, i * tm, n_ctx).astype(BF16)

    @pl.when(j < n_mix)
    def _():
        mix_ref[...] = jnp.dot(nb_ref[...], w_ref[...], preferred_element_type=F32)

    @pl.when(j >= n_mix)
    def _():
        gate_ref[...] = jnp.dot(nb_ref[...], w_ref[...], preferred_element_type=F32).astype(gate_ref.dtype)


def _in_proj(h, modl, g, w, n_ctx):
    b, s, d = h.shape
    tm, tn = 768, COL_TILE
    n_mix = MIX_W // tn
    n_tot = (MIX_W + GATE_W) // tn
    return pl.pallas_call(
        functools.partial(_in_proj_kernel, tm=tm, n_ctx=n_ctx, n_mix=n_mix),
        grid=(b, s // tm, n_tot),
        in_specs=[pl.BlockSpec((None, tm, d), lambda bi, i, j: (bi, i, 0)),
                  pl.BlockSpec((None, 2, 8, d), lambda bi, i, j: (bi, 0, 0, 0)),
                  pl.BlockSpec((1, d), lambda bi, i, j: (0, 0)),
                  pl.BlockSpec((d, tn), lambda bi, i, j: (0, j))],
        out_specs=[pl.BlockSpec((None, tm, tn), lambda bi, i, j: (bi, i, jnp.minimum(j, n_mix - 1))),
                   pl.BlockSpec((None, tm, tn), lambda bi, i, j: (bi, i, jnp.maximum(j - n_mix, 0)))],
        out_shape=[jax.ShapeDtypeStruct((b, s, MIX_W), F32), jax.ShapeDtypeStruct((b, s, GATE_W), BF16)],
        scratch_shapes=[pltpu.VMEM((tm, d), BF16)],
        compiler_params=_cparams(("arbitrary", "arbitrary", "arbitrary")),
        name="in_proj",
    )(h, modl, g, w)


def _peer_q_kernel(h_ref, mod_ref, g_ref, w_ref, q_ref, nb_ref, *, tm, n_ctx):
    i = pl.program_id(1)
    nb = _norm_modulate(h_ref, mod_ref, g_ref, 3, i * tm, n_ctx).astype(BF16)
    nb_ref[...] = nb
    q_ref[...] = jnp.dot(nb, w_ref[...], preferred_element_type=F32)


def _peer_q(h, modl, g, wq, n_ctx):
    b, s, d = h.shape
    tm = 768
    n = wq.shape[1]
    return pl.pallas_call(
        functools.partial(_peer_q_kernel, tm=tm, n_ctx=n_ctx),
        grid=(b, s // tm),
        in_specs=[pl.BlockSpec((None, tm, d), lambda bi, i: (bi, i, 0)),
                  pl.BlockSpec((None, 2, 8, d), lambda bi, i: (bi, 0, 0, 0)),
                  pl.BlockSpec((1, d), lambda bi, i: (0, 0)),
                  pl.BlockSpec((d, n), lambda bi, i: (0, 0))],
        out_specs=[pl.BlockSpec((None, tm, n), lambda bi, i: (bi, i, 0)),
                   pl.BlockSpec((None, tm, d), lambda bi, i: (bi, i, 0))],
        out_shape=[jax.ShapeDtypeStruct((b, s, n), F32), jax.ShapeDtypeStruct((b, s, d), BF16)],
        compiler_params=_cparams(("arbitrary", "arbitrary")),
        name="peer_q",
    )(h, modl, g, wq)


def _lane_group(shape, width):
    return lax.broadcasted_iota(I32, shape, 1) >> int(math.log2(width))


def _head_mean_sq(x, n_heads, dh):
    hid = _lane_group(x.shape, dh)
    x2 = x * x
    ms = jnp.zeros_like(x)
    for hh in range(n_heads):
        m = hid == hh
        s = jnp.sum(jnp.where(m, x2, 0.0), axis=-1, keepdims=True)
        ms = jnp.where(m, s, ms)
    return ms * (1.0 / dh)


def _rope(x, cos, sin_signed, qs):
    w = x.shape[-1]
    lane = lax.broadcasted_iota(I32, x.shape, 1)
    lo = (lane & (2 * qs - 1)) < qs
    partner = jnp.where(lo, pltpu.roll(x, w - qs, 1), pltpu.roll(x, qs, 1))
    return x * cos + partner * sin_signed


def _prep_kernel(na_ref, gqa_ref, dq_ref, dk_ref, dv_ref, cg_ref, sg_ref, cd_ref, sd_ref, qn_ref, kn_ref, o_ref):
    na = na_ref[...]
    o_ref[:, 0:256] = (na[:, 0:256] * (LOG2E * NA_DH ** -0.5)).astype(BF16)
    o_ref[:, 256:768] = na[:, 256:768].astype(BF16)
    gq = gqa_ref[:, 0:256]
    gk = gqa_ref[:, 256:384]
    cg, sg = cg_ref[...], sg_ref[...]
    gq = gq * lax.rsqrt(_head_mean_sq(gq, GQA_HEADS, GQA_DH) + EPS) * qn_ref[...]
    gk = gk * lax.rsqrt(_head_mean_sq(gk, GQA_KV, GQA_DH) + EPS) * kn_ref[...]
    o_ref[:, 768:1024] = (_rope(gq, cg, sg, GQA_DH // 4) * (LOG2E * GQA_DH ** -0.5)).astype(BF16)
    o_ref[:, 1024:1152] = _rope(gk, cg[:, 0:128], sg[:, 0:128], GQA_DH // 4).astype(BF16)
    o_ref[:, 1152:1280] = gqa_ref[:, 384:512].astype(BF16)
    cd, sd = cd_ref[...], sd_ref[...]
    o_ref[:, 1280:1536] = (_rope(dq_ref[...], cd, sd, DIFF_DH // 4) * (LOG2E * DIFF_DH ** -0.5)).astype(BF16)
    o_ref[:, 1536:1792] = _rope(dk_ref[...], cd, sd, DIFF_DH // 4).astype(BF16)
    o_ref[:, 1792:2048] = dv_ref[...].astype(BF16)


def _prep(mix, cg, sg, cd, sd, qn, kn):
    b, s, _ = mix.shape
    tm = ROW_TILE
    tab = pl.BlockSpec((tm, 256), lambda bi, i: (i, 0))
    return pl.pallas_call(
        _prep_kernel,
        grid=(b, s // tm),
        in_specs=[pl.BlockSpec((None, tm, 768), lambda bi, i: (bi, i, 0)),
                  pl.BlockSpec((None, tm, 512), lambda bi, i: (bi, i, 3)),
                  pl.BlockSpec((None, tm, 256), lambda bi, i: (bi, i, 8)),
                  pl.BlockSpec((None, tm, 256), lambda bi, i: (bi, i, 9)),
                  pl.BlockSpec((None, tm, 256), lambda bi, i: (bi, i, 10)),
                  tab, tab, tab, tab,
                  pl.BlockSpec((1, 256), lambda bi, i: (0, 0)),
                  pl.BlockSpec((1, 128), lambda bi, i: (0, 0))],
        out_specs=pl.BlockSpec((None, tm, 2048), lambda bi, i: (bi, i, 0)),
        out_shape=jax.ShapeDtypeStruct((b, s, 2048), BF16),
        compiler_params=_cparams(("arbitrary", "arbitrary")),
        name="qk_prep",
    )(mix, mix, mix, mix, mix, cg, sg, cd, sd, qn, kn)


def _rope_tables(n_ctx, n_lat, dh, width):
    qs = dh // 4
    t = np.arange(n_lat)
    rows, cols = (t // GRID_W).astype(np.float64), (t % GRID_W).astype(np.float64)
    lane = np.arange(width) % dh
    part = lane // (2 * qs)
    u = lane % (2 * qs)
    f = u % qs
    lo = u < qs
    freqs = ROPE_THETA ** (-f.astype(np.float64) / qs)
    pos = np.where(part[None, :] == 0, rows[:, None], cols[:, None])
    ang = (pos.astype(np.float32) * freqs.astype(np.float32)[None, :]).astype(np.float32)
    cos = np.cos(ang.astype(np.float64))
    sin = np.sin(ang.astype(np.float64)) * np.where(lo, -1.0, 1.0)[None, :]
    cos = np.concatenate([np.ones((n_ctx, width)), cos], axis=0)
    sin = np.concatenate([np.zeros((n_ctx, width)), sin], axis=0)
    return jnp.asarray(cos, F32), jnp.asarray(sin, F32)


def _softmax_rows(s):
    m = jnp.max(s, axis=-1, keepdims=True)
    p = jnp.exp2(s - m)
    return p, jnp.sum(p, axis=-1, keepdims=True)


def _dot_nt(a, b):
    return lax.dot_general(a, b, (((1,), (1,)), ((), ())), preferred_element_type=F32)


def _gqa_kernel(q_ref, k_ref, v_ref, o_ref, *, n_ctx):
    i = pl.program_id(1)

    def run(nk):
        k = k_ref[0:nk, :]
        v = v_ref[0:nk, :]
        q = q_ref[...]
        tq = q.shape[0]
        grp = _lane_group((tq, LANE), GQA_DH)

        def scores(g):
            mask = grp == g
            ql = jnp.concatenate([jnp.where(mask, q[:, 0:128], 0), jnp.where(mask, q[:, 128:256], 0)], axis=0)
            return _dot_nt(ql, k)

        outs = []
        s_next = scores(0)
        for g in range(GQA_KV):
            s = s_next
            if g + 1 < GQA_KV:
                s_next = scores(g + 1)
            p, l = _softmax_rows(s)
            outs.append(jnp.dot(p.astype(BF16), v, preferred_element_type=F32) / l)
        for half in range(2):
            sel = jnp.where(grp == 0, outs[0][half * tq:(half + 1) * tq], outs[1][half * tq:(half + 1) * tq])
            o_ref[:, half * 128:(half + 1) * 128] = sel.astype(o_ref.dtype)

    @pl.when(i == 0)
    def _():
        run(n_ctx)

    @pl.when(i > 0)
    def _():
        run(k_ref.shape[0])


def _gqa_attention(pb, n_ctx):
    b, s, _ = pb.shape
    tq = ROW_TILE
    return pl.pallas_call(
        functools.partial(_gqa_kernel, n_ctx=n_ctx),
        grid=(b, s // tq),
        in_specs=[pl.BlockSpec((None, tq, 256), lambda bi, i: (bi, i, 3)),
                  pl.BlockSpec((None, s, 128), lambda bi, i: (bi, 0, 8)),
                  pl.BlockSpec((None, s, 128), lambda bi, i: (bi, 0, 9))],
        out_specs=pl.BlockSpec((None, tq, 256), lambda bi, i: (bi, i, 0)),
        out_shape=jax.ShapeDtypeStruct((b, s, 256), BF16),
        compiler_params=_cparams(("arbitrary", "arbitrary")),
        name="gqa_attention",
    )(pb, pb, pb)


def _diff_kernel(q_ref, k_ref, v_ref, par_ref, g_ref, o_ref, *, n_ctx):
    i = pl.program_id(1)

    def run(nk):
        k = k_ref[0:nk, :]
        v = v_ref[0:nk, :]
        q = q_ref[...]
        tq = q.shape[0]
        lam = par_ref[0:1, 0:1]
        comp = _lane_group(q.shape, DIFF_DH)
        head = _lane_group(q.shape, DIFF_DV)
        acc = jnp.zeros(q.shape, F32)

        def scores(hh):
            ql = jnp.concatenate([jnp.where(comp == 2 * hh, q, 0), jnp.where(comp == 2 * hh + 1, q, 0)], axis=0)
            return _dot_nt(ql, k)

        s_next = scores(0)
        for hh in range(DIFF_HEADS):
            s = s_next
            if hh + 1 < DIFF_HEADS:
                s_next = scores(hh + 1)
            p, l = _softmax_rows(s)
            inv = 1.0 / l
            pc = (p[0:tq] * inv[0:tq] - p[tq:2 * tq] * (lam * inv[tq:2 * tq])).astype(BF16)
            o = jnp.dot(pc, v, preferred_element_type=F32)
            acc = jnp.where(head == hh, o, acc)
        y = acc * lax.rsqrt(_head_mean_sq(acc, DIFF_HEADS, DIFF_DV) + EPS) * g_ref[...]
        o_ref[...] = (y * par_ref[1:2, 0:1]).astype(o_ref.dtype)

    @pl.when(i == 0)
    def _():
        run(n_ctx)

    @pl.when(i > 0)
    def _():
        run(k_ref.shape[0])


def _diff_attention(pb, par, subln, n_ctx):
    b, s, _ = pb.shape
    tq = ROW_TILE
    return pl.pallas_call(
        functools.partial(_diff_kernel, n_ctx=n_ctx),
        grid=(b, s // tq),
        in_specs=[pl.BlockSpec((None, tq, 256), lambda bi, i: (bi, i, 5)),
                  pl.BlockSpec((None, s, 256), lambda bi, i: (bi, 0, 6)),
                  pl.BlockSpec((None, s, 256), lambda bi, i: (bi, 0, 7)),
                  pl.BlockSpec((8, 128), lambda bi, i: (0, 0)),
                  pl.BlockSpec((1, 256), lambda bi, i: (0, 0))],
        out_specs=pl.BlockSpec((None, tq, 256), lambda bi, i: (bi, i, 0)),
        out_shape=jax.ShapeDtypeStruct((b, s, 256), BF16),
        compiler_params=_cparams(("arbitrary", "arbitrary")),
        name="diff_attention",
    )(pb, pb, pb, par, subln)


def _na_kernel(q_ref, k_ref, v_ref, bias_ref, o_ref, *, n_ctx, n_rows):
    i = pl.program_id(1)
    tq = q_ref.shape[0]

    def stack_heads(q):
        head = _lane_group(q.shape, NA_DH)
        return jnp.concatenate([jnp.where(head == hh, q, 0) for hh in range(NA_HEADS)], axis=0)

    def unstack_heads(o, rows):
        head = _lane_group((rows, 256), NA_DH)
        out = jnp.zeros((rows, 256), F32)
        for hh in range(NA_HEADS):
            out = jnp.where(head == hh, o[hh * rows:(hh + 1) * rows], out)
        return out

    @pl.when(i == 0)
    def _():
        kc = k_ref[0:n_ctx, :]
        vc = v_ref[0:n_ctx, :]
        p, l = _softmax_rows(_dot_nt(stack_heads(q_ref[...]), kc))
        o = jnp.dot(p.astype(BF16), vc, preferred_element_type=F32) / l
        o_ref[...] = unstack_heads(o, tq).astype(o_ref.dtype)

    @pl.when(i > 0)
    def _():
        kc = k_ref[0:n_ctx, :]
        vc = v_ref[0:n_ctx, :]
        def scores(rr):
            r = (i - 1) * (tq // GRID_W) + rr
            rs = jnp.clip(r - NA_KH // 2, 0, n_rows - NA_KH)
            start = pl.multiple_of(n_ctx + rs * GRID_W, GRID_W)
            kcat = jnp.concatenate([kc, k_ref[pl.ds(start, NA_KH * GRID_W), :]], axis=0)
            q = q_ref[rr * GRID_W:(rr + 1) * GRID_W, :]
            return _dot_nt(stack_heads(q), kcat) + bias_ref[r - rs], start

        nxt = scores(0)
        for rr in range(tq // GRID_W):
            s, start = nxt
            if rr + 1 < tq // GRID_W:
                nxt = scores(rr + 1)
            vcat = jnp.concatenate([vc, v_ref[pl.ds(start, NA_KH * GRID_W), :]], axis=0)
            p, l = _softmax_rows(s)
            o = jnp.dot(p.astype(BF16), vcat, preferred_element_type=F32) / l
            o_ref[rr * GRID_W:(rr + 1) * GRID_W, :] = unstack_heads(o, GRID_W).astype(o_ref.dtype)


def _na_attention(pb, bias, n_ctx):
    b, s, _ = pb.shape
    tq = ROW_TILE
    n_rows = (s - n_ctx) // GRID_W
    return pl.pallas_call(
        functools.partial(_na_kernel, n_ctx=n_ctx, n_rows=n_rows),
        grid=(b, s // tq),
        in_specs=[pl.BlockSpec((None, tq, 256), lambda bi, i: (bi, i, 0)),
                  pl.BlockSpec((None, s, 256), lambda bi, i: (bi, 0, 1)),
                  pl.BlockSpec((None, s, 256), lambda bi, i: (bi, 0, 2)),
                  pl.BlockSpec(bias.shape, lambda bi, i: (0, 0, 0))],
        out_specs=pl.BlockSpec((None, tq, 256), lambda bi, i: (bi, i, 0)),
        out_shape=jax.ShapeDtypeStruct((b, s, 256), BF16),
        compiler_params=_cparams(("arbitrary", "arbitrary")),
        name="na_attention",
    )(pb, pb, pb, bias)


def _na_bias_table(rpb, n_ctx):
    depth = rpb.shape[0]
    pad = GRID_W - NA_KW
    padded = jnp.pad(rpb.astype(F32), ((0, 0), (0, 0), (0, 0), (pad, pad)))
    cols = jnp.stack([padded[..., GRID_W - 1 - w:2 * GRID_W - 1 - w] for w in range(GRID_W)], axis=3)
    vals = jnp.stack([cols[:, :, NA_KH - 1 - o:2 * NA_KH - 1 - o] for o in range(NA_KH)], axis=2)
    vals = jnp.transpose(vals, (0, 2, 1, 4, 3, 5))
    w = np.arange(GRID_W)[:, None, None]
    kc = np.arange(GRID_W)[None, None, :]
    cs = np.clip(w - NA_KW // 2, 0, GRID_W - NA_KW)
    inwin = np.broadcast_to((kc >= cs) & (kc < cs + NA_KW), (GRID_W, NA_KH, GRID_W))
    vals = jnp.where(jnp.asarray(inwin), vals * LOG2E, NEG).reshape(depth, NA_KH, NA_HEADS * GRID_W, NA_KH * GRID_W)
    return jnp.concatenate([jnp.zeros((depth, NA_KH, NA_HEADS * GRID_W, n_ctx), F32), vals], axis=-1)


def _hy_pre_kernel(u_ref, w_ref, b_ref, x0_ref, z_ref, *, bounds, tm):
    s = u_ref.shape[0]
    w0, w1, w2, bias = w_ref[0:1, :], w_ref[1:2, :], w_ref[2:3, :], b_ref[...]
    rid = lax.broadcasted_iota(I32, (tm, u_ref.shape[1]), 0)
    for c0 in range(0, s, tm):
        u = u_ref[c0:c0 + tm, :]
        prev = pltpu.roll(u, 1, 0)
        nxt = pltpu.roll(u, tm - 1, 0)
        first = jnp.zeros_like(w0) if c0 in bounds else u_ref[c0 - 1:c0, :]
        last = jnp.zeros_like(w0) if (c0 + tm) in bounds else u_ref[c0 + tm:c0 + tm + 1, :]
        prev = jnp.where(rid == 0, first, prev)
        nxt = jnp.where(rid == tm - 1, last, nxt)
        uc = prev * w0 + u * w1 + nxt * w2 + bias
        x0_ref[c0:c0 + tm, :] = uc[:, 0:HY_W]
        z_ref[c0:c0 + tm, :] = uc[:, HY_W:2 * HY_W] * uc[:, 2 * HY_W:3 * HY_W]


def _hy_pre(mix, w_short, b_short, n_ctx):
    b, s, _ = mix.shape
    return pl.pallas_call(
        functools.partial(_hy_pre_kernel, bounds=(0, n_ctx, s), tm=ROW_TILE),
        grid=(b,),
        in_specs=[pl.BlockSpec((None, s, 3 * HY_W), lambda bi: (bi, 0, 1)),
                  pl.BlockSpec((3, 3 * HY_W), lambda bi: (0, 0)),
                  pl.BlockSpec((1, 3 * HY_W), lambda bi: (0, 0))],
        out_specs=[pl.BlockSpec((None, s, HY_W), lambda bi: (bi, 0, 0)),
                   pl.BlockSpec((None, s, HY_W), lambda bi: (bi, 0, 0))],
        out_shape=[jax.ShapeDtypeStruct((b, s, HY_W), F32), jax.ShapeDtypeStruct((b, s, HY_W), F32)],
        compiler_params=_cparams(("arbitrary",)),
        name="hyena_pre",
    )(mix, w_short, b_short)


def _hy_filter_kernel(z_ref, w1_ref, b1_ref, w2_ref, b2_ref, w3_ref, b3_ref, w4_ref, f_ref, dec_ref, o_ref):
    def lin(a, w_ref, b_ref):
        return jnp.dot(a, w_ref[...], preferred_element_type=F32, precision=HIGHEST) + b_ref[...]

    hh = jnp.sin(f_ref[0:1, :] * lin(z_ref[...], w1_ref, b1_ref))
    hh = jnp.sin(f_ref[1:2, :] * lin(hh, w2_ref, b2_ref))
    hh = jnp.sin(f_ref[2:3, :] * lin(hh, w3_ref, b3_ref))
    hh = jnp.dot(hh, w4_ref[...], preferred_element_type=F32, precision=HIGHEST) * dec_ref[...]
    o_ref[...] = hh / jnp.sum(jnp.abs(hh), axis=0, keepdims=True)


def _pad_to(a, shape):
    return jnp.pad(a, [(0, t - s) for s, t in zip(a.shape, shape)])


def _hy_filters(n, w1, b1, w2, b2, w3, b3, w4, freq):
    depth = w1.shape[0]
    t = np.linspace(0.0, 1.0, n, dtype=np.float32)[:, None]
    w = np.float32(2.0 * math.pi / n) * np.arange(n, dtype=np.float32)[:, None]
    bands = np.linspace(1e-4, HY_BANDS - 1, HY_BANDS, dtype=np.float32)[None, :]
    z = np.concatenate([t, np.cos(w * bands), np.sin(w * bands)], axis=-1).astype(np.float32)
    z = np.pad(z, ((0, 0), (0, LANE - z.shape[1])))
    deltas = np.linspace(math.log(HY_TARGET) / HY_SLOW_DECAY, math.log(HY_TARGET) / HY_FAST_DECAY, HY_W, dtype=np.float32)
    deltas = np.tile(np.abs(deltas), 2)
    decay = np.exp(-t * deltas[None, :]).astype(np.float32)
    p = LANE
    args = (jnp.asarray(z), _pad_to(w1, (depth, p, p)), _pad_to(b1[:, None, :], (depth, 1, p)),
            _pad_to(w2, (depth, p, p)), _pad_to(b2[:, None, :], (depth, 1, p)),
            _pad_to(w3, (depth, p, p)), _pad_to(b3[:, None, :], (depth, 1, p)),
            _pad_to(w4, (depth, p, 2 * HY_W)), _pad_to(freq, (depth, 8, p)), jnp.asarray(decay))
    per_layer = lambda shp: pl.BlockSpec((None,) + shp, lambda l: (l,) + (0,) * len(shp))
    const = lambda shp: pl.BlockSpec(shp, lambda l: (0,) * len(shp))
    return pl.pallas_call(
        _hy_filter_kernel,
        grid=(depth,),
        in_specs=[const((n, p)), per_layer((p, p)), per_layer((1, p)), per_layer((p, p)), per_layer((1, p)),
                  per_layer((p, p)), per_layer((1, p)), per_layer((p, 2 * HY_W)), per_layer((8, p)),
                  const((n, 2 * HY_W))],
        out_specs=per_layer((n, 2 * HY_W)),
        out_shape=jax.ShapeDtypeStruct((depth, n, 2 * HY_W), F32),
        compiler_params=_cparams(("arbitrary",)),
        name="hyena_filters",
    )(*args)


def _conv_taps(filt):
    hf, hb = filt[..., :HY_W], filt[..., HY_W:]
    taps = jnp.concatenate([hb[:, :0:-1], (hf[:, 0:1] + hb[:, 0:1]), hf[:, 1:], jnp.zeros_like(hf[:, 0:1])], axis=1)
    return jnp.transpose(taps, (0, 2, 1))


def _hy_conv_kernel(z_ref, k_ref, o_ref, *, nblk, nb):
    cg = z_ref.shape[0]
    n2 = k_ref.shape[1]
    ncols = (2 * nblk - 1) * LANE

    def body(ci, _):
        krow = k_ref[pl.ds(ci, 1), :]
        kb = jnp.broadcast_to(krow, (LANE, n2))
        big = pltpu.roll(kb, n2 - (LANE - 1), 1, stride=1, stride_axis=0)[:, :ncols].astype(BF16)
        o_ref[ci] = jnp.zeros(o_ref.shape[1:], F32)
        for m in range(-(nblk - 1), nblk):
            km = big[:, (m + nblk - 1) * LANE:(m + nblk) * LANE]
            cnt = (nblk - abs(m)) * nb
            src = 0 if m >= 0 else -m * nb
            dst = m * nb if m >= 0 else 0
            zin = z_ref[ci, src:src + cnt, :]
            o_ref[ci, dst:dst + cnt, :] += jnp.dot(zin, km, preferred_element_type=F32)
        return 0

    lax.fori_loop(0, cg, body, 0)


def _hy_conv(zr, taps, nblk, nb):
    c, r, _ = zr.shape
    cg = 8
    return pl.pallas_call(
        functools.partial(_hy_conv_kernel, nblk=nblk, nb=nb),
        grid=(c // cg,),
        in_specs=[pl.BlockSpec((cg, r, LANE), lambda i: (i, 0, 0)),
                  pl.BlockSpec((cg, taps.shape[1]), lambda i: (i, 0))],
        out_specs=pl.BlockSpec((cg, r, LANE), lambda i: (i, 0, 0)),
        out_shape=jax.ShapeDtypeStruct((c, r, LANE), F32),
        compiler_params=_cparams(("arbitrary",)),
        name="hyena_conv",
    )(zr, taps)


def _hy_long_conv(z, taps):
    b, n, c = z.shape
    nblk = n // LANE
    zr = jnp.transpose(z.astype(BF16).reshape(b, nblk, LANE, c), (3, 1, 0, 2)).reshape(c, nblk * b, LANE)
    y = _hy_conv(zr, taps, nblk, b)
    return jnp.transpose(y.reshape(c, nblk, b, LANE), (2, 1, 3, 0)).reshape(b, n, c)


def _merge_kernel(h_ref, mod_ref, ya_ref, x0_ref, z_ref, yconv_ref, hb_ref, yc_ref, yd_ref, gate_ref,
                  wb_ref, wo_ref, o_ref, *, n_ctx, tm):
    i = pl.program_id(1)
    yb = (x0_ref[...] * (yconv_ref[...] + z_ref[...] * hb_ref[...])).astype(BF16)
    ys = (ya_ref[...], yb, yc_ref[...], yd_ref[...])
    acc = jnp.zeros(h_ref.shape, F32)
    d = h_ref.shape[1]
    for k in range(N_BRANCH):
        proj = jnp.dot(ys[k], wb_ref[k], preferred_element_type=F32)
        acc = acc + jax.nn.sigmoid(gate_ref[:, k * d:(k + 1) * d].astype(F32)) * proj
    out = jnp.dot(acc.astype(BF16), wo_ref[...], preferred_element_type=F32)
    row = i * tm + lax.broadcasted_iota(I32, out.shape, 0)
    gate = jnp.where(row < n_ctx, mod_ref[0, 2:3, :], mod_ref[1, 2:3, :])
    o_ref[...] = h_ref[...] + gate * out


def _merge(h, modl, ya, x0, z, yconv, hbias, yc, yd, gates, wb, wo, n_ctx):
    b, s, d = h.shape
    tm = ROW_TILE
    row = lambda w: pl.BlockSpec((None, tm, w), lambda bi, i: (bi, i, 0))
    return pl.pallas_call(
        functools.partial(_merge_kernel, n_ctx=n_ctx, tm=tm),
        grid=(b, s // tm),
        in_specs=[row(d), pl.BlockSpec((None, 2, 8, d), lambda bi, i: (bi, 0, 0, 0)),
                  row(256), row(256), row(256), row(256), pl.BlockSpec((1, 256), lambda bi, i: (0, 0)),
                  row(256), row(256), row(GATE_W),
                  pl.BlockSpec(wb.shape, lambda bi, i: (0, 0, 0)),
                  pl.BlockSpec(wo.shape, lambda bi, i: (0, 0))],
        out_specs=row(d),
        out_shape=jax.ShapeDtypeStruct((b, s, d), F32),
        compiler_params=_cparams(("arbitrary", "arbitrary")),
        name="merge",
    )(h, modl, ya, x0, z, yconv, hbias, yc, yd, gates, wb, wo)


def _topk_rows(s, label, k):
    vals, labs = [], []
    for _ in range(k):
        m = jnp.max(s, axis=0, keepdims=True)
        lb = jnp.min(jnp.where(s == m, label, float(2 ** 20)), axis=0, keepdims=True)
        vals.append(m)
        labs.append(lb)
        s = jnp.where(label == lb, -jnp.inf, s)
    return jnp.concatenate(vals, axis=0), jnp.concatenate(labs, axis=0).astype(I32)


_CAND_FIXED_A = ((0, 0), (0, 8), (1, 0), (2, 0), (3, 0))
_CAND_FIXED_B = ((0, 0), (0, 8), (1, 0), (2, 0))


def _select_rows(table, pos):
    out = jnp.zeros(pos.shape, table.dtype)
    for a in range(table.shape[0]):
        out = jnp.where(pos == a, table[a:a + 1, :], out)
    return out


def _peer_route_kernel(q_ref, keys_ref, i_ref, j_ref, g_ref):
    st = _dot_nt(keys_ref[...], q_ref[...].astype(BF16))
    t = st.shape[1]
    key_id = lax.broadcasted_iota(I32, (PEER_NKEYS, t), 0).astype(F32)
    sv1, si1 = _topk_rows(st[0:PEER_NKEYS], key_id, PEER_TOPK)
    sv2, si2 = _topk_rows(st[PEER_NKEYS:2 * PEER_NKEYS], key_id, PEER_TOPK)
    r8 = lax.broadcasted_iota(I32, (8, t), 0)
    cands, labels = [], []
    for a, b0 in _CAND_FIXED_A:
        cands.append(sv1[a:a + 1, :] + sv2[b0:b0 + 8, :])
        labels.append((a * PEER_TOPK + b0 + r8).astype(F32))
    for b, a0 in _CAND_FIXED_B:
        c = sv1[a0:a0 + 8, :] + sv2[b:b + 1, :]
        cands.append(jnp.where(r8 < 4, -jnp.inf, c) if a0 == 0 else c)
        labels.append(((a0 + r8) * PEER_TOPK + b).astype(F32))
    best, pos = _topk_rows(jnp.concatenate(cands, axis=0), jnp.concatenate(labels, axis=0), PEER_TOPK)
    i_ref[...] = _select_rows(si1, pos >> int(math.log2(PEER_TOPK)))
    j_ref[...] = _select_rows(si2, pos & (PEER_TOPK - 1))
    e = jnp.exp(best - jnp.max(best, axis=0, keepdims=True))
    g_ref[...] = e / jnp.sum(e, axis=0, keepdims=True)


def _peer_route(q, keys_blk):
    t, _ = q.shape
    tt = 512
    out = pl.BlockSpec((None, PEER_TOPK, tt), lambda ti, hh: (hh, 0, ti))
    shp = (PEER_HEADS, PEER_TOPK, t)
    return pl.pallas_call(
        _peer_route_kernel,
        grid=(t // tt, PEER_HEADS),
        in_specs=[pl.BlockSpec((tt, PEER_DK), lambda ti, hh: (ti, hh)),
                  pl.BlockSpec((None, 2 * PEER_NKEYS, PEER_DK), lambda ti, hh: (hh, 0, 0))],
        out_specs=[out, out, out],
        out_shape=[jax.ShapeDtypeStruct(shp, I32), jax.ShapeDtypeStruct(shp, I32), jax.ShapeDtypeStruct(shp, F32)],
        compiler_params=_cparams(("arbitrary", "arbitrary")),
        name="peer_route",
    )(q, keys_blk)


GS_HALF = PEER_NKEYS // 2
GS_PITCH = GS_HALF + 1
GS_UNROLL = 16
PEER_TM = 576
PEER_TE = 1024
U32 = jnp.uint32


def _peer_expert_kernel(h_ref, mod_ref, x_ref, i_ref, j_ref, g_ref, ut_ref, v_ref, o_ref, gs_ref, acc_ref,
                        *, n_ctx, tm):
    ti, e = pl.program_id(1), pl.program_id(2)
    n_e = pl.num_programs(2)
    te = ut_ref.shape[1]
    nk = PEER_NKEYS

    @pl.when((pl.program_id(0) == 0) & (ti == 0) & (e == 0))
    def _():
        acc_ref[...] = jnp.zeros(acc_ref.shape, F32)

    @pl.when(e == 0)
    def _():
        sub = lax.broadcasted_iota(I32, (nk, nk), 0)

        def build(t):
            irow = i_ref[pl.ds(t, 1), :]
            jrow = j_ref[pl.ds(t, 1), :]
            grow = g_ref[pl.ds(t, 1), :]
            at = jnp.where(sub == irow, 1.0, 0.0).astype(BF16)
            bt = jnp.where(sub == jrow, grow, 0.0).astype(BF16)
            gt = _dot_nt(at, bt).astype(BF16).astype(F32)
            lo = pltpu.bitcast(gt[0:GS_HALF], U32) >> 16
            hi = pltpu.bitcast(gt[GS_HALF:nk], U32)
            gs_ref[pl.ds(t * GS_PITCH, GS_HALF), :] = hi | lo

        def body(tb, _):
            for u in range(GS_UNROLL):
                build(tb * GS_UNROLL + u)
            return 0

        lax.fori_loop(0, tm // GS_UNROLL, body, 0)

    nb = te // nk // 2

    def gelu(a):
        return 0.5 * a * (1.0 + lax.erf(a * (2.0 ** -0.5)))

    hid = jnp.dot(x_ref[...], ut_ref[...], preferred_element_type=F32)
    first, second = [], []
    for k in range(nb):
        word = gs_ref[pl.ds(e * nb + k, tm, stride=GS_PITCH), :]
        g_lo = pltpu.bitcast(word << 16, F32)
        g_hi = pltpu.bitcast(word & jnp.uint32(0xFFFF0000), F32)
        first.append((g_lo * gelu(hid[:, k * nk:(k + 1) * nk])).astype(BF16))
        second.append((g_hi * gelu(hid[:, (nb + k) * nk:(nb + k + 1) * nk])).astype(BF16))
    acc_ref[...] += jnp.dot(jnp.concatenate(first + second, axis=1), v_ref[...], preferred_element_type=F32)

    @pl.when(e == n_e - 1)
    def _():
        row = ti * tm + lax.broadcasted_iota(I32, acc_ref.shape, 0)
        gate = jnp.where(row < n_ctx, mod_ref[0, 5:6, :], mod_ref[1, 5:6, :])
        o_ref[...] = h_ref[...] + gate * acc_ref[...]
        acc_ref[...] = jnp.zeros(acc_ref.shape, F32)


def _peer_experts(h, modl, xb, i_idx, j_idx, g, ut, v, n_ctx):
    b, s, d = h.shape
    tm, te = PEER_TM, PEER_TE
    n_chunks = ut.shape[0]
    slots = PEER_HEADS * PEER_TOPK
    row = lambda w: pl.BlockSpec((None, tm, w), lambda bi, i, e: (bi, i, 0))
    return pl.pallas_call(
        functools.partial(_peer_expert_kernel, n_ctx=n_ctx, tm=tm),
        grid=(b, s // tm, n_chunks),
        in_specs=[row(d), pl.BlockSpec((None, 2, 8, d), lambda bi, i, e: (bi, 0, 0, 0)),
                  row(d), row(slots), row(slots), row(slots),
                  pl.BlockSpec((None, d, te), lambda bi, i, e: (e, 0, 0)),
                  pl.BlockSpec((te, d), lambda bi, i, e: (e, 0))],
        out_specs=row(d),
        out_shape=jax.ShapeDtypeStruct((b, s, d), F32),
        scratch_shapes=[pltpu.VMEM((tm * GS_PITCH, PEER_NKEYS), U32), pltpu.VMEM((tm, d), F32)],
        compiler_params=_cparams(("arbitrary", "arbitrary", "arbitrary")),
        name="peer_experts",
    )(h, modl, xb, i_idx, j_idx, g, ut, v)


def _final_norm_kernel(x_ref, g_ref, o_ref):
    x = x_ref[...]
    o_ref[...] = x * lax.rsqrt(jnp.mean(x * x, axis=-1, keepdims=True) + EPS) * g_ref[...]


def _final_norm(h, g, n_ctx):
    b, s, d = h.shape
    tm = ROW_TILE
    n_lat = s - n_ctx
    off = n_ctx // tm
    return pl.pallas_call(
        _final_norm_kernel,
        grid=(b, n_lat // tm),
        in_specs=[pl.BlockSpec((None, tm, d), lambda bi, i: (bi, i + off, 0)),
                  pl.BlockSpec((1, d), lambda bi, i: (0, 0))],
        out_specs=pl.BlockSpec((None, tm, d), lambda bi, i: (bi, i, 0)),
        out_shape=jax.ShapeDtypeStruct((b, n_lat, d), F32),
        compiler_params=_cparams(("arbitrary", "arbitrary")),
        name="final_norm",
    )(h, g)


def _prep_w_in(w_in):
    depth, d, _ = w_in.shape
    mixw = w_in[:, :, :2816]
    gq = mixw[:, :, 1536:1792].reshape(depth, d, GQA_HEADS, GQA_DH)[:, :, (0, 2, 1, 3), :].reshape(depth, d, 256)
    mixw = jnp.concatenate([mixw[:, :, :1536], gq, mixw[:, :, 1792:], jnp.zeros((depth, d, MIX_W - 2816), w_in.dtype)], axis=-1)
    return jnp.concatenate([mixw, w_in[:, :, 2816:]], axis=-1).astype(BF16)


def _peer_chunk_order(w):
    n_exp, d = w.shape
    nb = PEER_TE // PEER_NKEYS // 2
    return w.reshape(2, GS_HALF // nb, nb, PEER_NKEYS, d).transpose(1, 0, 2, 3, 4).reshape(n_exp, d)


def _prep_peer_keys(keys):
    depth = keys.shape[0]
    half = PEER_DK // 2
    z = jnp.zeros((depth, PEER_HEADS, PEER_NKEYS, half), keys.dtype)
    top = jnp.concatenate([keys[:, :, 0], z], axis=-1)
    bot = jnp.concatenate([z, keys[:, :, 1]], axis=-1)
    return jnp.concatenate([top, bot], axis=2).astype(BF16)


def kernel(x, c, ctx, c_ctx, w_mod, b_mod, norm1_g, norm2_g, w_in, na_rpb, hy_short_w, hy_short_b, hy_w1, hy_b1, hy_w2, hy_b2, hy_w3, hy_b3, hy_w4, hy_freq, hy_bias, gqa_qn, gqa_kn, diff_lq1, diff_lk1, diff_lq2, diff_lk2, diff_subln, w_branch, w_out, peer_wq, peer_keys, peer_u, peer_v, final_g):
    B, L, D = x.shape
    C = ctx.shape[1]
    S = C + L
    depth = w_mod.shape[0]
    assert C == ROW_TILE and L % ROW_TILE == 0 and D == 1024 and L % GRID_W == 0

    h = jnp.concatenate([ctx, x], axis=1)

    r = -(-(B + 1) // 8) * 8
    cvec = jnp.zeros((r, D), F32).at[:B].set(c).at[B].set(c_ctx)
    modall = _modulation(cvec, w_mod, b_mod)
    mod_lat = modall[:, :B].reshape(depth, B, 1, 6, D)
    mod_ctx = jnp.broadcast_to(modall[:, B].reshape(depth, 1, 1, 6, D), (depth, B, 1, 6, D))
    mods = jnp.pad(jnp.concatenate([mod_ctx, mod_lat], axis=2), ((0, 0), (0, 0), (0, 0), (0, 2), (0, 0)))

    fargs = (hy_w1, hy_b1, hy_w2, hy_b2, hy_w3, hy_b3, hy_w4, hy_freq)
    taps_lat = _conv_taps(_hy_filters(L, *fargs))
    taps_ctx = _conv_taps(_hy_filters(C, *fargs))

    cg, sg = _rope_tables(C, L, GQA_DH, 256)
    cd, sd = _rope_tables(C, L, DIFF_DH, 256)

    lam_init = jnp.asarray([0.8 - 0.6 * math.exp(-0.3 * l) for l in range(depth)], F32)
    lam = (jnp.exp(jnp.sum(diff_lq1.astype(F32) * diff_lk1.astype(F32), axis=-1))
           - jnp.exp(jnp.sum(diff_lq2.astype(F32) * diff_lk2.astype(F32), axis=-1)) + lam_init)
    diff_par = jnp.zeros((depth, 8, LANE), F32).at[:, 0, :].set(lam[:, None]).at[:, 1, :].set(1.0 - lam_init[:, None])

    wb = w_branch.astype(BF16)
    wb = wb.at[:, 2].set(wb[:, 2].reshape(depth, GQA_HEADS, GQA_DH, D)[:, (0, 2, 1, 3)].reshape(depth, BRANCH_W, D))

    xs = dict(
        mods=mods, n1=norm1_g[:, None, :], n2=norm2_g[:, None, :], w_in=_prep_w_in(w_in),
        bias=_na_bias_table(na_rpb, C), sw=hy_short_w, sb=hy_short_b[:, None, :], hbias=hy_bias[:, None, :],
        taps_lat=taps_lat, taps_ctx=taps_ctx,
        qn=jnp.tile(gqa_qn, (1, GQA_HEADS))[:, None, :], kn=jnp.tile(gqa_kn, (1, GQA_KV))[:, None, :],
        diff_par=diff_par, subln=jnp.tile(diff_subln, (1, DIFF_HEADS))[:, None, :],
        wb=wb, wo=w_out.astype(BF16), wq=peer_wq.astype(BF16), keys=_prep_peer_keys(peer_keys),
        u=peer_u, v=peer_v,
    )

    def layer(h, p):
        mix, gates = _in_proj(h, p["mods"], p["n1"], p["w_in"], C)
        pb = _prep(mix, cg, sg, cd, sd, p["qn"], p["kn"])
        ya = _na_attention(pb, p["bias"], C)
        yc = _gqa_attention(pb, C)
        yd = _diff_attention(pb, p["diff_par"], p["subln"], C)
        x0, z = _hy_pre(mix, p["sw"], p["sb"], C)
        yconv = jnp.concatenate([_hy_long_conv(z[:, :C], p["taps_ctx"]), _hy_long_conv(z[:, C:], p["taps_lat"])], axis=1)
        h = _merge(h, p["mods"], ya, x0, z, yconv, p["hbias"], yc, yd, gates, p["wb"], p["wo"], C)

        q, xb = _peer_q(h, p["mods"], p["n2"], p["wq"], C)
        i_idx, j_idx, g = _peer_route(q.reshape(B * S, -1), p["keys"])
        slots = PEER_HEADS * PEER_TOPK
        to_rows = lambda a: jnp.transpose(a, (2, 0, 1)).reshape(B, S, slots)
        ut = jnp.transpose(_peer_chunk_order(p["u"].astype(BF16)).reshape(-1, PEER_TE, D), (0, 2, 1))
        h = _peer_experts(h, p["mods"], xb, to_rows(i_idx), to_rows(j_idx), to_rows(g), ut,
                          _peer_chunk_order(p["v"].astype(BF16)), C)
        return h, None

    h, _ = lax.scan(layer, h, xs)
    return _final_norm(h, final_g[None, :], C)
```

```python
import functools
import math

import numpy as np
import jax
import jax.numpy as jnp
from jax import lax
from jax.experimental import pallas as pl
from jax.experimental.pallas import tpu as pltpu

F32 = jnp.float32
BF16 = jnp.bfloat16
I32 = jnp.int32
HIGHEST = lax.Precision.HIGHEST

EPS = 1e-6
GRID_W = 64
ROPE_THETA = 10000.0
NA_HEADS, NA_DH, NA_KH, NA_KW = 4, 64, 8, 16
HY_W, HY_BANDS, HY_FFN = 256, 16, 64
HY_FAST_DECAY, HY_SLOW_DECAY, HY_TARGET = 0.3, 1.5, 1e-2
GQA_HEADS, GQA_KV, GQA_DH = 4, 2, 64
DIFF_HEADS, DIFF_DH, DIFF_DV = 4, 32, 64
N_BRANCH, BRANCH_W = 4, 256
PEER_HEADS, PEER_NKEYS, PEER_DK, PEER_TOPK = 8, 128, 128, 16

LANE = 128
ROW_TILE = 256
VMEM_LIMIT = 56 * 1024 * 1024
NEG = -1e30
LOG2E = math.log2(math.e)
MIX_W = 3072
GATE_W = N_BRANCH * 1024
COL_TILE = 1024


def _cparams(sem):
    return pltpu.CompilerParams(dimension_semantics=sem, vmem_limit_bytes=VMEM_LIMIT)


def _mod_kernel(c_ref, w_ref, b_ref, o_ref):
    cv = c_ref[...]
    a = cv * jax.nn.sigmoid(cv)
    o_ref[...] = jnp.dot(a, w_ref[...], preferred_element_type=F32, precision=HIGHEST) + b_ref[...]


def _modulation(cvec, w_mod, b_mod):
    depth, d, n = w_mod.shape
    r = cvec.shape[0]
    tn = 1024
    return pl.pallas_call(
        _mod_kernel,
        grid=(depth, n // tn),
        in_specs=[pl.BlockSpec((r, d), lambda l, j: (0, 0)),
                  pl.BlockSpec((None, d, tn), lambda l, j: (l, 0, j)),
                  pl.BlockSpec((None, 1, tn), lambda l, j: (l, 0, j))],
        out_specs=pl.BlockSpec((None, r, tn), lambda l, j: (l, 0, j)),
        out_shape=jax.ShapeDtypeStruct((depth, r, n), F32),
        compiler_params=_cparams(("arbitrary", "arbitrary")),
        name="modulation",
    )(cvec, w_mod, b_mod.reshape(depth, 1, n))


def _norm_modulate(x, mod_ref, g_ref, srow, row0, n_ctx):
    y = x * lax.rsqrt(jnp.mean(x * x, axis=-1, keepdims=True) + EPS) * g_ref[...]
    row = row0 + lax.broadcasted_iota(I32, x.shape, 0)
    is_ctx = row < n_ctx
    shift = jnp.where(is_ctx, mod_ref[0, srow:srow + 1, :], mod_ref[1, srow:srow + 1, :])
    scale = jnp.where(is_ctx, mod_ref[0, srow + 1:srow + 2, :], mod_ref[1, srow + 1:srow + 2, :])
    return y * (1.0 + scale) + shift


def _in_proj_kernel(h_ref, mod_ref, g_ref, w_ref, mix_ref, gate_ref, nb_ref, *, tm, n_ctx, n_mix):
    i, j = pl.program_id(1), pl.program_id(2)

    @pl.when(j == 0)
    def _():
        nb_ref[...] = _norm_modulate(h_ref[...], mod_ref, g_ref, 0, i * tm, n_ctx).astype(BF16)

    @pl.when(j < n_mix)
    def _():
        mix_ref[...] = jnp.dot(nb_ref[...], w_ref[...], preferred_element_type=F32)

    @pl.when(j >= n_mix)
    def _():
        gate_ref[...] = jnp.dot(nb_ref[...], w_ref[...], preferred_element_type=F32).astype(gate_ref.dtype)


def _in_proj(h, modl, g, w, n_ctx):
    b, s, d = h.shape
    tm, tn = 768, COL_TILE
    n_mix = MIX_W // tn
    n_tot = (MIX_W + GATE_W) // tn
    return pl.pallas_call(
        functools.partial(_in_proj_kernel, tm=tm, n_ctx=n_ctx, n_mix=n_mix),
        grid=(b, s // tm, n_tot),
        in_specs=[pl.BlockSpec((None, tm, d), lambda bi, i, j: (bi, i, 0)),
                  pl.BlockSpec((None, 2, 8, d), lambda bi, i, j: (bi, 0, 0, 0)),
                  pl.BlockSpec((1, d), lambda bi, i, j: (0, 0)),
                  pl.BlockSpec((d, tn), lambda bi, i, j: (0, j))],
        out_specs=[pl.BlockSpec((None, tm, tn), lambda bi, i, j: (bi, i, jnp.minimum(j, n_mix - 1))),
                   pl.BlockSpec((None, tm, tn), lambda bi, i, j: (bi, i, jnp.maximum(j - n_mix, 0)))],
        out_shape=[jax.ShapeDtypeStruct((b, s, MIX_W), F32), jax.ShapeDtypeStruct((b, s, GATE_W), BF16)],
        scratch_shapes=[pltpu.VMEM((tm, d), BF16)],
        compiler_params=_cparams(("arbitrary", "arbitrary", "arbitrary")),
        name="in_proj",
    )(h, modl, g, w)


def _lane_group(shape, width):
    return lax.broadcasted_iota(I32, shape, 1) >> int(math.log2(width))


def _head_mean_sq(x, n_heads, dh):
    hid = _lane_group(x.shape, dh)
    x2 = x * x
    ms = jnp.zeros_like(x)
    for hh in range(n_heads):
        m = hid == hh
        s = jnp.sum(jnp.where(m, x2, 0.0), axis=-1, keepdims=True)
        ms = jnp.where(m, s, ms)
    return ms * (1.0 / dh)


def _rope(x, cos, sin_signed, qs):
    w = x.shape[-1]
    lane = lax.broadcasted_iota(I32, x.shape, 1)
    lo = (lane & (2 * qs - 1)) < qs
    partner = jnp.where(lo, pltpu.roll(x, w - qs, 1), pltpu.roll(x, qs, 1))
    return x * cos + partner * sin_signed


def _prep_kernel(na_ref, gqa_ref, dq_ref, dk_ref, dv_ref, cg_ref, sg_ref, cd_ref, sd_ref, qn_ref, kn_ref, o_ref):
    na = na_ref[...]
    o_ref[:, 0:256] = (na[:, 0:256] * (LOG2E * NA_DH ** -0.5)).astype(BF16)
    o_ref[:, 256:768] = na[:, 256:768].astype(BF16)
    gq = gqa_ref[:, 0:256]
    gk = gqa_ref[:, 256:384]
    cg, sg = cg_ref[...], sg_ref[...]
    gq = gq * lax.rsqrt(_head_mean_sq(gq, GQA_HEADS, GQA_DH) + EPS) * qn_ref[...]
    gk = gk * lax.rsqrt(_head_mean_sq(gk, GQA_KV, GQA_DH) + EPS) * kn_ref[...]
    o_ref[:, 768:1024] = (_rope(gq, cg, sg, GQA_DH // 4) * (LOG2E * GQA_DH ** -0.5)).astype(BF16)
    o_ref[:, 1024:1152] = _rope(gk, cg[:, 0:128], sg[:, 0:128], GQA_DH // 4).astype(BF16)
    o_ref[:, 1152:1280] = gqa_ref[:, 384:512].astype(BF16)
    cd, sd = cd_ref[...], sd_ref[...]
    o_ref[:, 1280:1536] = (_rope(dq_ref[...], cd, sd, DIFF_DH // 4) * (LOG2E * DIFF_DH ** -0.5)).astype(BF16)
    o_ref[:, 1536:1792] = _rope(dk_ref[...], cd, sd, DIFF_DH // 4).astype(BF16)
    o_ref[:, 1792:2048] = dv_ref[...].astype(BF16)


def _prep(mix, cg, sg, cd, sd, qn, kn):
    b, s, _ = mix.shape
    tm = ROW_TILE
    tab = pl.BlockSpec((tm, 256), lambda bi, i: (i, 0))
    return pl.pallas_call(
        _prep_kernel,
        grid=(b, s // tm),
        in_specs=[pl.BlockSpec((None, tm, 768), lambda bi, i: (bi, i, 0)),
                  pl.BlockSpec((None, tm, 512), lambda bi, i: (bi, i, 3)),
                  pl.BlockSpec((None, tm, 256), lambda bi, i: (bi, i, 8)),
                  pl.BlockSpec((None, tm, 256), lambda bi, i: (bi, i, 9)),
                  pl.BlockSpec((None, tm, 256), lambda bi, i: (bi, i, 10)),
                  tab, tab, tab, tab,
                  pl.BlockSpec((1, 256), lambda bi, i: (0, 0)),
                  pl.BlockSpec((1, 128), lambda bi, i: (0, 0))],
        out_specs=pl.BlockSpec((None, tm, 2048), lambda bi, i: (bi, i, 0)),
        out_shape=jax.ShapeDtypeStruct((b, s, 2048), BF16),
        compiler_params=_cparams(("arbitrary", "arbitrary")),
        name="qk_prep",
    )(mix, mix, mix, mix, mix, cg, sg, cd, sd, qn, kn)


def _rope_tables(n_ctx, n_lat, dh, width):
    qs = dh // 4
    t = np.arange(n_lat)
    rows, cols = (t // GRID_W).astype(np.float64), (t % GRID_W).astype(np.float64)
    lane = np.arange(width) % dh
    part = lane // (2 * qs)
    u = lane % (2 * qs)
    f = u % qs
    lo = u < qs
    freqs = ROPE_THETA ** (-f.astype(np.float64) / qs)
    pos = np.where(part[None, :] == 0, rows[:, None], cols[:, None])
    ang = (pos.astype(np.float32) * freqs.astype(np.float32)[None, :]).astype(np.float32)
    cos = np.cos(ang.astype(np.float64))
    sin = np.sin(ang.astype(np.float64)) * np.where(lo, -1.0, 1.0)[None, :]
    cos = np.concatenate([np.ones((n_ctx, width)), cos], axis=0)
    sin = np.concatenate([np.zeros((n_ctx, width)), sin], axis=0)
    return jnp.asarray(cos, F32), jnp.asarray(sin, F32)


def _softmax_rows(s):
    m = jnp.max(s, axis=-1, keepdims=True)
    p = jnp.exp2(s - m)
    return p, jnp.sum(p, axis=-1, keepdims=True)


def _dot_nt(a, b):
    return lax.dot_general(a, b, (((1,), (1,)), ((), ())), preferred_element_type=F32)


def _gqa_kernel(q_ref, k_ref, v_ref, o_ref, *, n_ctx):
    i = pl.program_id(1)

    def run(nk):
        k = k_ref[0:nk, :]
        v = v_ref[0:nk, :]
        q = q_ref[...]
        tq = q.shape[0]
        grp = _lane_group((tq, LANE), GQA_DH)

        def scores(g):
            mask = grp == g
            ql = jnp.concatenate([jnp.where(mask, q[:, 0:128], 0), jnp.where(mask, q[:, 128:256], 0)], axis=0)
            return _dot_nt(ql, k)

        outs = []
        s_next = scores(0)
        for g in range(GQA_KV):
            s = s_next
            if g + 1 < GQA_KV:
                s_next = scores(g + 1)
            p, l = _softmax_rows(s)
            outs.append(jnp.dot(p.astype(BF16), v, preferred_element_type=F32) / l)
        for half in range(2):
            sel = jnp.where(grp == 0, outs[0][half * tq:(half + 1) * tq], outs[1][half * tq:(half + 1) * tq])
            o_ref[:, half * 128:(half + 1) * 128] = sel.astype(o_ref.dtype)

    @pl.when(i == 0)
    def _():
        run(n_ctx)

    @pl.when(i > 0)
    def _():
        run(k_ref.shape[0])


def _gqa_attention(pb, n_ctx):
    b, s, _ = pb.shape
    tq = ROW_TILE
    return pl.pallas_call(
        functools.partial(_gqa_kernel, n_ctx=n_ctx),
        grid=(b, s // tq),
        in_specs=[pl.BlockSpec((None, tq, 256), lambda bi, i: (bi, i, 3)),
                  pl.BlockSpec((None, s, 128), lambda bi, i: (bi, 0, 8)),
                  pl.BlockSpec((None, s, 128), lambda bi, i: (bi, 0, 9))],
        out_specs=pl.BlockSpec((None, tq, 256), lambda bi, i: (bi, i, 0)),
        out_shape=jax.ShapeDtypeStruct((b, s, 256), BF16),
        compiler_params=_cparams(("arbitrary", "arbitrary")),
        name="gqa_attention",
    )(pb, pb, pb)


def _diff_kernel(q_ref, k_ref, v_ref, par_ref, g_ref, o_ref, *, n_ctx):
    i = pl.program_id(1)

    def run(nk):
        k = k_ref[0:nk, :]
        v = v_ref[0:nk, :]
        q = q_ref[...]
        tq = q.shape[0]
        lam = par_ref[0:1, 0:1]
        comp = _lane_group(q.shape, DIFF_DH)
        head = _lane_group(q.shape, DIFF_DV)
        acc = jnp.zeros(q.shape, F32)

        def scores(hh):
            ql = jnp.concatenate([jnp.where(comp == 2 * hh, q, 0), jnp.where(comp == 2 * hh + 1, q, 0)], axis=0)
            return _dot_nt(ql, k)

        s_next = scores(0)
        for hh in range(DIFF_HEADS):
            s = s_next
            if hh + 1 < DIFF_HEADS:
                s_next = scores(hh + 1)
            p, l = _softmax_rows(s)
            inv = 1.0 / l
            pc = (p[0:tq] * inv[0:tq] - p[tq:2 * tq] * (lam * inv[tq:2 * tq])).astype(BF16)
            o = jnp.dot(pc, v, preferred_element_type=F32)
            acc = jnp.where(head == hh, o, acc)
        y = acc * lax.rsqrt(_head_mean_sq(acc, DIFF_HEADS, DIFF_DV) + EPS) * g_ref[...]
        o_ref[...] = (y * par_ref[1:2, 0:1]).astype(o_ref.dtype)

    @pl.when(i == 0)
    def _():
        run(n_ctx)

    @pl.when(i > 0)
    def _():
        run(k_ref.shape[0])


def _diff_attention(pb, par, subln, n_ctx):
    b, s, _ = pb.shape
    tq = ROW_TILE
    return pl.pallas_call(
        functools.partial(_diff_kernel, n_ctx=n_ctx),
        grid=(b, s // tq),
        in_specs=[pl.BlockSpec((None, tq, 256), lambda bi, i: (bi, i, 5)),
                  pl.BlockSpec((None, s, 256), lambda bi, i: (bi, 0, 6)),
                  pl.BlockSpec((None, s, 256), lambda bi, i: (bi, 0, 7)),
                  pl.BlockSpec((8, 128), lambda bi, i: (0, 0)),
                  pl.BlockSpec((1, 256), lambda bi, i: (0, 0))],
        out_specs=pl.BlockSpec((None, tq, 256), lambda bi, i: (bi, i, 0)),
        out_shape=jax.ShapeDtypeStruct((b, s, 256), BF16),
        compiler_params=_cparams(("arbitrary", "arbitrary")),
        name="diff_attention",
    )(pb, pb, pb, par, subln)


def _na_kernel(q_ref, k_ref, v_ref, bias_ref, o_ref, *, n_ctx, n_rows):
    i = pl.program_id(1)
    tq = q_ref.shape[0]

    def stack_heads(q):
        head = _lane_group(q.shape, NA_DH)
        return jnp.concatenate([jnp.where(head == hh, q, 0) for hh in range(NA_HEADS)], axis=0)

    def unstack_heads(o, rows):
        head = _lane_group((rows, 256), NA_DH)
        out = jnp.zeros((rows, 256), F32)
        for hh in range(NA_HEADS):
            out = jnp.where(head == hh, o[hh * rows:(hh + 1) * rows], out)
        return out

    @pl.when(i == 0)
    def _():
        kc = k_ref[0:n_ctx, :]
        vc = v_ref[0:n_ctx, :]
        p, l = _softmax_rows(_dot_nt(stack_heads(q_ref[...]), kc))
        o = jnp.dot(p.astype(BF16), vc, preferred_element_type=F32) / l
        o_ref[...] = unstack_heads(o, tq).astype(o_ref.dtype)

    @pl.when(i > 0)
    def _():
        kc = k_ref[0:n_ctx, :]
        vc = v_ref[0:n_ctx, :]
        def scores(rr):
            r = (i - 1) * (tq // GRID_W) + rr
            rs = jnp.clip(r - NA_KH // 2, 0, n_rows - NA_KH)
            start = pl.multiple_of(n_ctx + rs * GRID_W, GRID_W)
            kcat = jnp.concatenate([kc, k_ref[pl.ds(start, NA_KH * GRID_W), :]], axis=0)
            q = q_ref[rr * GRID_W:(rr + 1) * GRID_W, :]
            return _dot_nt(stack_heads(q), kcat) + bias_ref[r - rs], start

        nxt = scores(0)
        for rr in range(tq // GRID_W):
            s, start = nxt
            if rr + 1 < tq // GRID_W:
                nxt = scores(rr + 1)
            vcat = jnp.concatenate([vc, v_ref[pl.ds(start, NA_KH * GRID_W), :]], axis=0)
            p, l = _softmax_rows(s)
            o = jnp.dot(p.astype(BF16), vcat, preferred_element_type=F32) / l
            o_ref[rr * GRID_W:(rr + 1) * GRID_W, :] = unstack_heads(o, GRID_W).astype(o_ref.dtype)


def _na_attention(pb, bias, n_ctx):
    b, s, _ = pb.shape
    tq = ROW_TILE
    n_rows = (s - n_ctx) // GRID_W
    return pl.pallas_call(
        functools.partial(_na_kernel, n_ctx=n_ctx, n_rows=n_rows),
        grid=(b, s // tq),
        in_specs=[pl.BlockSpec((None, tq, 256), lambda bi, i: (bi, i, 0)),
                  pl.BlockSpec((None, s, 256), lambda bi, i: (bi, 0, 1)),
                  pl.BlockSpec((None, s, 256), lambda bi, i: (bi, 0, 2)),
                  pl.BlockSpec(bias.shape, lambda bi, i: (0, 0, 0))],
        out_specs=pl.BlockSpec((None, tq, 256), lambda bi, i: (bi, i, 0)),
        out_shape=jax.ShapeDtypeStruct((b, s, 256), BF16),
        compiler_params=_cparams(("arbitrary", "arbitrary")),
        name="na_attention",
    )(pb, pb, pb, bias)


def _na_bias_table(rpb, n_ctx):
    depth = rpb.shape[0]
    pad = GRID_W - NA_KW
    padded = jnp.pad(rpb.astype(F32), ((0, 0), (0, 0), (0, 0), (pad, pad)))
    cols = jnp.stack([padded[..., GRID_W - 1 - w:2 * GRID_W - 1 - w] for w in range(GRID_W)], axis=3)
    vals = jnp.stack([cols[:, :, NA_KH - 1 - o:2 * NA_KH - 1 - o] for o in range(NA_KH)], axis=2)
    vals = jnp.transpose(vals, (0, 2, 1, 4, 3, 5))
    w = np.arange(GRID_W)[:, None, None]
    kc = np.arange(GRID_W)[None, None, :]
    cs = np.clip(w - NA_KW // 2, 0, GRID_W - NA_KW)
    inwin = np.broadcast_to((kc >= cs) & (kc < cs + NA_KW), (GRID_W, NA_KH, GRID_W))
    vals = jnp.where(jnp.asarray(inwin), vals * LOG2E, NEG).reshape(depth, NA_KH, NA_HEADS * GRID_W, NA_KH * GRID_W)
    return jnp.concatenate([jnp.zeros((depth, NA_KH, NA_HEADS * GRID_W, n_ctx), F32), vals], axis=-1)


def _hy_pre_kernel(u_ref, w_ref, b_ref, x0_ref, z_ref, *, bounds, tm):
    s = u_ref.shape[0]
    w0, w1, w2, bias = w_ref[0:1, :], w_ref[1:2, :], w_ref[2:3, :], b_ref[...]
    rid = lax.broadcasted_iota(I32, (tm, u_ref.shape[1]), 0)
    for c0 in range(0, s, tm):
        u = u_ref[c0:c0 + tm, :]
        prev = pltpu.roll(u, 1, 0)
        nxt = pltpu.roll(u, tm - 1, 0)
        first = jnp.zeros_like(w0) if c0 in bounds else u_ref[c0 - 1:c0, :]
        last = jnp.zeros_like(w0) if (c0 + tm) in bounds else u_ref[c0 + tm:c0 + tm + 1, :]
        prev = jnp.where(rid == 0, first, prev)
        nxt = jnp.where(rid == tm - 1, last, nxt)
        uc = prev * w0 + u * w1 + nxt * w2 + bias
        x0_ref[c0:c0 + tm, :] = uc[:, 0:HY_W]
        z_ref[c0:c0 + tm, :] = uc[:, HY_W:2 * HY_W] * uc[:, 2 * HY_W:3 * HY_W]


def _hy_pre(mix, w_short, b_short, n_ctx):
    b, s, _ = mix.shape
    return pl.pallas_call(
        functools.partial(_hy_pre_kernel, bounds=(0, n_ctx, s), tm=ROW_TILE),
        grid=(b,),
        in_specs=[pl.BlockSpec((None, s, 3 * HY_W), lambda bi: (bi, 0, 1)),
                  pl.BlockSpec((3, 3 * HY_W), lambda bi: (0, 0)),
                  pl.BlockSpec((1, 3 * HY_W), lambda bi: (0, 0))],
        out_specs=[pl.BlockSpec((None, s, HY_W), lambda bi: (bi, 0, 0)),
                   pl.BlockSpec((None, s, HY_W), lambda bi: (bi, 0, 0))],
        out_shape=[jax.ShapeDtypeStruct((b, s, HY_W), F32), jax.ShapeDtypeStruct((b, s, HY_W), F32)],
        compiler_params=_cparams(("arbitrary",)),
        name="hyena_pre",
    )(mix, w_short, b_short)


def _hy_filter_kernel(z_ref, w1_ref, b1_ref, w2_ref, b2_ref, w3_ref, b3_ref, w4_ref, f_ref, dec_ref, o_ref):
    def lin(a, w_ref, b_ref):
        return jnp.dot(a, w_ref[...], preferred_element_type=F32, precision=HIGHEST) + b_ref[...]

    hh = jnp.sin(f_ref[0:1, :] * lin(z_ref[...], w1_ref, b1_ref))
    hh = jnp.sin(f_ref[1:2, :] * lin(hh, w2_ref, b2_ref))
    hh = jnp.sin(f_ref[2:3, :] * lin(hh, w3_ref, b3_ref))
    hh = jnp.dot(hh, w4_ref[...], preferred_element_type=F32, precision=HIGHEST) * dec_ref[...]
    o_ref[...] = hh / jnp.sum(jnp.abs(hh), axis=0, keepdims=True)


def _pad_to(a, shape):
    return jnp.pad(a, [(0, t - s) for s, t in zip(a.shape, shape)])


def _hy_filters(n, w1, b1, w2, b2, w3, b3, w4, freq):
    depth = w1.shape[0]
    t = np.linspace(0.0, 1.0, n, dtype=np.float32)[:, None]
    w = np.float32(2.0 * math.pi / n) * np.arange(n, dtype=np.float32)[:, None]
    bands = np.linspace(1e-4, HY_BANDS - 1, HY_BANDS, dtype=np.float32)[None, :]
    z = np.concatenate([t, np.cos(w * bands), np.sin(w * bands)], axis=-1).astype(np.float32)
    z = np.pad(z, ((0, 0), (0, LANE - z.shape[1])))
    deltas = np.linspace(math.log(HY_TARGET) / HY_SLOW_DECAY, math.log(HY_TARGET) / HY_FAST_DECAY, HY_W, dtype=np.float32)
    deltas = np.tile(np.abs(deltas), 2)
    decay = np.exp(-t * deltas[None, :]).astype(np.float32)
    p = LANE
    args = (jnp.asarray(z), _pad_to(w1, (depth, p, p)), _pad_to(b1[:, None, :], (depth, 1, p)),
            _pad_to(w2, (depth, p, p)), _pad_to(b2[:, None, :], (depth, 1, p)),
            _pad_to(w3, (depth, p, p)), _pad_to(b3[:, None, :], (depth, 1, p)),
            _pad_to(w4, (depth, p, 2 * HY_W)), _pad_to(freq, (depth, 8, p)), jnp.asarray(decay))
    per_layer = lambda shp: pl.BlockSpec((None,) + shp, lambda l: (l,) + (0,) * len(shp))
    const = lambda shp: pl.BlockSpec(shp, lambda l: (0,) * len(shp))
    return pl.pallas_call(
        _hy_filter_kernel,
        grid=(depth,),
        in_specs=[const((n, p)), per_layer((p, p)), per_layer((1, p)), per_layer((p, p)), per_layer((1, p)),
                  per_layer((p, p)), per_layer((1, p)), per_layer((p, 2 * HY_W)), per_layer((8, p)),
                  const((n, 2 * HY_W))],
        out_specs=per_layer((n, 2 * HY_W)),
        out_shape=jax.ShapeDtypeStruct((depth, n, 2 * HY_W), F32),
        compiler_params=_cparams(("arbitrary",)),
        name="hyena_filters",
    )(*args)


def _conv_taps(filt):
    hf, hb = filt[..., :HY_W], filt[..., HY_W:]
    taps = jnp.concatenate([hb[:, :0:-1], (hf[:, 0:1] + hb[:, 0:1]), hf[:, 1:], jnp.zeros_like(hf[:, 0:1])], axis=1)
    return jnp.transpose(taps, (0, 2, 1))


def _hy_conv_kernel(z_ref, k_ref, o_ref, *, nblk, nb):
    cg = z_ref.shape[0]
    n2 = k_ref.shape[1]
    ncols = (2 * nblk - 1) * LANE

    def body(ci, _):
        krow = k_ref[pl.ds(ci, 1), :]
        kb = jnp.broadcast_to(krow, (LANE, n2))
        big = pltpu.roll(kb, n2 - (LANE - 1), 1, stride=1, stride_axis=0)[:, :ncols].astype(BF16)
        o_ref[ci] = jnp.zeros(o_ref.shape[1:], F32)
        for m in range(-(nblk - 1), nblk):
            km = big[:, (m + nblk - 1) * LANE:(m + nblk) * LANE]
            cnt = (nblk - abs(m)) * nb
            src = 0 if m >= 0 else -m * nb
            dst = m * nb if m >= 0 else 0
            zin = z_ref[ci, src:src + cnt, :]
            o_ref[ci, dst:dst + cnt, :] += jnp.dot(zin, km, preferred_element_type=F32)
        return 0

    lax.fori_loop(0, cg, body, 0)


def _hy_conv(zr, taps, nblk, nb):
    c, r, _ = zr.shape
    cg = 8
    return pl.pallas_call(
        functools.partial(_hy_conv_kernel, nblk=nblk, nb=nb),
        grid=(c // cg,),
        in_specs=[pl.BlockSpec((cg, r, LANE), lambda i: (i, 0, 0)),
                  pl.BlockSpec((cg, taps.shape[1]), lambda i: (i, 0))],
        out_specs=pl.BlockSpec((cg, r, LANE), lambda i: (i, 0, 0)),
        out_shape=jax.ShapeDtypeStruct((c, r, LANE), F32),
        compiler_params=_cparams(("arbitrary",)),
        name="hyena_conv",
    )(zr, taps)


def _hy_long_conv(z, taps):
    b, n, c = z.shape
    nblk = n // LANE
    zr = jnp.transpose(z.astype(BF16).reshape(b, nblk, LANE, c), (3, 1, 0, 2)).reshape(c, nblk * b, LANE)
    y = _hy_conv(zr, taps, nblk, b)
    return jnp.transpose(y.reshape(c, nblk, b, LANE), (2, 1, 3, 0)).reshape(b, n, c)


def _merge_kernel(h_ref, mod_ref, ya_ref, x0_ref, z_ref, yconv_ref, hb_ref, yc_ref, yd_ref, gate_ref,
                  wb_ref, wo_ref, g2_ref, wq_ref, o_ref, q_ref, nb_ref, *, n_ctx, tm):
    i = pl.program_id(1)
    yb = (x0_ref[...] * (yconv_ref[...] + z_ref[...] * hb_ref[...])).astype(BF16)
    ys = (ya_ref[...], yb, yc_ref[...], yd_ref[...])
    acc = jnp.zeros(h_ref.shape, F32)
    d = h_ref.shape[1]
    for k in range(N_BRANCH):
        proj = jnp.dot(ys[k], wb_ref[k], preferred_element_type=F32)
        acc = acc + jax.nn.sigmoid(gate_ref[:, k * d:(k + 1) * d].astype(F32)) * proj
    out = jnp.dot(acc.astype(BF16), wo_ref[...], preferred_element_type=F32)
    row = i * tm + lax.broadcasted_iota(I32, out.shape, 0)
    gate = jnp.where(row < n_ctx, mod_ref[0, 2:3, :], mod_ref[1, 2:3, :])
    hn = h_ref[...] + gate * out
    o_ref[...] = hn
    nb = _norm_modulate(hn, mod_ref, g2_ref, 3, i * tm, n_ctx).astype(BF16)
    nb_ref[...] = nb
    q_ref[...] = jnp.dot(nb, wq_ref[...], preferred_element_type=F32)


def _merge(h, modl, ya, x0, z, yconv, hbias, yc, yd, gates, wb, wo, g2, wq, n_ctx):
    b, s, d = h.shape
    tm = ROW_TILE
    row = lambda w: pl.BlockSpec((None, tm, w), lambda bi, i: (bi, i, 0))
    return pl.pallas_call(
        functools.partial(_merge_kernel, n_ctx=n_ctx, tm=tm),
        grid=(b, s // tm),
        in_specs=[row(d), pl.BlockSpec((None, 2, 8, d), lambda bi, i: (bi, 0, 0, 0)),
                  row(256), row(256), row(256), row(256), pl.BlockSpec((1, 256), lambda bi, i: (0, 0)),
                  row(256), row(256), row(GATE_W),
                  pl.BlockSpec(wb.shape, lambda bi, i: (0, 0, 0)),
                  pl.BlockSpec(wo.shape, lambda bi, i: (0, 0)),
                  pl.BlockSpec((1, d), lambda bi, i: (0, 0)),
                  pl.BlockSpec(wq.shape, lambda bi, i: (0, 0))],
        out_specs=[row(d), row(wq.shape[1]), row(d)],
        out_shape=[jax.ShapeDtypeStruct((b, s, d), F32), jax.ShapeDtypeStruct((b, s, wq.shape[1]), F32),
                   jax.ShapeDtypeStruct((b, s, d), BF16)],
        compiler_params=_cparams(("arbitrary", "arbitrary")),
        name="merge",
    )(h, modl, ya, x0, z, yconv, hbias, yc, yd, gates, wb, wo, g2, wq)


def _topk_rows(s, label, k):
    vals, labs = [], []
    for _ in range(k):
        m = jnp.max(s, axis=0, keepdims=True)
        lb = jnp.min(jnp.where(s == m, label, float(2 ** 20)), axis=0, keepdims=True)
        vals.append(m)
        labs.append(lb)
        s = jnp.where(label == lb, -jnp.inf, s)
    return jnp.concatenate(vals, axis=0), jnp.concatenate(labs, axis=0).astype(I32)


_CAND_FIXED_A = ((0, 0), (0, 8), (1, 0), (2, 0), (3, 0))
_CAND_FIXED_B = ((0, 0), (0, 8), (1, 0), (2, 0))


def _select_rows(table, pos):
    out = jnp.zeros(pos.shape, table.dtype)
    for a in range(table.shape[0]):
        out = jnp.where(pos == a, table[a:a + 1, :], out)
    return out


def _peer_route_kernel(q_ref, keys_ref, i_ref, j_ref, g_ref):
    st = _dot_nt(keys_ref[...], q_ref[...].astype(BF16))
    t = st.shape[1]
    key_id = lax.broadcasted_iota(I32, (PEER_NKEYS, t), 0).astype(F32)
    sv1, si1 = _topk_rows(st[0:PEER_NKEYS], key_id, PEER_TOPK)
    sv2, si2 = _topk_rows(st[PEER_NKEYS:2 * PEER_NKEYS], key_id, PEER_TOPK)
    r8 = lax.broadcasted_iota(I32, (8, t), 0)
    cands, labels = [], []
    for a, b0 in _CAND_FIXED_A:
        cands.append(sv1[a:a + 1, :] + sv2[b0:b0 + 8, :])
        labels.append((a * PEER_TOPK + b0 + r8).astype(F32))
    for b, a0 in _CAND_FIXED_B:
        c = sv1[a0:a0 + 8, :] + sv2[b:b + 1, :]
        cands.append(jnp.where(r8 < 4, -jnp.inf, c) if a0 == 0 else c)
        labels.append(((a0 + r8) * PEER_TOPK + b).astype(F32))
    best, pos = _topk_rows(jnp.concatenate(cands, axis=0), jnp.concatenate(labels, axis=0), PEER_TOPK)
    i_ref[...] = _select_rows(si1, pos >> int(math.log2(PEER_TOPK)))
    j_ref[...] = _select_rows(si2, pos & (PEER_TOPK - 1))
    e = jnp.exp(best - jnp.max(best, axis=0, keepdims=True))
    g_ref[...] = e / jnp.sum(e, axis=0, keepdims=True)


def _peer_route(q, keys_blk):
    t, _ = q.shape
    tt = 1024
    assert t % tt == 0
    out = pl.BlockSpec((None, PEER_TOPK, tt), lambda ti, hh: (hh, 0, ti))
    shp = (PEER_HEADS, PEER_TOPK, t)
    return pl.pallas_call(
        _peer_route_kernel,
        grid=(t // tt, PEER_HEADS),
        in_specs=[pl.BlockSpec((tt, PEER_DK), lambda ti, hh: (ti, hh)),
                  pl.BlockSpec((None, 2 * PEER_NKEYS, PEER_DK), lambda ti, hh: (hh, 0, 0))],
        out_specs=[out, out, out],
        out_shape=[jax.ShapeDtypeStruct(shp, I32), jax.ShapeDtypeStruct(shp, I32), jax.ShapeDtypeStruct(shp, F32)],
        compiler_params=_cparams(("arbitrary", "arbitrary")),
        name="peer_route",
    )(q, keys_blk)


GS_HALF = PEER_NKEYS // 2
GS_PITCH = GS_HALF + 1
GS_UNROLL = 32
PEER_TM = 576
PEER_TE = 1024
U32 = jnp.uint32


def _peer_expert_kernel(h_ref, mod_ref, x_ref, i_ref, j_ref, g_ref, ut_ref, v_ref, o_ref, gs_ref, acc_ref,
                        *, n_ctx, tm):
    ti, e = pl.program_id(1), pl.program_id(2)
    n_e = pl.num_programs(2)
    te = ut_ref.shape[1]
    nk = PEER_NKEYS

    @pl.when((pl.program_id(0) == 0) & (ti == 0) & (e == 0))
    def _():
        acc_ref[...] = jnp.zeros(acc_ref.shape, F32)

    @pl.when(e == 0)
    def _():
        sub = lax.broadcasted_iota(I32, (nk, nk), 0)

        def build(t):
            irow = i_ref[pl.ds(t, 1), :]
            jrow = j_ref[pl.ds(t, 1), :]
            grow = g_ref[pl.ds(t, 1), :]
            at = jnp.where(sub == irow, 1.0, 0.0).astype(BF16)
            bt = jnp.where(sub == jrow, grow, 0.0).astype(BF16)
            gt = _dot_nt(at, bt).astype(BF16).astype(F32)
            lo = pltpu.bitcast(gt[0:GS_HALF], U32) >> 16
            hi = pltpu.bitcast(gt[GS_HALF:nk], U32)
            gs_ref[pl.ds(t * GS_PITCH, GS_HALF), :] = hi | lo

        def body(tb, _):
            for u in range(GS_UNROLL):
                build(tb * GS_UNROLL + u)
            return 0

        lax.fori_loop(0, tm // GS_UNROLL, body, 0)

    nb = te // nk // 2

    def gelu(a):
        return 0.5 * a * (1.0 + lax.erf(a * (2.0 ** -0.5)))

    hid = jnp.dot(x_ref[...], ut_ref[...], preferred_element_type=F32)
    first, second = [], []
    for k in range(nb):
        word = gs_ref[pl.ds(e * nb + k, tm, stride=GS_PITCH), :]
        g_lo = pltpu.bitcast(word << 16, F32)
        g_hi = pltpu.bitcast(word & jnp.uint32(0xFFFF0000), F32)
        first.append((g_lo * gelu(hid[:, k * nk:(k + 1) * nk])).astype(BF16))
        second.append((g_hi * gelu(hid[:, (nb + k) * nk:(nb + k + 1) * nk])).astype(BF16))
    acc_ref[...] += jnp.dot(jnp.concatenate(first + second, axis=1), v_ref[...], preferred_element_type=F32)

    @pl.when(e == n_e - 1)
    def _():
        row = ti * tm + lax.broadcasted_iota(I32, acc_ref.shape, 0)
        gate = jnp.where(row < n_ctx, mod_ref[0, 5:6, :], mod_ref[1, 5:6, :])
        o_ref[...] = h_ref[...] + gate * acc_ref[...]
        acc_ref[...] = jnp.zeros(acc_ref.shape, F32)


def _peer_experts(h, modl, xb, i_idx, j_idx, g, ut, v, n_ctx):
    b, s, d = h.shape
    tm, te = PEER_TM, PEER_TE
    n_chunks = ut.shape[0]
    slots = PEER_HEADS * PEER_TOPK
    row = lambda w: pl.BlockSpec((None, tm, w), lambda bi, i, e: (bi, i, 0))
    return pl.pallas_call(
        functools.partial(_peer_expert_kernel, n_ctx=n_ctx, tm=tm),
        grid=(b, s // tm, n_chunks),
        in_specs=[row(d), pl.BlockSpec((None, 2, 8, d), lambda bi, i, e: (bi, 0, 0, 0)),
                  row(d), row(slots), row(slots), row(slots),
                  pl.BlockSpec((None, d, te), lambda bi, i, e: (e, 0, 0)),
                  pl.BlockSpec((te, d), lambda bi, i, e: (e, 0))],
        out_specs=row(d),
        out_shape=jax.ShapeDtypeStruct((b, s, d), F32),
        scratch_shapes=[pltpu.VMEM((tm * GS_PITCH, PEER_NKEYS), U32), pltpu.VMEM((tm, d), F32)],
        compiler_params=_cparams(("arbitrary", "arbitrary", "arbitrary")),
        name="peer_experts",
    )(h, modl, xb, i_idx, j_idx, g, ut, v)


def _final_norm_kernel(x_ref, g_ref, o_ref):
    x = x_ref[...]
    o_ref[...] = x * lax.rsqrt(jnp.mean(x * x, axis=-1, keepdims=True) + EPS) * g_ref[...]


def _final_norm(h, g, n_ctx):
    b, s, d = h.shape
    tm = ROW_TILE
    n_lat = s - n_ctx
    off = n_ctx // tm
    return pl.pallas_call(
        _final_norm_kernel,
        grid=(b, n_lat // tm),
        in_specs=[pl.BlockSpec((None, tm, d), lambda bi, i: (bi, i + off, 0)),
                  pl.BlockSpec((1, d), lambda bi, i: (0, 0))],
        out_specs=pl.BlockSpec((None, tm, d), lambda bi, i: (bi, i, 0)),
        out_shape=jax.ShapeDtypeStruct((b, n_lat, d), F32),
        compiler_params=_cparams(("arbitrary", "arbitrary")),
        name="final_norm",
    )(h, g)


def _prep_w_in(w_in):
    depth, d, _ = w_in.shape
    mixw = w_in[:, :, :2816]
    gq = mixw[:, :, 1536:1792].reshape(depth, d, GQA_HEADS, GQA_DH)[:, :, (0, 2, 1, 3), :].reshape(depth, d, 256)
    mixw = jnp.concatenate([mixw[:, :, :1536], gq, mixw[:, :, 1792:], jnp.zeros((depth, d, MIX_W - 2816), w_in.dtype)], axis=-1)
    return jnp.concatenate([mixw, w_in[:, :, 2816:]], axis=-1).astype(BF16)


def _peer_chunk_order(w):
    n_exp, d = w.shape
    nb = PEER_TE // PEER_NKEYS // 2
    return w.reshape(2, GS_HALF // nb, nb, PEER_NKEYS, d).transpose(1, 0, 2, 3, 4).reshape(n_exp, d)


def _prep_peer_keys(keys):
    depth = keys.shape[0]
    half = PEER_DK // 2
    z = jnp.zeros((depth, PEER_HEADS, PEER_NKEYS, half), keys.dtype)
    top = jnp.concatenate([keys[:, :, 0], z], axis=-1)
    bot = jnp.concatenate([z, keys[:, :, 1]], axis=-1)
    return jnp.concatenate([top, bot], axis=2).astype(BF16)


def kernel(x, c, ctx, c_ctx, w_mod, b_mod, norm1_g, norm2_g, w_in, na_rpb, hy_short_w, hy_short_b, hy_w1, hy_b1, hy_w2, hy_b2, hy_w3, hy_b3, hy_w4, hy_freq, hy_bias, gqa_qn, gqa_kn, diff_lq1, diff_lk1, diff_lq2, diff_lk2, diff_subln, w_branch, w_out, peer_wq, peer_keys, peer_u, peer_v, final_g):
    B, L, D = x.shape
    C = ctx.shape[1]
    S = C + L
    depth = w_mod.shape[0]
    assert C == ROW_TILE and L % ROW_TILE == 0 and D == 1024 and L % GRID_W == 0

    h = jnp.concatenate([ctx, x], axis=1)

    r = -(-(B + 1) // 8) * 8
    cvec = jnp.zeros((r, D), F32).at[:B].set(c).at[B].set(c_ctx)
    modall = _modulation(cvec, w_mod, b_mod)
    mod_lat = modall[:, :B].reshape(depth, B, 1, 6, D)
    mod_ctx = jnp.broadcast_to(modall[:, B].reshape(depth, 1, 1, 6, D), (depth, B, 1, 6, D))
    mods = jnp.pad(jnp.concatenate([mod_ctx, mod_lat], axis=2), ((0, 0), (0, 0), (0, 0), (0, 2), (0, 0)))

    fargs = (hy_w1, hy_b1, hy_w2, hy_b2, hy_w3, hy_b3, hy_w4, hy_freq)
    taps_lat = _conv_taps(_hy_filters(L, *fargs))
    taps_ctx = _conv_taps(_hy_filters(C, *fargs))

    cg, sg = _rope_tables(C, L, GQA_DH, 256)
    cd, sd = _rope_tables(C, L, DIFF_DH, 256)

    lam_init = jnp.asarray([0.8 - 0.6 * math.exp(-0.3 * l) for l in range(depth)], F32)
    lam = (jnp.exp(jnp.sum(diff_lq1.astype(F32) * diff_lk1.astype(F32), axis=-1))
           - jnp.exp(jnp.sum(diff_lq2.astype(F32) * diff_lk2.astype(F32), axis=-1)) + lam_init)
    diff_par = jnp.zeros((depth, 8, LANE), F32).at[:, 0, :].set(lam[:, None]).at[:, 1, :].set(1.0 - lam_init[:, None])

    wb = w_branch.astype(BF16)
    wb = wb.at[:, 2].set(wb[:, 2].reshape(depth, GQA_HEADS, GQA_DH, D)[:, (0, 2, 1, 3)].reshape(depth, BRANCH_W, D))

    xs = dict(
        mods=mods, n1=norm1_g[:, None, :], n2=norm2_g[:, None, :], w_in=_prep_w_in(w_in),
        bias=_na_bias_table(na_rpb, C), sw=hy_short_w, sb=hy_short_b[:, None, :], hbias=hy_bias[:, None, :],
        taps_lat=taps_lat, taps_ctx=taps_ctx,
        qn=jnp.tile(gqa_qn, (1, GQA_HEADS))[:, None, :], kn=jnp.tile(gqa_kn, (1, GQA_KV))[:, None, :],
        diff_par=diff_par, subln=jnp.tile(diff_subln, (1, DIFF_HEADS))[:, None, :],
        wb=wb, wo=w_out.astype(BF16), wq=peer_wq.astype(BF16), keys=_prep_peer_keys(peer_keys),
        u=peer_u, v=peer_v,
    )

    def layer(h, p):
        mix, gates = _in_proj(h, p["mods"], p["n1"], p["w_in"], C)
        pb = _prep(mix, cg, sg, cd, sd, p["qn"], p["kn"])
        ya = _na_attention(pb, p["bias"], C)
        yc = _gqa_attention(pb, C)
        yd = _diff_attention(pb, p["diff_par"], p["subln"], C)
        x0, z = _hy_pre(mix, p["sw"], p["sb"], C)
        yconv = jnp.concatenate([_hy_long_conv(z[:, :C], p["taps_ctx"]), _hy_long_conv(z[:, C:], p["taps_lat"])], axis=1)
        h, q, xb = _merge(h, p["mods"], ya, x0, z, yconv, p["hbias"], yc, yd, gates, p["wb"], p["wo"],
                          p["n2"], p["wq"], C)
        i_idx, j_idx, g = _peer_route(q.reshape(B * S, -1), p["keys"])
        slots = PEER_HEADS * PEER_TOPK
        to_rows = lambda a: jnp.transpose(a, (2, 0, 1)).reshape(B, S, slots)
        ut = jnp.transpose(_peer_chunk_order(p["u"].astype(BF16)).reshape(-1, PEER_TE, D), (0, 2, 1))
        h = _peer_experts(h, p["mods"], xb, to_rows(i_idx), to_rows(j_idx), to_rows(g), ut,
                          _peer_chunk_order(p["v"].astype(BF16)), C)
        return h, None

    h, _ = lax.scan(layer, h, xs)
    return _final_norm(h, final_g[None, :], C)
```

```python
import functools
import math

import numpy as np
import jax
import jax.numpy as jnp
from jax import lax
from jax.experimental import pallas as pl
from jax.experimental.pallas import tpu as pltpu

F32 = jnp.float32
BF16 = jnp.bfloat16
I32 = jnp.int32
HIGHEST = lax.Precision.HIGHEST

EPS = 1e-6
GRID_W = 64
ROPE_THETA = 10000.0
NA_HEADS, NA_DH, NA_KH, NA_KW = 4, 64, 8, 16
HY_W, HY_BANDS, HY_FFN = 256, 16, 64
HY_FAST_DECAY, HY_SLOW_DECAY, HY_TARGET = 0.3, 1.5, 1e-2
GQA_HEADS, GQA_KV, GQA_DH = 4, 2, 64
DIFF_HEADS, DIFF_DH, DIFF_DV = 4, 32, 64
N_BRANCH, BRANCH_W = 4, 256
PEER_HEADS, PEER_NKEYS, PEER_DK, PEER_TOPK = 8, 128, 128, 16

LANE = 128
ROW_TILE = 256
VMEM_LIMIT = 56 * 1024 * 1024
NEG = -1e30
LOG2E = math.log2(math.e)
MIX_W = 3072
GATE_W = N_BRANCH * 1024
COL_TILE = 1024


def _cparams(sem):
    return pltpu.CompilerParams(dimension_semantics=sem, vmem_limit_bytes=VMEM_LIMIT)


def _mod_kernel(c_ref, w_ref, b_ref, o_ref):
    cv = c_ref[...]
    a = cv * jax.nn.sigmoid(cv)
    o_ref[...] = jnp.dot(a, w_ref[...], preferred_element_type=F32, precision=HIGHEST) + b_ref[...]


def _modulation(cvec, w_mod, b_mod):
    depth, d, n = w_mod.shape
    r = cvec.shape[0]
    tn = 1024
    return pl.pallas_call(
        _mod_kernel,
        grid=(depth, n // tn),
        in_specs=[pl.BlockSpec((r, d), lambda l, j: (0, 0)),
                  pl.BlockSpec((None, d, tn), lambda l, j: (l, 0, j)),
                  pl.BlockSpec((None, 1, tn), lambda l, j: (l, 0, j))],
        out_specs=pl.BlockSpec((None, r, tn), lambda l, j: (l, 0, j)),
        out_shape=jax.ShapeDtypeStruct((depth, r, n), F32),
        compiler_params=_cparams(("arbitrary", "arbitrary")),
        name="modulation",
    )(cvec, w_mod, b_mod.reshape(depth, 1, n))


def _norm_modulate(x, mod_ref, g_ref, srow, row0, n_ctx):
    y = x * lax.rsqrt(jnp.mean(x * x, axis=-1, keepdims=True) + EPS) * g_ref[...]
    row = row0 + lax.broadcasted_iota(I32, x.shape, 0)
    is_ctx = row < n_ctx
    shift = jnp.where(is_ctx, mod_ref[0, srow:srow + 1, :], mod_ref[1, srow:srow + 1, :])
    scale = jnp.where(is_ctx, mod_ref[0, srow + 1:srow + 2, :], mod_ref[1, srow + 1:srow + 2, :])
    return y * (1.0 + scale) + shift


def _in_proj_kernel(h_ref, mod_ref, g_ref, w_ref, cg_ref, sg_ref, cd_ref, sd_ref, qn_ref, kn_ref,
                    na_ref, gqa_ref, diff_ref, hy_ref, gate_ref, nb_ref, *, tm, n_ctx, n_mix):
    i, j = pl.program_id(1), pl.program_id(2)

    @pl.when(j == 0)
    def _():
        nb_ref[...] = _norm_modulate(h_ref[...], mod_ref, g_ref, 0, i * tm, n_ctx).astype(BF16)

    def proj():
        return jnp.dot(nb_ref[...], w_ref[...], preferred_element_type=F32)

    @pl.when(j == 0)
    def _():
        res = proj()
        na_ref[:, 0:256] = (res[:, 0:256] * (LOG2E * NA_DH ** -0.5)).astype(BF16)
        na_ref[:, 256:768] = res[:, 256:768].astype(BF16)
        hy_ref[:, 0:HY_W] = res[:, 768:1024]

    @pl.when(j == 1)
    def _():
        res = proj()
        hy_ref[:, HY_W:3 * HY_W] = res[:, 0:512]
        cg, sg = cg_ref[...], sg_ref[...]
        gq, gk = res[:, 512:768], res[:, 768:896]
        gq = gq * lax.rsqrt(_head_mean_sq(gq, GQA_HEADS, GQA_DH) + EPS) * qn_ref[...]
        gk = gk * lax.rsqrt(_head_mean_sq(gk, GQA_KV, GQA_DH) + EPS) * kn_ref[...]
        gqa_ref[:, 0:256] = (_rope(gq, cg, sg, GQA_DH // 4) * (LOG2E * GQA_DH ** -0.5)).astype(BF16)
        gqa_ref[:, 256:384] = _rope(gk, cg[:, 0:128], sg[:, 0:128], GQA_DH // 4).astype(BF16)
        gqa_ref[:, 384:512] = res[:, 896:1024].astype(BF16)

    @pl.when(j == 2)
    def _():
        res = proj()
        cd, sd = cd_ref[...], sd_ref[...]
        diff_ref[:, 0:256] = (_rope(res[:, 0:256], cd, sd, DIFF_DH // 4) * (LOG2E * DIFF_DH ** -0.5)).astype(BF16)
        diff_ref[:, 256:512] = _rope(res[:, 256:512], cd, sd, DIFF_DH // 4).astype(BF16)
        diff_ref[:, 512:768] = res[:, 512:768].astype(BF16)

    @pl.when(j >= n_mix)
    def _():
        gate_ref[...] = proj().astype(gate_ref.dtype)


def _in_proj(h, modl, g, w, cg, sg, cd, sd, qn, kn, n_ctx):
    b, s, d = h.shape
    tm, tn = 768, COL_TILE
    n_mix = MIX_W // tn
    n_tot = (MIX_W + GATE_W) // tn
    assert n_mix == 3
    const = lambda shp: pl.BlockSpec(shp, lambda bi, i, j: (0,) * len(shp))
    tab = pl.BlockSpec((tm, 256), lambda bi, i, j: (i, 0))
    rows = lambda wd: pl.BlockSpec((None, tm, wd), lambda bi, i, j: (bi, i, 0))
    return pl.pallas_call(
        functools.partial(_in_proj_kernel, tm=tm, n_ctx=n_ctx, n_mix=n_mix),
        grid=(b, s // tm, n_tot),
        in_specs=[rows(d), pl.BlockSpec((None, 2, 8, d), lambda bi, i, j: (bi, 0, 0, 0)), const((1, d)),
                  pl.BlockSpec((d, tn), lambda bi, i, j: (0, j)),
                  tab, tab, tab, tab, const((1, 256)), const((1, 128))],
        out_specs=[rows(768), rows(512), rows(768), rows(3 * HY_W),
                   pl.BlockSpec((None, tm, tn), lambda bi, i, j: (bi, i, jnp.maximum(j - n_mix, 0)))],
        out_shape=[jax.ShapeDtypeStruct((b, s, 768), BF16), jax.ShapeDtypeStruct((b, s, 512), BF16),
                   jax.ShapeDtypeStruct((b, s, 768), BF16), jax.ShapeDtypeStruct((b, s, 3 * HY_W), F32),
                   jax.ShapeDtypeStruct((b, s, GATE_W), BF16)],
        scratch_shapes=[pltpu.VMEM((tm, d), BF16)],
        compiler_params=_cparams(("arbitrary", "arbitrary", "arbitrary")),
        name="in_proj",
    )(h, modl, g, w, cg, sg, cd, sd, qn, kn)


def _lane_group(shape, width):
    return lax.broadcasted_iota(I32, shape, 1) >> int(math.log2(width))


def _head_mean_sq(x, n_heads, dh):
    hid = _lane_group(x.shape, dh)
    x2 = x * x
    ms = jnp.zeros_like(x)
    for hh in range(n_heads):
        m = hid == hh
        s = jnp.sum(jnp.where(m, x2, 0.0), axis=-1, keepdims=True)
        ms = jnp.where(m, s, ms)
    return ms * (1.0 / dh)


def _rope(x, cos, sin_signed, qs):
    w = x.shape[-1]
    lane = lax.broadcasted_iota(I32, x.shape, 1)
    lo = (lane & (2 * qs - 1)) < qs
    partner = jnp.where(lo, pltpu.roll(x, w - qs, 1), pltpu.roll(x, qs, 1))
    return x * cos + partner * sin_signed


def _rope_tables(n_ctx, n_lat, dh, width):
    qs = dh // 4
    t = np.arange(n_lat)
    rows, cols = (t // GRID_W).astype(np.float64), (t % GRID_W).astype(np.float64)
    lane = np.arange(width) % dh
    part = lane // (2 * qs)
    u = lane % (2 * qs)
    f = u % qs
    lo = u < qs
    freqs = ROPE_THETA ** (-f.astype(np.float64) / qs)
    pos = np.where(part[None, :] == 0, rows[:, None], cols[:, None])
    ang = (pos.astype(np.float32) * freqs.astype(np.float32)[None, :]).astype(np.float32)
    cos = np.cos(ang.astype(np.float64))
    sin = np.sin(ang.astype(np.float64)) * np.where(lo, -1.0, 1.0)[None, :]
    cos = np.concatenate([np.ones((n_ctx, width)), cos], axis=0)
    sin = np.concatenate([np.zeros((n_ctx, width)), sin], axis=0)
    return jnp.asarray(cos, F32), jnp.asarray(sin, F32)


def _softmax_rows(s):
    m = jnp.max(s, axis=-1, keepdims=True)
    p = jnp.exp2(s - m)
    return p, jnp.sum(p, axis=-1, keepdims=True)


def _dot_nt(a, b):
    return lax.dot_general(a, b, (((1,), (1,)), ((), ())), preferred_element_type=F32)


def _gqa_kernel(q_ref, k_ref, v_ref, o_ref, *, n_ctx):
    i = pl.program_id(1)

    def run(nk):
        k = k_ref[0:nk, :]
        v = v_ref[0:nk, :]
        q = q_ref[...]
        tq = q.shape[0]
        grp = _lane_group((tq, LANE), GQA_DH)

        def scores(g):
            mask = grp == g
            ql = jnp.concatenate([jnp.where(mask, q[:, 0:128], 0), jnp.where(mask, q[:, 128:256], 0)], axis=0)
            return _dot_nt(ql, k)

        outs = []
        s_next = scores(0)
        for g in range(GQA_KV):
            s = s_next
            if g + 1 < GQA_KV:
                s_next = scores(g + 1)
            p, l = _softmax_rows(s)
            outs.append(jnp.dot(p.astype(BF16), v, preferred_element_type=F32) / l)
        for half in range(2):
            sel = jnp.where(grp == 0, outs[0][half * tq:(half + 1) * tq], outs[1][half * tq:(half + 1) * tq])
            o_ref[:, half * 128:(half + 1) * 128] = sel.astype(o_ref.dtype)

    @pl.when(i == 0)
    def _():
        run(n_ctx)

    @pl.when(i > 0)
    def _():
        run(k_ref.shape[0])


def _gqa_attention(pb, n_ctx):
    b, s, _ = pb.shape
    tq = ROW_TILE
    return pl.pallas_call(
        functools.partial(_gqa_kernel, n_ctx=n_ctx),
        grid=(b, s // tq),
        in_specs=[pl.BlockSpec((None, tq, 256), lambda bi, i: (bi, i, 0)),
                  pl.BlockSpec((None, s, 128), lambda bi, i: (bi, 0, 2)),
                  pl.BlockSpec((None, s, 128), lambda bi, i: (bi, 0, 3))],
        out_specs=pl.BlockSpec((None, tq, 256), lambda bi, i: (bi, i, 0)),
        out_shape=jax.ShapeDtypeStruct((b, s, 256), BF16),
        compiler_params=_cparams(("arbitrary", "arbitrary")),
        name="gqa_attention",
    )(pb, pb, pb)


def _diff_kernel(q_ref, k_ref, v_ref, par_ref, g_ref, o_ref, *, n_ctx):
    i = pl.program_id(1)

    def run(nk):
        k = k_ref[0:nk, :]
        v = v_ref[0:nk, :]
        q = q_ref[...]
        tq = q.shape[0]
        lam = par_ref[0:1, 0:1]
        comp = _lane_group(q.shape, DIFF_DH)
        head = _lane_group(q.shape, DIFF_DV)
        acc = jnp.zeros(q.shape, F32)

        def scores(hh):
            ql = jnp.concatenate([jnp.where(comp == 2 * hh, q, 0), jnp.where(comp == 2 * hh + 1, q, 0)], axis=0)
            return _dot_nt(ql, k)

        s_next = scores(0)
        for hh in range(DIFF_HEADS):
            s = s_next
            if hh + 1 < DIFF_HEADS:
                s_next = scores(hh + 1)
            p, l = _softmax_rows(s)
            inv = 1.0 / l
            pc = (p[0:tq] * inv[0:tq] - p[tq:2 * tq] * (lam * inv[tq:2 * tq])).astype(BF16)
            o = jnp.dot(pc, v, preferred_element_type=F32)
            acc = jnp.where(head == hh, o, acc)
        y = acc * lax.rsqrt(_head_mean_sq(acc, DIFF_HEADS, DIFF_DV) + EPS) * g_ref[...]
        o_ref[...] = (y * par_ref[1:2, 0:1]).astype(o_ref.dtype)

    @pl.when(i == 0)
    def _():
        run(n_ctx)

    @pl.when(i > 0)
    def _():
        run(k_ref.shape[0])


def _diff_attention(pb, par, subln, n_ctx):
    b, s, _ = pb.shape
    tq = ROW_TILE
    return pl.pallas_call(
        functools.partial(_diff_kernel, n_ctx=n_ctx),
        grid=(b, s // tq),
        in_specs=[pl.BlockSpec((None, tq, 256), lambda bi, i: (bi, i, 0)),
                  pl.BlockSpec((None, s, 256), lambda bi, i: (bi, 0, 1)),
                  pl.BlockSpec((None, s, 256), lambda bi, i: (bi, 0, 2)),
                  pl.BlockSpec((8, 128), lambda bi, i: (0, 0)),
                  pl.BlockSpec((1, 256), lambda bi, i: (0, 0))],
        out_specs=pl.BlockSpec((None, tq, 256), lambda bi, i: (bi, i, 0)),
        out_shape=jax.ShapeDtypeStruct((b, s, 256), BF16),
        compiler_params=_cparams(("arbitrary", "arbitrary")),
        name="diff_attention",
    )(pb, pb, pb, par, subln)


def _na_kernel(q_ref, k_ref, v_ref, bias_ref, o_ref, *, n_ctx, n_rows):
    i = pl.program_id(1)
    tq = q_ref.shape[0]

    def stack_heads(q):
        head = _lane_group(q.shape, NA_DH)
        return jnp.concatenate([jnp.where(head == hh, q, 0) for hh in range(NA_HEADS)], axis=0)

    def unstack_heads(o, rows):
        head = _lane_group((rows, 256), NA_DH)
        out = jnp.zeros((rows, 256), F32)
        for hh in range(NA_HEADS):
            out = jnp.where(head == hh, o[hh * rows:(hh + 1) * rows], out)
        return out

    @pl.when(i == 0)
    def _():
        kc = k_ref[0:n_ctx, :]
        vc = v_ref[0:n_ctx, :]
        p, l = _softmax_rows(_dot_nt(stack_heads(q_ref[...]), kc))
        o = jnp.dot(p.astype(BF16), vc, preferred_element_type=F32) / l
        o_ref[...] = unstack_heads(o, tq).astype(o_ref.dtype)

    @pl.when(i > 0)
    def _():
        kc = k_ref[0:n_ctx, :]
        vc = v_ref[0:n_ctx, :]
        def scores(rr):
            r = (i - 1) * (tq // GRID_W) + rr
            rs = jnp.clip(r - NA_KH // 2, 0, n_rows - NA_KH)
            start = pl.multiple_of(n_ctx + rs * GRID_W, GRID_W)
            kcat = jnp.concatenate([kc, k_ref[pl.ds(start, NA_KH * GRID_W), :]], axis=0)
            q = q_ref[rr * GRID_W:(rr + 1) * GRID_W, :]
            return _dot_nt(stack_heads(q), kcat) + bias_ref[r - rs], start

        nxt = scores(0)
        for rr in range(tq // GRID_W):
            s, start = nxt
            if rr + 1 < tq // GRID_W:
                nxt = scores(rr + 1)
            vcat = jnp.concatenate([vc, v_ref[pl.ds(start, NA_KH * GRID_W), :]], axis=0)
            p, l = _softmax_rows(s)
            o = jnp.dot(p.astype(BF16), vcat, preferred_element_type=F32) / l
            o_ref[rr * GRID_W:(rr + 1) * GRID_W, :] = unstack_heads(o, GRID_W).astype(o_ref.dtype)


def _na_attention(pb, bias, n_ctx):
    b, s, _ = pb.shape
    tq = ROW_TILE
    n_rows = (s - n_ctx) // GRID_W
    return pl.pallas_call(
        functools.partial(_na_kernel, n_ctx=n_ctx, n_rows=n_rows),
        grid=(b, s // tq),
        in_specs=[pl.BlockSpec((None, tq, 256), lambda bi, i: (bi, i, 0)),
                  pl.BlockSpec((None, s, 256), lambda bi, i: (bi, 0, 1)),
                  pl.BlockSpec((None, s, 256), lambda bi, i: (bi, 0, 2)),
                  pl.BlockSpec(bias.shape, lambda bi, i: (0, 0, 0))],
        out_specs=pl.BlockSpec((None, tq, 256), lambda bi, i: (bi, i, 0)),
        out_shape=jax.ShapeDtypeStruct((b, s, 256), BF16),
        compiler_params=_cparams(("arbitrary", "arbitrary")),
        name="na_attention",
    )(pb, pb, pb, bias)


def _na_bias_table(rpb, n_ctx):
    depth = rpb.shape[0]
    pad = GRID_W - NA_KW
    padded = jnp.pad(rpb.astype(F32), ((0, 0), (0, 0), (0, 0), (pad, pad)))
    cols = jnp.stack([padded[..., GRID_W - 1 - w:2 * GRID_W - 1 - w] for w in range(GRID_W)], axis=3)
    vals = jnp.stack([cols[:, :, NA_KH - 1 - o:2 * NA_KH - 1 - o] for o in range(NA_KH)], axis=2)
    vals = jnp.transpose(vals, (0, 2, 1, 4, 3, 5))
    w = np.arange(GRID_W)[:, None, None]
    kc = np.arange(GRID_W)[None, None, :]
    cs = np.clip(w - NA_KW // 2, 0, GRID_W - NA_KW)
    inwin = np.broadcast_to((kc >= cs) & (kc < cs + NA_KW), (GRID_W, NA_KH, GRID_W))
    vals = jnp.where(jnp.asarray(inwin), vals * LOG2E, NEG).reshape(depth, NA_KH, NA_HEADS * GRID_W, NA_KH * GRID_W)
    return jnp.concatenate([jnp.zeros((depth, NA_KH, NA_HEADS * GRID_W, n_ctx), F32), vals], axis=-1)


def _hy_pre_kernel(u_ref, w_ref, b_ref, x0_ref, z_ref, *, bounds, tm):
    s = u_ref.shape[0]
    w0, w1, w2, bias = w_ref[0:1, :], w_ref[1:2, :], w_ref[2:3, :], b_ref[...]
    rid = lax.broadcasted_iota(I32, (tm, u_ref.shape[1]), 0)
    for c0 in range(0, s, tm):
        u = u_ref[c0:c0 + tm, :]
        prev = pltpu.roll(u, 1, 0)
        nxt = pltpu.roll(u, tm - 1, 0)
        first = jnp.zeros_like(w0) if c0 in bounds else u_ref[c0 - 1:c0, :]
        last = jnp.zeros_like(w0) if (c0 + tm) in bounds else u_ref[c0 + tm:c0 + tm + 1, :]
        prev = jnp.where(rid == 0, first, prev)
        nxt = jnp.where(rid == tm - 1, last, nxt)
        uc = prev * w0 + u * w1 + nxt * w2 + bias
        x0_ref[c0:c0 + tm, :] = uc[:, 0:HY_W]
        z_ref[c0:c0 + tm, :] = uc[:, HY_W:2 * HY_W] * uc[:, 2 * HY_W:3 * HY_W]


def _hy_pre(mix, w_short, b_short, n_ctx):
    b, s, _ = mix.shape
    return pl.pallas_call(
        functools.partial(_hy_pre_kernel, bounds=(0, n_ctx, s), tm=ROW_TILE),
        grid=(b,),
        in_specs=[pl.BlockSpec((None, s, 3 * HY_W), lambda bi: (bi, 0, 0)),
                  pl.BlockSpec((3, 3 * HY_W), lambda bi: (0, 0)),
                  pl.BlockSpec((1, 3 * HY_W), lambda bi: (0, 0))],
        out_specs=[pl.BlockSpec((None, s, HY_W), lambda bi: (bi, 0, 0)),
                   pl.BlockSpec((None, s, HY_W), lambda bi: (bi, 0, 0))],
        out_shape=[jax.ShapeDtypeStruct((b, s, HY_W), F32), jax.ShapeDtypeStruct((b, s, HY_W), F32)],
        compiler_params=_cparams(("arbitrary",)),
        name="hyena_pre",
    )(mix, w_short, b_short)


def _hy_filter_kernel(z_ref, w1_ref, b1_ref, w2_ref, b2_ref, w3_ref, b3_ref, w4_ref, f_ref, dec_ref, o_ref):
    def lin(a, w_ref, b_ref):
        return jnp.dot(a, w_ref[...], preferred_element_type=F32, precision=HIGHEST) + b_ref[...]

    hh = jnp.sin(f_ref[0:1, :] * lin(z_ref[...], w1_ref, b1_ref))
    hh = jnp.sin(f_ref[1:2, :] * lin(hh, w2_ref, b2_ref))
    hh = jnp.sin(f_ref[2:3, :] * lin(hh, w3_ref, b3_ref))
    hh = jnp.dot(hh, w4_ref[...], preferred_element_type=F32, precision=HIGHEST) * dec_ref[...]
    o_ref[...] = hh / jnp.sum(jnp.abs(hh), axis=0, keepdims=True)


def _pad_to(a, shape):
    return jnp.pad(a, [(0, t - s) for s, t in zip(a.shape, shape)])


def _hy_filters(n, w1, b1, w2, b2, w3, b3, w4, freq):
    depth = w1.shape[0]
    t = np.linspace(0.0, 1.0, n, dtype=np.float32)[:, None]
    w = np.float32(2.0 * math.pi / n) * np.arange(n, dtype=np.float32)[:, None]
    bands = np.linspace(1e-4, HY_BANDS - 1, HY_BANDS, dtype=np.float32)[None, :]
    z = np.concatenate([t, np.cos(w * bands), np.sin(w * bands)], axis=-1).astype(np.float32)
    z = np.pad(z, ((0, 0), (0, LANE - z.shape[1])))
    deltas = np.linspace(math.log(HY_TARGET) / HY_SLOW_DECAY, math.log(HY_TARGET) / HY_FAST_DECAY, HY_W, dtype=np.float32)
    deltas = np.tile(np.abs(deltas), 2)
    decay = np.exp(-t * deltas[None, :]).astype(np.float32)
    p = LANE
    args = (jnp.asarray(z), _pad_to(w1, (depth, p, p)), _pad_to(b1[:, None, :], (depth, 1, p)),
            _pad_to(w2, (depth, p, p)), _pad_to(b2[:, None, :], (depth, 1, p)),
            _pad_to(w3, (depth, p, p)), _pad_to(b3[:, None, :], (depth, 1, p)),
            _pad_to(w4, (depth, p, 2 * HY_W)), _pad_to(freq, (depth, 8, p)), jnp.asarray(decay))
    per_layer = lambda shp: pl.BlockSpec((None,) + shp, lambda l: (l,) + (0,) * len(shp))
    const = lambda shp: pl.BlockSpec(shp, lambda l: (0,) * len(shp))
    return pl.pallas_call(
        _hy_filter_kernel,
        grid=(depth,),
        in_specs=[const((n, p)), per_layer((p, p)), per_layer((1, p)), per_layer((p, p)), per_layer((1, p)),
                  per_layer((p, p)), per_layer((1, p)), per_layer((p, 2 * HY_W)), per_layer((8, p)),
                  const((n, 2 * HY_W))],
        out_specs=per_layer((n, 2 * HY_W)),
        out_shape=jax.ShapeDtypeStruct((depth, n, 2 * HY_W), F32),
        compiler_params=_cparams(("arbitrary",)),
        name="hyena_filters",
    )(*args)


def _conv_taps(filt):
    hf, hb = filt[..., :HY_W], filt[..., HY_W:]
    taps = jnp.concatenate([hb[:, :0:-1], (hf[:, 0:1] + hb[:, 0:1]), hf[:, 1:], jnp.zeros_like(hf[:, 0:1])], axis=1)
    return jnp.transpose(taps, (0, 2, 1))


def _hy_conv_kernel(z_ref, k_ref, o_ref, *, nblk, nb):
    cg = z_ref.shape[0]
    n2 = k_ref.shape[1]
    ncols = (2 * nblk - 1) * LANE

    def body(ci, _):
        krow = k_ref[pl.ds(ci, 1), :]
        kb = jnp.broadcast_to(krow, (LANE, n2))
        big = pltpu.roll(kb, n2 - (LANE - 1), 1, stride=1, stride_axis=0)[:, :ncols].astype(BF16)
        o_ref[ci] = jnp.zeros(o_ref.shape[1:], F32)
        for m in range(-(nblk - 1), nblk):
            km = big[:, (m + nblk - 1) * LANE:(m + nblk) * LANE]
            cnt = (nblk - abs(m)) * nb
            src = 0 if m >= 0 else -m * nb
            dst = m * nb if m >= 0 else 0
            zin = z_ref[ci, src:src + cnt, :]
            o_ref[ci, dst:dst + cnt, :] += jnp.dot(zin, km, preferred_element_type=F32)
        return 0

    lax.fori_loop(0, cg, body, 0)


def _hy_conv(zr, taps, nblk, nb):
    c, r, _ = zr.shape
    cg = 8
    return pl.pallas_call(
        functools.partial(_hy_conv_kernel, nblk=nblk, nb=nb),
        grid=(c // cg,),
        in_specs=[pl.BlockSpec((cg, r, LANE), lambda i: (i, 0, 0)),
                  pl.BlockSpec((cg, taps.shape[1]), lambda i: (i, 0))],
        out_specs=pl.BlockSpec((cg, r, LANE), lambda i: (i, 0, 0)),
        out_shape=jax.ShapeDtypeStruct((c, r, LANE), F32),
        compiler_params=_cparams(("arbitrary",)),
        name="hyena_conv",
    )(zr, taps)


def _hy_long_conv(z, taps):
    b, n, c = z.shape
    nblk = n // LANE
    zr = jnp.transpose(z.astype(BF16).reshape(b, nblk, LANE, c), (3, 1, 0, 2)).reshape(c, nblk * b, LANE)
    y = _hy_conv(zr, taps, nblk, b)
    return jnp.transpose(y.reshape(c, nblk, b, LANE), (2, 1, 3, 0)).reshape(b, n, c)


def _merge_kernel(h_ref, mod_ref, ya_ref, x0_ref, z_ref, yconv_ref, hb_ref, yc_ref, yd_ref, gate_ref,
                  wb_ref, wo_ref, g2_ref, wq_ref, o_ref, q_ref, nb_ref, *, n_ctx, tm):
    i = pl.program_id(1)
    yb = (x0_ref[...] * (yconv_ref[...] + z_ref[...] * hb_ref[...])).astype(BF16)
    ys = (ya_ref[...], yb, yc_ref[...], yd_ref[...])
    acc = jnp.zeros(h_ref.shape, F32)
    d = h_ref.shape[1]
    for k in range(N_BRANCH):
        proj = jnp.dot(ys[k], wb_ref[k], preferred_element_type=F32)
        acc = acc + jax.nn.sigmoid(gate_ref[:, k * d:(k + 1) * d].astype(F32)) * proj
    out = jnp.dot(acc.astype(BF16), wo_ref[...], preferred_element_type=F32)
    row = i * tm + lax.broadcasted_iota(I32, out.shape, 0)
    gate = jnp.where(row < n_ctx, mod_ref[0, 2:3, :], mod_ref[1, 2:3, :])
    hn = h_ref[...] + gate * out
    o_ref[...] = hn
    nb = _norm_modulate(hn, mod_ref, g2_ref, 3, i * tm, n_ctx).astype(BF16)
    nb_ref[...] = nb
    q_ref[...] = jnp.dot(nb, wq_ref[...], preferred_element_type=F32)


def _merge(h, modl, ya, x0, z, yconv, hbias, yc, yd, gates, wb, wo, g2, wq, n_ctx):
    b, s, d = h.shape
    tm = ROW_TILE
    row = lambda w: pl.BlockSpec((None, tm, w), lambda bi, i: (bi, i, 0))
    return pl.pallas_call(
        functools.partial(_merge_kernel, n_ctx=n_ctx, tm=tm),
        grid=(b, s // tm),
        in_specs=[row(d), pl.BlockSpec((None, 2, 8, d), lambda bi, i: (bi, 0, 0, 0)),
                  row(256), row(256), row(256), row(256), pl.BlockSpec((1, 256), lambda bi, i: (0, 0)),
                  row(256), row(256), row(GATE_W),
                  pl.BlockSpec(wb.shape, lambda bi, i: (0, 0, 0)),
                  pl.BlockSpec(wo.shape, lambda bi, i: (0, 0)),
                  pl.BlockSpec((1, d), lambda bi, i: (0, 0)),
                  pl.BlockSpec(wq.shape, lambda bi, i: (0, 0))],
        out_specs=[row(d), row(wq.shape[1]), row(d)],
        out_shape=[jax.ShapeDtypeStruct((b, s, d), F32), jax.ShapeDtypeStruct((b, s, wq.shape[1]), F32),
                   jax.ShapeDtypeStruct((b, s, d), BF16)],
        compiler_params=_cparams(("arbitrary", "arbitrary")),
        name="merge",
    )(h, modl, ya, x0, z, yconv, hbias, yc, yd, gates, wb, wo, g2, wq)


def _topk_rows(s, label, k):
    vals, labs = [], []
    for _ in range(k):
        m = jnp.max(s, axis=0, keepdims=True)
        lb = jnp.min(jnp.where(s == m, label, float(2 ** 20)), axis=0, keepdims=True)
        vals.append(m)
        labs.append(lb)
        s = jnp.where(label == lb, -jnp.inf, s)
    return jnp.concatenate(vals, axis=0), jnp.concatenate(labs, axis=0).astype(I32)


_CAND_FIXED_A = ((0, 0), (0, 8), (1, 0), (2, 0), (3, 0))
_CAND_FIXED_B = ((0, 0), (0, 8), (1, 0), (2, 0))


def _select_rows(table, pos):
    out = jnp.zeros(pos.shape, table.dtype)
    for a in range(table.shape[0]):
        out = jnp.where(pos == a, table[a:a + 1, :], out)
    return out


def _peer_route_kernel(q_ref, keys_ref, i_ref, j_ref, g_ref):
    st = _dot_nt(keys_ref[...], q_ref[...].astype(BF16))
    t = st.shape[1]
    key_id = lax.broadcasted_iota(I32, (PEER_NKEYS, t), 0).astype(F32)
    sv1, si1 = _topk_rows(st[0:PEER_NKEYS], key_id, PEER_TOPK)
    sv2, si2 = _topk_rows(st[PEER_NKEYS:2 * PEER_NKEYS], key_id, PEER_TOPK)
    r8 = lax.broadcasted_iota(I32, (8, t), 0)
    cands, labels = [], []
    for a, b0 in _CAND_FIXED_A:
        cands.append(sv1[a:a + 1, :] + sv2[b0:b0 + 8, :])
        labels.append((a * PEER_TOPK + b0 + r8).astype(F32))
    for b, a0 in _CAND_FIXED_B:
        c = sv1[a0:a0 + 8, :] + sv2[b:b + 1, :]
        cands.append(jnp.where(r8 < 4, -jnp.inf, c) if a0 == 0 else c)
        labels.append(((a0 + r8) * PEER_TOPK + b).astype(F32))
    best, pos = _topk_rows(jnp.concatenate(cands, axis=0), jnp.concatenate(labels, axis=0), PEER_TOPK)
    i_ref[...] = _select_rows(si1, pos >> int(math.log2(PEER_TOPK)))
    j_ref[...] = _select_rows(si2, pos & (PEER_TOPK - 1))
    e = jnp.exp(best - jnp.max(best, axis=0, keepdims=True))
    g_ref[...] = e / jnp.sum(e, axis=0, keepdims=True)


def _peer_route(q, keys_blk):
    t, _ = q.shape
    tt = 1024
    assert t % tt == 0
    out = pl.BlockSpec((None, PEER_TOPK, tt), lambda ti, hh: (hh, 0, ti))
    shp = (PEER_HEADS, PEER_TOPK, t)
    return pl.pallas_call(
        _peer_route_kernel,
        grid=(t // tt, PEER_HEADS),
        in_specs=[pl.BlockSpec((tt, PEER_DK), lambda ti, hh: (ti, hh)),
                  pl.BlockSpec((None, 2 * PEER_NKEYS, PEER_DK), lambda ti, hh: (hh, 0, 0))],
        out_specs=[out, out, out],
        out_shape=[jax.ShapeDtypeStruct(shp, I32), jax.ShapeDtypeStruct(shp, I32), jax.ShapeDtypeStruct(shp, F32)],
        compiler_params=_cparams(("arbitrary", "arbitrary")),
        name="peer_route",
    )(q, keys_blk)


GS_HALF = PEER_NKEYS // 2
GS_PITCH = GS_HALF + 1
GS_UNROLL = 32
PEER_TM = 576
PEER_TE = 1024
U32 = jnp.uint32


def _peer_expert_kernel(h_ref, mod_ref, x_ref, i_ref, j_ref, g_ref, ut_ref, v_ref, o_ref, gs_ref, acc_ref,
                        *, n_ctx, tm):
    ti, e = pl.program_id(1), pl.program_id(2)
    n_e = pl.num_programs(2)
    te = ut_ref.shape[1]
    nk = PEER_NKEYS

    @pl.when((pl.program_id(0) == 0) & (ti == 0) & (e == 0))
    def _():
        acc_ref[...] = jnp.zeros(acc_ref.shape, F32)

    @pl.when(e == 0)
    def _():
        sub = lax.broadcasted_iota(I32, (nk, nk), 0)

        def build(t):
            irow = i_ref[pl.ds(t, 1), :]
            jrow = j_ref[pl.ds(t, 1), :]
            grow = g_ref[pl.ds(t, 1), :]
            at = jnp.where(sub == irow, 1.0, 0.0).astype(BF16)
            bt = jnp.where(sub == jrow, grow, 0.0).astype(BF16)
            gt = _dot_nt(at, bt).astype(BF16).astype(F32)
            lo = pltpu.bitcast(gt[0:GS_HALF], U32) >> 16
            hi = pltpu.bitcast(gt[GS_HALF:nk], U32)
            gs_ref[pl.ds(t * GS_PITCH, GS_HALF), :] = hi | lo

        def body(tb, _):
            for u in range(GS_UNROLL):
                build(tb * GS_UNROLL + u)
            return 0

        lax.fori_loop(0, tm // GS_UNROLL, body, 0)

    nb = te // nk // 2

    def gelu(a):
        return 0.5 * a * (1.0 + lax.erf(a * (2.0 ** -0.5)))

    hid = jnp.dot(x_ref[...], ut_ref[...], preferred_element_type=F32)
    first, second = [], []
    for k in range(nb):
        word = gs_ref[pl.ds(e * nb + k, tm, stride=GS_PITCH), :]
        g_lo = pltpu.bitcast(word << 16, F32)
        g_hi = pltpu.bitcast(word & jnp.uint32(0xFFFF0000), F32)
        first.append((g_lo * gelu(hid[:, k * nk:(k + 1) * nk])).astype(BF16))
        second.append((g_hi * gelu(hid[:, (nb + k) * nk:(nb + k + 1) * nk])).astype(BF16))
    acc_ref[...] += jnp.dot(jnp.concatenate(first + second, axis=1), v_ref[...], preferred_element_type=F32)

    @pl.when(e == n_e - 1)
    def _():
        row = ti * tm + lax.broadcasted_iota(I32, acc_ref.shape, 0)
        gate = jnp.where(row < n_ctx, mod_ref[0, 5:6, :], mod_ref[1, 5:6, :])
        o_ref[...] = h_ref[...] + gate * acc_ref[...]
        acc_ref[...] = jnp.zeros(acc_ref.shape, F32)


def _peer_experts(h, modl, xb, i_idx, j_idx, g, ut, v, n_ctx):
    b, s, d = h.shape
    tm, te = PEER_TM, PEER_TE
    n_chunks = ut.shape[0]
    slots = PEER_HEADS * PEER_TOPK
    row = lambda w: pl.BlockSpec((None, tm, w), lambda bi, i, e: (bi, i, 0))
    return pl.pallas_call(
        functools.partial(_peer_expert_kernel, n_ctx=n_ctx, tm=tm),
        grid=(b, s // tm, n_chunks),
        in_specs=[row(d), pl.BlockSpec((None, 2, 8, d), lambda bi, i, e: (bi, 0, 0, 0)),
                  row(d), row(slots), row(slots), row(slots),
                  pl.BlockSpec((None, d, te), lambda bi, i, e: (e, 0, 0)),
                  pl.BlockSpec((te, d), lambda bi, i, e: (e, 0))],
        out_specs=row(d),
        out_shape=jax.ShapeDtypeStruct((b, s, d), F32),
        scratch_shapes=[pltpu.VMEM((tm * GS_PITCH, PEER_NKEYS), U32), pltpu.VMEM((tm, d), F32)],
        compiler_params=_cparams(("arbitrary", "arbitrary", "arbitrary")),
        name="peer_experts",
    )(h, modl, xb, i_idx, j_idx, g, ut, v)


def _final_norm_kernel(x_ref, g_ref, o_ref):
    x = x_ref[...]
    o_ref[...] = x * lax.rsqrt(jnp.mean(x * x, axis=-1, keepdims=True) + EPS) * g_ref[...]


def _final_norm(h, g, n_ctx):
    b, s, d = h.shape
    tm = ROW_TILE
    n_lat = s - n_ctx
    off = n_ctx // tm
    return pl.pallas_call(
        _final_norm_kernel,
        grid=(b, n_lat // tm),
        in_specs=[pl.BlockSpec((None, tm, d), lambda bi, i: (bi, i + off, 0)),
                  pl.BlockSpec((1, d), lambda bi, i: (0, 0))],
        out_specs=pl.BlockSpec((None, tm, d), lambda bi, i: (bi, i, 0)),
        out_shape=jax.ShapeDtypeStruct((b, n_lat, d), F32),
        compiler_params=_cparams(("arbitrary", "arbitrary")),
        name="final_norm",
    )(h, g)


def _prep_w_in(w_in):
    depth, d, _ = w_in.shape
    mixw = w_in[:, :, :2816]
    gq = mixw[:, :, 1536:1792].reshape(depth, d, GQA_HEADS, GQA_DH)[:, :, (0, 2, 1, 3), :].reshape(depth, d, 256)
    mixw = jnp.concatenate([mixw[:, :, :1536], gq, mixw[:, :, 1792:], jnp.zeros((depth, d, MIX_W - 2816), w_in.dtype)], axis=-1)
    return jnp.concatenate([mixw, w_in[:, :, 2816:]], axis=-1).astype(BF16)


def _peer_chunk_order(w):
    n_exp, d = w.shape
    nb = PEER_TE // PEER_NKEYS // 2
    return w.reshape(2, GS_HALF // nb, nb, PEER_NKEYS, d).transpose(1, 0, 2, 3, 4).reshape(n_exp, d)


def _prep_peer_keys(keys):
    depth = keys.shape[0]
    half = PEER_DK // 2
    z = jnp.zeros((depth, PEER_HEADS, PEER_NKEYS, half), keys.dtype)
    top = jnp.concatenate([keys[:, :, 0], z], axis=-1)
    bot = jnp.concatenate([z, keys[:, :, 1]], axis=-1)
    return jnp.concatenate([top, bot], axis=2).astype(BF16)


def kernel(x, c, ctx, c_ctx, w_mod, b_mod, norm1_g, norm2_g, w_in, na_rpb, hy_short_w, hy_short_b, hy_w1, hy_b1, hy_w2, hy_b2, hy_w3, hy_b3, hy_w4, hy_freq, hy_bias, gqa_qn, gqa_kn, diff_lq1, diff_lk1, diff_lq2, diff_lk2, diff_subln, w_branch, w_out, peer_wq, peer_keys, peer_u, peer_v, final_g):
    B, L, D = x.shape
    C = ctx.shape[1]
    S = C + L
    depth = w_mod.shape[0]
    assert C == ROW_TILE and L % ROW_TILE == 0 and D == 1024 and L % GRID_W == 0

    h = jnp.concatenate([ctx, x], axis=1)

    r = -(-(B + 1) // 8) * 8
    cvec = jnp.zeros((r, D), F32).at[:B].set(c).at[B].set(c_ctx)
    modall = _modulation(cvec, w_mod, b_mod)
    mod_lat = modall[:, :B].reshape(depth, B, 1, 6, D)
    mod_ctx = jnp.broadcast_to(modall[:, B].reshape(depth, 1, 1, 6, D), (depth, B, 1, 6, D))
    mods = jnp.pad(jnp.concatenate([mod_ctx, mod_lat], axis=2), ((0, 0), (0, 0), (0, 0), (0, 2), (0, 0)))

    fargs = (hy_w1, hy_b1, hy_w2, hy_b2, hy_w3, hy_b3, hy_w4, hy_freq)
    taps_lat = _conv_taps(_hy_filters(L, *fargs))
    taps_ctx = _conv_taps(_hy_filters(C, *fargs))

    cg, sg = _rope_tables(C, L, GQA_DH, 256)
    cd, sd = _rope_tables(C, L, DIFF_DH, 256)

    lam_init = jnp.asarray([0.8 - 0.6 * math.exp(-0.3 * l) for l in range(depth)], F32)
    lam = (jnp.exp(jnp.sum(diff_lq1.astype(F32) * diff_lk1.astype(F32), axis=-1))
           - jnp.exp(jnp.sum(diff_lq2.astype(F32) * diff_lk2.astype(F32), axis=-1)) + lam_init)
    diff_par = jnp.zeros((depth, 8, LANE), F32).at[:, 0, :].set(lam[:, None]).at[:, 1, :].set(1.0 - lam_init[:, None])

    wb = w_branch.astype(BF16)
    wb = wb.at[:, 2].set(wb[:, 2].reshape(depth, GQA_HEADS, GQA_DH, D)[:, (0, 2, 1, 3)].reshape(depth, BRANCH_W, D))

    xs = dict(
        mods=mods, n1=norm1_g[:, None, :], n2=norm2_g[:, None, :], w_in=_prep_w_in(w_in),
        bias=_na_bias_table(na_rpb, C), sw=hy_short_w, sb=hy_short_b[:, None, :], hbias=hy_bias[:, None, :],
        taps_lat=taps_lat, taps_ctx=taps_ctx,
        qn=jnp.tile(gqa_qn, (1, GQA_HEADS))[:, None, :], kn=jnp.tile(gqa_kn, (1, GQA_KV))[:, None, :],
        diff_par=diff_par, subln=jnp.tile(diff_subln, (1, DIFF_HEADS))[:, None, :],
        wb=wb, wo=w_out.astype(BF16), wq=peer_wq.astype(BF16), keys=_prep_peer_keys(peer_keys),
        u=peer_u, v=peer_v,
    )

    def layer(h, p):
        pb_na, pb_gqa, pb_diff, hy, gates = _in_proj(h, p["mods"], p["n1"], p["w_in"], cg, sg, cd, sd,
                                                     p["qn"], p["kn"], C)
        ya = _na_attention(pb_na, p["bias"], C)
        yc = _gqa_attention(pb_gqa, C)
        yd = _diff_attention(pb_diff, p["diff_par"], p["subln"], C)
        x0, z = _hy_pre(hy, p["sw"], p["sb"], C)
        yconv = jnp.concatenate([_hy_long_conv(z[:, :C], p["taps_ctx"]), _hy_long_conv(z[:, C:], p["taps_lat"])], axis=1)
        h, q, xb = _merge(h, p["mods"], ya, x0, z, yconv, p["hbias"], yc, yd, gates, p["wb"], p["wo"],
                          p["n2"], p["wq"], C)
        i_idx, j_idx, g = _peer_route(q.reshape(B * S, -1), p["keys"])
        slots = PEER_HEADS * PEER_TOPK
        to_rows = lambda a: jnp.transpose(a, (2, 0, 1)).reshape(B, S, slots)
        ut = jnp.transpose(_peer_chunk_order(p["u"].astype(BF16)).reshape(-1, PEER_TE, D), (0, 2, 1))
        h = _peer_experts(h, p["mods"], xb, to_rows(i_idx), to_rows(j_idx), to_rows(g), ut,
                          _peer_chunk_order(p["v"].astype(BF16)), C)
        return h, None

    h, _ = lax.scan(layer, h, xs)
    return _final_norm(h, final_g[None, :], C)
```

```python
import functools
import math

import numpy as np
import jax
import jax.numpy as jnp
from jax import lax
from jax.experimental import pallas as pl
from jax.experimental.pallas import tpu as pltpu

F32 = jnp.float32
BF16 = jnp.bfloat16
I32 = jnp.int32
HIGHEST = lax.Precision.HIGHEST

EPS = 1e-6
GRID_W = 64
ROPE_THETA = 10000.0
NA_HEADS, NA_DH, NA_KH, NA_KW = 4, 64, 8, 16
HY_W, HY_BANDS, HY_FFN = 256, 16, 64
HY_FAST_DECAY, HY_SLOW_DECAY, HY_TARGET = 0.3, 1.5, 1e-2
GQA_HEADS, GQA_KV, GQA_DH = 4, 2, 64
DIFF_HEADS, DIFF_DH, DIFF_DV = 4, 32, 64
N_BRANCH, BRANCH_W = 4, 256
PEER_HEADS, PEER_NKEYS, PEER_DK, PEER_TOPK = 8, 128, 128, 16

LANE = 128
ROW_TILE = 256
VMEM_LIMIT = 56 * 1024 * 1024
NEG = -1e30
LOG2E = math.log2(math.e)
MIX_W = 3072
GATE_W = N_BRANCH * 1024
COL_TILE = 1024


def _cparams(sem):
    return pltpu.CompilerParams(dimension_semantics=sem, vmem_limit_bytes=VMEM_LIMIT)


def _mod_kernel(c_ref, w_ref, b_ref, o_ref):
    cv = c_ref[...]
    a = cv * jax.nn.sigmoid(cv)
    o_ref[...] = jnp.dot(a, w_ref[...], preferred_element_type=F32, precision=HIGHEST) + b_ref[...]


def _modulation(cvec, w_mod, b_mod):
    depth, d, n = w_mod.shape
    r = cvec.shape[0]
    tn = 1024
    return pl.pallas_call(
        _mod_kernel,
        grid=(depth, n // tn),
        in_specs=[pl.BlockSpec((r, d), lambda l, j: (0, 0)),
                  pl.BlockSpec((None, d, tn), lambda l, j: (l, 0, j)),
                  pl.BlockSpec((None, 1, tn), lambda l, j: (l, 0, j))],
        out_specs=pl.BlockSpec((None, r, tn), lambda l, j: (l, 0, j)),
        out_shape=jax.ShapeDtypeStruct((depth, r, n), F32),
        compiler_params=_cparams(("arbitrary", "arbitrary")),
        name="modulation",
    )(cvec, w_mod, b_mod.reshape(depth, 1, n))


def _norm_modulate(x, mod_ref, g_ref, srow, row0, n_ctx):
    y = x * lax.rsqrt(jnp.mean(x * x, axis=-1, keepdims=True) + EPS) * g_ref[...]
    row = row0 + lax.broadcasted_iota(I32, x.shape, 0)
    is_ctx = row < n_ctx
    shift = jnp.where(is_ctx, mod_ref[0, srow:srow + 1, :], mod_ref[1, srow:srow + 1, :])
    scale = jnp.where(is_ctx, mod_ref[0, srow + 1:srow + 2, :], mod_ref[1, srow + 1:srow + 2, :])
    return y * (1.0 + scale) + shift


def _in_proj_kernel(h_ref, mod_ref, g_ref, w_ref, cg_ref, sg_ref, cd_ref, sd_ref, qn_ref, kn_ref,
                    na_ref, gqa_ref, diff_ref, hy_ref, gate_ref, nb_ref, *, tm, n_ctx, n_mix):
    i, j = pl.program_id(1), pl.program_id(2)

    @pl.when(j == 0)
    def _():
        nb_ref[...] = _norm_modulate(h_ref[...], mod_ref, g_ref, 0, i * tm, n_ctx).astype(BF16)

    def proj():
        return jnp.dot(nb_ref[...], w_ref[...], preferred_element_type=F32)

    @pl.when(j == 0)
    def _():
        res = proj()
        na_ref[:, 0:256] = (res[:, 0:256] * (LOG2E * NA_DH ** -0.5)).astype(BF16)
        na_ref[:, 256:768] = res[:, 256:768].astype(BF16)
        hy_ref[:, 0:HY_W] = res[:, 768:1024]

    @pl.when(j == 1)
    def _():
        res = proj()
        hy_ref[:, HY_W:3 * HY_W] = res[:, 0:512]
        cg, sg = cg_ref[...], sg_ref[...]
        gq, gk = res[:, 512:768], res[:, 768:896]
        gq = gq * lax.rsqrt(_head_mean_sq(gq, GQA_HEADS, GQA_DH) + EPS) * qn_ref[...]
        gk = gk * lax.rsqrt(_head_mean_sq(gk, GQA_KV, GQA_DH) + EPS) * kn_ref[...]
        gqa_ref[:, 0:256] = (_rope(gq, cg, sg, GQA_DH // 4) * (LOG2E * GQA_DH ** -0.5)).astype(BF16)
        gqa_ref[:, 256:384] = _rope(gk, cg[:, 0:128], sg[:, 0:128], GQA_DH // 4).astype(BF16)
        gqa_ref[:, 384:512] = res[:, 896:1024].astype(BF16)

    @pl.when(j == 2)
    def _():
        res = proj()
        cd, sd = cd_ref[...], sd_ref[...]
        diff_ref[:, 0:256] = (_rope(res[:, 0:256], cd, sd, DIFF_DH // 4) * (LOG2E * DIFF_DH ** -0.5)).astype(BF16)
        diff_ref[:, 256:512] = _rope(res[:, 256:512], cd, sd, DIFF_DH // 4).astype(BF16)
        diff_ref[:, 512:768] = res[:, 512:768].astype(BF16)

    @pl.when(j >= n_mix)
    def _():
        gate_ref[...] = proj().astype(gate_ref.dtype)


def _in_proj(h, modl, g, w, cg, sg, cd, sd, qn, kn, n_ctx):
    b, s, d = h.shape
    tm, tn = 1152, COL_TILE
    n_mix = MIX_W // tn
    n_tot = (MIX_W + GATE_W) // tn
    assert n_mix == 3
    const = lambda shp: pl.BlockSpec(shp, lambda bi, i, j: (0,) * len(shp))
    tab = pl.BlockSpec((tm, 256), lambda bi, i, j: (i, 0))
    rows = lambda wd: pl.BlockSpec((None, tm, wd), lambda bi, i, j: (bi, i, 0))
    return pl.pallas_call(
        functools.partial(_in_proj_kernel, tm=tm, n_ctx=n_ctx, n_mix=n_mix),
        grid=(b, s // tm, n_tot),
        in_specs=[rows(d), pl.BlockSpec((None, 2, 8, d), lambda bi, i, j: (bi, 0, 0, 0)), const((1, d)),
                  pl.BlockSpec((d, tn), lambda bi, i, j: (0, j)),
                  tab, tab, tab, tab, const((1, 256)), const((1, 128))],
        out_specs=[rows(768), rows(512), rows(768), rows(3 * HY_W),
                   pl.BlockSpec((None, tm, tn), lambda bi, i, j: (bi, i, jnp.maximum(j - n_mix, 0)))],
        out_shape=[jax.ShapeDtypeStruct((b, s, 768), BF16), jax.ShapeDtypeStruct((b, s, 512), BF16),
                   jax.ShapeDtypeStruct((b, s, 768), BF16), jax.ShapeDtypeStruct((b, s, 3 * HY_W), F32),
                   jax.ShapeDtypeStruct((b, s, GATE_W), BF16)],
        scratch_shapes=[pltpu.VMEM((tm, d), BF16)],
        compiler_params=_cparams(("arbitrary", "arbitrary", "arbitrary")),
        name="in_proj",
    )(h, modl, g, w, cg, sg, cd, sd, qn, kn)


def _lane_group(shape, width):
    return lax.broadcasted_iota(I32, shape, 1) >> int(math.log2(width))


def _head_mean_sq(x, n_heads, dh):
    hid = _lane_group(x.shape, dh)
    x2 = x * x
    ms = jnp.zeros_like(x)
    for hh in range(n_heads):
        m = hid == hh
        s = jnp.sum(jnp.where(m, x2, 0.0), axis=-1, keepdims=True)
        ms = jnp.where(m, s, ms)
    return ms * (1.0 / dh)


def _rope(x, cos, sin_signed, qs):
    w = x.shape[-1]
    lane = lax.broadcasted_iota(I32, x.shape, 1)
    lo = (lane & (2 * qs - 1)) < qs
    partner = jnp.where(lo, pltpu.roll(x, w - qs, 1), pltpu.roll(x, qs, 1))
    return x * cos + partner * sin_signed


def _rope_tables(n_ctx, n_lat, dh, width):
    qs = dh // 4
    t = np.arange(n_lat)
    rows, cols = (t // GRID_W).astype(np.float64), (t % GRID_W).astype(np.float64)
    lane = np.arange(width) % dh
    part = lane // (2 * qs)
    u = lane % (2 * qs)
    f = u % qs
    lo = u < qs
    freqs = ROPE_THETA ** (-f.astype(np.float64) / qs)
    pos = np.where(part[None, :] == 0, rows[:, None], cols[:, None])
    ang = (pos.astype(np.float32) * freqs.astype(np.float32)[None, :]).astype(np.float32)
    cos = np.cos(ang.astype(np.float64))
    sin = np.sin(ang.astype(np.float64)) * np.where(lo, -1.0, 1.0)[None, :]
    cos = np.concatenate([np.ones((n_ctx, width)), cos], axis=0)
    sin = np.concatenate([np.zeros((n_ctx, width)), sin], axis=0)
    return jnp.asarray(cos, F32), jnp.asarray(sin, F32)


def _softmax_rows(s):
    m = jnp.max(s, axis=-1, keepdims=True)
    p = jnp.exp2(s - m)
    return p, jnp.sum(p, axis=-1, keepdims=True)


def _dot_nt(a, b):
    return lax.dot_general(a, b, (((1,), (1,)), ((), ())), preferred_element_type=F32)


def _gqa_kernel(q_ref, k_ref, v_ref, o_ref, *, n_ctx):
    i = pl.program_id(1)

    def run(nk):
        k = k_ref[0:nk, :]
        v = v_ref[0:nk, :]
        q = q_ref[...]
        tq = q.shape[0]
        grp = _lane_group((tq, LANE), GQA_DH)

        def scores(g):
            mask = grp == g
            ql = jnp.concatenate([jnp.where(mask, q[:, 0:128], 0), jnp.where(mask, q[:, 128:256], 0)], axis=0)
            return _dot_nt(ql, k)

        outs = []
        s_next = scores(0)
        for g in range(GQA_KV):
            s = s_next
            if g + 1 < GQA_KV:
                s_next = scores(g + 1)
            p, l = _softmax_rows(s)
            outs.append(jnp.dot(p.astype(BF16), v, preferred_element_type=F32) / l)
        for half in range(2):
            sel = jnp.where(grp == 0, outs[0][half * tq:(half + 1) * tq], outs[1][half * tq:(half + 1) * tq])
            o_ref[:, half * 128:(half + 1) * 128] = sel.astype(o_ref.dtype)

    @pl.when(i == 0)
    def _():
        run(n_ctx)

    @pl.when(i > 0)
    def _():
        run(k_ref.shape[0])


def _gqa_attention(pb, n_ctx):
    b, s, _ = pb.shape
    tq = ROW_TILE
    return pl.pallas_call(
        functools.partial(_gqa_kernel, n_ctx=n_ctx),
        grid=(b, s // tq),
        in_specs=[pl.BlockSpec((None, tq, 256), lambda bi, i: (bi, i, 0)),
                  pl.BlockSpec((None, s, 128), lambda bi, i: (bi, 0, 2)),
                  pl.BlockSpec((None, s, 128), lambda bi, i: (bi, 0, 3))],
        out_specs=pl.BlockSpec((None, tq, 256), lambda bi, i: (bi, i, 0)),
        out_shape=jax.ShapeDtypeStruct((b, s, 256), BF16),
        compiler_params=_cparams(("arbitrary", "arbitrary")),
        name="gqa_attention",
    )(pb, pb, pb)


def _diff_kernel(q_ref, k_ref, v_ref, par_ref, g_ref, o_ref, *, n_ctx):
    i = pl.program_id(1)

    def run(nk):
        k = k_ref[0:nk, :]
        v = v_ref[0:nk, :]
        q = q_ref[...]
        tq = q.shape[0]
        lam = par_ref[0:1, 0:1]
        comp = _lane_group(q.shape, DIFF_DH)
        head = _lane_group(q.shape, DIFF_DV)
        acc = jnp.zeros(q.shape, F32)

        def scores(hh):
            ql = jnp.concatenate([jnp.where(comp == 2 * hh, q, 0), jnp.where(comp == 2 * hh + 1, q, 0)], axis=0)
            return _dot_nt(ql, k)

        s_next = scores(0)
        for hh in range(DIFF_HEADS):
            s = s_next
            if hh + 1 < DIFF_HEADS:
                s_next = scores(hh + 1)
            p, l = _softmax_rows(s)
            inv = 1.0 / l
            pc = (p[0:tq] * inv[0:tq] - p[tq:2 * tq] * (lam * inv[tq:2 * tq])).astype(BF16)
            o = jnp.dot(pc, v, preferred_element_type=F32)
            acc = jnp.where(head == hh, o, acc)
        y = acc * lax.rsqrt(_head_mean_sq(acc, DIFF_HEADS, DIFF_DV) + EPS) * g_ref[...]
        o_ref[...] = (y * par_ref[1:2, 0:1]).astype(o_ref.dtype)

    @pl.when(i == 0)
    def _():
        run(n_ctx)

    @pl.when(i > 0)
    def _():
        run(k_ref.shape[0])


def _diff_attention(pb, par, subln, n_ctx):
    b, s, _ = pb.shape
    tq = ROW_TILE
    return pl.pallas_call(
        functools.partial(_diff_kernel, n_ctx=n_ctx),
        grid=(b, s // tq),
        in_specs=[pl.BlockSpec((None, tq, 256), lambda bi, i: (bi, i, 0)),
                  pl.BlockSpec((None, s, 256), lambda bi, i: (bi, 0, 1)),
                  pl.BlockSpec((None, s, 256), lambda bi, i: (bi, 0, 2)),
                  pl.BlockSpec((8, 128), lambda bi, i: (0, 0)),
                  pl.BlockSpec((1, 256), lambda bi, i: (0, 0))],
        out_specs=pl.BlockSpec((None, tq, 256), lambda bi, i: (bi, i, 0)),
        out_shape=jax.ShapeDtypeStruct((b, s, 256), BF16),
        compiler_params=_cparams(("arbitrary", "arbitrary")),
        name="diff_attention",
    )(pb, pb, pb, par, subln)


def _na_kernel(q_ref, k_ref, v_ref, bias_ref, o_ref, *, n_ctx, n_rows):
    i = pl.program_id(1)
    tq = q_ref.shape[0]

    def stack_heads(q):
        head = _lane_group(q.shape, NA_DH)
        return jnp.concatenate([jnp.where(head == hh, q, 0) for hh in range(NA_HEADS)], axis=0)

    def unstack_heads(o, rows):
        head = _lane_group((rows, 256), NA_DH)
        out = jnp.zeros((rows, 256), F32)
        for hh in range(NA_HEADS):
            out = jnp.where(head == hh, o[hh * rows:(hh + 1) * rows], out)
        return out

    @pl.when(i == 0)
    def _():
        kc = k_ref[0:n_ctx, :]
        vc = v_ref[0:n_ctx, :]
        p, l = _softmax_rows(_dot_nt(stack_heads(q_ref[...]), kc))
        o = jnp.dot(p.astype(BF16), vc, preferred_element_type=F32) / l
        o_ref[...] = unstack_heads(o, tq).astype(o_ref.dtype)

    @pl.when(i > 0)
    def _():
        kc = k_ref[0:n_ctx, :]
        vc = v_ref[0:n_ctx, :]
        def scores(rr):
            r = (i - 1) * (tq // GRID_W) + rr
            rs = jnp.clip(r - NA_KH // 2, 0, n_rows - NA_KH)
            start = pl.multiple_of(n_ctx + rs * GRID_W, GRID_W)
            kcat = jnp.concatenate([kc, k_ref[pl.ds(start, NA_KH * GRID_W), :]], axis=0)
            q = q_ref[rr * GRID_W:(rr + 1) * GRID_W, :]
            return _dot_nt(stack_heads(q), kcat) + bias_ref[r - rs], start

        nxt = scores(0)
        for rr in range(tq // GRID_W):
            s, start = nxt
            if rr + 1 < tq // GRID_W:
                nxt = scores(rr + 1)
            vcat = jnp.concatenate([vc, v_ref[pl.ds(start, NA_KH * GRID_W), :]], axis=0)
            p, l = _softmax_rows(s)
            o = jnp.dot(p.astype(BF16), vcat, preferred_element_type=F32) / l
            o_ref[rr * GRID_W:(rr + 1) * GRID_W, :] = unstack_heads(o, GRID_W).astype(o_ref.dtype)


def _na_attention(pb, bias, n_ctx):
    b, s, _ = pb.shape
    tq = ROW_TILE
    n_rows = (s - n_ctx) // GRID_W
    return pl.pallas_call(
        functools.partial(_na_kernel, n_ctx=n_ctx, n_rows=n_rows),
        grid=(b, s // tq),
        in_specs=[pl.BlockSpec((None, tq, 256), lambda bi, i: (bi, i, 0)),
                  pl.BlockSpec((None, s, 256), lambda bi, i: (bi, 0, 1)),
                  pl.BlockSpec((None, s, 256), lambda bi, i: (bi, 0, 2)),
                  pl.BlockSpec(bias.shape, lambda bi, i: (0, 0, 0))],
        out_specs=pl.BlockSpec((None, tq, 256), lambda bi, i: (bi, i, 0)),
        out_shape=jax.ShapeDtypeStruct((b, s, 256), BF16),
        compiler_params=_cparams(("arbitrary", "arbitrary")),
        name="na_attention",
    )(pb, pb, pb, bias)


def _na_bias_table(rpb, n_ctx):
    depth = rpb.shape[0]
    pad = GRID_W - NA_KW
    padded = jnp.pad(rpb.astype(F32), ((0, 0), (0, 0), (0, 0), (pad, pad)))
    cols = jnp.stack([padded[..., GRID_W - 1 - w:2 * GRID_W - 1 - w] for w in range(GRID_W)], axis=3)
    vals = jnp.stack([cols[:, :, NA_KH - 1 - o:2 * NA_KH - 1 - o] for o in range(NA_KH)], axis=2)
    vals = jnp.transpose(vals, (0, 2, 1, 4, 3, 5))
    w = np.arange(GRID_W)[:, None, None]
    kc = np.arange(GRID_W)[None, None, :]
    cs = np.clip(w - NA_KW // 2, 0, GRID_W - NA_KW)
    inwin = np.broadcast_to((kc >= cs) & (kc < cs + NA_KW), (GRID_W, NA_KH, GRID_W))
    vals = jnp.where(jnp.asarray(inwin), vals * LOG2E, NEG).reshape(depth, NA_KH, NA_HEADS * GRID_W, NA_KH * GRID_W)
    return jnp.concatenate([jnp.zeros((depth, NA_KH, NA_HEADS * GRID_W, n_ctx), F32), vals], axis=-1)


def _hy_pre_kernel(u_ref, w_ref, b_ref, x0_ref, z_ref, *, bounds, tm):
    s = u_ref.shape[0]
    w0, w1, w2, bias = w_ref[0:1, :], w_ref[1:2, :], w_ref[2:3, :], b_ref[...]
    rid = lax.broadcasted_iota(I32, (tm, u_ref.shape[1]), 0)
    for c0 in range(0, s, tm):
        u = u_ref[c0:c0 + tm, :]
        prev = pltpu.roll(u, 1, 0)
        nxt = pltpu.roll(u, tm - 1, 0)
        first = jnp.zeros_like(w0) if c0 in bounds else u_ref[c0 - 1:c0, :]
        last = jnp.zeros_like(w0) if (c0 + tm) in bounds else u_ref[c0 + tm:c0 + tm + 1, :]
        prev = jnp.where(rid == 0, first, prev)
        nxt = jnp.where(rid == tm - 1, last, nxt)
        uc = prev * w0 + u * w1 + nxt * w2 + bias
        x0_ref[c0:c0 + tm, :] = uc[:, 0:HY_W]
        z_ref[c0:c0 + tm, :] = uc[:, HY_W:2 * HY_W] * uc[:, 2 * HY_W:3 * HY_W]


def _hy_pre(mix, w_short, b_short, n_ctx):
    b, s, _ = mix.shape
    return pl.pallas_call(
        functools.partial(_hy_pre_kernel, bounds=(0, n_ctx, s), tm=ROW_TILE),
        grid=(b,),
        in_specs=[pl.BlockSpec((None, s, 3 * HY_W), lambda bi: (bi, 0, 0)),
                  pl.BlockSpec((3, 3 * HY_W), lambda bi: (0, 0)),
                  pl.BlockSpec((1, 3 * HY_W), lambda bi: (0, 0))],
        out_specs=[pl.BlockSpec((None, s, HY_W), lambda bi: (bi, 0, 0)),
                   pl.BlockSpec((None, s, HY_W), lambda bi: (bi, 0, 0))],
        out_shape=[jax.ShapeDtypeStruct((b, s, HY_W), F32), jax.ShapeDtypeStruct((b, s, HY_W), F32)],
        compiler_params=_cparams(("arbitrary",)),
        name="hyena_pre",
    )(mix, w_short, b_short)


def _hy_filter_kernel(z_ref, w1_ref, b1_ref, w2_ref, b2_ref, w3_ref, b3_ref, w4_ref, f_ref, dec_ref, o_ref):
    def lin(a, w_ref, b_ref):
        return jnp.dot(a, w_ref[...], preferred_element_type=F32, precision=HIGHEST) + b_ref[...]

    hh = jnp.sin(f_ref[0:1, :] * lin(z_ref[...], w1_ref, b1_ref))
    hh = jnp.sin(f_ref[1:2, :] * lin(hh, w2_ref, b2_ref))
    hh = jnp.sin(f_ref[2:3, :] * lin(hh, w3_ref, b3_ref))
    hh = jnp.dot(hh, w4_ref[...], preferred_element_type=F32, precision=HIGHEST) * dec_ref[...]
    o_ref[...] = hh / jnp.sum(jnp.abs(hh), axis=0, keepdims=True)


def _pad_to(a, shape):
    return jnp.pad(a, [(0, t - s) for s, t in zip(a.shape, shape)])


def _hy_filters(n, w1, b1, w2, b2, w3, b3, w4, freq):
    depth = w1.shape[0]
    t = np.linspace(0.0, 1.0, n, dtype=np.float32)[:, None]
    w = np.float32(2.0 * math.pi / n) * np.arange(n, dtype=np.float32)[:, None]
    bands = np.linspace(1e-4, HY_BANDS - 1, HY_BANDS, dtype=np.float32)[None, :]
    z = np.concatenate([t, np.cos(w * bands), np.sin(w * bands)], axis=-1).astype(np.float32)
    z = np.pad(z, ((0, 0), (0, LANE - z.shape[1])))
    deltas = np.linspace(math.log(HY_TARGET) / HY_SLOW_DECAY, math.log(HY_TARGET) / HY_FAST_DECAY, HY_W, dtype=np.float32)
    deltas = np.tile(np.abs(deltas), 2)
    decay = np.exp(-t * deltas[None, :]).astype(np.float32)
    p = LANE
    args = (jnp.asarray(z), _pad_to(w1, (depth, p, p)), _pad_to(b1[:, None, :], (depth, 1, p)),
            _pad_to(w2, (depth, p, p)), _pad_to(b2[:, None, :], (depth, 1, p)),
            _pad_to(w3, (depth, p, p)), _pad_to(b3[:, None, :], (depth, 1, p)),
            _pad_to(w4, (depth, p, 2 * HY_W)), _pad_to(freq, (depth, 8, p)), jnp.asarray(decay))
    per_layer = lambda shp: pl.BlockSpec((None,) + shp, lambda l: (l,) + (0,) * len(shp))
    const = lambda shp: pl.BlockSpec(shp, lambda l: (0,) * len(shp))
    return pl.pallas_call(
        _hy_filter_kernel,
        grid=(depth,),
        in_specs=[const((n, p)), per_layer((p, p)), per_layer((1, p)), per_layer((p, p)), per_layer((1, p)),
                  per_layer((p, p)), per_layer((1, p)), per_layer((p, 2 * HY_W)), per_layer((8, p)),
                  const((n, 2 * HY_W))],
        out_specs=per_layer((n, 2 * HY_W)),
        out_shape=jax.ShapeDtypeStruct((depth, n, 2 * HY_W), F32),
        compiler_params=_cparams(("arbitrary",)),
        name="hyena_filters",
    )(*args)


def _conv_taps(filt):
    hf, hb = filt[..., :HY_W], filt[..., HY_W:]
    taps = jnp.concatenate([hb[:, :0:-1], (hf[:, 0:1] + hb[:, 0:1]), hf[:, 1:], jnp.zeros_like(hf[:, 0:1])], axis=1)
    return jnp.transpose(taps, (0, 2, 1))


def _hy_conv_kernel(z_ref, k_ref, o_ref, *, nblk, nb):
    cg = z_ref.shape[0]
    n2 = k_ref.shape[1]
    ncols = (2 * nblk - 1) * LANE

    def body(ci, _):
        krow = k_ref[pl.ds(ci, 1), :]
        kb = jnp.broadcast_to(krow, (LANE, n2))
        big = pltpu.roll(kb, n2 - (LANE - 1), 1, stride=1, stride_axis=0)[:, :ncols].astype(BF16)
        o_ref[ci] = jnp.zeros(o_ref.shape[1:], F32)
        for m in range(-(nblk - 1), nblk):
            km = big[:, (m + nblk - 1) * LANE:(m + nblk) * LANE]
            cnt = (nblk - abs(m)) * nb
            src = 0 if m >= 0 else -m * nb
            dst = m * nb if m >= 0 else 0
            zin = z_ref[ci, src:src + cnt, :]
            o_ref[ci, dst:dst + cnt, :] += jnp.dot(zin, km, preferred_element_type=F32)
        return 0

    lax.fori_loop(0, cg, body, 0)


def _hy_conv(zr, taps, nblk, nb):
    c, r, _ = zr.shape
    cg = 8
    return pl.pallas_call(
        functools.partial(_hy_conv_kernel, nblk=nblk, nb=nb),
        grid=(c // cg,),
        in_specs=[pl.BlockSpec((cg, r, LANE), lambda i: (i, 0, 0)),
                  pl.BlockSpec((cg, taps.shape[1]), lambda i: (i, 0))],
        out_specs=pl.BlockSpec((cg, r, LANE), lambda i: (i, 0, 0)),
        out_shape=jax.ShapeDtypeStruct((c, r, LANE), F32),
        compiler_params=_cparams(("arbitrary",)),
        name="hyena_conv",
    )(zr, taps)


def _hy_long_conv(z, taps):
    b, n, c = z.shape
    nblk = n // LANE
    zr = jnp.transpose(z.astype(BF16).reshape(b, nblk, LANE, c), (3, 1, 0, 2)).reshape(c, nblk * b, LANE)
    y = _hy_conv(zr, taps, nblk, b)
    return jnp.transpose(y.reshape(c, nblk, b, LANE), (2, 1, 3, 0)).reshape(b, n, c)


def _merge_kernel(h_ref, mod_ref, ya_ref, x0_ref, z_ref, yconv_ref, hb_ref, yc_ref, yd_ref, gate_ref,
                  wb_ref, wo_ref, g2_ref, wq_ref, o_ref, q_ref, nb_ref, *, n_ctx, tm):
    i = pl.program_id(1)
    yb = (x0_ref[...] * (yconv_ref[...] + z_ref[...] * hb_ref[...])).astype(BF16)
    ys = (ya_ref[...], yb, yc_ref[...], yd_ref[...])
    acc = jnp.zeros(h_ref.shape, F32)
    d = h_ref.shape[1]
    for k in range(N_BRANCH):
        proj = jnp.dot(ys[k], wb_ref[k], preferred_element_type=F32)
        acc = acc + jax.nn.sigmoid(gate_ref[:, k * d:(k + 1) * d].astype(F32)) * proj
    out = jnp.dot(acc.astype(BF16), wo_ref[...], preferred_element_type=F32)
    row = i * tm + lax.broadcasted_iota(I32, out.shape, 0)
    gate = jnp.where(row < n_ctx, mod_ref[0, 2:3, :], mod_ref[1, 2:3, :])
    hn = h_ref[...] + gate * out
    o_ref[...] = hn
    nb = _norm_modulate(hn, mod_ref, g2_ref, 3, i * tm, n_ctx).astype(BF16)
    nb_ref[...] = nb
    q_ref[...] = jnp.dot(nb, wq_ref[...], preferred_element_type=F32).astype(q_ref.dtype)


def _merge(h, modl, ya, x0, z, yconv, hbias, yc, yd, gates, wb, wo, g2, wq, n_ctx):
    b, s, d = h.shape
    tm = ROW_TILE
    row = lambda w: pl.BlockSpec((None, tm, w), lambda bi, i: (bi, i, 0))
    return pl.pallas_call(
        functools.partial(_merge_kernel, n_ctx=n_ctx, tm=tm),
        grid=(b, s // tm),
        in_specs=[row(d), pl.BlockSpec((None, 2, 8, d), lambda bi, i: (bi, 0, 0, 0)),
                  row(256), row(256), row(256), row(256), pl.BlockSpec((1, 256), lambda bi, i: (0, 0)),
                  row(256), row(256), row(GATE_W),
                  pl.BlockSpec(wb.shape, lambda bi, i: (0, 0, 0)),
                  pl.BlockSpec(wo.shape, lambda bi, i: (0, 0)),
                  pl.BlockSpec((1, d), lambda bi, i: (0, 0)),
                  pl.BlockSpec(wq.shape, lambda bi, i: (0, 0))],
        out_specs=[row(d), row(wq.shape[1]), row(d)],
        out_shape=[jax.ShapeDtypeStruct((b, s, d), F32), jax.ShapeDtypeStruct((b, s, wq.shape[1]), BF16),
                   jax.ShapeDtypeStruct((b, s, d), BF16)],
        compiler_params=_cparams(("arbitrary", "arbitrary")),
        name="merge",
    )(h, modl, ya, x0, z, yconv, hbias, yc, yd, gates, wb, wo, g2, wq)


def _topk_rows(s, label, k):
    vals, labs = [], []
    for _ in range(k):
        m = jnp.max(s, axis=0, keepdims=True)
        lb = jnp.min(jnp.where(s == m, label, float(2 ** 20)), axis=0, keepdims=True)
        vals.append(m)
        labs.append(lb)
        s = jnp.where(label == lb, -jnp.inf, s)
    return jnp.concatenate(vals, axis=0), jnp.concatenate(labs, axis=0).astype(I32)


_CAND_FIXED_A = ((0, 0), (0, 8), (1, 0), (2, 0), (3, 0))
_CAND_FIXED_B = ((0, 0), (0, 8), (1, 0), (2, 0))


def _select_rows(table, pos):
    out = jnp.zeros(pos.shape, table.dtype)
    for a in range(table.shape[0]):
        out = jnp.where(pos == a, table[a:a + 1, :], out)
    return out


def _peer_route_kernel(q_ref, keys_ref, i_ref, j_ref, g_ref):
    st = _dot_nt(keys_ref[...], q_ref[...].astype(BF16))
    t = st.shape[1]
    key_id = lax.broadcasted_iota(I32, (PEER_NKEYS, t), 0).astype(F32)
    sv1, si1 = _topk_rows(st[0:PEER_NKEYS], key_id, PEER_TOPK)
    sv2, si2 = _topk_rows(st[PEER_NKEYS:2 * PEER_NKEYS], key_id, PEER_TOPK)
    r8 = lax.broadcasted_iota(I32, (8, t), 0)
    cands, labels = [], []
    for a, b0 in _CAND_FIXED_A:
        cands.append(sv1[a:a + 1, :] + sv2[b0:b0 + 8, :])
        labels.append((a * PEER_TOPK + b0 + r8).astype(F32))
    for b, a0 in _CAND_FIXED_B:
        c = sv1[a0:a0 + 8, :] + sv2[b:b + 1, :]
        cands.append(jnp.where(r8 < 4, -jnp.inf, c) if a0 == 0 else c)
        labels.append(((a0 + r8) * PEER_TOPK + b).astype(F32))
    best, pos = _topk_rows(jnp.concatenate(cands, axis=0), jnp.concatenate(labels, axis=0), PEER_TOPK)
    i_ref[...] = _select_rows(si1, pos >> int(math.log2(PEER_TOPK)))
    j_ref[...] = _select_rows(si2, pos & (PEER_TOPK - 1))
    e = jnp.exp(best - jnp.max(best, axis=0, keepdims=True))
    g_ref[...] = e / jnp.sum(e, axis=0, keepdims=True)


def _peer_route(q, keys_blk):
    t, _ = q.shape
    tt = 1024
    assert t % tt == 0
    out = pl.BlockSpec((None, PEER_TOPK, tt), lambda ti, hh: (hh, 0, ti))
    shp = (PEER_HEADS, PEER_TOPK, t)
    return pl.pallas_call(
        _peer_route_kernel,
        grid=(t // tt, PEER_HEADS),
        in_specs=[pl.BlockSpec((tt, PEER_DK), lambda ti, hh: (ti, hh)),
                  pl.BlockSpec((None, 2 * PEER_NKEYS, PEER_DK), lambda ti, hh: (hh, 0, 0))],
        out_specs=[out, out, out],
        out_shape=[jax.ShapeDtypeStruct(shp, I32), jax.ShapeDtypeStruct(shp, I32), jax.ShapeDtypeStruct(shp, F32)],
        compiler_params=_cparams(("arbitrary", "arbitrary")),
        name="peer_route",
    )(q, keys_blk)


GS_HALF = PEER_NKEYS // 2
GS_PITCH = GS_HALF + 1
GS_UNROLL = 64
PEER_TM = 576
PEER_TE = 2048
U32 = jnp.uint32


def _peer_expert_kernel(h_ref, mod_ref, x_ref, i_ref, j_ref, g_ref, ut_ref, v_ref, o_ref, gs_ref, acc_ref,
                        *, n_ctx, tm):
    ti, e = pl.program_id(1), pl.program_id(2)
    n_e = pl.num_programs(2)
    te = ut_ref.shape[1]
    nk = PEER_NKEYS

    @pl.when((pl.program_id(0) == 0) & (ti == 0) & (e == 0))
    def _():
        acc_ref[...] = jnp.zeros(acc_ref.shape, F32)

    @pl.when(e == 0)
    def _():
        sub = lax.broadcasted_iota(I32, (nk, nk), 0)

        def build(t):
            irow = i_ref[pl.ds(t, 1), :]
            jrow = j_ref[pl.ds(t, 1), :]
            grow = g_ref[pl.ds(t, 1), :]
            at = jnp.where(sub == irow, 1.0, 0.0).astype(BF16)
            bt = jnp.where(sub == jrow, grow, 0.0).astype(BF16)
            gt = _dot_nt(at, bt).astype(BF16).astype(F32)
            lo = pltpu.bitcast(gt[0:GS_HALF], U32) >> 16
            hi = pltpu.bitcast(gt[GS_HALF:nk], U32)
            gs_ref[pl.ds(t * GS_PITCH, GS_HALF), :] = hi | lo

        def body(tb, _):
            for u in range(GS_UNROLL):
                build(tb * GS_UNROLL + u)
            return 0

        lax.fori_loop(0, tm // GS_UNROLL, body, 0)

    nb = te // nk // 2

    def gelu(a):
        return 0.5 * a * (1.0 + lax.erf(a * (2.0 ** -0.5)))

    hid = jnp.dot(x_ref[...], ut_ref[...], preferred_element_type=F32)
    first, second = [], []
    for k in range(nb):
        word = gs_ref[pl.ds(e * nb + k, tm, stride=GS_PITCH), :]
        g_lo = pltpu.bitcast(word << 16, F32)
        g_hi = pltpu.bitcast(word & jnp.uint32(0xFFFF0000), F32)
        first.append((g_lo * gelu(hid[:, k * nk:(k + 1) * nk])).astype(BF16))
        second.append((g_hi * gelu(hid[:, (nb + k) * nk:(nb + k + 1) * nk])).astype(BF16))
    acc_ref[...] += jnp.dot(jnp.concatenate(first + second, axis=1), v_ref[...], preferred_element_type=F32)

    @pl.when(e == n_e - 1)
    def _():
        row = ti * tm + lax.broadcasted_iota(I32, acc_ref.shape, 0)
        gate = jnp.where(row < n_ctx, mod_ref[0, 5:6, :], mod_ref[1, 5:6, :])
        o_ref[...] = h_ref[...] + gate * acc_ref[...]
        acc_ref[...] = jnp.zeros(acc_ref.shape, F32)


def _peer_experts(h, modl, xb, i_idx, j_idx, g, ut, v, n_ctx):
    b, s, d = h.shape
    tm, te = PEER_TM, PEER_TE
    n_chunks = ut.shape[0]
    slots = PEER_HEADS * PEER_TOPK
    row = lambda w: pl.BlockSpec((None, tm, w), lambda bi, i, e: (bi, i, 0), pipeline_mode=pl.Buffered(1))
    return pl.pallas_call(
        functools.partial(_peer_expert_kernel, n_ctx=n_ctx, tm=tm),
        grid=(b, s // tm, n_chunks),
        in_specs=[row(d), pl.BlockSpec((None, 2, 8, d), lambda bi, i, e: (bi, 0, 0, 0)),
                  row(d), row(slots), row(slots), row(slots),
                  pl.BlockSpec((None, d, te), lambda bi, i, e: (e, 0, 0)),
                  pl.BlockSpec((te, d), lambda bi, i, e: (e, 0))],
        out_specs=pl.BlockSpec((None, tm, d), lambda bi, i, e: (bi, i, 0)),
        out_shape=jax.ShapeDtypeStruct((b, s, d), F32),
        scratch_shapes=[pltpu.VMEM((tm * GS_PITCH, PEER_NKEYS), U32), pltpu.VMEM((tm, d), F32)],
        compiler_params=_cparams(("arbitrary", "arbitrary", "arbitrary")),
        name="peer_experts",
    )(h, modl, xb, i_idx, j_idx, g, ut, v)


def _final_norm_kernel(x_ref, g_ref, o_ref):
    x = x_ref[...]
    o_ref[...] = x * lax.rsqrt(jnp.mean(x * x, axis=-1, keepdims=True) + EPS) * g_ref[...]


def _final_norm(h, g, n_ctx):
    b, s, d = h.shape
    tm = ROW_TILE
    n_lat = s - n_ctx
    off = n_ctx // tm
    return pl.pallas_call(
        _final_norm_kernel,
        grid=(b, n_lat // tm),
        in_specs=[pl.BlockSpec((None, tm, d), lambda bi, i: (bi, i + off, 0)),
                  pl.BlockSpec((1, d), lambda bi, i: (0, 0))],
        out_specs=pl.BlockSpec((None, tm, d), lambda bi, i: (bi, i, 0)),
        out_shape=jax.ShapeDtypeStruct((b, n_lat, d), F32),
        compiler_params=_cparams(("arbitrary", "arbitrary")),
        name="final_norm",
    )(h, g)


def _prep_w_in(w_in):
    depth, d, _ = w_in.shape
    mixw = w_in[:, :, :2816]
    gq = mixw[:, :, 1536:1792].reshape(depth, d, GQA_HEADS, GQA_DH)[:, :, (0, 2, 1, 3), :].reshape(depth, d, 256)
    mixw = jnp.concatenate([mixw[:, :, :1536], gq, mixw[:, :, 1792:], jnp.zeros((depth, d, MIX_W - 2816), w_in.dtype)], axis=-1)
    return jnp.concatenate([mixw, w_in[:, :, 2816:]], axis=-1).astype(BF16)


def _peer_chunk_order(w):
    n_exp, d = w.shape
    nb = PEER_TE // PEER_NKEYS // 2
    return w.reshape(2, GS_HALF // nb, nb, PEER_NKEYS, d).transpose(1, 0, 2, 3, 4).reshape(n_exp, d)


def _prep_peer_keys(keys):
    depth = keys.shape[0]
    half = PEER_DK // 2
    z = jnp.zeros((depth, PEER_HEADS, PEER_NKEYS, half), keys.dtype)
    top = jnp.concatenate([keys[:, :, 0], z], axis=-1)
    bot = jnp.concatenate([z, keys[:, :, 1]], axis=-1)
    return jnp.concatenate([top, bot], axis=2).astype(BF16)


def kernel(x, c, ctx, c_ctx, w_mod, b_mod, norm1_g, norm2_g, w_in, na_rpb, hy_short_w, hy_short_b, hy_w1, hy_b1, hy_w2, hy_b2, hy_w3, hy_b3, hy_w4, hy_freq, hy_bias, gqa_qn, gqa_kn, diff_lq1, diff_lk1, diff_lq2, diff_lk2, diff_subln, w_branch, w_out, peer_wq, peer_keys, peer_u, peer_v, final_g):
    B, L, D = x.shape
    C = ctx.shape[1]
    S = C + L
    depth = w_mod.shape[0]
    assert C == ROW_TILE and L % ROW_TILE == 0 and D == 1024 and L % GRID_W == 0

    h = jnp.concatenate([ctx, x], axis=1)

    r = -(-(B + 1) // 8) * 8
    cvec = jnp.zeros((r, D), F32).at[:B].set(c).at[B].set(c_ctx)
    modall = _modulation(cvec, w_mod, b_mod)
    mod_lat = modall[:, :B].reshape(depth, B, 1, 6, D)
    mod_ctx = jnp.broadcast_to(modall[:, B].reshape(depth, 1, 1, 6, D), (depth, B, 1, 6, D))
    mods = jnp.pad(jnp.concatenate([mod_ctx, mod_lat], axis=2), ((0, 0), (0, 0), (0, 0), (0, 2), (0, 0)))

    fargs = (hy_w1, hy_b1, hy_w2, hy_b2, hy_w3, hy_b3, hy_w4, hy_freq)
    taps_lat = _conv_taps(_hy_filters(L, *fargs))
    taps_ctx = _conv_taps(_hy_filters(C, *fargs))

    cg, sg = _rope_tables(C, L, GQA_DH, 256)
    cd, sd = _rope_tables(C, L, DIFF_DH, 256)

    lam_init = jnp.asarray([0.8 - 0.6 * math.exp(-0.3 * l) for l in range(depth)], F32)
    lam = (jnp.exp(jnp.sum(diff_lq1.astype(F32) * diff_lk1.astype(F32), axis=-1))
           - jnp.exp(jnp.sum(diff_lq2.astype(F32) * diff_lk2.astype(F32), axis=-1)) + lam_init)
    diff_par = jnp.zeros((depth, 8, LANE), F32).at[:, 0, :].set(lam[:, None]).at[:, 1, :].set(1.0 - lam_init[:, None])

    wb = w_branch.astype(BF16)
    wb = wb.at[:, 2].set(wb[:, 2].reshape(depth, GQA_HEADS, GQA_DH, D)[:, (0, 2, 1, 3)].reshape(depth, BRANCH_W, D))

    xs = dict(
        mods=mods, n1=norm1_g[:, None, :], n2=norm2_g[:, None, :], w_in=_prep_w_in(w_in),
        bias=_na_bias_table(na_rpb, C), sw=hy_short_w, sb=hy_short_b[:, None, :], hbias=hy_bias[:, None, :],
        taps_lat=taps_lat, taps_ctx=taps_ctx,
        qn=jnp.tile(gqa_qn, (1, GQA_HEADS))[:, None, :], kn=jnp.tile(gqa_kn, (1, GQA_KV))[:, None, :],
        diff_par=diff_par, subln=jnp.tile(diff_subln, (1, DIFF_HEADS))[:, None, :],
        wb=wb, wo=w_out.astype(BF16), wq=peer_wq.astype(BF16), keys=_prep_peer_keys(peer_keys),
        u=peer_u, v=peer_v,
    )

    def layer(h, p):
        pb_na, pb_gqa, pb_diff, hy, gates = _in_proj(h, p["mods"], p["n1"], p["w_in"], cg, sg, cd, sd,
                                                     p["qn"], p["kn"], C)
        ya = _na_attention(pb_na, p["bias"], C)
        yc = _gqa_attention(pb_gqa, C)
        yd = _diff_attention(pb_diff, p["diff_par"], p["subln"], C)
        x0, z = _hy_pre(hy, p["sw"], p["sb"], C)
        yconv = jnp.concatenate([_hy_long_conv(z[:, :C], p["taps_ctx"]), _hy_long_conv(z[:, C:], p["taps_lat"])], axis=1)
        h, q, xb = _merge(h, p["mods"], ya, x0, z, yconv, p["hbias"], yc, yd, gates, p["wb"], p["wo"],
                          p["n2"], p["wq"], C)
        i_idx, j_idx, g = _peer_route(q.reshape(B * S, -1), p["keys"])
        slots = PEER_HEADS * PEER_TOPK
        to_rows = lambda a: jnp.transpose(a, (2, 0, 1)).reshape(B, S, slots)
        ut = jnp.transpose(_peer_chunk_order(p["u"].astype(BF16)).reshape(-1, PEER_TE, D), (0, 2, 1))
        h = _peer_experts(h, p["mods"], xb, to_rows(i_idx), to_rows(j_idx), to_rows(g), ut,
                          _peer_chunk_order(p["v"].astype(BF16)), C)
        return h, None

    h, _ = lax.scan(layer, h, xs)
    return _final_norm(h, final_g[None, :], C)
```

```python
import functools
import math

import numpy as np
import jax
import jax.numpy as jnp
from jax import lax
from jax.experimental import pallas as pl
from jax.experimental.pallas import tpu as pltpu

F32 = jnp.float32
BF16 = jnp.bfloat16
I32 = jnp.int32
HIGHEST = lax.Precision.HIGHEST

EPS = 1e-6
GRID_W = 64
ROPE_THETA = 10000.0
NA_HEADS, NA_DH, NA_KH, NA_KW = 4, 64, 8, 16
HY_W, HY_BANDS, HY_FFN = 256, 16, 64
HY_FAST_DECAY, HY_SLOW_DECAY, HY_TARGET = 0.3, 1.5, 1e-2
GQA_HEADS, GQA_KV, GQA_DH = 4, 2, 64
DIFF_HEADS, DIFF_DH, DIFF_DV = 4, 32, 64
N_BRANCH, BRANCH_W = 4, 256
PEER_HEADS, PEER_NKEYS, PEER_DK, PEER_TOPK = 8, 128, 128, 16

LANE = 128
ROW_TILE = 256
VMEM_LIMIT = 56 * 1024 * 1024
NEG = -1e30
LOG2E = math.log2(math.e)
MIX_W = 3072
GATE_W = N_BRANCH * 1024
COL_TILE = 1024


def _cparams(sem):
    return pltpu.CompilerParams(dimension_semantics=sem, vmem_limit_bytes=VMEM_LIMIT)


def _mod_kernel(c_ref, w_ref, b_ref, o_ref):
    cv = c_ref[...]
    a = cv * jax.nn.sigmoid(cv)
    o_ref[...] = jnp.dot(a, w_ref[...], preferred_element_type=F32, precision=HIGHEST) + b_ref[...]


def _modulation(cvec, w_mod, b_mod):
    depth, d, n = w_mod.shape
    r = cvec.shape[0]
    tn = 1024
    return pl.pallas_call(
        _mod_kernel,
        grid=(depth, n // tn),
        in_specs=[pl.BlockSpec((r, d), lambda l, j: (0, 0)),
                  pl.BlockSpec((None, d, tn), lambda l, j: (l, 0, j)),
                  pl.BlockSpec((None, 1, tn), lambda l, j: (l, 0, j))],
        out_specs=pl.BlockSpec((None, r, tn), lambda l, j: (l, 0, j)),
        out_shape=jax.ShapeDtypeStruct((depth, r, n), F32),
        compiler_params=_cparams(("arbitrary", "arbitrary")),
        name="modulation",
    )(cvec, w_mod, b_mod.reshape(depth, 1, n))


def _norm_modulate(x, mod_ref, g_ref, srow, row0, n_ctx):
    y = x * lax.rsqrt(jnp.mean(x * x, axis=-1, keepdims=True) + EPS) * g_ref[...]
    row = row0 + lax.broadcasted_iota(I32, x.shape, 0)
    is_ctx = row < n_ctx
    shift = jnp.where(is_ctx, mod_ref[0, srow:srow + 1, :], mod_ref[1, srow:srow + 1, :])
    scale = jnp.where(is_ctx, mod_ref[0, srow + 1:srow + 2, :], mod_ref[1, srow + 1:srow + 2, :])
    return y * (1.0 + scale) + shift


def _in_proj_kernel(h_ref, mod_ref, g_ref, w_ref, cg_ref, sg_ref, cd_ref, sd_ref, qn_ref, kn_ref,
                    na_ref, gqa_ref, diff_ref, hy_ref, gate_ref, nb_ref, *, tm, n_ctx, n_mix):
    i, j = pl.program_id(1), pl.program_id(2)

    @pl.when(j == 0)
    def _():
        nb_ref[...] = _norm_modulate(h_ref[...], mod_ref, g_ref, 0, i * tm, n_ctx).astype(BF16)

    def proj():
        return jnp.dot(nb_ref[...], w_ref[...], preferred_element_type=F32)

    @pl.when(j == 0)
    def _():
        res = proj()
        na_ref[:, 0:256] = (res[:, 0:256] * (LOG2E * NA_DH ** -0.5)).astype(BF16)
        na_ref[:, 256:768] = res[:, 256:768].astype(BF16)
        hy_ref[:, 0:HY_W] = res[:, 768:1024]

    @pl.when(j == 1)
    def _():
        res = proj()
        hy_ref[:, HY_W:3 * HY_W] = res[:, 0:512]
        cg, sg = cg_ref[...], sg_ref[...]
        gq, gk = res[:, 512:768], res[:, 768:896]
        gq = gq * lax.rsqrt(_head_mean_sq(gq, GQA_HEADS, GQA_DH) + EPS) * qn_ref[...]
        gk = gk * lax.rsqrt(_head_mean_sq(gk, GQA_KV, GQA_DH) + EPS) * kn_ref[...]
        gqa_ref[:, 0:256] = (_rope(gq, cg, sg, GQA_DH // 4) * (LOG2E * GQA_DH ** -0.5)).astype(BF16)
        gqa_ref[:, 256:384] = _rope(gk, cg[:, 0:128], sg[:, 0:128], GQA_DH // 4).astype(BF16)
        gqa_ref[:, 384:512] = res[:, 896:1024].astype(BF16)

    @pl.when(j == 2)
    def _():
        res = proj()
        cd, sd = cd_ref[...], sd_ref[...]
        diff_ref[:, 0:256] = (_rope(res[:, 0:256], cd, sd, DIFF_DH // 4) * (LOG2E * DIFF_DH ** -0.5)).astype(BF16)
        diff_ref[:, 256:512] = _rope(res[:, 256:512], cd, sd, DIFF_DH // 4).astype(BF16)
        diff_ref[:, 512:768] = res[:, 512:768].astype(BF16)

    @pl.when(j >= n_mix)
    def _():
        gate_ref[...] = proj().astype(gate_ref.dtype)


def _in_proj(h, modl, g, w, cg, sg, cd, sd, qn, kn, n_ctx):
    b, s, d = h.shape
    tm, tn = 1152, COL_TILE
    n_mix = MIX_W // tn
    n_tot = (MIX_W + GATE_W) // tn
    assert n_mix == 3
    const = lambda shp: pl.BlockSpec(shp, lambda bi, i, j: (0,) * len(shp))
    tab = pl.BlockSpec((tm, 256), lambda bi, i, j: (i, 0))
    rows = lambda wd: pl.BlockSpec((None, tm, wd), lambda bi, i, j: (bi, i, 0))
    return pl.pallas_call(
        functools.partial(_in_proj_kernel, tm=tm, n_ctx=n_ctx, n_mix=n_mix),
        grid=(b, s // tm, n_tot),
        in_specs=[rows(d), pl.BlockSpec((None, 2, 8, d), lambda bi, i, j: (bi, 0, 0, 0)), const((1, d)),
                  pl.BlockSpec((d, tn), lambda bi, i, j: (0, j)),
                  tab, tab, tab, tab, const((1, 256)), const((1, 128))],
        out_specs=[rows(768), rows(512), rows(768), rows(3 * HY_W),
                   pl.BlockSpec((None, tm, tn), lambda bi, i, j: (bi, i, jnp.maximum(j - n_mix, 0)))],
        out_shape=[jax.ShapeDtypeStruct((b, s, 768), BF16), jax.ShapeDtypeStruct((b, s, 512), BF16),
                   jax.ShapeDtypeStruct((b, s, 768), BF16), jax.ShapeDtypeStruct((b, s, 3 * HY_W), F32),
                   jax.ShapeDtypeStruct((b, s, GATE_W), BF16)],
        scratch_shapes=[pltpu.VMEM((tm, d), BF16)],
        compiler_params=_cparams(("arbitrary", "arbitrary", "arbitrary")),
        name="in_proj",
    )(h, modl, g, w, cg, sg, cd, sd, qn, kn)


def _lane_group(shape, width):
    return lax.broadcasted_iota(I32, shape, 1) >> int(math.log2(width))


def _head_mean_sq(x, n_heads, dh):
    hid = _lane_group(x.shape, dh)
    x2 = x * x
    ms = jnp.zeros_like(x)
    for hh in range(n_heads):
        m = hid == hh
        s = jnp.sum(jnp.where(m, x2, 0.0), axis=-1, keepdims=True)
        ms = jnp.where(m, s, ms)
    return ms * (1.0 / dh)


def _rope(x, cos, sin_signed, qs):
    w = x.shape[-1]
    lane = lax.broadcasted_iota(I32, x.shape, 1)
    lo = (lane & (2 * qs - 1)) < qs
    partner = jnp.where(lo, pltpu.roll(x, w - qs, 1), pltpu.roll(x, qs, 1))
    return x * cos + partner * sin_signed


def _rope_tables(n_ctx, n_lat, dh, width):
    qs = dh // 4
    t = np.arange(n_lat)
    rows, cols = (t // GRID_W).astype(np.float64), (t % GRID_W).astype(np.float64)
    lane = np.arange(width) % dh
    part = lane // (2 * qs)
    u = lane % (2 * qs)
    f = u % qs
    lo = u < qs
    freqs = ROPE_THETA ** (-f.astype(np.float64) / qs)
    pos = np.where(part[None, :] == 0, rows[:, None], cols[:, None])
    ang = (pos.astype(np.float32) * freqs.astype(np.float32)[None, :]).astype(np.float32)
    cos = np.cos(ang.astype(np.float64))
    sin = np.sin(ang.astype(np.float64)) * np.where(lo, -1.0, 1.0)[None, :]
    cos = np.concatenate([np.ones((n_ctx, width)), cos], axis=0)
    sin = np.concatenate([np.zeros((n_ctx, width)), sin], axis=0)
    return jnp.asarray(cos, F32), jnp.asarray(sin, F32)


def _softmax_rows(s):
    m = jnp.max(s, axis=-1, keepdims=True)
    p = jnp.exp2(s - m)
    return p, jnp.sum(p, axis=-1, keepdims=True)


def _dot_nt(a, b):
    return lax.dot_general(a, b, (((1,), (1,)), ((), ())), preferred_element_type=F32)


def _gqa_kernel(q_ref, k_ref, v_ref, o_ref, *, n_ctx):
    i = pl.program_id(1)

    def run(nk):
        k = k_ref[0:nk, :]
        v = v_ref[0:nk, :]
        q = q_ref[...]
        tq = q.shape[0]
        grp = _lane_group((tq, LANE), GQA_DH)

        def scores(g):
            mask = grp == g
            ql = jnp.concatenate([jnp.where(mask, q[:, 0:128], 0), jnp.where(mask, q[:, 128:256], 0)], axis=0)
            return _dot_nt(ql, k)

        outs = []
        s_next = scores(0)
        for g in range(GQA_KV):
            s = s_next
            if g + 1 < GQA_KV:
                s_next = scores(g + 1)
            p, l = _softmax_rows(s)
            outs.append(jnp.dot(p.astype(BF16), v, preferred_element_type=F32) / l)
        for half in range(2):
            sel = jnp.where(grp == 0, outs[0][half * tq:(half + 1) * tq], outs[1][half * tq:(half + 1) * tq])
            o_ref[:, half * 128:(half + 1) * 128] = sel.astype(o_ref.dtype)

    @pl.when(i == 0)
    def _():
        run(n_ctx)

    @pl.when(i > 0)
    def _():
        run(k_ref.shape[0])


def _gqa_attention(pb, n_ctx):
    b, s, _ = pb.shape
    tq = ROW_TILE
    return pl.pallas_call(
        functools.partial(_gqa_kernel, n_ctx=n_ctx),
        grid=(b, s // tq),
        in_specs=[pl.BlockSpec((None, tq, 256), lambda bi, i: (bi, i, 0)),
                  pl.BlockSpec((None, s, 128), lambda bi, i: (bi, 0, 2)),
                  pl.BlockSpec((None, s, 128), lambda bi, i: (bi, 0, 3))],
        out_specs=pl.BlockSpec((None, tq, 256), lambda bi, i: (bi, i, 0)),
        out_shape=jax.ShapeDtypeStruct((b, s, 256), BF16),
        compiler_params=_cparams(("arbitrary", "arbitrary")),
        name="gqa_attention",
    )(pb, pb, pb)


def _diff_kernel(q_ref, k_ref, v_ref, par_ref, g_ref, o_ref, *, n_ctx):
    i = pl.program_id(1)

    def run(nk):
        k = k_ref[0:nk, :]
        v = v_ref[0:nk, :]
        q = q_ref[...]
        tq = q.shape[0]
        lam = par_ref[0:1, 0:1]
        comp = _lane_group(q.shape, DIFF_DH)
        head = _lane_group(q.shape, DIFF_DV)
        acc = jnp.zeros(q.shape, F32)

        def scores(hh):
            ql = jnp.concatenate([jnp.where(comp == 2 * hh, q, 0), jnp.where(comp == 2 * hh + 1, q, 0)], axis=0)
            return _dot_nt(ql, k)

        s_next = scores(0)
        for hh in range(DIFF_HEADS):
            s = s_next
            if hh + 1 < DIFF_HEADS:
                s_next = scores(hh + 1)
            p, l = _softmax_rows(s)
            inv = 1.0 / l
            pc = (p[0:tq] * inv[0:tq] - p[tq:2 * tq] * (lam * inv[tq:2 * tq])).astype(BF16)
            o = jnp.dot(pc, v, preferred_element_type=F32)
            acc = jnp.where(head == hh, o, acc)
        y = acc * lax.rsqrt(_head_mean_sq(acc, DIFF_HEADS, DIFF_DV) + EPS) * g_ref[...]
        o_ref[...] = (y * par_ref[1:2, 0:1]).astype(o_ref.dtype)

    @pl.when(i == 0)
    def _():
        run(n_ctx)

    @pl.when(i > 0)
    def _():
        run(k_ref.shape[0])


def _diff_attention(pb, par, subln, n_ctx):
    b, s, _ = pb.shape
    tq = ROW_TILE
    return pl.pallas_call(
        functools.partial(_diff_kernel, n_ctx=n_ctx),
        grid=(b, s // tq),
        in_specs=[pl.BlockSpec((None, tq, 256), lambda bi, i: (bi, i, 0)),
                  pl.BlockSpec((None, s, 256), lambda bi, i: (bi, 0, 1)),
                  pl.BlockSpec((None, s, 256), lambda bi, i: (bi, 0, 2)),
                  pl.BlockSpec((8, 128), lambda bi, i: (0, 0)),
                  pl.BlockSpec((1, 256), lambda bi, i: (0, 0))],
        out_specs=pl.BlockSpec((None, tq, 256), lambda bi, i: (bi, i, 0)),
        out_shape=jax.ShapeDtypeStruct((b, s, 256), BF16),
        compiler_params=_cparams(("arbitrary", "arbitrary")),
        name="diff_attention",
    )(pb, pb, pb, par, subln)


def _na_kernel(q_ref, k_ref, v_ref, bias_ref, o_ref, *, n_ctx, n_rows):
    i = pl.program_id(1)
    tq = q_ref.shape[0]

    def stack_heads(q):
        head = _lane_group(q.shape, NA_DH)
        return jnp.concatenate([jnp.where(head == hh, q, 0) for hh in range(NA_HEADS)], axis=0)

    def unstack_heads(o, rows):
        head = _lane_group((rows, 256), NA_DH)
        out = jnp.zeros((rows, 256), F32)
        for hh in range(NA_HEADS):
            out = jnp.where(head == hh, o[hh * rows:(hh + 1) * rows], out)
        return out

    @pl.when(i == 0)
    def _():
        kc = k_ref[0:n_ctx, :]
        vc = v_ref[0:n_ctx, :]
        p, l = _softmax_rows(_dot_nt(stack_heads(q_ref[...]), kc))
        o = jnp.dot(p.astype(BF16), vc, preferred_element_type=F32) / l
        o_ref[...] = unstack_heads(o, tq).astype(o_ref.dtype)

    @pl.when(i > 0)
    def _():
        kc = k_ref[0:n_ctx, :]
        vc = v_ref[0:n_ctx, :]
        def scores(rr):
            r = (i - 1) * (tq // GRID_W) + rr
            rs = jnp.clip(r - NA_KH // 2, 0, n_rows - NA_KH)
            start = pl.multiple_of(n_ctx + rs * GRID_W, GRID_W)
            kcat = jnp.concatenate([kc, k_ref[pl.ds(start, NA_KH * GRID_W), :]], axis=0)
            q = q_ref[rr * GRID_W:(rr + 1) * GRID_W, :]
            return _dot_nt(stack_heads(q), kcat) + bias_ref[r - rs], start

        nxt = scores(0)
        for rr in range(tq // GRID_W):
            s, start = nxt
            if rr + 1 < tq // GRID_W:
                nxt = scores(rr + 1)
            vcat = jnp.concatenate([vc, v_ref[pl.ds(start, NA_KH * GRID_W), :]], axis=0)
            p, l = _softmax_rows(s)
            o = jnp.dot(p.astype(BF16), vcat, preferred_element_type=F32) / l
            o_ref[rr * GRID_W:(rr + 1) * GRID_W, :] = unstack_heads(o, GRID_W).astype(o_ref.dtype)


def _na_attention(pb, bias, n_ctx):
    b, s, _ = pb.shape
    tq = ROW_TILE
    n_rows = (s - n_ctx) // GRID_W
    return pl.pallas_call(
        functools.partial(_na_kernel, n_ctx=n_ctx, n_rows=n_rows),
        grid=(b, s // tq),
        in_specs=[pl.BlockSpec((None, tq, 256), lambda bi, i: (bi, i, 0)),
                  pl.BlockSpec((None, s, 256), lambda bi, i: (bi, 0, 1)),
                  pl.BlockSpec((None, s, 256), lambda bi, i: (bi, 0, 2)),
                  pl.BlockSpec(bias.shape, lambda bi, i: (0, 0, 0))],
        out_specs=pl.BlockSpec((None, tq, 256), lambda bi, i: (bi, i, 0)),
        out_shape=jax.ShapeDtypeStruct((b, s, 256), BF16),
        compiler_params=_cparams(("arbitrary", "arbitrary")),
        name="na_attention",
    )(pb, pb, pb, bias)


def _na_bias_table(rpb, n_ctx):
    depth = rpb.shape[0]
    pad = GRID_W - NA_KW
    padded = jnp.pad(rpb.astype(F32), ((0, 0), (0, 0), (0, 0), (pad, pad)))
    cols = jnp.stack([padded[..., GRID_W - 1 - w:2 * GRID_W - 1 - w] for w in range(GRID_W)], axis=3)
    vals = jnp.stack([cols[:, :, NA_KH - 1 - o:2 * NA_KH - 1 - o] for o in range(NA_KH)], axis=2)
    vals = jnp.transpose(vals, (0, 2, 1, 4, 3, 5))
    w = np.arange(GRID_W)[:, None, None]
    kc = np.arange(GRID_W)[None, None, :]
    cs = np.clip(w - NA_KW // 2, 0, GRID_W - NA_KW)
    inwin = np.broadcast_to((kc >= cs) & (kc < cs + NA_KW), (GRID_W, NA_KH, GRID_W))
    vals = jnp.where(jnp.asarray(inwin), vals * LOG2E, NEG).reshape(depth, NA_KH, NA_HEADS * GRID_W, NA_KH * GRID_W)
    return jnp.concatenate([jnp.zeros((depth, NA_KH, NA_HEADS * GRID_W, n_ctx), F32), vals], axis=-1)


def _hy_pre_kernel(u_ref, w_ref, b_ref, x0_ref, z_ref, *, bounds, tm):
    s = u_ref.shape[0]
    w0, w1, w2, bias = w_ref[0:1, :], w_ref[1:2, :], w_ref[2:3, :], b_ref[...]
    rid = lax.broadcasted_iota(I32, (tm, u_ref.shape[1]), 0)
    for c0 in range(0, s, tm):
        u = u_ref[c0:c0 + tm, :]
        prev = pltpu.roll(u, 1, 0)
        nxt = pltpu.roll(u, tm - 1, 0)
        first = jnp.zeros_like(w0) if c0 in bounds else u_ref[c0 - 1:c0, :]
        last = jnp.zeros_like(w0) if (c0 + tm) in bounds else u_ref[c0 + tm:c0 + tm + 1, :]
        prev = jnp.where(rid == 0, first, prev)
        nxt = jnp.where(rid == tm - 1, last, nxt)
        uc = prev * w0 + u * w1 + nxt * w2 + bias
        x0_ref[c0:c0 + tm, :] = uc[:, 0:HY_W]
        z_ref[c0:c0 + tm, :] = uc[:, HY_W:2 * HY_W] * uc[:, 2 * HY_W:3 * HY_W]


def _hy_pre(mix, w_short, b_short, n_ctx):
    b, s, _ = mix.shape
    return pl.pallas_call(
        functools.partial(_hy_pre_kernel, bounds=(0, n_ctx, s), tm=ROW_TILE),
        grid=(b,),
        in_specs=[pl.BlockSpec((None, s, 3 * HY_W), lambda bi: (bi, 0, 0)),
                  pl.BlockSpec((3, 3 * HY_W), lambda bi: (0, 0)),
                  pl.BlockSpec((1, 3 * HY_W), lambda bi: (0, 0))],
        out_specs=[pl.BlockSpec((None, s, HY_W), lambda bi: (bi, 0, 0)),
                   pl.BlockSpec((None, s, HY_W), lambda bi: (bi, 0, 0))],
        out_shape=[jax.ShapeDtypeStruct((b, s, HY_W), F32), jax.ShapeDtypeStruct((b, s, HY_W), F32)],
        compiler_params=_cparams(("arbitrary",)),
        name="hyena_pre",
    )(mix, w_short, b_short)


def _hy_filter_kernel(z_ref, w1_ref, b1_ref, w2_ref, b2_ref, w3_ref, b3_ref, w4_ref, f_ref, dec_ref, o_ref):
    def lin(a, w_ref, b_ref):
        return jnp.dot(a, w_ref[...], preferred_element_type=F32, precision=HIGHEST) + b_ref[...]

    hh = jnp.sin(f_ref[0:1, :] * lin(z_ref[...], w1_ref, b1_ref))
    hh = jnp.sin(f_ref[1:2, :] * lin(hh, w2_ref, b2_ref))
    hh = jnp.sin(f_ref[2:3, :] * lin(hh, w3_ref, b3_ref))
    hh = jnp.dot(hh, w4_ref[...], preferred_element_type=F32, precision=HIGHEST) * dec_ref[...]
    o_ref[...] = hh / jnp.sum(jnp.abs(hh), axis=0, keepdims=True)


def _pad_to(a, shape):
    return jnp.pad(a, [(0, t - s) for s, t in zip(a.shape, shape)])


def _hy_filters(n, w1, b1, w2, b2, w3, b3, w4, freq):
    depth = w1.shape[0]
    t = np.linspace(0.0, 1.0, n, dtype=np.float32)[:, None]
    w = np.float32(2.0 * math.pi / n) * np.arange(n, dtype=np.float32)[:, None]
    bands = np.linspace(1e-4, HY_BANDS - 1, HY_BANDS, dtype=np.float32)[None, :]
    z = np.concatenate([t, np.cos(w * bands), np.sin(w * bands)], axis=-1).astype(np.float32)
    z = np.pad(z, ((0, 0), (0, LANE - z.shape[1])))
    deltas = np.linspace(math.log(HY_TARGET) / HY_SLOW_DECAY, math.log(HY_TARGET) / HY_FAST_DECAY, HY_W, dtype=np.float32)
    deltas = np.tile(np.abs(deltas), 2)
    decay = np.exp(-t * deltas[None, :]).astype(np.float32)
    p = LANE
    args = (jnp.asarray(z), _pad_to(w1, (depth, p, p)), _pad_to(b1[:, None, :], (depth, 1, p)),
            _pad_to(w2, (depth, p, p)), _pad_to(b2[:, None, :], (depth, 1, p)),
            _pad_to(w3, (depth, p, p)), _pad_to(b3[:, None, :], (depth, 1, p)),
            _pad_to(w4, (depth, p, 2 * HY_W)), _pad_to(freq, (depth, 8, p)), jnp.asarray(decay))
    per_layer = lambda shp: pl.BlockSpec((None,) + shp, lambda l: (l,) + (0,) * len(shp))
    const = lambda shp: pl.BlockSpec(shp, lambda l: (0,) * len(shp))
    return pl.pallas_call(
        _hy_filter_kernel,
        grid=(depth,),
        in_specs=[const((n, p)), per_layer((p, p)), per_layer((1, p)), per_layer((p, p)), per_layer((1, p)),
                  per_layer((p, p)), per_layer((1, p)), per_layer((p, 2 * HY_W)), per_layer((8, p)),
                  const((n, 2 * HY_W))],
        out_specs=per_layer((n, 2 * HY_W)),
        out_shape=jax.ShapeDtypeStruct((depth, n, 2 * HY_W), F32),
        compiler_params=_cparams(("arbitrary",)),
        name="hyena_filters",
    )(*args)


def _conv_taps(filt):
    hf, hb = filt[..., :HY_W], filt[..., HY_W:]
    taps = jnp.concatenate([hb[:, :0:-1], (hf[:, 0:1] + hb[:, 0:1]), hf[:, 1:], jnp.zeros_like(hf[:, 0:1])], axis=1)
    return jnp.transpose(taps, (0, 2, 1))


def _hy_conv_kernel(z_ref, k_ref, o_ref, *, nblk, nb):
    cg = z_ref.shape[0]
    n2 = k_ref.shape[1]
    ncols = (2 * nblk - 1) * LANE

    def body(ci, _):
        krow = k_ref[pl.ds(ci, 1), :]
        kb = jnp.broadcast_to(krow, (LANE, n2))
        big = pltpu.roll(kb, n2 - (LANE - 1), 1, stride=1, stride_axis=0)[:, :ncols].astype(BF16)
        o_ref[ci] = jnp.zeros(o_ref.shape[1:], F32)
        for m in range(-(nblk - 1), nblk):
            km = big[:, (m + nblk - 1) * LANE:(m + nblk) * LANE]
            cnt = (nblk - abs(m)) * nb
            src = 0 if m >= 0 else -m * nb
            dst = m * nb if m >= 0 else 0
            zin = z_ref[ci, src:src + cnt, :]
            o_ref[ci, dst:dst + cnt, :] += jnp.dot(zin, km, preferred_element_type=F32)
        return 0

    lax.fori_loop(0, cg, body, 0)


def _hy_conv(zr, taps, nblk, nb):
    c, r, _ = zr.shape
    cg = 8
    return pl.pallas_call(
        functools.partial(_hy_conv_kernel, nblk=nblk, nb=nb),
        grid=(c // cg,),
        in_specs=[pl.BlockSpec((cg, r, LANE), lambda i: (i, 0, 0)),
                  pl.BlockSpec((cg, taps.shape[1]), lambda i: (i, 0))],
        out_specs=pl.BlockSpec((cg, r, LANE), lambda i: (i, 0, 0)),
        out_shape=jax.ShapeDtypeStruct((c, r, LANE), F32),
        compiler_params=_cparams(("arbitrary",)),
        name="hyena_conv",
    )(zr, taps)


def _hy_long_conv(z, taps):
    b, n, c = z.shape
    nblk = n // LANE
    zr = jnp.transpose(z.astype(BF16).reshape(b, nblk, LANE, c), (3, 1, 0, 2)).reshape(c, nblk * b, LANE)
    y = _hy_conv(zr, taps, nblk, b)
    return jnp.transpose(y.reshape(c, nblk, b, LANE), (2, 1, 3, 0)).reshape(b, n, c)


def _merge_kernel(h_ref, mod_ref, ya_ref, x0_ref, z_ref, yconv_ref, hb_ref, yc_ref, yd_ref, gate_ref,
                  wb_ref, wo_ref, g2_ref, wq_ref, o_ref, q_ref, nb_ref, *, n_ctx, tm):
    i = pl.program_id(1)
    yb = (x0_ref[...] * (yconv_ref[...] + z_ref[...] * hb_ref[...])).astype(BF16)
    ys = (ya_ref[...], yb, yc_ref[...], yd_ref[...])
    acc = jnp.zeros(h_ref.shape, F32)
    d = h_ref.shape[1]
    for k in range(N_BRANCH):
        proj = jnp.dot(ys[k], wb_ref[k], preferred_element_type=F32)
        acc = acc + jax.nn.sigmoid(gate_ref[:, k * d:(k + 1) * d].astype(F32)) * proj
    out = jnp.dot(acc.astype(BF16), wo_ref[...], preferred_element_type=F32)
    row = i * tm + lax.broadcasted_iota(I32, out.shape, 0)
    gate = jnp.where(row < n_ctx, mod_ref[0, 2:3, :], mod_ref[1, 2:3, :])
    hn = h_ref[...] + gate * out
    o_ref[...] = hn
    nb = _norm_modulate(hn, mod_ref, g2_ref, 3, i * tm, n_ctx).astype(BF16)
    nb_ref[...] = nb
    q_ref[...] = jnp.dot(nb, wq_ref[...], preferred_element_type=F32).astype(q_ref.dtype)


def _merge(h, modl, ya, x0, z, yconv, hbias, yc, yd, gates, wb, wo, g2, wq, n_ctx):
    b, s, d = h.shape
    tm = ROW_TILE
    row = lambda w: pl.BlockSpec((None, tm, w), lambda bi, i: (bi, i, 0))
    return pl.pallas_call(
        functools.partial(_merge_kernel, n_ctx=n_ctx, tm=tm),
        grid=(b, s // tm),
        in_specs=[row(d), pl.BlockSpec((None, 2, 8, d), lambda bi, i: (bi, 0, 0, 0)),
                  row(256), row(256), row(256), row(256), pl.BlockSpec((1, 256), lambda bi, i: (0, 0)),
                  row(256), row(256), row(GATE_W),
                  pl.BlockSpec(wb.shape, lambda bi, i: (0, 0, 0)),
                  pl.BlockSpec(wo.shape, lambda bi, i: (0, 0)),
                  pl.BlockSpec((1, d), lambda bi, i: (0, 0)),
                  pl.BlockSpec(wq.shape, lambda bi, i: (0, 0))],
        out_specs=[row(d), row(wq.shape[1]), row(d)],
        out_shape=[jax.ShapeDtypeStruct((b, s, d), F32), jax.ShapeDtypeStruct((b, s, wq.shape[1]), BF16),
                   jax.ShapeDtypeStruct((b, s, d), BF16)],
        compiler_params=_cparams(("arbitrary", "arbitrary")),
        name="merge",
    )(h, modl, ya, x0, z, yconv, hbias, yc, yd, gates, wb, wo, g2, wq)


def _topk_rows(s, label, k):
    vals, labs = [], []
    for _ in range(k):
        m = jnp.max(s, axis=0, keepdims=True)
        lb = jnp.min(jnp.where(s == m, label, float(2 ** 20)), axis=0, keepdims=True)
        vals.append(m)
        labs.append(lb)
        s = jnp.where(label == lb, -jnp.inf, s)
    return jnp.concatenate(vals, axis=0), jnp.concatenate(labs, axis=0).astype(I32)


def _topk_sorted_columns(s, k):
    n = s.shape[0] // 8
    sub = lax.broadcasted_iota(I32, (8, s.shape[1]), 0)
    vals = [s[8 * v:8 * v + 8, :] for v in range(n)]
    rows = [(sub + 8 * v).astype(F32) for v in range(n)]
    for p in range(n):
        for v in range(p % 2, n - 1, 2):
            swap = vals[v + 1] > vals[v]
            vals[v], vals[v + 1] = jnp.where(swap, vals[v + 1], vals[v]), jnp.where(swap, vals[v], vals[v + 1])
            rows[v], rows[v + 1] = jnp.where(swap, rows[v + 1], rows[v]), jnp.where(swap, rows[v], rows[v + 1])
    out_v, out_r = [], []
    for r in range(k):
        m = jnp.max(vals[0], axis=0, keepdims=True)
        best = jnp.min(jnp.where(vals[0] == m, rows[0], float(2 ** 20)), axis=0, keepdims=True)
        out_v.append(m)
        out_r.append(best)
        if r + 1 < k:
            pop = rows[0] == best
            for v in range(min(n - 1, k - 1 - r)):
                vals[v] = jnp.where(pop, vals[v + 1], vals[v])
                rows[v] = jnp.where(pop, rows[v + 1], rows[v])
            if n - 1 < k - 1 - r:
                vals[n - 1] = jnp.where(pop, -jnp.inf, vals[n - 1])
    return jnp.concatenate(out_v, axis=0), jnp.concatenate(out_r, axis=0).astype(I32)


_CAND_FIXED_A = ((0, 0), (0, 8), (1, 0), (2, 0), (3, 0))
_CAND_FIXED_B = ((0, 0), (0, 8), (1, 0), (2, 0))


def _select_rows(table, pos):
    out = jnp.zeros(pos.shape, table.dtype)
    for a in range(table.shape[0]):
        out = jnp.where(pos == a, table[a:a + 1, :], out)
    return out


def _peer_route_kernel(q_ref, keys_ref, i_ref, j_ref, g_ref):
    st = _dot_nt(keys_ref[...], q_ref[...].astype(BF16))
    t = st.shape[1]
    sv1, si1 = _topk_sorted_columns(st[0:PEER_NKEYS], PEER_TOPK)
    sv2, si2 = _topk_sorted_columns(st[PEER_NKEYS:2 * PEER_NKEYS], PEER_TOPK)
    r8 = lax.broadcasted_iota(I32, (8, t), 0)
    cands, labels = [], []
    for a, b0 in _CAND_FIXED_A:
        cands.append(sv1[a:a + 1, :] + sv2[b0:b0 + 8, :])
        labels.append((a * PEER_TOPK + b0 + r8).astype(F32))
    for b, a0 in _CAND_FIXED_B:
        c = sv1[a0:a0 + 8, :] + sv2[b:b + 1, :]
        cands.append(jnp.where(r8 < 4, -jnp.inf, c) if a0 == 0 else c)
        labels.append(((a0 + r8) * PEER_TOPK + b).astype(F32))
    best, pos = _topk_rows(jnp.concatenate(cands, axis=0), jnp.concatenate(labels, axis=0), PEER_TOPK)
    i_ref[...] = _select_rows(si1, pos >> int(math.log2(PEER_TOPK)))
    j_ref[...] = _select_rows(si2, pos & (PEER_TOPK - 1))
    e = jnp.exp(best - jnp.max(best, axis=0, keepdims=True))
    g_ref[...] = e / jnp.sum(e, axis=0, keepdims=True)


def _peer_route(q, keys_blk):
    t, _ = q.shape
    tt = 1024
    assert t % tt == 0
    out = pl.BlockSpec((None, PEER_TOPK, tt), lambda ti, hh: (hh, 0, ti))
    shp = (PEER_HEADS, PEER_TOPK, t)
    return pl.pallas_call(
        _peer_route_kernel,
        grid=(t // tt, PEER_HEADS),
        in_specs=[pl.BlockSpec((tt, PEER_DK), lambda ti, hh: (ti, hh)),
                  pl.BlockSpec((None, 2 * PEER_NKEYS, PEER_DK), lambda ti, hh: (hh, 0, 0))],
        out_specs=[out, out, out],
        out_shape=[jax.ShapeDtypeStruct(shp, I32), jax.ShapeDtypeStruct(shp, I32), jax.ShapeDtypeStruct(shp, F32)],
        compiler_params=_cparams(("arbitrary", "arbitrary")),
        name="peer_route",
    )(q, keys_blk)


GS_HALF = PEER_NKEYS // 2
GS_PITCH = GS_HALF + 1
GS_UNROLL = 64
PEER_TM = 576
PEER_TE = 2048
U32 = jnp.uint32


def _peer_expert_kernel(h_ref, mod_ref, x_ref, i_ref, j_ref, g_ref, ut_ref, v_ref, o_ref, gs_ref, acc_ref,
                        *, n_ctx, tm):
    ti, e = pl.program_id(1), pl.program_id(2)
    n_e = pl.num_programs(2)
    te = ut_ref.shape[1]
    nk = PEER_NKEYS

    @pl.when((pl.program_id(0) == 0) & (ti == 0) & (e == 0))
    def _():
        acc_ref[...] = jnp.zeros(acc_ref.shape, F32)

    @pl.when(e == 0)
    def _():
        sub = lax.broadcasted_iota(I32, (nk, nk), 0)

        def build(t):
            irow = i_ref[pl.ds(t, 1), :]
            jrow = j_ref[pl.ds(t, 1), :]
            grow = g_ref[pl.ds(t, 1), :]
            at = jnp.where(sub == irow, 1.0, 0.0).astype(BF16)
            bt = jnp.where(sub == jrow, grow, 0.0).astype(BF16)
            gt = _dot_nt(at, bt).astype(BF16).astype(F32)
            lo = pltpu.bitcast(gt[0:GS_HALF], U32) >> 16
            hi = pltpu.bitcast(gt[GS_HALF:nk], U32)
            gs_ref[pl.ds(t * GS_PITCH, GS_HALF), :] = hi | lo

        def body(tb, _):
            for u in range(GS_UNROLL):
                build(tb * GS_UNROLL + u)
            return 0

        lax.fori_loop(0, tm // GS_UNROLL, body, 0)

    nb = te // nk // 2

    def gelu(a):
        return 0.5 * a * (1.0 + lax.erf(a * (2.0 ** -0.5)))

    hid = jnp.dot(x_ref[...], ut_ref[...], preferred_element_type=F32)
    first, second = [], []
    for k in range(nb):
        word = gs_ref[pl.ds(e * nb + k, tm, stride=GS_PITCH), :]
        g_lo = pltpu.bitcast(word << 16, F32)
        g_hi = pltpu.bitcast(word & jnp.uint32(0xFFFF0000), F32)
        first.append((g_lo * gelu(hid[:, k * nk:(k + 1) * nk])).astype(BF16))
        second.append((g_hi * gelu(hid[:, (nb + k) * nk:(nb + k + 1) * nk])).astype(BF16))
    acc_ref[...] += jnp.dot(jnp.concatenate(first + second, axis=1), v_ref[...], preferred_element_type=F32)

    @pl.when(e == n_e - 1)
    def _():
        row = ti * tm + lax.broadcasted_iota(I32, acc_ref.shape, 0)
        gate = jnp.where(row < n_ctx, mod_ref[0, 5:6, :], mod_ref[1, 5:6, :])
        o_ref[...] = h_ref[...] + gate * acc_ref[...]
        acc_ref[...] = jnp.zeros(acc_ref.shape, F32)


def _peer_experts(h, modl, xb, i_idx, j_idx, g, ut, v, n_ctx):
    b, s, d = h.shape
    tm, te = PEER_TM, PEER_TE
    n_chunks = ut.shape[0]
    slots = PEER_HEADS * PEER_TOPK
    row = lambda w: pl.BlockSpec((None, tm, w), lambda bi, i, e: (bi, i, 0), pipeline_mode=pl.Buffered(1))
    return pl.pallas_call(
        functools.partial(_peer_expert_kernel, n_ctx=n_ctx, tm=tm),
        grid=(b, s // tm, n_chunks),
        in_specs=[row(d), pl.BlockSpec((None, 2, 8, d), lambda bi, i, e: (bi, 0, 0, 0)),
                  row(d), row(slots), row(slots), row(slots),
                  pl.BlockSpec((None, d, te), lambda bi, i, e: (e, 0, 0)),
                  pl.BlockSpec((te, d), lambda bi, i, e: (e, 0))],
        out_specs=pl.BlockSpec((None, tm, d), lambda bi, i, e: (bi, i, 0)),
        out_shape=jax.ShapeDtypeStruct((b, s, d), F32),
        scratch_shapes=[pltpu.VMEM((tm * GS_PITCH, PEER_NKEYS), U32), pltpu.VMEM((tm, d), F32)],
        compiler_params=_cparams(("arbitrary", "arbitrary", "arbitrary")),
        name="peer_experts",
    )(h, modl, xb, i_idx, j_idx, g, ut, v)


def _final_norm_kernel(x_ref, g_ref, o_ref):
    x = x_ref[...]
    o_ref[...] = x * lax.rsqrt(jnp.mean(x * x, axis=-1, keepdims=True) + EPS) * g_ref[...]


def _final_norm(h, g, n_ctx):
    b, s, d = h.shape
    tm = ROW_TILE
    n_lat = s - n_ctx
    off = n_ctx // tm
    return pl.pallas_call(
        _final_norm_kernel,
        grid=(b, n_lat // tm),
        in_specs=[pl.BlockSpec((None, tm, d), lambda bi, i: (bi, i + off, 0)),
                  pl.BlockSpec((1, d), lambda bi, i: (0, 0))],
        out_specs=pl.BlockSpec((None, tm, d), lambda bi, i: (bi, i, 0)),
        out_shape=jax.ShapeDtypeStruct((b, n_lat, d), F32),
        compiler_params=_cparams(("arbitrary", "arbitrary")),
        name="final_norm",
    )(h, g)


def _prep_w_in(w_in):
    depth, d, _ = w_in.shape
    mixw = w_in[:, :, :2816]
    gq = mixw[:, :, 1536:1792].reshape(depth, d, GQA_HEADS, GQA_DH)[:, :, (0, 2, 1, 3), :].reshape(depth, d, 256)
    mixw = jnp.concatenate([mixw[:, :, :1536], gq, mixw[:, :, 1792:], jnp.zeros((depth, d, MIX_W - 2816), w_in.dtype)], axis=-1)
    return jnp.concatenate([mixw, w_in[:, :, 2816:]], axis=-1).astype(BF16)


def _peer_chunk_order(w):
    n_exp, d = w.shape
    nb = PEER_TE // PEER_NKEYS // 2
    return w.reshape(2, GS_HALF // nb, nb, PEER_NKEYS, d).transpose(1, 0, 2, 3, 4).reshape(n_exp, d)


def _prep_peer_keys(keys):
    depth = keys.shape[0]
    half = PEER_DK // 2
    z = jnp.zeros((depth, PEER_HEADS, PEER_NKEYS, half), keys.dtype)
    top = jnp.concatenate([keys[:, :, 0], z], axis=-1)
    bot = jnp.concatenate([z, keys[:, :, 1]], axis=-1)
    return jnp.concatenate([top, bot], axis=2).astype(BF16)


def kernel(x, c, ctx, c_ctx, w_mod, b_mod, norm1_g, norm2_g, w_in, na_rpb, hy_short_w, hy_short_b, hy_w1, hy_b1, hy_w2, hy_b2, hy_w3, hy_b3, hy_w4, hy_freq, hy_bias, gqa_qn, gqa_kn, diff_lq1, diff_lk1, diff_lq2, diff_lk2, diff_subln, w_branch, w_out, peer_wq, peer_keys, peer_u, peer_v, final_g):
    B, L, D = x.shape
    C = ctx.shape[1]
    S = C + L
    depth = w_mod.shape[0]
    assert C == ROW_TILE and L % ROW_TILE == 0 and D == 1024 and L % GRID_W == 0

    h = jnp.concatenate([ctx, x], axis=1)

    r = -(-(B + 1) // 8) * 8
    cvec = jnp.zeros((r, D), F32).at[:B].set(c).at[B].set(c_ctx)
    modall = _modulation(cvec, w_mod, b_mod)
    mod_lat = modall[:, :B].reshape(depth, B, 1, 6, D)
    mod_ctx = jnp.broadcast_to(modall[:, B].reshape(depth, 1, 1, 6, D), (depth, B, 1, 6, D))
    mods = jnp.pad(jnp.concatenate([mod_ctx, mod_lat], axis=2), ((0, 0), (0, 0), (0, 0), (0, 2), (0, 0)))

    fargs = (hy_w1, hy_b1, hy_w2, hy_b2, hy_w3, hy_b3, hy_w4, hy_freq)
    taps_lat = _conv_taps(_hy_filters(L, *fargs))
    taps_ctx = _conv_taps(_hy_filters(C, *fargs))

    cg, sg = _rope_tables(C, L, GQA_DH, 256)
    cd, sd = _rope_tables(C, L, DIFF_DH, 256)

    lam_init = jnp.asarray([0.8 - 0.6 * math.exp(-0.3 * l) for l in range(depth)], F32)
    lam = (jnp.exp(jnp.sum(diff_lq1.astype(F32) * diff_lk1.astype(F32), axis=-1))
           - jnp.exp(jnp.sum(diff_lq2.astype(F32) * diff_lk2.astype(F32), axis=-1)) + lam_init)
    diff_par = jnp.zeros((depth, 8, LANE), F32).at[:, 0, :].set(lam[:, None]).at[:, 1, :].set(1.0 - lam_init[:, None])

    wb = w_branch.astype(BF16)
    wb = wb.at[:, 2].set(wb[:, 2].reshape(depth, GQA_HEADS, GQA_DH, D)[:, (0, 2, 1, 3)].reshape(depth, BRANCH_W, D))

    xs = dict(
        mods=mods, n1=norm1_g[:, None, :], n2=norm2_g[:, None, :], w_in=_prep_w_in(w_in),
        bias=_na_bias_table(na_rpb, C), sw=hy_short_w, sb=hy_short_b[:, None, :], hbias=hy_bias[:, None, :],
        taps_lat=taps_lat, taps_ctx=taps_ctx,
        qn=jnp.tile(gqa_qn, (1, GQA_HEADS))[:, None, :], kn=jnp.tile(gqa_kn, (1, GQA_KV))[:, None, :],
        diff_par=diff_par, subln=jnp.tile(diff_subln, (1, DIFF_HEADS))[:, None, :],
        wb=wb, wo=w_out.astype(BF16), wq=peer_wq.astype(BF16), keys=_prep_peer_keys(peer_keys),
        u=peer_u, v=peer_v,
    )

    def layer(h, p):
        pb_na, pb_gqa, pb_diff, hy, gates = _in_proj(h, p["mods"], p["n1"], p["w_in"], cg, sg, cd, sd,
                                                     p["qn"], p["kn"], C)
        ya = _na_attention(pb_na, p["bias"], C)
        yc = _gqa_attention(pb_gqa, C)
        yd = _diff_attention(pb_diff, p["diff_par"], p["subln"], C)
        x0, z = _hy_pre(hy, p["sw"], p["sb"], C)
        yconv = jnp.concatenate([_hy_long_conv(z[:, :C], p["taps_ctx"]), _hy_long_conv(z[:, C:], p["taps_lat"])], axis=1)
        h, q, xb = _merge(h, p["mods"], ya, x0, z, yconv, p["hbias"], yc, yd, gates, p["wb"], p["wo"],
                          p["n2"], p["wq"], C)
        i_idx, j_idx, g = _peer_route(q.reshape(B * S, -1), p["keys"])
        slots = PEER_HEADS * PEER_TOPK
        to_rows = lambda a: jnp.transpose(a, (2, 0, 1)).reshape(B, S, slots)
        ut = jnp.transpose(_peer_chunk_order(p["u"].astype(BF16)).reshape(-1, PEER_TE, D), (0, 2, 1))
        h = _peer_experts(h, p["mods"], xb, to_rows(i_idx), to_rows(j_idx), to_rows(g), ut,
                          _peer_chunk_order(p["v"].astype(BF16)), C)
        return h, None

    h, _ = lax.scan(layer, h, xs)
    return _final_norm(h, final_g[None, :], C)
```

```python
import functools
import math

import numpy as np
import jax
import jax.numpy as jnp
from jax import lax
from jax.experimental import pallas as pl
from jax.experimental.pallas import tpu as pltpu

F32 = jnp.float32
BF16 = jnp.bfloat16
I32 = jnp.int32
HIGHEST = lax.Precision.HIGHEST

EPS = 1e-6
GRID_W = 64
ROPE_THETA = 10000.0
NA_HEADS, NA_DH, NA_KH, NA_KW = 4, 64, 8, 16
HY_W, HY_BANDS, HY_FFN = 256, 16, 64
HY_FAST_DECAY, HY_SLOW_DECAY, HY_TARGET = 0.3, 1.5, 1e-2
GQA_HEADS, GQA_KV, GQA_DH = 4, 2, 64
DIFF_HEADS, DIFF_DH, DIFF_DV = 4, 32, 64
N_BRANCH, BRANCH_W = 4, 256
PEER_HEADS, PEER_NKEYS, PEER_DK, PEER_TOPK = 8, 128, 128, 16

LANE = 128
ROW_TILE = 256
VMEM_LIMIT = 56 * 1024 * 1024
NEG = -1e30
LOG2E = math.log2(math.e)
MIX_W = 3072
GATE_W = N_BRANCH * 1024
COL_TILE = 1024


def _cparams(sem):
    return pltpu.CompilerParams(dimension_semantics=sem, vmem_limit_bytes=VMEM_LIMIT)


def _mod_kernel(c_ref, w_ref, b_ref, o_ref):
    cv = c_ref[...]
    a = cv * jax.nn.sigmoid(cv)
    o_ref[...] = jnp.dot(a, w_ref[...], preferred_element_type=F32, precision=HIGHEST) + b_ref[...]


def _modulation(cvec, w_mod, b_mod):
    depth, d, n = w_mod.shape
    r = cvec.shape[0]
    tn = 1024
    return pl.pallas_call(
        _mod_kernel,
        grid=(depth, n // tn),
        in_specs=[pl.BlockSpec((r, d), lambda l, j: (0, 0)),
                  pl.BlockSpec((None, d, tn), lambda l, j: (l, 0, j)),
                  pl.BlockSpec((None, 1, tn), lambda l, j: (l, 0, j))],
        out_specs=pl.BlockSpec((None, r, tn), lambda l, j: (l, 0, j)),
        out_shape=jax.ShapeDtypeStruct((depth, r, n), F32),
        compiler_params=_cparams(("arbitrary", "arbitrary")),
        name="modulation",
    )(cvec, w_mod, b_mod.reshape(depth, 1, n))


def _norm_modulate(x, mod_ref, g_ref, srow, row0, n_ctx):
    y = x * lax.rsqrt(jnp.mean(x * x, axis=-1, keepdims=True) + EPS) * g_ref[...]
    row = row0 + lax.broadcasted_iota(I32, x.shape, 0)
    is_ctx = row < n_ctx
    shift = jnp.where(is_ctx, mod_ref[0, srow:srow + 1, :], mod_ref[1, srow:srow + 1, :])
    scale = jnp.where(is_ctx, mod_ref[0, srow + 1:srow + 2, :], mod_ref[1, srow + 1:srow + 2, :])
    return y * (1.0 + scale) + shift


def _in_proj_kernel(h_ref, mod_ref, g_ref, w_ref, cg_ref, sg_ref, cd_ref, sd_ref, qn_ref, kn_ref, bd_ref,
                    na_ref, gqa_ref, diff_ref, hy_ref, gate_ref, nb_ref, *, tm, n_ctx, n_mix):
    i, j = pl.program_id(1), pl.program_id(2)

    @pl.when(j == 0)
    def _():
        nb_ref[...] = _norm_modulate(h_ref[...], mod_ref, g_ref, 0, i * tm, n_ctx).astype(BF16)

    def proj():
        return jnp.dot(nb_ref[...], w_ref[...], preferred_element_type=F32)

    @pl.when(j == 0)
    def _():
        res = proj()
        na_ref[:, 0:256] = (res[:, 0:256] * (LOG2E * NA_DH ** -0.5)).astype(BF16)
        na_ref[:, 256:768] = res[:, 256:768].astype(BF16)
        hy_ref[:, 0:HY_W] = res[:, 768:1024]

    @pl.when(j == 1)
    def _():
        res = jnp.dot(nb_ref[...], w_ref[:, 512:1024], preferred_element_type=F32)
        hy_ref[:, HY_W:3 * HY_W] = jnp.dot(nb_ref[...], w_ref[:, 0:512], preferred_element_type=F32)
        cg, sg = cg_ref[...], sg_ref[...]
        gq, gk = res[:, 0:256], res[:, 256:384]
        bd = bd_ref[...]
        gq = gq * lax.rsqrt(_head_mean_sq_mxu(gq, GQA_DH, bd) + EPS) * qn_ref[...]
        gk = gk * lax.rsqrt(_head_mean_sq_mxu(gk, GQA_DH, bd[0:128, 0:128]) + EPS) * kn_ref[...]
        gqa_ref[:, 0:256] = (_rope(gq, cg, sg, GQA_DH // 4) * (LOG2E * GQA_DH ** -0.5)).astype(BF16)
        gqa_ref[:, 256:384] = _rope(gk, cg[:, 0:128], sg[:, 0:128], GQA_DH // 4).astype(BF16)
        gqa_ref[:, 384:512] = res[:, 384:512].astype(BF16)

    @pl.when(j == 2)
    def _():
        res = proj()
        cd, sd = cd_ref[...], sd_ref[...]
        diff_ref[:, 0:256] = (_rope(res[:, 0:256], cd, sd, DIFF_DH // 4) * (LOG2E * DIFF_DH ** -0.5)).astype(BF16)
        diff_ref[:, 256:512] = _rope(res[:, 256:512], cd, sd, DIFF_DH // 4).astype(BF16)
        diff_ref[:, 512:768] = res[:, 512:768].astype(BF16)

    @pl.when(j >= n_mix)
    def _():
        gate_ref[...] = proj().astype(gate_ref.dtype)


def _in_proj(h, modl, g, w, cg, sg, cd, sd, qn, kn, n_ctx):
    b, s, d = h.shape
    tm, tn = 1152, COL_TILE
    n_mix = MIX_W // tn
    n_tot = (MIX_W + GATE_W) // tn
    assert n_mix == 3
    const = lambda shp: pl.BlockSpec(shp, lambda bi, i, j: (0,) * len(shp))
    tab = pl.BlockSpec((tm, 256), lambda bi, i, j: (i, 0))
    rows = lambda wd: pl.BlockSpec((None, tm, wd), lambda bi, i, j: (bi, i, 0))
    head_ones = jnp.asarray(np.kron(np.eye(GQA_HEADS), np.ones((GQA_DH, GQA_DH))), BF16)
    return pl.pallas_call(
        functools.partial(_in_proj_kernel, tm=tm, n_ctx=n_ctx, n_mix=n_mix),
        grid=(b, s // tm, n_tot),
        in_specs=[rows(d), pl.BlockSpec((None, 2, 8, d), lambda bi, i, j: (bi, 0, 0, 0)), const((1, d)),
                  pl.BlockSpec((d, tn), lambda bi, i, j: (0, j)),
                  tab, tab, tab, tab, const((1, 256)), const((1, 128)), const((256, 256))],
        out_specs=[rows(768), rows(512), rows(768), rows(3 * HY_W),
                   pl.BlockSpec((None, tm, tn), lambda bi, i, j: (bi, i, jnp.maximum(j - n_mix, 0)))],
        out_shape=[jax.ShapeDtypeStruct((b, s, 768), BF16), jax.ShapeDtypeStruct((b, s, 512), BF16),
                   jax.ShapeDtypeStruct((b, s, 768), BF16), jax.ShapeDtypeStruct((b, s, 3 * HY_W), F32),
                   jax.ShapeDtypeStruct((b, s, GATE_W), BF16)],
        scratch_shapes=[pltpu.VMEM((tm, d), BF16)],
        compiler_params=_cparams(("arbitrary", "arbitrary", "arbitrary")),
        name="in_proj",
    )(h, modl, g, w, cg, sg, cd, sd, qn, kn, head_ones)


def _lane_group(shape, width):
    return lax.broadcasted_iota(I32, shape, 1) >> int(math.log2(width))


def _head_mean_sq(x, n_heads, dh):
    hid = _lane_group(x.shape, dh)
    x2 = x * x
    ms = jnp.zeros_like(x)
    for hh in range(n_heads):
        m = hid == hh
        s = jnp.sum(jnp.where(m, x2, 0.0), axis=-1, keepdims=True)
        ms = jnp.where(m, s, ms)
    return ms * (1.0 / dh)


def _head_mean_sq_mxu(x, dh, head_ones):
    x2 = x * x
    hi = x2.astype(BF16)
    lo = (x2 - hi.astype(F32)).astype(BF16)
    s = jnp.dot(hi, head_ones, preferred_element_type=F32) + jnp.dot(lo, head_ones, preferred_element_type=F32)
    return s * (1.0 / dh)


def _rope(x, cos, sin_signed, qs):
    w = x.shape[-1]
    lane = lax.broadcasted_iota(I32, x.shape, 1)
    lo = (lane & (2 * qs - 1)) < qs
    partner = jnp.where(lo, pltpu.roll(x, w - qs, 1), pltpu.roll(x, qs, 1))
    return x * cos + partner * sin_signed


def _rope_tables(n_ctx, n_lat, dh, width):
    qs = dh // 4
    t = np.arange(n_lat)
    rows, cols = (t // GRID_W).astype(np.float64), (t % GRID_W).astype(np.float64)
    lane = np.arange(width) % dh
    part = lane // (2 * qs)
    u = lane % (2 * qs)
    f = u % qs
    lo = u < qs
    freqs = ROPE_THETA ** (-f.astype(np.float64) / qs)
    pos = np.where(part[None, :] == 0, rows[:, None], cols[:, None])
    ang = (pos.astype(np.float32) * freqs.astype(np.float32)[None, :]).astype(np.float32)
    cos = np.cos(ang.astype(np.float64))
    sin = np.sin(ang.astype(np.float64)) * np.where(lo, -1.0, 1.0)[None, :]
    cos = np.concatenate([np.ones((n_ctx, width)), cos], axis=0)
    sin = np.concatenate([np.zeros((n_ctx, width)), sin], axis=0)
    return jnp.asarray(cos, F32), jnp.asarray(sin, F32)


def _softmax_rows(s):
    m = jnp.max(s, axis=-1, keepdims=True)
    p = jnp.exp2(s - m)
    return p, jnp.sum(p, axis=-1, keepdims=True)


def _dot_nt(a, b):
    return lax.dot_general(a, b, (((1,), (1,)), ((), ())), preferred_element_type=F32)


def _gqa_kernel(q_ref, k_ref, v_ref, o_ref, *, n_ctx):
    i = pl.program_id(1)

    def run(nk):
        k = k_ref[0:nk, :]
        v = v_ref[0:nk, :]
        q = q_ref[...]
        tq = q.shape[0]
        grp = _lane_group((tq, LANE), GQA_DH)

        def scores(g):
            mask = grp == g
            ql = jnp.concatenate([jnp.where(mask, q[:, 0:128], 0), jnp.where(mask, q[:, 128:256], 0)], axis=0)
            return _dot_nt(ql, k)

        outs = []
        s_next = scores(0)
        for g in range(GQA_KV):
            s = s_next
            if g + 1 < GQA_KV:
                s_next = scores(g + 1)
            p, l = _softmax_rows(s)
            outs.append(jnp.dot(p.astype(BF16), v, preferred_element_type=F32) / l)
        for half in range(2):
            sel = jnp.where(grp == 0, outs[0][half * tq:(half + 1) * tq], outs[1][half * tq:(half + 1) * tq])
            o_ref[:, half * 128:(half + 1) * 128] = sel.astype(o_ref.dtype)

    @pl.when(i == 0)
    def _():
        run(n_ctx)

    @pl.when(i > 0)
    def _():
        run(k_ref.shape[0])


def _gqa_attention(pb, n_ctx):
    b, s, _ = pb.shape
    tq = ROW_TILE
    return pl.pallas_call(
        functools.partial(_gqa_kernel, n_ctx=n_ctx),
        grid=(b, s // tq),
        in_specs=[pl.BlockSpec((None, tq, 256), lambda bi, i: (bi, i, 0)),
                  pl.BlockSpec((None, s, 128), lambda bi, i: (bi, 0, 2)),
                  pl.BlockSpec((None, s, 128), lambda bi, i: (bi, 0, 3))],
        out_specs=pl.BlockSpec((None, tq, 256), lambda bi, i: (bi, i, 0)),
        out_shape=jax.ShapeDtypeStruct((b, s, 256), BF16),
        compiler_params=_cparams(("arbitrary", "arbitrary")),
        name="gqa_attention",
    )(pb, pb, pb)


def _diff_kernel(q_ref, k_ref, v_ref, par_ref, g_ref, o_ref, *, n_ctx):
    i = pl.program_id(1)

    def run(nk):
        k = k_ref[0:nk, :]
        v = v_ref[0:nk, :]
        q = q_ref[...]
        tq = q.shape[0]
        lam = par_ref[0:1, 0:1]
        comp = _lane_group(q.shape, DIFF_DH)
        head = _lane_group(q.shape, DIFF_DV)
        acc = jnp.zeros(q.shape, F32)

        def scores(hh):
            ql = jnp.concatenate([jnp.where(comp == 2 * hh, q, 0), jnp.where(comp == 2 * hh + 1, q, 0)], axis=0)
            return _dot_nt(ql, k)

        s_next = scores(0)
        for hh in range(DIFF_HEADS):
            s = s_next
            if hh + 1 < DIFF_HEADS:
                s_next = scores(hh + 1)
            p, l = _softmax_rows(s)
            inv = 1.0 / l
            pc = (p[0:tq] * inv[0:tq] - p[tq:2 * tq] * (lam * inv[tq:2 * tq])).astype(BF16)
            o = jnp.dot(pc, v, preferred_element_type=F32)
            acc = jnp.where(head == hh, o, acc)
        y = acc * lax.rsqrt(_head_mean_sq(acc, DIFF_HEADS, DIFF_DV) + EPS) * g_ref[...]
        o_ref[...] = (y * par_ref[1:2, 0:1]).astype(o_ref.dtype)

    @pl.when(i == 0)
    def _():
        run(n_ctx)

    @pl.when(i > 0)
    def _():
        run(k_ref.shape[0])


def _diff_attention(pb, par, subln, n_ctx):
    b, s, _ = pb.shape
    tq = ROW_TILE
    return pl.pallas_call(
        functools.partial(_diff_kernel, n_ctx=n_ctx),
        grid=(b, s // tq),
        in_specs=[pl.BlockSpec((None, tq, 256), lambda bi, i: (bi, i, 0)),
                  pl.BlockSpec((None, s, 256), lambda bi, i: (bi, 0, 1)),
                  pl.BlockSpec((None, s, 256), lambda bi, i: (bi, 0, 2)),
                  pl.BlockSpec((8, 128), lambda bi, i: (0, 0)),
                  pl.BlockSpec((1, 256), lambda bi, i: (0, 0))],
        out_specs=pl.BlockSpec((None, tq, 256), lambda bi, i: (bi, i, 0)),
        out_shape=jax.ShapeDtypeStruct((b, s, 256), BF16),
        compiler_params=_cparams(("arbitrary", "arbitrary")),
        name="diff_attention",
    )(pb, pb, pb, par, subln)


def _na_kernel(q_ref, k_ref, v_ref, bias_ref, o_ref, *, n_ctx, n_rows):
    i = pl.program_id(1)
    tq = q_ref.shape[0]

    def stack_heads(q):
        head = _lane_group(q.shape, NA_DH)
        return jnp.concatenate([jnp.where(head == hh, q, 0) for hh in range(NA_HEADS)], axis=0)

    def unstack_heads(o, rows):
        head = _lane_group((rows, 256), NA_DH)
        out = jnp.zeros((rows, 256), F32)
        for hh in range(NA_HEADS):
            out = jnp.where(head == hh, o[hh * rows:(hh + 1) * rows], out)
        return out

    @pl.when(i == 0)
    def _():
        kc = k_ref[0:n_ctx, :]
        vc = v_ref[0:n_ctx, :]
        p, l = _softmax_rows(_dot_nt(stack_heads(q_ref[...]), kc))
        o = jnp.dot(p.astype(BF16), vc, preferred_element_type=F32) / l
        o_ref[...] = unstack_heads(o, tq).astype(o_ref.dtype)

    @pl.when(i > 0)
    def _():
        kc = k_ref[0:n_ctx, :]
        vc = v_ref[0:n_ctx, :]
        def scores(rr):
            r = (i - 1) * (tq // GRID_W) + rr
            rs = jnp.clip(r - NA_KH // 2, 0, n_rows - NA_KH)
            start = pl.multiple_of(n_ctx + rs * GRID_W, GRID_W)
            kcat = jnp.concatenate([kc, k_ref[pl.ds(start, NA_KH * GRID_W), :]], axis=0)
            q = q_ref[rr * GRID_W:(rr + 1) * GRID_W, :]
            return _dot_nt(stack_heads(q), kcat) + bias_ref[r - rs], start

        nxt = scores(0)
        for rr in range(tq // GRID_W):
            s, start = nxt
            if rr + 1 < tq // GRID_W:
                nxt = scores(rr + 1)
            vcat = jnp.concatenate([vc, v_ref[pl.ds(start, NA_KH * GRID_W), :]], axis=0)
            p, l = _softmax_rows(s)
            o = jnp.dot(p.astype(BF16), vcat, preferred_element_type=F32) / l
            o_ref[rr * GRID_W:(rr + 1) * GRID_W, :] = unstack_heads(o, GRID_W).astype(o_ref.dtype)


def _na_attention(pb, bias, n_ctx):
    b, s, _ = pb.shape
    tq = ROW_TILE
    n_rows = (s - n_ctx) // GRID_W
    return pl.pallas_call(
        functools.partial(_na_kernel, n_ctx=n_ctx, n_rows=n_rows),
        grid=(b, s // tq),
        in_specs=[pl.BlockSpec((None, tq, 256), lambda bi, i: (bi, i, 0)),
                  pl.BlockSpec((None, s, 256), lambda bi, i: (bi, 0, 1)),
                  pl.BlockSpec((None, s, 256), lambda bi, i: (bi, 0, 2)),
                  pl.BlockSpec(bias.shape, lambda bi, i: (0, 0, 0))],
        out_specs=pl.BlockSpec((None, tq, 256), lambda bi, i: (bi, i, 0)),
        out_shape=jax.ShapeDtypeStruct((b, s, 256), BF16),
        compiler_params=_cparams(("arbitrary", "arbitrary")),
        name="na_attention",
    )(pb, pb, pb, bias)


def _na_bias_table(rpb, n_ctx):
    depth = rpb.shape[0]
    pad = GRID_W - NA_KW
    padded = jnp.pad(rpb.astype(F32), ((0, 0), (0, 0), (0, 0), (pad, pad)))
    cols = jnp.stack([padded[..., GRID_W - 1 - w:2 * GRID_W - 1 - w] for w in range(GRID_W)], axis=3)
    vals = jnp.stack([cols[:, :, NA_KH - 1 - o:2 * NA_KH - 1 - o] for o in range(NA_KH)], axis=2)
    vals = jnp.transpose(vals, (0, 2, 1, 4, 3, 5))
    w = np.arange(GRID_W)[:, None, None]
    kc = np.arange(GRID_W)[None, None, :]
    cs = np.clip(w - NA_KW // 2, 0, GRID_W - NA_KW)
    inwin = np.broadcast_to((kc >= cs) & (kc < cs + NA_KW), (GRID_W, NA_KH, GRID_W))
    vals = jnp.where(jnp.asarray(inwin), vals * LOG2E, NEG).reshape(depth, NA_KH, NA_HEADS * GRID_W, NA_KH * GRID_W)
    return jnp.concatenate([jnp.zeros((depth, NA_KH, NA_HEADS * GRID_W, n_ctx), F32), vals], axis=-1)


def _hy_pre_kernel(u_ref, w_ref, b_ref, x0_ref, z_ref, *, bounds, tm):
    s = u_ref.shape[0]
    w0, w1, w2, bias = w_ref[0:1, :], w_ref[1:2, :], w_ref[2:3, :], b_ref[...]
    rid = lax.broadcasted_iota(I32, (tm, u_ref.shape[1]), 0)
    for c0 in range(0, s, tm):
        u = u_ref[c0:c0 + tm, :]
        prev = pltpu.roll(u, 1, 0)
        nxt = pltpu.roll(u, tm - 1, 0)
        first = jnp.zeros_like(w0) if c0 in bounds else u_ref[c0 - 1:c0, :]
        last = jnp.zeros_like(w0) if (c0 + tm) in bounds else u_ref[c0 + tm:c0 + tm + 1, :]
        prev = jnp.where(rid == 0, first, prev)
        nxt = jnp.where(rid == tm - 1, last, nxt)
        uc = prev * w0 + u * w1 + nxt * w2 + bias
        x0_ref[c0:c0 + tm, :] = uc[:, 0:HY_W]
        z_ref[c0:c0 + tm, :] = uc[:, HY_W:2 * HY_W] * uc[:, 2 * HY_W:3 * HY_W]


def _hy_pre(mix, w_short, b_short, n_ctx):
    b, s, _ = mix.shape
    return pl.pallas_call(
        functools.partial(_hy_pre_kernel, bounds=(0, n_ctx, s), tm=ROW_TILE),
        grid=(b,),
        in_specs=[pl.BlockSpec((None, s, 3 * HY_W), lambda bi: (bi, 0, 0)),
                  pl.BlockSpec((3, 3 * HY_W), lambda bi: (0, 0)),
                  pl.BlockSpec((1, 3 * HY_W), lambda bi: (0, 0))],
        out_specs=[pl.BlockSpec((None, s, HY_W), lambda bi: (bi, 0, 0)),
                   pl.BlockSpec((None, s, HY_W), lambda bi: (bi, 0, 0))],
        out_shape=[jax.ShapeDtypeStruct((b, s, HY_W), F32), jax.ShapeDtypeStruct((b, s, HY_W), F32)],
        compiler_params=_cparams(("arbitrary",)),
        name="hyena_pre",
    )(mix, w_short, b_short)


def _hy_filter_kernel(z_ref, w1_ref, b1_ref, w2_ref, b2_ref, w3_ref, b3_ref, w4_ref, f_ref, dec_ref, o_ref):
    def lin(a, w_ref, b_ref):
        return jnp.dot(a, w_ref[...], preferred_element_type=F32, precision=HIGHEST) + b_ref[...]

    hh = jnp.sin(f_ref[0:1, :] * lin(z_ref[...], w1_ref, b1_ref))
    hh = jnp.sin(f_ref[1:2, :] * lin(hh, w2_ref, b2_ref))
    hh = jnp.sin(f_ref[2:3, :] * lin(hh, w3_ref, b3_ref))
    hh = jnp.dot(hh, w4_ref[...], preferred_element_type=F32, precision=HIGHEST) * dec_ref[...]
    o_ref[...] = hh / jnp.sum(jnp.abs(hh), axis=0, keepdims=True)


def _pad_to(a, shape):
    return jnp.pad(a, [(0, t - s) for s, t in zip(a.shape, shape)])


def _hy_filters(n, w1, b1, w2, b2, w3, b3, w4, freq):
    depth = w1.shape[0]
    t = np.linspace(0.0, 1.0, n, dtype=np.float32)[:, None]
    w = np.float32(2.0 * math.pi / n) * np.arange(n, dtype=np.float32)[:, None]
    bands = np.linspace(1e-4, HY_BANDS - 1, HY_BANDS, dtype=np.float32)[None, :]
    z = np.concatenate([t, np.cos(w * bands), np.sin(w * bands)], axis=-1).astype(np.float32)
    z = np.pad(z, ((0, 0), (0, LANE - z.shape[1])))
    deltas = np.linspace(math.log(HY_TARGET) / HY_SLOW_DECAY, math.log(HY_TARGET) / HY_FAST_DECAY, HY_W, dtype=np.float32)
    deltas = np.tile(np.abs(deltas), 2)
    decay = np.exp(-t * deltas[None, :]).astype(np.float32)
    p = LANE
    args = (jnp.asarray(z), _pad_to(w1, (depth, p, p)), _pad_to(b1[:, None, :], (depth, 1, p)),
            _pad_to(w2, (depth, p, p)), _pad_to(b2[:, None, :], (depth, 1, p)),
            _pad_to(w3, (depth, p, p)), _pad_to(b3[:, None, :], (depth, 1, p)),
            _pad_to(w4, (depth, p, 2 * HY_W)), _pad_to(freq, (depth, 8, p)), jnp.asarray(decay))
    per_layer = lambda shp: pl.BlockSpec((None,) + shp, lambda l: (l,) + (0,) * len(shp))
    const = lambda shp: pl.BlockSpec(shp, lambda l: (0,) * len(shp))
    return pl.pallas_call(
        _hy_filter_kernel,
        grid=(depth,),
        in_specs=[const((n, p)), per_layer((p, p)), per_layer((1, p)), per_layer((p, p)), per_layer((1, p)),
                  per_layer((p, p)), per_layer((1, p)), per_layer((p, 2 * HY_W)), per_layer((8, p)),
                  const((n, 2 * HY_W))],
        out_specs=per_layer((n, 2 * HY_W)),
        out_shape=jax.ShapeDtypeStruct((depth, n, 2 * HY_W), F32),
        compiler_params=_cparams(("arbitrary",)),
        name="hyena_filters",
    )(*args)


def _conv_taps(filt):
    hf, hb = filt[..., :HY_W], filt[..., HY_W:]
    taps = jnp.concatenate([hb[:, :0:-1], (hf[:, 0:1] + hb[:, 0:1]), hf[:, 1:], jnp.zeros_like(hf[:, 0:1])], axis=1)
    return jnp.transpose(taps, (0, 2, 1))


def _hy_conv_kernel(z_ref, k_ref, o_ref, *, nblk, nb):
    cg = z_ref.shape[0]
    n2 = k_ref.shape[1]
    ncols = (2 * nblk - 1) * LANE

    def body(ci, _):
        krow = k_ref[pl.ds(ci, 1), :]
        kb = jnp.broadcast_to(krow, (LANE, n2))
        big = pltpu.roll(kb, n2 - (LANE - 1), 1, stride=1, stride_axis=0)[:, :ncols].astype(BF16)
        o_ref[ci] = jnp.zeros(o_ref.shape[1:], F32)
        for m in range(-(nblk - 1), nblk):
            km = big[:, (m + nblk - 1) * LANE:(m + nblk) * LANE]
            cnt = (nblk - abs(m)) * nb
            src = 0 if m >= 0 else -m * nb
            dst = m * nb if m >= 0 else 0
            zin = z_ref[ci, src:src + cnt, :]
            o_ref[ci, dst:dst + cnt, :] += jnp.dot(zin, km, preferred_element_type=F32)
        return 0

    lax.fori_loop(0, cg, body, 0)


def _hy_conv(zr, taps, nblk, nb):
    c, r, _ = zr.shape
    cg = 8
    return pl.pallas_call(
        functools.partial(_hy_conv_kernel, nblk=nblk, nb=nb),
        grid=(c // cg,),
        in_specs=[pl.BlockSpec((cg, r, LANE), lambda i: (i, 0, 0)),
                  pl.BlockSpec((cg, taps.shape[1]), lambda i: (i, 0))],
        out_specs=pl.BlockSpec((cg, r, LANE), lambda i: (i, 0, 0)),
        out_shape=jax.ShapeDtypeStruct((c, r, LANE), F32),
        compiler_params=_cparams(("arbitrary",)),
        name="hyena_conv",
    )(zr, taps)


def _hy_long_conv(z, taps):
    b, n, c = z.shape
    nblk = n // LANE
    zr = jnp.transpose(z.astype(BF16).reshape(b, nblk, LANE, c), (3, 1, 0, 2)).reshape(c, nblk * b, LANE)
    y = _hy_conv(zr, taps, nblk, b)
    return jnp.transpose(y.reshape(c, nblk, b, LANE), (2, 1, 3, 0)).reshape(b, n, c)


def _merge_kernel(h_ref, mod_ref, ya_ref, x0_ref, z_ref, yconv_ref, hb_ref, yc_ref, yd_ref, gate_ref,
                  wb_ref, wo_ref, g2_ref, wq_ref, o_ref, q_ref, nb_ref, *, n_ctx, tm):
    i = pl.program_id(1)
    yb = (x0_ref[...] * (yconv_ref[...] + z_ref[...] * hb_ref[...])).astype(BF16)
    ys = (ya_ref[...], yb, yc_ref[...], yd_ref[...])
    acc = jnp.zeros(h_ref.shape, F32)
    d = h_ref.shape[1]
    for k in range(N_BRANCH):
        proj = jnp.dot(ys[k], wb_ref[k], preferred_element_type=F32)
        acc = acc + jax.nn.sigmoid(gate_ref[:, k * d:(k + 1) * d].astype(F32)) * proj
    out = jnp.dot(acc.astype(BF16), wo_ref[...], preferred_element_type=F32)
    row = i * tm + lax.broadcasted_iota(I32, out.shape, 0)
    gate = jnp.where(row < n_ctx, mod_ref[0, 2:3, :], mod_ref[1, 2:3, :])
    hn = h_ref[...] + gate * out
    o_ref[...] = hn
    nb = _norm_modulate(hn, mod_ref, g2_ref, 3, i * tm, n_ctx).astype(BF16)
    nb_ref[...] = nb
    q_ref[...] = jnp.dot(nb, wq_ref[...], preferred_element_type=F32).astype(q_ref.dtype)


def _merge(h, modl, ya, x0, z, yconv, hbias, yc, yd, gates, wb, wo, g2, wq, n_ctx):
    b, s, d = h.shape
    tm = ROW_TILE
    row = lambda w: pl.BlockSpec((None, tm, w), lambda bi, i: (bi, i, 0))
    return pl.pallas_call(
        functools.partial(_merge_kernel, n_ctx=n_ctx, tm=tm),
        grid=(b, s // tm),
        in_specs=[row(d), pl.BlockSpec((None, 2, 8, d), lambda bi, i: (bi, 0, 0, 0)),
                  row(256), row(256), row(256), row(256), pl.BlockSpec((1, 256), lambda bi, i: (0, 0)),
                  row(256), row(256), row(GATE_W),
                  pl.BlockSpec(wb.shape, lambda bi, i: (0, 0, 0)),
                  pl.BlockSpec(wo.shape, lambda bi, i: (0, 0)),
                  pl.BlockSpec((1, d), lambda bi, i: (0, 0)),
                  pl.BlockSpec(wq.shape, lambda bi, i: (0, 0))],
        out_specs=[row(d), row(wq.shape[1]), row(d)],
        out_shape=[jax.ShapeDtypeStruct((b, s, d), F32), jax.ShapeDtypeStruct((b, s, wq.shape[1]), BF16),
                   jax.ShapeDtypeStruct((b, s, d), BF16)],
        compiler_params=_cparams(("arbitrary", "arbitrary")),
        name="merge",
    )(h, modl, ya, x0, z, yconv, hbias, yc, yd, gates, wb, wo, g2, wq)


def _topk_rows(s, label, k):
    vals, labs = [], []
    for _ in range(k):
        m = jnp.max(s, axis=0, keepdims=True)
        lb = jnp.min(jnp.where(s == m, label, float(2 ** 20)), axis=0, keepdims=True)
        vals.append(m)
        labs.append(lb)
        s = jnp.where(label == lb, -jnp.inf, s)
    return jnp.concatenate(vals, axis=0), jnp.concatenate(labs, axis=0).astype(I32)


def _topk_sorted_columns(s, k):
    n = s.shape[0] // 8
    sub = lax.broadcasted_iota(I32, (8, s.shape[1]), 0)
    vals = [s[8 * v:8 * v + 8, :] for v in range(n)]
    rows = [(sub + 8 * v).astype(F32) for v in range(n)]
    for p in range(n):
        for v in range(p % 2, n - 1, 2):
            swap = vals[v + 1] > vals[v]
            vals[v], vals[v + 1] = jnp.where(swap, vals[v + 1], vals[v]), jnp.where(swap, vals[v], vals[v + 1])
            rows[v], rows[v + 1] = jnp.where(swap, rows[v + 1], rows[v]), jnp.where(swap, rows[v], rows[v + 1])
    out_v, out_r = [], []
    for r in range(k):
        m = jnp.max(vals[0], axis=0, keepdims=True)
        best = jnp.min(jnp.where(vals[0] == m, rows[0], float(2 ** 20)), axis=0, keepdims=True)
        out_v.append(m)
        out_r.append(best)
        if r + 1 < k:
            pop = rows[0] == best
            for v in range(min(n - 1, k - 1 - r)):
                vals[v] = jnp.where(pop, vals[v + 1], vals[v])
                rows[v] = jnp.where(pop, rows[v + 1], rows[v])
            if n - 1 < k - 1 - r:
                vals[n - 1] = jnp.where(pop, -jnp.inf, vals[n - 1])
    return jnp.concatenate(out_v, axis=0), jnp.concatenate(out_r, axis=0).astype(I32)


_CAND_FIXED_A = ((0, 0), (0, 8), (1, 0), (2, 0), (3, 0))
_CAND_FIXED_B = ((0, 0), (0, 8), (1, 0), (2, 0))


def _select_rows(table, pos):
    out = jnp.zeros(pos.shape, table.dtype)
    for a in range(table.shape[0]):
        out = jnp.where(pos == a, table[a:a + 1, :], out)
    return out


def _peer_route_kernel(q_ref, keys_ref, i_ref, j_ref, g_ref):
    st = _dot_nt(keys_ref[...], q_ref[...].astype(BF16))
    t = st.shape[1]
    sv1, si1 = _topk_sorted_columns(st[0:PEER_NKEYS], PEER_TOPK)
    sv2, si2 = _topk_sorted_columns(st[PEER_NKEYS:2 * PEER_NKEYS], PEER_TOPK)
    r8 = lax.broadcasted_iota(I32, (8, t), 0)
    cands, labels = [], []
    for a, b0 in _CAND_FIXED_A:
        cands.append(sv1[a:a + 1, :] + sv2[b0:b0 + 8, :])
        labels.append((a * PEER_TOPK + b0 + r8).astype(F32))
    for b, a0 in _CAND_FIXED_B:
        c = sv1[a0:a0 + 8, :] + sv2[b:b + 1, :]
        cands.append(jnp.where(r8 < 4, -jnp.inf, c) if a0 == 0 else c)
        labels.append(((a0 + r8) * PEER_TOPK + b).astype(F32))
    best, pos = _topk_rows(jnp.concatenate(cands, axis=0), jnp.concatenate(labels, axis=0), PEER_TOPK)
    i_ref[...] = _select_rows(si1, pos >> int(math.log2(PEER_TOPK)))
    j_ref[...] = _select_rows(si2, pos & (PEER_TOPK - 1))
    e = jnp.exp(best - jnp.max(best, axis=0, keepdims=True))
    g_ref[...] = e / jnp.sum(e, axis=0, keepdims=True)


def _peer_route(q, keys_blk):
    t, _ = q.shape
    tt = 1024
    assert t % tt == 0
    out = pl.BlockSpec((None, PEER_TOPK, tt), lambda ti, hh: (hh, 0, ti))
    shp = (PEER_HEADS, PEER_TOPK, t)
    return pl.pallas_call(
        _peer_route_kernel,
        grid=(t // tt, PEER_HEADS),
        in_specs=[pl.BlockSpec((tt, PEER_DK), lambda ti, hh: (ti, hh)),
                  pl.BlockSpec((None, 2 * PEER_NKEYS, PEER_DK), lambda ti, hh: (hh, 0, 0))],
        out_specs=[out, out, out],
        out_shape=[jax.ShapeDtypeStruct(shp, I32), jax.ShapeDtypeStruct(shp, I32), jax.ShapeDtypeStruct(shp, F32)],
        compiler_params=_cparams(("arbitrary", "arbitrary")),
        name="peer_route",
    )(q, keys_blk)


GS_HALF = PEER_NKEYS // 2
GS_PITCH = GS_HALF + 1
GS_UNROLL = 64
PEER_TM = 576
PEER_TE = 2048
U32 = jnp.uint32


def _peer_expert_kernel(h_ref, mod_ref, x_ref, i_ref, j_ref, g_ref, ut_ref, v_ref, o_ref, gs_ref, acc_ref,
                        *, n_ctx, tm):
    ti, e = pl.program_id(1), pl.program_id(2)
    n_e = pl.num_programs(2)
    te = ut_ref.shape[1]
    nk = PEER_NKEYS

    @pl.when((pl.program_id(0) == 0) & (ti == 0) & (e == 0))
    def _():
        acc_ref[...] = jnp.zeros(acc_ref.shape, F32)

    @pl.when(e == 0)
    def _():
        sub = lax.broadcasted_iota(I32, (nk, nk), 0)

        def build(t):
            irow = i_ref[pl.ds(t, 1), :]
            jrow = j_ref[pl.ds(t, 1), :]
            grow = g_ref[pl.ds(t, 1), :]
            at = jnp.where(sub == irow, 1.0, 0.0).astype(BF16)
            bt = jnp.where(sub == jrow, grow, 0.0).astype(BF16)
            gt = _dot_nt(at, bt).astype(BF16).astype(F32)
            lo = pltpu.bitcast(gt[0:GS_HALF], U32) >> 16
            hi = pltpu.bitcast(gt[GS_HALF:nk], U32)
            gs_ref[pl.ds(t * GS_PITCH, GS_HALF), :] = hi | lo

        def body(tb, _):
            for u in range(GS_UNROLL):
                build(tb * GS_UNROLL + u)
            return 0

        lax.fori_loop(0, tm // GS_UNROLL, body, 0)

    nb = te // nk // 2

    def gelu(a):
        return 0.5 * a * (1.0 + lax.erf(a * (2.0 ** -0.5)))

    hid = jnp.dot(x_ref[...], ut_ref[...], preferred_element_type=F32)
    first, second = [], []
    for k in range(nb):
        word = gs_ref[pl.ds(e * nb + k, tm, stride=GS_PITCH), :]
        g_lo = pltpu.bitcast(word << 16, F32)
        g_hi = pltpu.bitcast(word & jnp.uint32(0xFFFF0000), F32)
        first.append((g_lo * gelu(hid[:, k * nk:(k + 1) * nk])).astype(BF16))
        second.append((g_hi * gelu(hid[:, (nb + k) * nk:(nb + k + 1) * nk])).astype(BF16))
    acc_ref[...] += jnp.dot(jnp.concatenate(first + second, axis=1), v_ref[...], preferred_element_type=F32)

    @pl.when(e == n_e - 1)
    def _():
        row = ti * tm + lax.broadcasted_iota(I32, acc_ref.shape, 0)
        gate = jnp.where(row < n_ctx, mod_ref[0, 5:6, :], mod_ref[1, 5:6, :])
        o_ref[...] = h_ref[...] + gate * acc_ref[...]
        acc_ref[...] = jnp.zeros(acc_ref.shape, F32)


def _peer_experts(h, modl, xb, i_idx, j_idx, g, ut, v, n_ctx):
    b, s, d = h.shape
    tm, te = PEER_TM, PEER_TE
    n_chunks = ut.shape[0]
    slots = PEER_HEADS * PEER_TOPK
    row = lambda w: pl.BlockSpec((None, tm, w), lambda bi, i, e: (bi, i, 0), pipeline_mode=pl.Buffered(1))
    return pl.pallas_call(
        functools.partial(_peer_expert_kernel, n_ctx=n_ctx, tm=tm),
        grid=(b, s // tm, n_chunks),
        in_specs=[row(d), pl.BlockSpec((None, 2, 8, d), lambda bi, i, e: (bi, 0, 0, 0)),
                  row(d), row(slots), row(slots), row(slots),
                  pl.BlockSpec((None, d, te), lambda bi, i, e: (e, 0, 0)),
                  pl.BlockSpec((te, d), lambda bi, i, e: (e, 0))],
        out_specs=pl.BlockSpec((None, tm, d), lambda bi, i, e: (bi, i, 0)),
        out_shape=jax.ShapeDtypeStruct((b, s, d), F32),
        scratch_shapes=[pltpu.VMEM((tm * GS_PITCH, PEER_NKEYS), U32), pltpu.VMEM((tm, d), F32)],
        compiler_params=_cparams(("arbitrary", "arbitrary", "arbitrary")),
        name="peer_experts",
    )(h, modl, xb, i_idx, j_idx, g, ut, v)


def _final_norm_kernel(x_ref, g_ref, o_ref):
    x = x_ref[...]
    o_ref[...] = x * lax.rsqrt(jnp.mean(x * x, axis=-1, keepdims=True) + EPS) * g_ref[...]


def _final_norm(h, g, n_ctx):
    b, s, d = h.shape
    tm = ROW_TILE
    n_lat = s - n_ctx
    off = n_ctx // tm
    return pl.pallas_call(
        _final_norm_kernel,
        grid=(b, n_lat // tm),
        in_specs=[pl.BlockSpec((None, tm, d), lambda bi, i: (bi, i + off, 0)),
                  pl.BlockSpec((1, d), lambda bi, i: (0, 0))],
        out_specs=pl.BlockSpec((None, tm, d), lambda bi, i: (bi, i, 0)),
        out_shape=jax.ShapeDtypeStruct((b, n_lat, d), F32),
        compiler_params=_cparams(("arbitrary", "arbitrary")),
        name="final_norm",
    )(h, g)


def _prep_w_in(w_in):
    depth, d, _ = w_in.shape
    mixw = w_in[:, :, :2816]
    gq = mixw[:, :, 1536:1792].reshape(depth, d, GQA_HEADS, GQA_DH)[:, :, (0, 2, 1, 3), :].reshape(depth, d, 256)
    mixw = jnp.concatenate([mixw[:, :, :1536], gq, mixw[:, :, 1792:], jnp.zeros((depth, d, MIX_W - 2816), w_in.dtype)], axis=-1)
    return jnp.concatenate([mixw, w_in[:, :, 2816:]], axis=-1).astype(BF16)


def _peer_chunk_order(w):
    n_exp, d = w.shape
    nb = PEER_TE // PEER_NKEYS // 2
    return w.reshape(2, GS_HALF // nb, nb, PEER_NKEYS, d).transpose(1, 0, 2, 3, 4).reshape(n_exp, d)


def _prep_peer_keys(keys):
    depth = keys.shape[0]
    half = PEER_DK // 2
    z = jnp.zeros((depth, PEER_HEADS, PEER_NKEYS, half), keys.dtype)
    top = jnp.concatenate([keys[:, :, 0], z], axis=-1)
    bot = jnp.concatenate([z, keys[:, :, 1]], axis=-1)
    return jnp.concatenate([top, bot], axis=2).astype(BF16)


def kernel(x, c, ctx, c_ctx, w_mod, b_mod, norm1_g, norm2_g, w_in, na_rpb, hy_short_w, hy_short_b, hy_w1, hy_b1, hy_w2, hy_b2, hy_w3, hy_b3, hy_w4, hy_freq, hy_bias, gqa_qn, gqa_kn, diff_lq1, diff_lk1, diff_lq2, diff_lk2, diff_subln, w_branch, w_out, peer_wq, peer_keys, peer_u, peer_v, final_g):
    B, L, D = x.shape
    C = ctx.shape[1]
    S = C + L
    depth = w_mod.shape[0]
    assert C == ROW_TILE and L % ROW_TILE == 0 and D == 1024 and L % GRID_W == 0

    h = jnp.concatenate([ctx, x], axis=1)

    r = -(-(B + 1) // 8) * 8
    cvec = jnp.zeros((r, D), F32).at[:B].set(c).at[B].set(c_ctx)
    modall = _modulation(cvec, w_mod, b_mod)
    mod_lat = modall[:, :B].reshape(depth, B, 1, 6, D)
    mod_ctx = jnp.broadcast_to(modall[:, B].reshape(depth, 1, 1, 6, D), (depth, B, 1, 6, D))
    mods = jnp.pad(jnp.concatenate([mod_ctx, mod_lat], axis=2), ((0, 0), (0, 0), (0, 0), (0, 2), (0, 0)))

    fargs = (hy_w1, hy_b1, hy_w2, hy_b2, hy_w3, hy_b3, hy_w4, hy_freq)
    taps_lat = _conv_taps(_hy_filters(L, *fargs))
    taps_ctx = _conv_taps(_hy_filters(C, *fargs))

    cg, sg = _rope_tables(C, L, GQA_DH, 256)
    cd, sd = _rope_tables(C, L, DIFF_DH, 256)

    lam_init = jnp.asarray([0.8 - 0.6 * math.exp(-0.3 * l) for l in range(depth)], F32)
    lam = (jnp.exp(jnp.sum(diff_lq1.astype(F32) * diff_lk1.astype(F32), axis=-1))
           - jnp.exp(jnp.sum(diff_lq2.astype(F32) * diff_lk2.astype(F32), axis=-1)) + lam_init)
    diff_par = jnp.zeros((depth, 8, LANE), F32).at[:, 0, :].set(lam[:, None]).at[:, 1, :].set(1.0 - lam_init[:, None])

    wb = w_branch.astype(BF16)
    wb = wb.at[:, 2].set(wb[:, 2].reshape(depth, GQA_HEADS, GQA_DH, D)[:, (0, 2, 1, 3)].reshape(depth, BRANCH_W, D))

    xs = dict(
        mods=mods, n1=norm1_g[:, None, :], n2=norm2_g[:, None, :], w_in=_prep_w_in(w_in),
        bias=_na_bias_table(na_rpb, C), sw=hy_short_w, sb=hy_short_b[:, None, :], hbias=hy_bias[:, None, :],
        taps_lat=taps_lat, taps_ctx=taps_ctx,
        qn=jnp.tile(gqa_qn, (1, GQA_HEADS))[:, None, :], kn=jnp.tile(gqa_kn, (1, GQA_KV))[:, None, :],
        diff_par=diff_par, subln=jnp.tile(diff_subln, (1, DIFF_HEADS))[:, None, :],
        wb=wb, wo=w_out.astype(BF16), wq=peer_wq.astype(BF16), keys=_prep_peer_keys(peer_keys),
        u=peer_u, v=peer_v,
    )

    def layer(h, p):
        pb_na, pb_gqa, pb_diff, hy, gates = _in_proj(h, p["mods"], p["n1"], p["w_in"], cg, sg, cd, sd,
                                                     p["qn"], p["kn"], C)
        ya = _na_attention(pb_na, p["bias"], C)
        yc = _gqa_attention(pb_gqa, C)
        yd = _diff_attention(pb_diff, p["diff_par"], p["subln"], C)
        x0, z = _hy_pre(hy, p["sw"], p["sb"], C)
        yconv = jnp.concatenate([_hy_long_conv(z[:, :C], p["taps_ctx"]), _hy_long_conv(z[:, C:], p["taps_lat"])], axis=1)
        h, q, xb = _merge(h, p["mods"], ya, x0, z, yconv, p["hbias"], yc, yd, gates, p["wb"], p["wo"],
                          p["n2"], p["wq"], C)
        i_idx, j_idx, g = _peer_route(q.reshape(B * S, -1), p["keys"])
        slots = PEER_HEADS * PEER_TOPK
        to_rows = lambda a: jnp.transpose(a, (2, 0, 1)).reshape(B, S, slots)
        ut = jnp.transpose(_peer_chunk_order(p["u"].astype(BF16)).reshape(-1, PEER_TE, D), (0, 2, 1))
        h = _peer_experts(h, p["mods"], xb, to_rows(i_idx), to_rows(j_idx), to_rows(g), ut,
                          _peer_chunk_order(p["v"].astype(BF16)), C)
        return h, None

    h, _ = lax.scan(layer, h, xs)
    return _final_norm(h, final_g[None, :], C)
```

```python
import functools
import math

import numpy as np
import jax
import jax.numpy as jnp
from jax import lax
from jax.experimental import pallas as pl
from jax.experimental.pallas import tpu as pltpu

F32 = jnp.float32
BF16 = jnp.bfloat16
I32 = jnp.int32
HIGHEST = lax.Precision.HIGHEST

EPS = 1e-6
GRID_W = 64
ROPE_THETA = 10000.0
NA_HEADS, NA_DH, NA_KH, NA_KW = 4, 64, 8, 16
HY_W, HY_BANDS, HY_FFN = 256, 16, 64
HY_FAST_DECAY, HY_SLOW_DECAY, HY_TARGET = 0.3, 1.5, 1e-2
GQA_HEADS, GQA_KV, GQA_DH = 4, 2, 64
DIFF_HEADS, DIFF_DH, DIFF_DV = 4, 32, 64
N_BRANCH, BRANCH_W = 4, 256
PEER_HEADS, PEER_NKEYS, PEER_DK, PEER_TOPK = 8, 128, 128, 16

LANE = 128
ROW_TILE = 256
VMEM_LIMIT = 56 * 1024 * 1024
NEG = -1e30
LOG2E = math.log2(math.e)
MIX_W = 3072
GATE_W = N_BRANCH * 1024
COL_TILE = 1024


def _cparams(sem):
    return pltpu.CompilerParams(dimension_semantics=sem, vmem_limit_bytes=VMEM_LIMIT)


def _mod_kernel(c_ref, w_ref, b_ref, o_ref):
    cv = c_ref[...]
    a = cv * jax.nn.sigmoid(cv)
    o_ref[...] = jnp.dot(a, w_ref[...], preferred_element_type=F32, precision=HIGHEST) + b_ref[...]


def _modulation(cvec, w_mod, b_mod):
    depth, d, n = w_mod.shape
    r = cvec.shape[0]
    tn = 1024
    return pl.pallas_call(
        _mod_kernel,
        grid=(depth, n // tn),
        in_specs=[pl.BlockSpec((r, d), lambda l, j: (0, 0)),
                  pl.BlockSpec((None, d, tn), lambda l, j: (l, 0, j)),
                  pl.BlockSpec((None, 1, tn), lambda l, j: (l, 0, j))],
        out_specs=pl.BlockSpec((None, r, tn), lambda l, j: (l, 0, j)),
        out_shape=jax.ShapeDtypeStruct((depth, r, n), F32),
        compiler_params=_cparams(("arbitrary", "arbitrary")),
        name="modulation",
    )(cvec, w_mod, b_mod.reshape(depth, 1, n))


def _norm_modulate(x, mod_ref, g_ref, srow, row0, n_ctx):
    y = x * lax.rsqrt(jnp.mean(x * x, axis=-1, keepdims=True) + EPS) * g_ref[...]
    row = row0 + lax.broadcasted_iota(I32, x.shape, 0)
    is_ctx = row < n_ctx
    shift = jnp.where(is_ctx, mod_ref[0, srow:srow + 1, :], mod_ref[1, srow:srow + 1, :])
    scale = jnp.where(is_ctx, mod_ref[0, srow + 1:srow + 2, :], mod_ref[1, srow + 1:srow + 2, :])
    return y * (1.0 + scale) + shift


def _in_proj_kernel(h_ref, mod_ref, g_ref, w_ref, cg_ref, sg_ref, cd_ref, sd_ref, qn_ref, kn_ref, bd_ref,
                    na_ref, gqa_ref, diff_ref, hy_ref, gate_ref, nb_ref, *, tm, n_ctx, n_mix):
    i, j = pl.program_id(1), pl.program_id(2)

    @pl.when(j == 0)
    def _():
        nb_ref[...] = _norm_modulate(h_ref[...], mod_ref, g_ref, 0, i * tm, n_ctx).astype(BF16)

    def proj():
        return jnp.dot(nb_ref[...], w_ref[...], preferred_element_type=F32)

    @pl.when(j == 0)
    def _():
        res = proj()
        na_ref[:, 0:256] = (res[:, 0:256] * (LOG2E * NA_DH ** -0.5)).astype(BF16)
        na_ref[:, 256:768] = res[:, 256:768].astype(BF16)
        hy_ref[:, 0:HY_W] = res[:, 768:1024]

    @pl.when(j == 1)
    def _():
        res = jnp.dot(nb_ref[...], w_ref[:, 512:1024], preferred_element_type=F32)
        hy_ref[:, HY_W:3 * HY_W] = jnp.dot(nb_ref[...], w_ref[:, 0:512], preferred_element_type=F32)
        cg, sg = cg_ref[...], sg_ref[...]
        gq, gk = res[:, 0:256], res[:, 256:384]
        bd = bd_ref[...]
        gq = gq * lax.rsqrt(_head_mean_sq_mxu(gq, GQA_DH, bd) + EPS) * qn_ref[...]
        gk = gk * lax.rsqrt(_head_mean_sq_mxu(gk, GQA_DH, bd[0:128, 0:128]) + EPS) * kn_ref[...]
        gqa_ref[:, 0:256] = (_rope(gq, cg, sg, GQA_DH // 4) * (LOG2E * GQA_DH ** -0.5)).astype(BF16)
        gqa_ref[:, 256:384] = _rope(gk, cg[:, 0:128], sg[:, 0:128], GQA_DH // 4).astype(BF16)
        gqa_ref[:, 384:512] = res[:, 384:512].astype(BF16)

    @pl.when(j == 2)
    def _():
        res = proj()
        cd, sd = cd_ref[...], sd_ref[...]
        diff_ref[:, 0:256] = (_rope(res[:, 0:256], cd, sd, DIFF_DH // 4) * (LOG2E * DIFF_DH ** -0.5)).astype(BF16)
        diff_ref[:, 256:512] = _rope(res[:, 256:512], cd, sd, DIFF_DH // 4).astype(BF16)
        diff_ref[:, 512:768] = res[:, 512:768].astype(BF16)

    @pl.when(j >= n_mix)
    def _():
        gate_ref[...] = proj().astype(gate_ref.dtype)


def _in_proj(h, modl, g, w, cg, sg, cd, sd, qn, kn, n_ctx):
    b, s, d = h.shape
    tm, tn = 1152, COL_TILE
    n_mix = MIX_W // tn
    n_tot = (MIX_W + GATE_W) // tn
    assert n_mix == 3
    const = lambda shp: pl.BlockSpec(shp, lambda bi, i, j: (0,) * len(shp))
    tab = pl.BlockSpec((tm, 256), lambda bi, i, j: (i, 0))
    rows = lambda wd: pl.BlockSpec((None, tm, wd), lambda bi, i, j: (bi, i, 0))
    head_ones = jnp.asarray(np.kron(np.eye(GQA_HEADS), np.ones((GQA_DH, GQA_DH))), BF16)
    return pl.pallas_call(
        functools.partial(_in_proj_kernel, tm=tm, n_ctx=n_ctx, n_mix=n_mix),
        grid=(b, s // tm, n_tot),
        in_specs=[rows(d), pl.BlockSpec((None, 2, 8, d), lambda bi, i, j: (bi, 0, 0, 0)), const((1, d)),
                  pl.BlockSpec((d, tn), lambda bi, i, j: (0, j)),
                  tab, tab, tab, tab, const((1, 256)), const((1, 128)), const((256, 256))],
        out_specs=[rows(768), rows(512), rows(768), rows(3 * HY_W),
                   pl.BlockSpec((None, tm, tn), lambda bi, i, j: (bi, i, jnp.maximum(j - n_mix, 0)))],
        out_shape=[jax.ShapeDtypeStruct((b, s, 768), BF16), jax.ShapeDtypeStruct((b, s, 512), BF16),
                   jax.ShapeDtypeStruct((b, s, 768), BF16), jax.ShapeDtypeStruct((b, s, 3 * HY_W), F32),
                   jax.ShapeDtypeStruct((b, s, GATE_W), BF16)],
        scratch_shapes=[pltpu.VMEM((tm, d), BF16)],
        compiler_params=_cparams(("arbitrary", "arbitrary", "arbitrary")),
        name="in_proj",
    )(h, modl, g, w, cg, sg, cd, sd, qn, kn, head_ones)


def _lane_group(shape, width):
    return lax.broadcasted_iota(I32, shape, 1) >> int(math.log2(width))


def _head_mean_sq(x, n_heads, dh):
    hid = _lane_group(x.shape, dh)
    x2 = x * x
    ms = jnp.zeros_like(x)
    for hh in range(n_heads):
        m = hid == hh
        s = jnp.sum(jnp.where(m, x2, 0.0), axis=-1, keepdims=True)
        ms = jnp.where(m, s, ms)
    return ms * (1.0 / dh)


def _head_mean_sq_mxu(x, dh, head_ones):
    x2 = x * x
    hi = x2.astype(BF16)
    lo = (x2 - hi.astype(F32)).astype(BF16)
    s = jnp.dot(hi, head_ones, preferred_element_type=F32) + jnp.dot(lo, head_ones, preferred_element_type=F32)
    return s * (1.0 / dh)


def _rope(x, cos, sin_signed, qs):
    w = x.shape[-1]
    lane = lax.broadcasted_iota(I32, x.shape, 1)
    lo = (lane & (2 * qs - 1)) < qs
    partner = jnp.where(lo, pltpu.roll(x, w - qs, 1), pltpu.roll(x, qs, 1))
    return x * cos + partner * sin_signed


def _rope_tables(n_ctx, n_lat, dh, width):
    qs = dh // 4
    t = np.arange(n_lat)
    rows, cols = (t // GRID_W).astype(np.float64), (t % GRID_W).astype(np.float64)
    lane = np.arange(width) % dh
    part = lane // (2 * qs)
    u = lane % (2 * qs)
    f = u % qs
    lo = u < qs
    freqs = ROPE_THETA ** (-f.astype(np.float64) / qs)
    pos = np.where(part[None, :] == 0, rows[:, None], cols[:, None])
    ang = (pos.astype(np.float32) * freqs.astype(np.float32)[None, :]).astype(np.float32)
    cos = np.cos(ang.astype(np.float64))
    sin = np.sin(ang.astype(np.float64)) * np.where(lo, -1.0, 1.0)[None, :]
    cos = np.concatenate([np.ones((n_ctx, width)), cos], axis=0)
    sin = np.concatenate([np.zeros((n_ctx, width)), sin], axis=0)
    return jnp.asarray(cos, F32), jnp.asarray(sin, F32)


def _softmax_rows(s):
    m = jnp.max(s, axis=-1, keepdims=True)
    p = jnp.exp2(s - m)
    return p, jnp.sum(p, axis=-1, keepdims=True)


def _dot_nt(a, b):
    return lax.dot_general(a, b, (((1,), (1,)), ((), ())), preferred_element_type=F32)


def _gqa_kernel(q_ref, k_ref, v_ref, o_ref, *, n_ctx):
    i = pl.program_id(1)

    def run(nk):
        k = k_ref[0:nk, :]
        v = v_ref[0:nk, :]
        q = q_ref[...]
        tq = q.shape[0]
        grp = _lane_group((tq, LANE), GQA_DH)

        def scores(g):
            mask = grp == g
            ql = jnp.concatenate([jnp.where(mask, q[:, 0:128], 0), jnp.where(mask, q[:, 128:256], 0)], axis=0)
            return _dot_nt(ql, k)

        outs = []
        s_next = scores(0)
        for g in range(GQA_KV):
            s = s_next
            if g + 1 < GQA_KV:
                s_next = scores(g + 1)
            p, l = _softmax_rows(s)
            outs.append(jnp.dot(p.astype(BF16), v, preferred_element_type=F32) / l)
        for half in range(2):
            sel = jnp.where(grp == 0, outs[0][half * tq:(half + 1) * tq], outs[1][half * tq:(half + 1) * tq])
            o_ref[:, half * 128:(half + 1) * 128] = sel.astype(o_ref.dtype)

    @pl.when(i == 0)
    def _():
        run(n_ctx)

    @pl.when(i > 0)
    def _():
        run(k_ref.shape[0])


def _gqa_attention(pb, n_ctx):
    b, s, _ = pb.shape
    tq = ROW_TILE
    return pl.pallas_call(
        functools.partial(_gqa_kernel, n_ctx=n_ctx),
        grid=(b, s // tq),
        in_specs=[pl.BlockSpec((None, tq, 256), lambda bi, i: (bi, i, 0)),
                  pl.BlockSpec((None, s, 128), lambda bi, i: (bi, 0, 2)),
                  pl.BlockSpec((None, s, 128), lambda bi, i: (bi, 0, 3))],
        out_specs=pl.BlockSpec((None, tq, 256), lambda bi, i: (bi, i, 0)),
        out_shape=jax.ShapeDtypeStruct((b, s, 256), BF16),
        compiler_params=_cparams(("arbitrary", "arbitrary")),
        name="gqa_attention",
    )(pb, pb, pb)


def _diff_kernel(q_ref, k_ref, v_ref, par_ref, g_ref, o_ref, *, n_ctx):
    i = pl.program_id(1)

    def run(nk):
        k = k_ref[0:nk, :]
        v = v_ref[0:nk, :]
        q = q_ref[...]
        tq = q.shape[0]
        lam = par_ref[0:1, 0:1]
        comp = _lane_group(q.shape, DIFF_DH)
        head = _lane_group(q.shape, DIFF_DV)
        acc = jnp.zeros(q.shape, F32)

        def scores(hh):
            ql = jnp.concatenate([jnp.where(comp == 2 * hh, q, 0), jnp.where(comp == 2 * hh + 1, q, 0)], axis=0)
            return _dot_nt(ql, k)

        s_next = scores(0)
        for hh in range(DIFF_HEADS):
            s = s_next
            if hh + 1 < DIFF_HEADS:
                s_next = scores(hh + 1)
            p, l = _softmax_rows(s)
            inv = 1.0 / l
            pc = (p[0:tq] * inv[0:tq] - p[tq:2 * tq] * (lam * inv[tq:2 * tq])).astype(BF16)
            o = jnp.dot(pc, v, preferred_element_type=F32)
            acc = jnp.where(head == hh, o, acc)
        y = acc * lax.rsqrt(_head_mean_sq(acc, DIFF_HEADS, DIFF_DV) + EPS) * g_ref[...]
        o_ref[...] = (y * par_ref[1:2, 0:1]).astype(o_ref.dtype)

    @pl.when(i == 0)
    def _():
        run(n_ctx)

    @pl.when(i > 0)
    def _():
        run(k_ref.shape[0])


def _diff_attention(pb, par, subln, n_ctx):
    b, s, _ = pb.shape
    tq = ROW_TILE
    return pl.pallas_call(
        functools.partial(_diff_kernel, n_ctx=n_ctx),
        grid=(b, s // tq),
        in_specs=[pl.BlockSpec((None, tq, 256), lambda bi, i: (bi, i, 0)),
                  pl.BlockSpec((None, s, 256), lambda bi, i: (bi, 0, 1)),
                  pl.BlockSpec((None, s, 256), lambda bi, i: (bi, 0, 2)),
                  pl.BlockSpec((8, 128), lambda bi, i: (0, 0)),
                  pl.BlockSpec((1, 256), lambda bi, i: (0, 0))],
        out_specs=pl.BlockSpec((None, tq, 256), lambda bi, i: (bi, i, 0)),
        out_shape=jax.ShapeDtypeStruct((b, s, 256), BF16),
        compiler_params=_cparams(("arbitrary", "arbitrary")),
        name="diff_attention",
    )(pb, pb, pb, par, subln)


def _na_kernel(q_ref, k_ref, v_ref, bias_ref, o_ref, *, n_ctx, n_rows):
    i = pl.program_id(1)
    tq = q_ref.shape[0]

    def stack_heads(q):
        head = _lane_group(q.shape, NA_DH)
        return jnp.concatenate([jnp.where(head == hh, q, 0) for hh in range(NA_HEADS)], axis=0)

    def unstack_heads(o, rows):
        head = _lane_group((rows, 256), NA_DH)
        out = jnp.zeros((rows, 256), F32)
        for hh in range(NA_HEADS):
            out = jnp.where(head == hh, o[hh * rows:(hh + 1) * rows], out)
        return out

    @pl.when(i == 0)
    def _():
        kc = k_ref[0:n_ctx, :]
        vc = v_ref[0:n_ctx, :]
        p, l = _softmax_rows(_dot_nt(stack_heads(q_ref[...]), kc))
        o = jnp.dot(p.astype(BF16), vc, preferred_element_type=F32) / l
        o_ref[...] = unstack_heads(o, tq).astype(o_ref.dtype)

    @pl.when(i > 0)
    def _():
        kc = k_ref[0:n_ctx, :]
        vc = v_ref[0:n_ctx, :]
        def scores(rr):
            r = (i - 1) * (tq // GRID_W) + rr
            rs = jnp.clip(r - NA_KH // 2, 0, n_rows - NA_KH)
            start = pl.multiple_of(n_ctx + rs * GRID_W, GRID_W)
            kcat = jnp.concatenate([kc, k_ref[pl.ds(start, NA_KH * GRID_W), :]], axis=0)
            q = q_ref[rr * GRID_W:(rr + 1) * GRID_W, :]
            return _dot_nt(stack_heads(q), kcat) + bias_ref[r - rs], start

        nxt = scores(0)
        for rr in range(tq // GRID_W):
            s, start = nxt
            if rr + 1 < tq // GRID_W:
                nxt = scores(rr + 1)
            vcat = jnp.concatenate([vc, v_ref[pl.ds(start, NA_KH * GRID_W), :]], axis=0)
            p, l = _softmax_rows(s)
            o = jnp.dot(p.astype(BF16), vcat, preferred_element_type=F32) / l
            o_ref[rr * GRID_W:(rr + 1) * GRID_W, :] = unstack_heads(o, GRID_W).astype(o_ref.dtype)


def _na_attention(pb, bias, n_ctx):
    b, s, _ = pb.shape
    tq = ROW_TILE
    n_rows = (s - n_ctx) // GRID_W
    return pl.pallas_call(
        functools.partial(_na_kernel, n_ctx=n_ctx, n_rows=n_rows),
        grid=(b, s // tq),
        in_specs=[pl.BlockSpec((None, tq, 256), lambda bi, i: (bi, i, 0)),
                  pl.BlockSpec((None, s, 256), lambda bi, i: (bi, 0, 1)),
                  pl.BlockSpec((None, s, 256), lambda bi, i: (bi, 0, 2)),
                  pl.BlockSpec(bias.shape, lambda bi, i: (0, 0, 0))],
        out_specs=pl.BlockSpec((None, tq, 256), lambda bi, i: (bi, i, 0)),
        out_shape=jax.ShapeDtypeStruct((b, s, 256), BF16),
        compiler_params=_cparams(("arbitrary", "arbitrary")),
        name="na_attention",
    )(pb, pb, pb, bias)


def _na_bias_table(rpb, n_ctx):
    depth = rpb.shape[0]
    pad = GRID_W - NA_KW
    padded = jnp.pad(rpb.astype(F32), ((0, 0), (0, 0), (0, 0), (pad, pad)))
    cols = jnp.stack([padded[..., GRID_W - 1 - w:2 * GRID_W - 1 - w] for w in range(GRID_W)], axis=3)
    vals = jnp.stack([cols[:, :, NA_KH - 1 - o:2 * NA_KH - 1 - o] for o in range(NA_KH)], axis=2)
    vals = jnp.transpose(vals, (0, 2, 1, 4, 3, 5))
    w = np.arange(GRID_W)[:, None, None]
    kc = np.arange(GRID_W)[None, None, :]
    cs = np.clip(w - NA_KW // 2, 0, GRID_W - NA_KW)
    inwin = np.broadcast_to((kc >= cs) & (kc < cs + NA_KW), (GRID_W, NA_KH, GRID_W))
    vals = jnp.where(jnp.asarray(inwin), vals * LOG2E, NEG).reshape(depth, NA_KH, NA_HEADS * GRID_W, NA_KH * GRID_W)
    return jnp.concatenate([jnp.zeros((depth, NA_KH, NA_HEADS * GRID_W, n_ctx), F32), vals], axis=-1)


def _hy_pre_kernel(u_ref, w_ref, b_ref, x0_ref, z_ref, *, bounds, tm):
    s = u_ref.shape[0]
    w0, w1, w2, bias = w_ref[0:1, :], w_ref[1:2, :], w_ref[2:3, :], b_ref[...]
    rid = lax.broadcasted_iota(I32, (tm, u_ref.shape[1]), 0)
    for c0 in range(0, s, tm):
        u = u_ref[c0:c0 + tm, :]
        prev = pltpu.roll(u, 1, 0)
        nxt = pltpu.roll(u, tm - 1, 0)
        first = jnp.zeros_like(w0) if c0 in bounds else u_ref[c0 - 1:c0, :]
        last = jnp.zeros_like(w0) if (c0 + tm) in bounds else u_ref[c0 + tm:c0 + tm + 1, :]
        prev = jnp.where(rid == 0, first, prev)
        nxt = jnp.where(rid == tm - 1, last, nxt)
        uc = prev * w0 + u * w1 + nxt * w2 + bias
        x0_ref[c0:c0 + tm, :] = uc[:, 0:HY_W]
        z_ref[c0:c0 + tm, :] = uc[:, HY_W:2 * HY_W] * uc[:, 2 * HY_W:3 * HY_W]


def _hy_pre(mix, w_short, b_short, n_ctx):
    b, s, _ = mix.shape
    return pl.pallas_call(
        functools.partial(_hy_pre_kernel, bounds=(0, n_ctx, s), tm=ROW_TILE),
        grid=(b,),
        in_specs=[pl.BlockSpec((None, s, 3 * HY_W), lambda bi: (bi, 0, 0)),
                  pl.BlockSpec((3, 3 * HY_W), lambda bi: (0, 0)),
                  pl.BlockSpec((1, 3 * HY_W), lambda bi: (0, 0))],
        out_specs=[pl.BlockSpec((None, s, HY_W), lambda bi: (bi, 0, 0)),
                   pl.BlockSpec((None, s, HY_W), lambda bi: (bi, 0, 0))],
        out_shape=[jax.ShapeDtypeStruct((b, s, HY_W), F32), jax.ShapeDtypeStruct((b, s, HY_W), F32)],
        compiler_params=_cparams(("arbitrary",)),
        name="hyena_pre",
    )(mix, w_short, b_short)


def _hy_filter_kernel(z_ref, w1_ref, b1_ref, w2_ref, b2_ref, w3_ref, b3_ref, w4_ref, f_ref, dec_ref, o_ref):
    def lin(a, w_ref, b_ref):
        return jnp.dot(a, w_ref[...], preferred_element_type=F32, precision=HIGHEST) + b_ref[...]

    hh = jnp.sin(f_ref[0:1, :] * lin(z_ref[...], w1_ref, b1_ref))
    hh = jnp.sin(f_ref[1:2, :] * lin(hh, w2_ref, b2_ref))
    hh = jnp.sin(f_ref[2:3, :] * lin(hh, w3_ref, b3_ref))
    hh = jnp.dot(hh, w4_ref[...], preferred_element_type=F32, precision=HIGHEST) * dec_ref[...]
    o_ref[...] = hh / jnp.sum(jnp.abs(hh), axis=0, keepdims=True)


def _pad_to(a, shape):
    return jnp.pad(a, [(0, t - s) for s, t in zip(a.shape, shape)])


def _hy_filters(n, w1, b1, w2, b2, w3, b3, w4, freq):
    depth = w1.shape[0]
    t = np.linspace(0.0, 1.0, n, dtype=np.float32)[:, None]
    w = np.float32(2.0 * math.pi / n) * np.arange(n, dtype=np.float32)[:, None]
    bands = np.linspace(1e-4, HY_BANDS - 1, HY_BANDS, dtype=np.float32)[None, :]
    z = np.concatenate([t, np.cos(w * bands), np.sin(w * bands)], axis=-1).astype(np.float32)
    z = np.pad(z, ((0, 0), (0, LANE - z.shape[1])))
    deltas = np.linspace(math.log(HY_TARGET) / HY_SLOW_DECAY, math.log(HY_TARGET) / HY_FAST_DECAY, HY_W, dtype=np.float32)
    deltas = np.tile(np.abs(deltas), 2)
    decay = np.exp(-t * deltas[None, :]).astype(np.float32)
    p = LANE
    args = (jnp.asarray(z), _pad_to(w1, (depth, p, p)), _pad_to(b1[:, None, :], (depth, 1, p)),
            _pad_to(w2, (depth, p, p)), _pad_to(b2[:, None, :], (depth, 1, p)),
            _pad_to(w3, (depth, p, p)), _pad_to(b3[:, None, :], (depth, 1, p)),
            _pad_to(w4, (depth, p, 2 * HY_W)), _pad_to(freq, (depth, 8, p)), jnp.asarray(decay))
    per_layer = lambda shp: pl.BlockSpec((None,) + shp, lambda l: (l,) + (0,) * len(shp))
    const = lambda shp: pl.BlockSpec(shp, lambda l: (0,) * len(shp))
    return pl.pallas_call(
        _hy_filter_kernel,
        grid=(depth,),
        in_specs=[const((n, p)), per_layer((p, p)), per_layer((1, p)), per_layer((p, p)), per_layer((1, p)),
                  per_layer((p, p)), per_layer((1, p)), per_layer((p, 2 * HY_W)), per_layer((8, p)),
                  const((n, 2 * HY_W))],
        out_specs=per_layer((n, 2 * HY_W)),
        out_shape=jax.ShapeDtypeStruct((depth, n, 2 * HY_W), F32),
        compiler_params=_cparams(("arbitrary",)),
        name="hyena_filters",
    )(*args)


def _conv_taps(filt):
    hf, hb = filt[..., :HY_W], filt[..., HY_W:]
    taps = jnp.concatenate([hb[:, :0:-1], (hf[:, 0:1] + hb[:, 0:1]), hf[:, 1:], jnp.zeros_like(hf[:, 0:1])], axis=1)
    return jnp.transpose(taps, (0, 2, 1))


def _hy_conv_kernel(z_ref, k_ref, o_ref, *, nblk, nb):
    cg = z_ref.shape[0]
    n2 = k_ref.shape[1]
    ncols = (2 * nblk - 1) * LANE

    def body(ci, _):
        krow = k_ref[pl.ds(ci, 1), :]
        kb = jnp.broadcast_to(krow, (LANE, n2))
        big = pltpu.roll(kb, n2 - (LANE - 1), 1, stride=1, stride_axis=0)[:, :ncols].astype(BF16)
        o_ref[ci] = jnp.zeros(o_ref.shape[1:], F32)
        for m in range(-(nblk - 1), nblk):
            km = big[:, (m + nblk - 1) * LANE:(m + nblk) * LANE]
            cnt = (nblk - abs(m)) * nb
            src = 0 if m >= 0 else -m * nb
            dst = m * nb if m >= 0 else 0
            zin = z_ref[ci, src:src + cnt, :]
            o_ref[ci, dst:dst + cnt, :] += jnp.dot(zin, km, preferred_element_type=F32)
        return 0

    lax.fori_loop(0, cg, body, 0)


def _hy_conv(zr, taps, nblk, nb):
    c, r, _ = zr.shape
    cg = 8
    return pl.pallas_call(
        functools.partial(_hy_conv_kernel, nblk=nblk, nb=nb),
        grid=(c // cg,),
        in_specs=[pl.BlockSpec((cg, r, LANE), lambda i: (i, 0, 0)),
                  pl.BlockSpec((cg, taps.shape[1]), lambda i: (i, 0))],
        out_specs=pl.BlockSpec((cg, r, LANE), lambda i: (i, 0, 0)),
        out_shape=jax.ShapeDtypeStruct((c, r, LANE), F32),
        compiler_params=_cparams(("arbitrary",)),
        name="hyena_conv",
    )(zr, taps)


def _hy_long_conv(z, taps):
    b, n, c = z.shape
    nblk = n // LANE
    zr = jnp.transpose(z.astype(BF16).reshape(b, nblk, LANE, c), (3, 1, 0, 2)).reshape(c, nblk * b, LANE)
    y = _hy_conv(zr, taps, nblk, b)
    return jnp.transpose(y.reshape(c, nblk, b, LANE), (2, 1, 3, 0)).reshape(b, n, c)


MERGE_TM = 768
MERGE_SUB = 256


def _merge_kernel(h_ref, mod_ref, ya_ref, x0_ref, z_ref, yconv_ref, hb_ref, yc_ref, yd_ref, gate_ref,
                  wb_ref, wo_ref, g2_ref, wq_ref, o_ref, q_ref, nb_ref, *, n_ctx, tm):
    i = pl.program_id(1)
    d = h_ref.shape[1]
    sub = MERGE_SUB

    def branches(r0):
        rs = slice(r0, r0 + sub)
        yb = (x0_ref[rs, :] * (yconv_ref[rs, :] + z_ref[rs, :] * hb_ref[...])).astype(BF16)
        ys = (ya_ref[rs, :], yb, yc_ref[rs, :], yd_ref[rs, :])
        acc = jnp.zeros((sub, d), F32)
        for k in range(N_BRANCH):
            proj = jnp.dot(ys[k], wb_ref[k], preferred_element_type=F32)
            acc = acc + jax.nn.sigmoid(gate_ref[rs, k * d:(k + 1) * d].astype(F32)) * proj
        return acc.astype(BF16)

    def residual(r0, acc):
        rs = slice(r0, r0 + sub)
        out = jnp.dot(acc, wo_ref[...], preferred_element_type=F32)
        row = i * tm + r0 + lax.broadcasted_iota(I32, out.shape, 0)
        gate = jnp.where(row < n_ctx, mod_ref[0, 2:3, :], mod_ref[1, 2:3, :])
        hn = h_ref[rs, :] + gate * out
        o_ref[rs, :] = hn
        nb = _norm_modulate(hn, mod_ref, g2_ref, 3, i * tm + r0, n_ctx).astype(BF16)
        nb_ref[rs, :] = nb
        return nb

    def query(r0, nb):
        q_ref[r0:r0 + sub, :] = jnp.dot(nb, wq_ref[...], preferred_element_type=F32).astype(q_ref.dtype)

    n_sub = tm // sub
    acc, nb = {}, {}
    for step in range(n_sub + 2):
        if step < n_sub:
            acc[step] = branches(step * sub)
        if 0 <= step - 1 < n_sub:
            nb[step - 1] = residual((step - 1) * sub, acc.pop(step - 1))
        if 0 <= step - 2 < n_sub:
            query((step - 2) * sub, nb.pop(step - 2))


def _merge(h, modl, ya, x0, z, yconv, hbias, yc, yd, gates, wb, wo, g2, wq, n_ctx):
    b, s, d = h.shape
    tm = MERGE_TM
    row = lambda w: pl.BlockSpec((None, tm, w), lambda bi, i: (bi, i, 0))
    return pl.pallas_call(
        functools.partial(_merge_kernel, n_ctx=n_ctx, tm=tm),
        grid=(b, s // tm),
        in_specs=[row(d), pl.BlockSpec((None, 2, 8, d), lambda bi, i: (bi, 0, 0, 0)),
                  row(256), row(256), row(256), row(256), pl.BlockSpec((1, 256), lambda bi, i: (0, 0)),
                  row(256), row(256), row(GATE_W),
                  pl.BlockSpec(wb.shape, lambda bi, i: (0, 0, 0)),
                  pl.BlockSpec(wo.shape, lambda bi, i: (0, 0)),
                  pl.BlockSpec((1, d), lambda bi, i: (0, 0)),
                  pl.BlockSpec(wq.shape, lambda bi, i: (0, 0))],
        out_specs=[row(d), row(wq.shape[1]), row(d)],
        out_shape=[jax.ShapeDtypeStruct((b, s, d), F32), jax.ShapeDtypeStruct((b, s, wq.shape[1]), BF16),
                   jax.ShapeDtypeStruct((b, s, d), BF16)],
        compiler_params=_cparams(("arbitrary", "arbitrary")),
        name="merge",
    )(h, modl, ya, x0, z, yconv, hbias, yc, yd, gates, wb, wo, g2, wq)


def _topk_rows(s, label, k):
    vals, labs = [], []
    for _ in range(k):
        m = jnp.max(s, axis=0, keepdims=True)
        lb = jnp.min(jnp.where(s == m, label, float(2 ** 20)), axis=0, keepdims=True)
        vals.append(m)
        labs.append(lb)
        s = jnp.where(label == lb, -jnp.inf, s)
    return jnp.concatenate(vals, axis=0), jnp.concatenate(labs, axis=0).astype(I32)


def _topk_sorted_columns(s, k):
    n = s.shape[0] // 8
    sub = lax.broadcasted_iota(I32, (8, s.shape[1]), 0)
    vals = [s[8 * v:8 * v + 8, :] for v in range(n)]
    rows = [(sub + 8 * v).astype(F32) for v in range(n)]
    for p in range(n):
        for v in range(p % 2, n - 1, 2):
            swap = vals[v + 1] > vals[v]
            vals[v], vals[v + 1] = jnp.where(swap, vals[v + 1], vals[v]), jnp.where(swap, vals[v], vals[v + 1])
            rows[v], rows[v + 1] = jnp.where(swap, rows[v + 1], rows[v]), jnp.where(swap, rows[v], rows[v + 1])
    out_v, out_r = [], []
    for r in range(k):
        m = jnp.max(vals[0], axis=0, keepdims=True)
        best = jnp.min(jnp.where(vals[0] == m, rows[0], float(2 ** 20)), axis=0, keepdims=True)
        out_v.append(m)
        out_r.append(best)
        if r + 1 < k:
            pop = rows[0] == best
            for v in range(min(n - 1, k - 1 - r)):
                vals[v] = jnp.where(pop, vals[v + 1], vals[v])
                rows[v] = jnp.where(pop, rows[v + 1], rows[v])
            if n - 1 < k - 1 - r:
                vals[n - 1] = jnp.where(pop, -jnp.inf, vals[n - 1])
    return jnp.concatenate(out_v, axis=0), jnp.concatenate(out_r, axis=0).astype(I32)


_CAND_FIXED_A = ((0, 0), (0, 8), (1, 0), (2, 0), (3, 0))
_CAND_FIXED_B = ((0, 0), (0, 8), (1, 0), (2, 0))


def _select_rows(table, pos):
    out = jnp.zeros(pos.shape, table.dtype)
    for a in range(table.shape[0]):
        out = jnp.where(pos == a, table[a:a + 1, :], out)
    return out


def _peer_route_kernel(q_ref, keys_ref, i_ref, j_ref, g_ref):
    st = _dot_nt(keys_ref[...], q_ref[...].astype(BF16))
    t = st.shape[1]
    sv1, si1 = _topk_sorted_columns(st[0:PEER_NKEYS], PEER_TOPK)
    sv2, si2 = _topk_sorted_columns(st[PEER_NKEYS:2 * PEER_NKEYS], PEER_TOPK)
    r8 = lax.broadcasted_iota(I32, (8, t), 0)
    cands, labels = [], []
    for a, b0 in _CAND_FIXED_A:
        cands.append(sv1[a:a + 1, :] + sv2[b0:b0 + 8, :])
        labels.append((a * PEER_TOPK + b0 + r8).astype(F32))
    for b, a0 in _CAND_FIXED_B:
        c = sv1[a0:a0 + 8, :] + sv2[b:b + 1, :]
        cands.append(jnp.where(r8 < 4, -jnp.inf, c) if a0 == 0 else c)
        labels.append(((a0 + r8) * PEER_TOPK + b).astype(F32))
    best, pos = _topk_rows(jnp.concatenate(cands, axis=0), jnp.concatenate(labels, axis=0), PEER_TOPK)
    i_ref[...] = _select_rows(si1, pos >> int(math.log2(PEER_TOPK)))
    j_ref[...] = _select_rows(si2, pos & (PEER_TOPK - 1))
    e = jnp.exp(best - jnp.max(best, axis=0, keepdims=True))
    g_ref[...] = e / jnp.sum(e, axis=0, keepdims=True)


def _peer_route(q, keys_blk):
    t, _ = q.shape
    tt = 1024
    assert t % tt == 0
    out = pl.BlockSpec((None, PEER_TOPK, tt), lambda ti, hh: (hh, 0, ti))
    shp = (PEER_HEADS, PEER_TOPK, t)
    return pl.pallas_call(
        _peer_route_kernel,
        grid=(t // tt, PEER_HEADS),
        in_specs=[pl.BlockSpec((tt, PEER_DK), lambda ti, hh: (ti, hh)),
                  pl.BlockSpec((None, 2 * PEER_NKEYS, PEER_DK), lambda ti, hh: (hh, 0, 0))],
        out_specs=[out, out, out],
        out_shape=[jax.ShapeDtypeStruct(shp, I32), jax.ShapeDtypeStruct(shp, I32), jax.ShapeDtypeStruct(shp, F32)],
        compiler_params=_cparams(("arbitrary", "arbitrary")),
        name="peer_route",
    )(q, keys_blk)


GS_HALF = PEER_NKEYS // 2
GS_PITCH = GS_HALF + 1
GS_UNROLL = 64
PEER_TM = 576
PEER_TE = 2048
U32 = jnp.uint32


def _peer_expert_kernel(h_ref, mod_ref, x_ref, i_ref, j_ref, g_ref, ut_ref, v_ref, o_ref, gs_ref, acc_ref,
                        *, n_ctx, tm):
    ti, e = pl.program_id(1), pl.program_id(2)
    n_e = pl.num_programs(2)
    te = ut_ref.shape[1]
    nk = PEER_NKEYS

    @pl.when((pl.program_id(0) == 0) & (ti == 0) & (e == 0))
    def _():
        acc_ref[...] = jnp.zeros(acc_ref.shape, F32)

    @pl.when(e == 0)
    def _():
        sub = lax.broadcasted_iota(I32, (nk, nk), 0)

        def build(t):
            irow = i_ref[pl.ds(t, 1), :]
            jrow = j_ref[pl.ds(t, 1), :]
            grow = g_ref[pl.ds(t, 1), :]
            at = jnp.where(sub == irow, 1.0, 0.0).astype(BF16)
            bt = jnp.where(sub == jrow, grow, 0.0).astype(BF16)
            gt = _dot_nt(at, bt).astype(BF16).astype(F32)
            lo = pltpu.bitcast(gt[0:GS_HALF], U32) >> 16
            hi = pltpu.bitcast(gt[GS_HALF:nk], U32)
            gs_ref[pl.ds(t * GS_PITCH, GS_HALF), :] = hi | lo

        def body(tb, _):
            for u in range(GS_UNROLL):
                build(tb * GS_UNROLL + u)
            return 0

        lax.fori_loop(0, tm // GS_UNROLL, body, 0)

    nb = te // nk // 2

    def gelu(a):
        return 0.5 * a * (1.0 + lax.erf(a * (2.0 ** -0.5)))

    hid = jnp.dot(x_ref[...], ut_ref[...], preferred_element_type=F32)
    first, second = [], []
    for k in range(nb):
        word = gs_ref[pl.ds(e * nb + k, tm, stride=GS_PITCH), :]
        g_lo = pltpu.bitcast(word << 16, F32)
        g_hi = pltpu.bitcast(word & jnp.uint32(0xFFFF0000), F32)
        first.append((g_lo * gelu(hid[:, k * nk:(k + 1) * nk])).astype(BF16))
        second.append((g_hi * gelu(hid[:, (nb + k) * nk:(nb + k + 1) * nk])).astype(BF16))
    acc_ref[...] += jnp.dot(jnp.concatenate(first + second, axis=1), v_ref[...], preferred_element_type=F32)

    @pl.when(e == n_e - 1)
    def _():
        row = ti * tm + lax.broadcasted_iota(I32, acc_ref.shape, 0)
        gate = jnp.where(row < n_ctx, mod_ref[0, 5:6, :], mod_ref[1, 5:6, :])
        o_ref[...] = h_ref[...] + gate * acc_ref[...]
        acc_ref[...] = jnp.zeros(acc_ref.shape, F32)


def _peer_experts(h, modl, xb, i_idx, j_idx, g, ut, v, n_ctx):
    b, s, d = h.shape
    tm, te = PEER_TM, PEER_TE
    n_chunks = ut.shape[0]
    slots = PEER_HEADS * PEER_TOPK
    row = lambda w: pl.BlockSpec((None, tm, w), lambda bi, i, e: (bi, i, 0), pipeline_mode=pl.Buffered(1))
    return pl.pallas_call(
        functools.partial(_peer_expert_kernel, n_ctx=n_ctx, tm=tm),
        grid=(b, s // tm, n_chunks),
        in_specs=[row(d), pl.BlockSpec((None, 2, 8, d), lambda bi, i, e: (bi, 0, 0, 0)),
                  row(d), row(slots), row(slots), row(slots),
                  pl.BlockSpec((None, d, te), lambda bi, i, e: (e, 0, 0)),
                  pl.BlockSpec((te, d), lambda bi, i, e: (e, 0))],
        out_specs=pl.BlockSpec((None, tm, d), lambda bi, i, e: (bi, i, 0)),
        out_shape=jax.ShapeDtypeStruct((b, s, d), F32),
        scratch_shapes=[pltpu.VMEM((tm * GS_PITCH, PEER_NKEYS), U32), pltpu.VMEM((tm, d), F32)],
        compiler_params=_cparams(("arbitrary", "arbitrary", "arbitrary")),
        name="peer_experts",
    )(h, modl, xb, i_idx, j_idx, g, ut, v)


def _final_norm_kernel(x_ref, g_ref, o_ref):
    x = x_ref[...]
    o_ref[...] = x * lax.rsqrt(jnp.mean(x * x, axis=-1, keepdims=True) + EPS) * g_ref[...]


def _final_norm(h, g, n_ctx):
    b, s, d = h.shape
    tm = ROW_TILE
    n_lat = s - n_ctx
    off = n_ctx // tm
    return pl.pallas_call(
        _final_norm_kernel,
        grid=(b, n_lat // tm),
        in_specs=[pl.BlockSpec((None, tm, d), lambda bi, i: (bi, i + off, 0)),
                  pl.BlockSpec((1, d), lambda bi, i: (0, 0))],
        out_specs=pl.BlockSpec((None, tm, d), lambda bi, i: (bi, i, 0)),
        out_shape=jax.ShapeDtypeStruct((b, n_lat, d), F32),
        compiler_params=_cparams(("arbitrary", "arbitrary")),
        name="final_norm",
    )(h, g)


def _prep_w_in(w_in):
    depth, d, _ = w_in.shape
    mixw = w_in[:, :, :2816]
    gq = mixw[:, :, 1536:1792].reshape(depth, d, GQA_HEADS, GQA_DH)[:, :, (0, 2, 1, 3), :].reshape(depth, d, 256)
    mixw = jnp.concatenate([mixw[:, :, :1536], gq, mixw[:, :, 1792:], jnp.zeros((depth, d, MIX_W - 2816), w_in.dtype)], axis=-1)
    return jnp.concatenate([mixw, w_in[:, :, 2816:]], axis=-1).astype(BF16)


def _peer_chunk_order(w):
    n_exp, d = w.shape
    nb = PEER_TE // PEER_NKEYS // 2
    return w.reshape(2, GS_HALF // nb, nb, PEER_NKEYS, d).transpose(1, 0, 2, 3, 4).reshape(n_exp, d)


def _prep_peer_keys(keys):
    depth = keys.shape[0]
    half = PEER_DK // 2
    z = jnp.zeros((depth, PEER_HEADS, PEER_NKEYS, half), keys.dtype)
    top = jnp.concatenate([keys[:, :, 0], z], axis=-1)
    bot = jnp.concatenate([z, keys[:, :, 1]], axis=-1)
    return jnp.concatenate([top, bot], axis=2).astype(BF16)


def kernel(x, c, ctx, c_ctx, w_mod, b_mod, norm1_g, norm2_g, w_in, na_rpb, hy_short_w, hy_short_b, hy_w1, hy_b1, hy_w2, hy_b2, hy_w3, hy_b3, hy_w4, hy_freq, hy_bias, gqa_qn, gqa_kn, diff_lq1, diff_lk1, diff_lq2, diff_lk2, diff_subln, w_branch, w_out, peer_wq, peer_keys, peer_u, peer_v, final_g):
    B, L, D = x.shape
    C = ctx.shape[1]
    S = C + L
    depth = w_mod.shape[0]
    assert C == ROW_TILE and L % ROW_TILE == 0 and D == 1024 and L % GRID_W == 0

    h = jnp.concatenate([ctx, x], axis=1)

    r = -(-(B + 1) // 8) * 8
    cvec = jnp.zeros((r, D), F32).at[:B].set(c).at[B].set(c_ctx)
    modall = _modulation(cvec, w_mod, b_mod)
    mod_lat = modall[:, :B].reshape(depth, B, 1, 6, D)
    mod_ctx = jnp.broadcast_to(modall[:, B].reshape(depth, 1, 1, 6, D), (depth, B, 1, 6, D))
    mods = jnp.pad(jnp.concatenate([mod_ctx, mod_lat], axis=2), ((0, 0), (0, 0), (0, 0), (0, 2), (0, 0)))

    fargs = (hy_w1, hy_b1, hy_w2, hy_b2, hy_w3, hy_b3, hy_w4, hy_freq)
    taps_lat = _conv_taps(_hy_filters(L, *fargs))
    taps_ctx = _conv_taps(_hy_filters(C, *fargs))

    cg, sg = _rope_tables(C, L, GQA_DH, 256)
    cd, sd = _rope_tables(C, L, DIFF_DH, 256)

    lam_init = jnp.asarray([0.8 - 0.6 * math.exp(-0.3 * l) for l in range(depth)], F32)
    lam = (jnp.exp(jnp.sum(diff_lq1.astype(F32) * diff_lk1.astype(F32), axis=-1))
           - jnp.exp(jnp.sum(diff_lq2.astype(F32) * diff_lk2.astype(F32), axis=-1)) + lam_init)
    diff_par = jnp.zeros((depth, 8, LANE), F32).at[:, 0, :].set(lam[:, None]).at[:, 1, :].set(1.0 - lam_init[:, None])

    wb = w_branch.astype(BF16)
    wb = wb.at[:, 2].set(wb[:, 2].reshape(depth, GQA_HEADS, GQA_DH, D)[:, (0, 2, 1, 3)].reshape(depth, BRANCH_W, D))

    xs = dict(
        mods=mods, n1=norm1_g[:, None, :], n2=norm2_g[:, None, :], w_in=_prep_w_in(w_in),
        bias=_na_bias_table(na_rpb, C), sw=hy_short_w, sb=hy_short_b[:, None, :], hbias=hy_bias[:, None, :],
        taps_lat=taps_lat, taps_ctx=taps_ctx,
        qn=jnp.tile(gqa_qn, (1, GQA_HEADS))[:, None, :], kn=jnp.tile(gqa_kn, (1, GQA_KV))[:, None, :],
        diff_par=diff_par, subln=jnp.tile(diff_subln, (1, DIFF_HEADS))[:, None, :],
        wb=wb, wo=w_out.astype(BF16), wq=peer_wq.astype(BF16), keys=_prep_peer_keys(peer_keys),
        u=peer_u, v=peer_v,
    )

    def layer(h, p):
        pb_na, pb_gqa, pb_diff, hy, gates = _in_proj(h, p["mods"], p["n1"], p["w_in"], cg, sg, cd, sd,
                                                     p["qn"], p["kn"], C)
        ya = _na_attention(pb_na, p["bias"], C)
        yc = _gqa_attention(pb_gqa, C)
        yd = _diff_attention(pb_diff, p["diff_par"], p["subln"], C)
        x0, z = _hy_pre(hy, p["sw"], p["sb"], C)
        yconv = jnp.concatenate([_hy_long_conv(z[:, :C], p["taps_ctx"]), _hy_long_conv(z[:, C:], p["taps_lat"])], axis=1)
        h, q, xb = _merge(h, p["mods"], ya, x0, z, yconv, p["hbias"], yc, yd, gates, p["wb"], p["wo"],
                          p["n2"], p["wq"], C)
        i_idx, j_idx, g = _peer_route(q.reshape(B * S, -1), p["keys"])
        slots = PEER_HEADS * PEER_TOPK
        to_rows = lambda a: jnp.transpose(a, (2, 0, 1)).reshape(B, S, slots)
        ut = jnp.transpose(_peer_chunk_order(p["u"].astype(BF16)).reshape(-1, PEER_TE, D), (0, 2, 1))
        h = _peer_experts(h, p["mods"], xb, to_rows(i_idx), to_rows(j_idx), to_rows(g), ut,
                          _peer_chunk_order(p["v"].astype(BF16)), C)
        return h, None

    h, _ = lax.scan(layer, h, xs)
    return _final_norm(h, final_g[None, :], C)
```

```python
import functools
import math

import numpy as np
import jax
import jax.numpy as jnp
from jax import lax
from jax.experimental import pallas as pl
from jax.experimental.pallas import tpu as pltpu

F32 = jnp.float32
BF16 = jnp.bfloat16
I32 = jnp.int32
HIGHEST = lax.Precision.HIGHEST

EPS = 1e-6
GRID_W = 64
ROPE_THETA = 10000.0
NA_HEADS, NA_DH, NA_KH, NA_KW = 4, 64, 8, 16
HY_W, HY_BANDS, HY_FFN = 256, 16, 64
HY_FAST_DECAY, HY_SLOW_DECAY, HY_TARGET = 0.3, 1.5, 1e-2
GQA_HEADS, GQA_KV, GQA_DH = 4, 2, 64
DIFF_HEADS, DIFF_DH, DIFF_DV = 4, 32, 64
N_BRANCH, BRANCH_W = 4, 256
PEER_HEADS, PEER_NKEYS, PEER_DK, PEER_TOPK = 8, 128, 128, 16

LANE = 128
ROW_TILE = 256
VMEM_LIMIT = 56 * 1024 * 1024
NEG = -1e30
LOG2E = math.log2(math.e)
MIX_W = 3072
GATE_W = N_BRANCH * 1024
COL_TILE = 1024


def _cparams(sem):
    return pltpu.CompilerParams(dimension_semantics=sem, vmem_limit_bytes=VMEM_LIMIT)


def _mod_kernel(c_ref, w_ref, b_ref, o_ref):
    cv = c_ref[...]
    a = cv * jax.nn.sigmoid(cv)
    o_ref[...] = jnp.dot(a, w_ref[...], preferred_element_type=F32, precision=HIGHEST) + b_ref[...]


def _modulation(cvec, w_mod, b_mod):
    depth, d, n = w_mod.shape
    r = cvec.shape[0]
    tn = 1024
    return pl.pallas_call(
        _mod_kernel,
        grid=(depth, n // tn),
        in_specs=[pl.BlockSpec((r, d), lambda l, j: (0, 0)),
                  pl.BlockSpec((None, d, tn), lambda l, j: (l, 0, j)),
                  pl.BlockSpec((None, 1, tn), lambda l, j: (l, 0, j))],
        out_specs=pl.BlockSpec((None, r, tn), lambda l, j: (l, 0, j)),
        out_shape=jax.ShapeDtypeStruct((depth, r, n), F32),
        compiler_params=_cparams(("arbitrary", "arbitrary")),
        name="modulation",
    )(cvec, w_mod, b_mod.reshape(depth, 1, n))


def _norm_modulate(x, mod_ref, g_ref, srow, row0, n_ctx):
    y = x * lax.rsqrt(jnp.mean(x * x, axis=-1, keepdims=True) + EPS) * g_ref[...]
    row = row0 + lax.broadcasted_iota(I32, x.shape, 0)
    is_ctx = row < n_ctx
    shift = jnp.where(is_ctx, mod_ref[0, srow:srow + 1, :], mod_ref[1, srow:srow + 1, :])
    scale = jnp.where(is_ctx, mod_ref[0, srow + 1:srow + 2, :], mod_ref[1, srow + 1:srow + 2, :])
    return y * (1.0 + scale) + shift


def _in_proj_kernel(h_ref, mod_ref, g_ref, w_ref, cg_ref, sg_ref, cd_ref, sd_ref, qn_ref, kn_ref, bd_ref,
                    na_ref, gqa_ref, diff_ref, hy_ref, gate_ref, nb_ref, *, tm, n_ctx, n_mix):
    i, j = pl.program_id(1), pl.program_id(2)

    @pl.when(j == 0)
    def _():
        nb_ref[...] = _norm_modulate(h_ref[...], mod_ref, g_ref, 0, i * tm, n_ctx).astype(BF16)

    def proj():
        return jnp.dot(nb_ref[...], w_ref[...], preferred_element_type=F32)

    @pl.when(j == 0)
    def _():
        res = proj()
        na_ref[:, 0:256] = (res[:, 0:256] * (LOG2E * NA_DH ** -0.5)).astype(BF16)
        na_ref[:, 256:768] = res[:, 256:768].astype(BF16)
        hy_ref[:, 0:HY_W] = res[:, 768:1024]

    @pl.when(j == 1)
    def _():
        res = jnp.dot(nb_ref[...], w_ref[:, 512:1024], preferred_element_type=F32)
        hy_ref[:, HY_W:3 * HY_W] = jnp.dot(nb_ref[...], w_ref[:, 0:512], preferred_element_type=F32)
        cg, sg = cg_ref[...], sg_ref[...]
        gq, gk = res[:, 0:256], res[:, 256:384]
        bd = bd_ref[...]
        gq = gq * lax.rsqrt(_head_mean_sq_mxu(gq, GQA_DH, bd) + EPS) * qn_ref[...]
        gk = gk * lax.rsqrt(_head_mean_sq_mxu(gk, GQA_DH, bd[0:128, 0:128]) + EPS) * kn_ref[...]
        gqa_ref[:, 0:256] = (_rope(gq, cg, sg, GQA_DH // 4) * (LOG2E * GQA_DH ** -0.5)).astype(BF16)
        gqa_ref[:, 256:384] = _rope(gk, cg[:, 0:128], sg[:, 0:128], GQA_DH // 4).astype(BF16)
        gqa_ref[:, 384:512] = res[:, 384:512].astype(BF16)

    @pl.when(j == 2)
    def _():
        res = proj()
        cd, sd = cd_ref[...], sd_ref[...]
        diff_ref[:, 0:256] = (_rope(res[:, 0:256], cd, sd, DIFF_DH // 4) * (LOG2E * DIFF_DH ** -0.5)).astype(BF16)
        diff_ref[:, 256:512] = _rope(res[:, 256:512], cd, sd, DIFF_DH // 4).astype(BF16)
        diff_ref[:, 512:768] = res[:, 512:768].astype(BF16)

    @pl.when(j >= n_mix)
    def _():
        gate_ref[...] = proj().astype(gate_ref.dtype)


def _in_proj(h, modl, g, w, cg, sg, cd, sd, qn, kn, n_ctx):
    b, s, d = h.shape
    tm, tn = 1152, COL_TILE
    n_mix = MIX_W // tn
    n_tot = (MIX_W + GATE_W) // tn
    assert n_mix == 3
    const = lambda shp: pl.BlockSpec(shp, lambda bi, i, j: (0,) * len(shp))
    tab = pl.BlockSpec((tm, 256), lambda bi, i, j: (i, 0))
    rows = lambda wd: pl.BlockSpec((None, tm, wd), lambda bi, i, j: (bi, i, 0))
    head_ones = jnp.asarray(np.kron(np.eye(GQA_HEADS), np.ones((GQA_DH, GQA_DH))), BF16)
    return pl.pallas_call(
        functools.partial(_in_proj_kernel, tm=tm, n_ctx=n_ctx, n_mix=n_mix),
        grid=(b, s // tm, n_tot),
        in_specs=[rows(d), pl.BlockSpec((None, 2, 8, d), lambda bi, i, j: (bi, 0, 0, 0)), const((1, d)),
                  pl.BlockSpec((d, tn), lambda bi, i, j: (0, j)),
                  tab, tab, tab, tab, const((1, 256)), const((1, 128)), const((256, 256))],
        out_specs=[rows(768), rows(512), rows(768), rows(3 * HY_W),
                   pl.BlockSpec((None, tm, tn), lambda bi, i, j: (bi, i, jnp.maximum(j - n_mix, 0)))],
        out_shape=[jax.ShapeDtypeStruct((b, s, 768), BF16), jax.ShapeDtypeStruct((b, s, 512), BF16),
                   jax.ShapeDtypeStruct((b, s, 768), BF16), jax.ShapeDtypeStruct((b, s, 3 * HY_W), F32),
                   jax.ShapeDtypeStruct((b, s, GATE_W), BF16)],
        scratch_shapes=[pltpu.VMEM((tm, d), BF16)],
        compiler_params=_cparams(("arbitrary", "arbitrary", "arbitrary")),
        name="in_proj",
    )(h, modl, g, w, cg, sg, cd, sd, qn, kn, head_ones)


def _lane_group(shape, width):
    return lax.broadcasted_iota(I32, shape, 1) >> int(math.log2(width))


def _head_mean_sq(x, n_heads, dh):
    hid = _lane_group(x.shape, dh)
    x2 = x * x
    ms = jnp.zeros_like(x)
    for hh in range(n_heads):
        m = hid == hh
        s = jnp.sum(jnp.where(m, x2, 0.0), axis=-1, keepdims=True)
        ms = jnp.where(m, s, ms)
    return ms * (1.0 / dh)


def _head_mean_sq_mxu(x, dh, head_ones):
    x2 = x * x
    hi = x2.astype(BF16)
    lo = (x2 - hi.astype(F32)).astype(BF16)
    s = jnp.dot(hi, head_ones, preferred_element_type=F32) + jnp.dot(lo, head_ones, preferred_element_type=F32)
    return s * (1.0 / dh)


def _rope(x, cos, sin_signed, qs):
    w = x.shape[-1]
    lane = lax.broadcasted_iota(I32, x.shape, 1)
    lo = (lane & (2 * qs - 1)) < qs
    partner = jnp.where(lo, pltpu.roll(x, w - qs, 1), pltpu.roll(x, qs, 1))
    return x * cos + partner * sin_signed


def _rope_tables(n_ctx, n_lat, dh, width):
    qs = dh // 4
    t = np.arange(n_lat)
    rows, cols = (t // GRID_W).astype(np.float64), (t % GRID_W).astype(np.float64)
    lane = np.arange(width) % dh
    part = lane // (2 * qs)
    u = lane % (2 * qs)
    f = u % qs
    lo = u < qs
    freqs = ROPE_THETA ** (-f.astype(np.float64) / qs)
    pos = np.where(part[None, :] == 0, rows[:, None], cols[:, None])
    ang = (pos.astype(np.float32) * freqs.astype(np.float32)[None, :]).astype(np.float32)
    cos = np.cos(ang.astype(np.float64))
    sin = np.sin(ang.astype(np.float64)) * np.where(lo, -1.0, 1.0)[None, :]
    cos = np.concatenate([np.ones((n_ctx, width)), cos], axis=0)
    sin = np.concatenate([np.zeros((n_ctx, width)), sin], axis=0)
    return jnp.asarray(cos, F32), jnp.asarray(sin, F32)


def _softmax_rows(s):
    m = jnp.max(s, axis=-1, keepdims=True)
    p = jnp.exp2(s - m)
    return p, jnp.sum(p, axis=-1, keepdims=True)


def _dot_nt(a, b):
    return lax.dot_general(a, b, (((1,), (1,)), ((), ())), preferred_element_type=F32)


def _gqa_kernel(q_ref, k_ref, v_ref, o_ref, *, n_ctx):
    i = pl.program_id(1)

    def run(nk):
        k = k_ref[0:nk, :]
        v = v_ref[0:nk, :]
        q = q_ref[...]
        tq = q.shape[0]
        grp = _lane_group((tq, LANE), GQA_DH)

        def scores(g):
            mask = grp == g
            ql = jnp.concatenate([jnp.where(mask, q[:, 0:128], 0), jnp.where(mask, q[:, 128:256], 0)], axis=0)
            return _dot_nt(ql, k)

        v1 = jnp.concatenate([v, jnp.ones_like(v)], axis=1)

        outs = []
        s_next = scores(0)
        for g in range(GQA_KV):
            s = s_next
            if g + 1 < GQA_KV:
                s_next = scores(g + 1)
            p = jnp.exp2(s - jnp.max(s, axis=-1, keepdims=True))
            ov = jnp.dot(p.astype(BF16), v1, preferred_element_type=F32)
            outs.append(ov[:, 0:LANE] / ov[:, LANE:2 * LANE])
        for half in range(2):
            sel = jnp.where(grp == 0, outs[0][half * tq:(half + 1) * tq], outs[1][half * tq:(half + 1) * tq])
            o_ref[:, half * 128:(half + 1) * 128] = sel.astype(o_ref.dtype)

    @pl.when(i == 0)
    def _():
        run(n_ctx)

    @pl.when(i > 0)
    def _():
        run(k_ref.shape[0])


def _gqa_attention(pb, n_ctx):
    b, s, _ = pb.shape
    tq = ROW_TILE
    return pl.pallas_call(
        functools.partial(_gqa_kernel, n_ctx=n_ctx),
        grid=(b, s // tq),
        in_specs=[pl.BlockSpec((None, tq, 256), lambda bi, i: (bi, i, 0)),
                  pl.BlockSpec((None, s, 128), lambda bi, i: (bi, 0, 2)),
                  pl.BlockSpec((None, s, 128), lambda bi, i: (bi, 0, 3))],
        out_specs=pl.BlockSpec((None, tq, 256), lambda bi, i: (bi, i, 0)),
        out_shape=jax.ShapeDtypeStruct((b, s, 256), BF16),
        compiler_params=_cparams(("arbitrary", "arbitrary")),
        name="gqa_attention",
    )(pb, pb, pb)


def _diff_kernel(q_ref, k_ref, v_ref, par_ref, g_ref, o_ref, *, n_ctx):
    i = pl.program_id(1)

    def run(nk):
        k = k_ref[0:nk, :]
        v = v_ref[0:nk, :]
        q = q_ref[...]
        tq = q.shape[0]
        lam = par_ref[0:1, 0:1]
        comp = _lane_group(q.shape, DIFF_DH)
        head = _lane_group(q.shape, DIFF_DV)
        acc = jnp.zeros(q.shape, F32)

        def scores(hh):
            ql = jnp.concatenate([jnp.where(comp == 2 * hh, q, 0), jnp.where(comp == 2 * hh + 1, q, 0)], axis=0)
            return _dot_nt(ql, k)

        s_next = scores(0)
        for hh in range(DIFF_HEADS):
            s = s_next
            if hh + 1 < DIFF_HEADS:
                s_next = scores(hh + 1)
            p, l = _softmax_rows(s)
            inv = 1.0 / l
            pc = (p[0:tq] * inv[0:tq] - p[tq:2 * tq] * (lam * inv[tq:2 * tq])).astype(BF16)
            o = jnp.dot(pc, v, preferred_element_type=F32)
            acc = jnp.where(head == hh, o, acc)
        y = acc * lax.rsqrt(_head_mean_sq(acc, DIFF_HEADS, DIFF_DV) + EPS) * g_ref[...]
        o_ref[...] = (y * par_ref[1:2, 0:1]).astype(o_ref.dtype)

    @pl.when(i == 0)
    def _():
        run(n_ctx)

    @pl.when(i > 0)
    def _():
        run(k_ref.shape[0])


def _diff_attention(pb, par, subln, n_ctx):
    b, s, _ = pb.shape
    tq = ROW_TILE
    return pl.pallas_call(
        functools.partial(_diff_kernel, n_ctx=n_ctx),
        grid=(b, s // tq),
        in_specs=[pl.BlockSpec((None, tq, 256), lambda bi, i: (bi, i, 0)),
                  pl.BlockSpec((None, s, 256), lambda bi, i: (bi, 0, 1)),
                  pl.BlockSpec((None, s, 256), lambda bi, i: (bi, 0, 2)),
                  pl.BlockSpec((8, 128), lambda bi, i: (0, 0)),
                  pl.BlockSpec((1, 256), lambda bi, i: (0, 0))],
        out_specs=pl.BlockSpec((None, tq, 256), lambda bi, i: (bi, i, 0)),
        out_shape=jax.ShapeDtypeStruct((b, s, 256), BF16),
        compiler_params=_cparams(("arbitrary", "arbitrary")),
        name="diff_attention",
    )(pb, pb, pb, par, subln)


def _na_kernel(q_ref, k_ref, v_ref, bias_ref, o_ref, *, n_ctx, n_rows):
    i = pl.program_id(1)
    tq = q_ref.shape[0]

    def stack_heads(q):
        head = _lane_group(q.shape, NA_DH)
        return jnp.concatenate([jnp.where(head == hh, q, 0) for hh in range(NA_HEADS)], axis=0)

    def unstack_heads(o, rows):
        head = _lane_group((rows, 256), NA_DH)
        out = jnp.zeros((rows, 256), F32)
        for hh in range(NA_HEADS):
            out = jnp.where(head == hh, o[hh * rows:(hh + 1) * rows], out)
        return out

    @pl.when(i == 0)
    def _():
        kc = k_ref[0:n_ctx, :]
        vc = v_ref[0:n_ctx, :]
        p, l = _softmax_rows(_dot_nt(stack_heads(q_ref[...]), kc))
        o = jnp.dot(p.astype(BF16), vc, preferred_element_type=F32) / l
        o_ref[...] = unstack_heads(o, tq).astype(o_ref.dtype)

    @pl.when(i > 0)
    def _():
        kc = k_ref[0:n_ctx, :]
        vc = v_ref[0:n_ctx, :]
        def scores(rr):
            r = (i - 1) * (tq // GRID_W) + rr
            rs = jnp.clip(r - NA_KH // 2, 0, n_rows - NA_KH)
            start = pl.multiple_of(n_ctx + rs * GRID_W, GRID_W)
            kcat = jnp.concatenate([kc, k_ref[pl.ds(start, NA_KH * GRID_W), :]], axis=0)
            q = q_ref[rr * GRID_W:(rr + 1) * GRID_W, :]
            return _dot_nt(stack_heads(q), kcat) + bias_ref[r - rs], start

        nxt = scores(0)
        for rr in range(tq // GRID_W):
            s, start = nxt
            if rr + 1 < tq // GRID_W:
                nxt = scores(rr + 1)
            vcat = jnp.concatenate([vc, v_ref[pl.ds(start, NA_KH * GRID_W), :]], axis=0)
            p, l = _softmax_rows(s)
            o = jnp.dot(p.astype(BF16), vcat, preferred_element_type=F32) / l
            o_ref[rr * GRID_W:(rr + 1) * GRID_W, :] = unstack_heads(o, GRID_W).astype(o_ref.dtype)


def _na_attention(pb, bias, n_ctx):
    b, s, _ = pb.shape
    tq = ROW_TILE
    n_rows = (s - n_ctx) // GRID_W
    return pl.pallas_call(
        functools.partial(_na_kernel, n_ctx=n_ctx, n_rows=n_rows),
        grid=(b, s // tq),
        in_specs=[pl.BlockSpec((None, tq, 256), lambda bi, i: (bi, i, 0)),
                  pl.BlockSpec((None, s, 256), lambda bi, i: (bi, 0, 1)),
                  pl.BlockSpec((None, s, 256), lambda bi, i: (bi, 0, 2)),
                  pl.BlockSpec(bias.shape, lambda bi, i: (0, 0, 0))],
        out_specs=pl.BlockSpec((None, tq, 256), lambda bi, i: (bi, i, 0)),
        out_shape=jax.ShapeDtypeStruct((b, s, 256), BF16),
        compiler_params=_cparams(("arbitrary", "arbitrary")),
        name="na_attention",
    )(pb, pb, pb, bias)


def _na_bias_table(rpb, n_ctx):
    depth = rpb.shape[0]
    pad = GRID_W - NA_KW
    padded = jnp.pad(rpb.astype(F32), ((0, 0), (0, 0), (0, 0), (pad, pad)))
    cols = jnp.stack([padded[..., GRID_W - 1 - w:2 * GRID_W - 1 - w] for w in range(GRID_W)], axis=3)
    vals = jnp.stack([cols[:, :, NA_KH - 1 - o:2 * NA_KH - 1 - o] for o in range(NA_KH)], axis=2)
    vals = jnp.transpose(vals, (0, 2, 1, 4, 3, 5))
    w = np.arange(GRID_W)[:, None, None]
    kc = np.arange(GRID_W)[None, None, :]
    cs = np.clip(w - NA_KW // 2, 0, GRID_W - NA_KW)
    inwin = np.broadcast_to((kc >= cs) & (kc < cs + NA_KW), (GRID_W, NA_KH, GRID_W))
    vals = jnp.where(jnp.asarray(inwin), vals * LOG2E, NEG).reshape(depth, NA_KH, NA_HEADS * GRID_W, NA_KH * GRID_W)
    return jnp.concatenate([jnp.zeros((depth, NA_KH, NA_HEADS * GRID_W, n_ctx), F32), vals], axis=-1)


def _hy_pre_kernel(u_ref, w_ref, b_ref, x0_ref, z_ref, *, bounds, tm):
    s = u_ref.shape[0]
    w0, w1, w2, bias = w_ref[0:1, :], w_ref[1:2, :], w_ref[2:3, :], b_ref[...]
    rid = lax.broadcasted_iota(I32, (tm, u_ref.shape[1]), 0)
    for c0 in range(0, s, tm):
        u = u_ref[c0:c0 + tm, :]
        prev = pltpu.roll(u, 1, 0)
        nxt = pltpu.roll(u, tm - 1, 0)
        first = jnp.zeros_like(w0) if c0 in bounds else u_ref[c0 - 1:c0, :]
        last = jnp.zeros_like(w0) if (c0 + tm) in bounds else u_ref[c0 + tm:c0 + tm + 1, :]
        prev = jnp.where(rid == 0, first, prev)
        nxt = jnp.where(rid == tm - 1, last, nxt)
        uc = prev * w0 + u * w1 + nxt * w2 + bias
        x0_ref[c0:c0 + tm, :] = uc[:, 0:HY_W]
        z_ref[c0:c0 + tm, :] = uc[:, HY_W:2 * HY_W] * uc[:, 2 * HY_W:3 * HY_W]


def _hy_pre(mix, w_short, b_short, n_ctx):
    b, s, _ = mix.shape
    return pl.pallas_call(
        functools.partial(_hy_pre_kernel, bounds=(0, n_ctx, s), tm=ROW_TILE),
        grid=(b,),
        in_specs=[pl.BlockSpec((None, s, 3 * HY_W), lambda bi: (bi, 0, 0)),
                  pl.BlockSpec((3, 3 * HY_W), lambda bi: (0, 0)),
                  pl.BlockSpec((1, 3 * HY_W), lambda bi: (0, 0))],
        out_specs=[pl.BlockSpec((None, s, HY_W), lambda bi: (bi, 0, 0)),
                   pl.BlockSpec((None, s, HY_W), lambda bi: (bi, 0, 0))],
        out_shape=[jax.ShapeDtypeStruct((b, s, HY_W), F32), jax.ShapeDtypeStruct((b, s, HY_W), F32)],
        compiler_params=_cparams(("arbitrary",)),
        name="hyena_pre",
    )(mix, w_short, b_short)


def _hy_filter_kernel(z_ref, w1_ref, b1_ref, w2_ref, b2_ref, w3_ref, b3_ref, w4_ref, f_ref, dec_ref, o_ref):
    def lin(a, w_ref, b_ref):
        return jnp.dot(a, w_ref[...], preferred_element_type=F32, precision=HIGHEST) + b_ref[...]

    hh = jnp.sin(f_ref[0:1, :] * lin(z_ref[...], w1_ref, b1_ref))
    hh = jnp.sin(f_ref[1:2, :] * lin(hh, w2_ref, b2_ref))
    hh = jnp.sin(f_ref[2:3, :] * lin(hh, w3_ref, b3_ref))
    hh = jnp.dot(hh, w4_ref[...], preferred_element_type=F32, precision=HIGHEST) * dec_ref[...]
    o_ref[...] = hh / jnp.sum(jnp.abs(hh), axis=0, keepdims=True)


def _pad_to(a, shape):
    return jnp.pad(a, [(0, t - s) for s, t in zip(a.shape, shape)])


def _hy_filters(n, w1, b1, w2, b2, w3, b3, w4, freq):
    depth = w1.shape[0]
    t = np.linspace(0.0, 1.0, n, dtype=np.float32)[:, None]
    w = np.float32(2.0 * math.pi / n) * np.arange(n, dtype=np.float32)[:, None]
    bands = np.linspace(1e-4, HY_BANDS - 1, HY_BANDS, dtype=np.float32)[None, :]
    z = np.concatenate([t, np.cos(w * bands), np.sin(w * bands)], axis=-1).astype(np.float32)
    z = np.pad(z, ((0, 0), (0, LANE - z.shape[1])))
    deltas = np.linspace(math.log(HY_TARGET) / HY_SLOW_DECAY, math.log(HY_TARGET) / HY_FAST_DECAY, HY_W, dtype=np.float32)
    deltas = np.tile(np.abs(deltas), 2)
    decay = np.exp(-t * deltas[None, :]).astype(np.float32)
    p = LANE
    args = (jnp.asarray(z), _pad_to(w1, (depth, p, p)), _pad_to(b1[:, None, :], (depth, 1, p)),
            _pad_to(w2, (depth, p, p)), _pad_to(b2[:, None, :], (depth, 1, p)),
            _pad_to(w3, (depth, p, p)), _pad_to(b3[:, None, :], (depth, 1, p)),
            _pad_to(w4, (depth, p, 2 * HY_W)), _pad_to(freq, (depth, 8, p)), jnp.asarray(decay))
    per_layer = lambda shp: pl.BlockSpec((None,) + shp, lambda l: (l,) + (0,) * len(shp))
    const = lambda shp: pl.BlockSpec(shp, lambda l: (0,) * len(shp))
    return pl.pallas_call(
        _hy_filter_kernel,
        grid=(depth,),
        in_specs=[const((n, p)), per_layer((p, p)), per_layer((1, p)), per_layer((p, p)), per_layer((1, p)),
                  per_layer((p, p)), per_layer((1, p)), per_layer((p, 2 * HY_W)), per_layer((8, p)),
                  const((n, 2 * HY_W))],
        out_specs=per_layer((n, 2 * HY_W)),
        out_shape=jax.ShapeDtypeStruct((depth, n, 2 * HY_W), F32),
        compiler_params=_cparams(("arbitrary",)),
        name="hyena_filters",
    )(*args)


def _conv_taps(filt):
    hf, hb = filt[..., :HY_W], filt[..., HY_W:]
    taps = jnp.concatenate([hb[:, :0:-1], (hf[:, 0:1] + hb[:, 0:1]), hf[:, 1:], jnp.zeros_like(hf[:, 0:1])], axis=1)
    return jnp.transpose(taps, (0, 2, 1))


def _hy_conv_kernel(z_ref, k_ref, o_ref, *, nblk, nb):
    cg = z_ref.shape[0]
    n2 = k_ref.shape[1]
    ncols = (2 * nblk - 1) * LANE

    def body(ci, _):
        krow = k_ref[pl.ds(ci, 1), :]
        kb = jnp.broadcast_to(krow, (LANE, n2))
        big = pltpu.roll(kb, n2 - (LANE - 1), 1, stride=1, stride_axis=0)[:, :ncols].astype(BF16)
        o_ref[ci] = jnp.zeros(o_ref.shape[1:], F32)
        for m in range(-(nblk - 1), nblk):
            km = big[:, (m + nblk - 1) * LANE:(m + nblk) * LANE]
            cnt = (nblk - abs(m)) * nb
            src = 0 if m >= 0 else -m * nb
            dst = m * nb if m >= 0 else 0
            zin = z_ref[ci, src:src + cnt, :]
            o_ref[ci, dst:dst + cnt, :] += jnp.dot(zin, km, preferred_element_type=F32)
        return 0

    lax.fori_loop(0, cg, body, 0)


def _hy_conv(zr, taps, nblk, nb):
    c, r, _ = zr.shape
    cg = 8
    return pl.pallas_call(
        functools.partial(_hy_conv_kernel, nblk=nblk, nb=nb),
        grid=(c // cg,),
        in_specs=[pl.BlockSpec((cg, r, LANE), lambda i: (i, 0, 0)),
                  pl.BlockSpec((cg, taps.shape[1]), lambda i: (i, 0))],
        out_specs=pl.BlockSpec((cg, r, LANE), lambda i: (i, 0, 0)),
        out_shape=jax.ShapeDtypeStruct((c, r, LANE), F32),
        compiler_params=_cparams(("arbitrary",)),
        name="hyena_conv",
    )(zr, taps)


def _hy_long_conv(z, taps):
    b, n, c = z.shape
    nblk = n // LANE
    zr = jnp.transpose(z.astype(BF16).reshape(b, nblk, LANE, c), (3, 1, 0, 2)).reshape(c, nblk * b, LANE)
    y = _hy_conv(zr, taps, nblk, b)
    return jnp.transpose(y.reshape(c, nblk, b, LANE), (2, 1, 3, 0)).reshape(b, n, c)


MERGE_TM = 768
MERGE_SUB = 256


def _merge_kernel(h_ref, mod_ref, ya_ref, x0_ref, z_ref, yconv_ref, hb_ref, yc_ref, yd_ref, gate_ref,
                  wb_ref, wo_ref, g2_ref, wq_ref, o_ref, q_ref, nb_ref, *, n_ctx, tm):
    i = pl.program_id(1)
    d = h_ref.shape[1]
    sub = MERGE_SUB

    def branches(r0):
        rs = slice(r0, r0 + sub)
        yb = (x0_ref[rs, :] * (yconv_ref[rs, :] + z_ref[rs, :] * hb_ref[...])).astype(BF16)
        ys = (ya_ref[rs, :], yb, yc_ref[rs, :], yd_ref[rs, :])
        acc = jnp.zeros((sub, d), F32)
        for k in range(N_BRANCH):
            proj = jnp.dot(ys[k], wb_ref[k], preferred_element_type=F32)
            acc = acc + jax.nn.sigmoid(gate_ref[rs, k * d:(k + 1) * d].astype(F32)) * proj
        return acc.astype(BF16)

    def residual(r0, acc):
        rs = slice(r0, r0 + sub)
        out = jnp.dot(acc, wo_ref[...], preferred_element_type=F32)
        row = i * tm + r0 + lax.broadcasted_iota(I32, out.shape, 0)
        gate = jnp.where(row < n_ctx, mod_ref[0, 2:3, :], mod_ref[1, 2:3, :])
        hn = h_ref[rs, :] + gate * out
        o_ref[rs, :] = hn
        nb = _norm_modulate(hn, mod_ref, g2_ref, 3, i * tm + r0, n_ctx).astype(BF16)
        nb_ref[rs, :] = nb
        return nb

    def query(r0, nb):
        q_ref[r0:r0 + sub, :] = jnp.dot(nb, wq_ref[...], preferred_element_type=F32).astype(q_ref.dtype)

    n_sub = tm // sub
    acc, nb = {}, {}
    for step in range(n_sub + 2):
        if step < n_sub:
            acc[step] = branches(step * sub)
        if 0 <= step - 1 < n_sub:
            nb[step - 1] = residual((step - 1) * sub, acc.pop(step - 1))
        if 0 <= step - 2 < n_sub:
            query((step - 2) * sub, nb.pop(step - 2))


def _merge(h, modl, ya, x0, z, yconv, hbias, yc, yd, gates, wb, wo, g2, wq, n_ctx):
    b, s, d = h.shape
    tm = MERGE_TM
    row = lambda w: pl.BlockSpec((None, tm, w), lambda bi, i: (bi, i, 0))
    return pl.pallas_call(
        functools.partial(_merge_kernel, n_ctx=n_ctx, tm=tm),
        grid=(b, s // tm),
        in_specs=[row(d), pl.BlockSpec((None, 2, 8, d), lambda bi, i: (bi, 0, 0, 0)),
                  row(256), row(256), row(256), row(256), pl.BlockSpec((1, 256), lambda bi, i: (0, 0)),
                  row(256), row(256), row(GATE_W),
                  pl.BlockSpec(wb.shape, lambda bi, i: (0, 0, 0)),
                  pl.BlockSpec(wo.shape, lambda bi, i: (0, 0)),
                  pl.BlockSpec((1, d), lambda bi, i: (0, 0)),
                  pl.BlockSpec(wq.shape, lambda bi, i: (0, 0))],
        out_specs=[row(d), row(wq.shape[1]), row(d)],
        out_shape=[jax.ShapeDtypeStruct((b, s, d), F32), jax.ShapeDtypeStruct((b, s, wq.shape[1]), BF16),
                   jax.ShapeDtypeStruct((b, s, d), BF16)],
        compiler_params=_cparams(("arbitrary", "arbitrary")),
        name="merge",
    )(h, modl, ya, x0, z, yconv, hbias, yc, yd, gates, wb, wo, g2, wq)


def _topk_rows(s, label, k):
    vals, labs = [], []
    for _ in range(k):
        m = jnp.max(s, axis=0, keepdims=True)
        lb = jnp.min(jnp.where(s == m, label, float(2 ** 20)), axis=0, keepdims=True)
        vals.append(m)
        labs.append(lb)
        s = jnp.where(label == lb, -jnp.inf, s)
    return jnp.concatenate(vals, axis=0), jnp.concatenate(labs, axis=0).astype(I32)


def _topk_sorted_columns(s, k):
    n = s.shape[0] // 8
    sub = lax.broadcasted_iota(I32, (8, s.shape[1]), 0)
    vals = [s[8 * v:8 * v + 8, :] for v in range(n)]
    rows = [(sub + 8 * v).astype(F32) for v in range(n)]
    for p in range(n):
        for v in range(p % 2, n - 1, 2):
            swap = vals[v + 1] > vals[v]
            vals[v], vals[v + 1] = jnp.where(swap, vals[v + 1], vals[v]), jnp.where(swap, vals[v], vals[v + 1])
            rows[v], rows[v + 1] = jnp.where(swap, rows[v + 1], rows[v]), jnp.where(swap, rows[v], rows[v + 1])
    out_v, out_r = [], []
    for r in range(k):
        m = jnp.max(vals[0], axis=0, keepdims=True)
        best = jnp.min(jnp.where(vals[0] == m, rows[0], float(2 ** 20)), axis=0, keepdims=True)
        out_v.append(m)
        out_r.append(best)
        if r + 1 < k:
            pop = rows[0] == best
            for v in range(min(n - 1, k - 1 - r)):
                vals[v] = jnp.where(pop, vals[v + 1], vals[v])
                rows[v] = jnp.where(pop, rows[v + 1], rows[v])
            if n - 1 < k - 1 - r:
                vals[n - 1] = jnp.where(pop, -jnp.inf, vals[n - 1])
    return jnp.concatenate(out_v, axis=0), jnp.concatenate(out_r, axis=0).astype(I32)


_CAND_FIXED_A = ((0, 0), (0, 8), (1, 0), (2, 0), (3, 0))
_CAND_FIXED_B = ((0, 0), (0, 8), (1, 0), (2, 0))


def _select_rows(table, pos):
    out = jnp.zeros(pos.shape, table.dtype)
    for a in range(table.shape[0]):
        out = jnp.where(pos == a, table[a:a + 1, :], out)
    return out


def _peer_route_kernel(q_ref, keys_ref, i_ref, j_ref, g_ref):
    st = _dot_nt(keys_ref[...], q_ref[...].astype(BF16))
    t = st.shape[1]
    sv1, si1 = _topk_sorted_columns(st[0:PEER_NKEYS], PEER_TOPK)
    sv2, si2 = _topk_sorted_columns(st[PEER_NKEYS:2 * PEER_NKEYS], PEER_TOPK)
    r8 = lax.broadcasted_iota(I32, (8, t), 0)
    cands, labels = [], []
    for a, b0 in _CAND_FIXED_A:
        cands.append(sv1[a:a + 1, :] + sv2[b0:b0 + 8, :])
        labels.append((a * PEER_TOPK + b0 + r8).astype(F32))
    for b, a0 in _CAND_FIXED_B:
        c = sv1[a0:a0 + 8, :] + sv2[b:b + 1, :]
        cands.append(jnp.where(r8 < 4, -jnp.inf, c) if a0 == 0 else c)
        labels.append(((a0 + r8) * PEER_TOPK + b).astype(F32))
    best, pos = _topk_rows(jnp.concatenate(cands, axis=0), jnp.concatenate(labels, axis=0), PEER_TOPK)
    i_ref[...] = _select_rows(si1, pos >> int(math.log2(PEER_TOPK)))
    j_ref[...] = _select_rows(si2, pos & (PEER_TOPK - 1))
    e = jnp.exp(best - jnp.max(best, axis=0, keepdims=True))
    g_ref[...] = e / jnp.sum(e, axis=0, keepdims=True)


def _peer_route(q, keys_blk):
    t, _ = q.shape
    tt = 1024
    assert t % tt == 0
    out = pl.BlockSpec((None, PEER_TOPK, tt), lambda ti, hh: (hh, 0, ti))
    shp = (PEER_HEADS, PEER_TOPK, t)
    return pl.pallas_call(
        _peer_route_kernel,
        grid=(t // tt, PEER_HEADS),
        in_specs=[pl.BlockSpec((tt, PEER_DK), lambda ti, hh: (ti, hh)),
                  pl.BlockSpec((None, 2 * PEER_NKEYS, PEER_DK), lambda ti, hh: (hh, 0, 0))],
        out_specs=[out, out, out],
        out_shape=[jax.ShapeDtypeStruct(shp, I32), jax.ShapeDtypeStruct(shp, I32), jax.ShapeDtypeStruct(shp, F32)],
        compiler_params=_cparams(("arbitrary", "arbitrary")),
        name="peer_route",
    )(q, keys_blk)


GS_HALF = PEER_NKEYS // 2
GS_PITCH = GS_HALF + 1
GS_UNROLL = 64
PEER_TM = 576
PEER_TE = 2048
U32 = jnp.uint32


def _peer_expert_kernel(h_ref, mod_ref, x_ref, i_ref, j_ref, g_ref, ut_ref, v_ref, o_ref, gs_ref, acc_ref,
                        *, n_ctx, tm):
    ti, e = pl.program_id(1), pl.program_id(2)
    n_e = pl.num_programs(2)
    te = ut_ref.shape[1]
    nk = PEER_NKEYS

    @pl.when((pl.program_id(0) == 0) & (ti == 0) & (e == 0))
    def _():
        acc_ref[...] = jnp.zeros(acc_ref.shape, F32)

    @pl.when(e == 0)
    def _():
        sub = lax.broadcasted_iota(I32, (nk, nk), 0)

        def build(t):
            irow = i_ref[pl.ds(t, 1), :]
            jrow = j_ref[pl.ds(t, 1), :]
            grow = g_ref[pl.ds(t, 1), :]
            at = jnp.where(sub == irow, 1.0, 0.0).astype(BF16)
            bt = jnp.where(sub == jrow, grow, 0.0).astype(BF16)
            gt = _dot_nt(at, bt).astype(BF16).astype(F32)
            lo = pltpu.bitcast(gt[0:GS_HALF], U32) >> 16
            hi = pltpu.bitcast(gt[GS_HALF:nk], U32)
            gs_ref[pl.ds(t * GS_PITCH, GS_HALF), :] = hi | lo

        def body(tb, _):
            for u in range(GS_UNROLL):
                build(tb * GS_UNROLL + u)
            return 0

        lax.fori_loop(0, tm // GS_UNROLL, body, 0)

    nb = te // nk // 2

    def gelu(a):
        return 0.5 * a * (1.0 + lax.erf(a * (2.0 ** -0.5)))

    hid = jnp.dot(x_ref[...], ut_ref[...], preferred_element_type=F32)
    first, second = [], []
    for k in range(nb):
        word = gs_ref[pl.ds(e * nb + k, tm, stride=GS_PITCH), :]
        g_lo = pltpu.bitcast(word << 16, F32)
        g_hi = pltpu.bitcast(word & jnp.uint32(0xFFFF0000), F32)
        first.append((g_lo * gelu(hid[:, k * nk:(k + 1) * nk])).astype(BF16))
        second.append((g_hi * gelu(hid[:, (nb + k) * nk:(nb + k + 1) * nk])).astype(BF16))
    acc_ref[...] += jnp.dot(jnp.concatenate(first + second, axis=1), v_ref[...], preferred_element_type=F32)

    @pl.when(e == n_e - 1)
    def _():
        row = ti * tm + lax.broadcasted_iota(I32, acc_ref.shape, 0)
        gate = jnp.where(row < n_ctx, mod_ref[0, 5:6, :], mod_ref[1, 5:6, :])
        o_ref[...] = h_ref[...] + gate * acc_ref[...]
        acc_ref[...] = jnp.zeros(acc_ref.shape, F32)


def _peer_experts(h, modl, xb, i_idx, j_idx, g, ut, v, n_ctx):
    b, s, d = h.shape
    tm, te = PEER_TM, PEER_TE
    n_chunks = ut.shape[0]
    slots = PEER_HEADS * PEER_TOPK
    row = lambda w: pl.BlockSpec((None, tm, w), lambda bi, i, e: (bi, i, 0), pipeline_mode=pl.Buffered(1))
    return pl.pallas_call(
        functools.partial(_peer_expert_kernel, n_ctx=n_ctx, tm=tm),
        grid=(b, s // tm, n_chunks),
        in_specs=[row(d), pl.BlockSpec((None, 2, 8, d), lambda bi, i, e: (bi, 0, 0, 0)),
                  row(d), row(slots), row(slots), row(slots),
                  pl.BlockSpec((None, d, te), lambda bi, i, e: (e, 0, 0)),
                  pl.BlockSpec((te, d), lambda bi, i, e: (e, 0))],
        out_specs=pl.BlockSpec((None, tm, d), lambda bi, i, e: (bi, i, 0)),
        out_shape=jax.ShapeDtypeStruct((b, s, d), F32),
        scratch_shapes=[pltpu.VMEM((tm * GS_PITCH, PEER_NKEYS), U32), pltpu.VMEM((tm, d), F32)],
        compiler_params=_cparams(("arbitrary", "arbitrary", "arbitrary")),
        name="peer_experts",
    )(h, modl, xb, i_idx, j_idx, g, ut, v)


def _final_norm_kernel(x_ref, g_ref, o_ref):
    x = x_ref[...]
    o_ref[...] = x * lax.rsqrt(jnp.mean(x * x, axis=-1, keepdims=True) + EPS) * g_ref[...]


def _final_norm(h, g, n_ctx):
    b, s, d = h.shape
    tm = ROW_TILE
    n_lat = s - n_ctx
    off = n_ctx // tm
    return pl.pallas_call(
        _final_norm_kernel,
        grid=(b, n_lat // tm),
        in_specs=[pl.BlockSpec((None, tm, d), lambda bi, i: (bi, i + off, 0)),
                  pl.BlockSpec((1, d), lambda bi, i: (0, 0))],
        out_specs=pl.BlockSpec((None, tm, d), lambda bi, i: (bi, i, 0)),
        out_shape=jax.ShapeDtypeStruct((b, n_lat, d), F32),
        compiler_params=_cparams(("arbitrary", "arbitrary")),
        name="final_norm",
    )(h, g)


def _prep_w_in(w_in):
    depth, d, _ = w_in.shape
    mixw = w_in[:, :, :2816]
    gq = mixw[:, :, 1536:1792].reshape(depth, d, GQA_HEADS, GQA_DH)[:, :, (0, 2, 1, 3), :].reshape(depth, d, 256)
    mixw = jnp.concatenate([mixw[:, :, :1536], gq, mixw[:, :, 1792:], jnp.zeros((depth, d, MIX_W - 2816), w_in.dtype)], axis=-1)
    return jnp.concatenate([mixw, w_in[:, :, 2816:]], axis=-1).astype(BF16)


def _peer_chunk_order(w):
    n_exp, d = w.shape
    nb = PEER_TE // PEER_NKEYS // 2
    return w.reshape(2, GS_HALF // nb, nb, PEER_NKEYS, d).transpose(1, 0, 2, 3, 4).reshape(n_exp, d)


def _prep_peer_keys(keys):
    depth = keys.shape[0]
    half = PEER_DK // 2
    z = jnp.zeros((depth, PEER_HEADS, PEER_NKEYS, half), keys.dtype)
    top = jnp.concatenate([keys[:, :, 0], z], axis=-1)
    bot = jnp.concatenate([z, keys[:, :, 1]], axis=-1)
    return jnp.concatenate([top, bot], axis=2).astype(BF16)


def kernel(x, c, ctx, c_ctx, w_mod, b_mod, norm1_g, norm2_g, w_in, na_rpb, hy_short_w, hy_short_b, hy_w1, hy_b1, hy_w2, hy_b2, hy_w3, hy_b3, hy_w4, hy_freq, hy_bias, gqa_qn, gqa_kn, diff_lq1, diff_lk1, diff_lq2, diff_lk2, diff_subln, w_branch, w_out, peer_wq, peer_keys, peer_u, peer_v, final_g):
    B, L, D = x.shape
    C = ctx.shape[1]
    S = C + L
    depth = w_mod.shape[0]
    assert C == ROW_TILE and L % ROW_TILE == 0 and D == 1024 and L % GRID_W == 0

    h = jnp.concatenate([ctx, x], axis=1)

    r = -(-(B + 1) // 8) * 8
    cvec = jnp.zeros((r, D), F32).at[:B].set(c).at[B].set(c_ctx)
    modall = _modulation(cvec, w_mod, b_mod)
    mod_lat = modall[:, :B].reshape(depth, B, 1, 6, D)
    mod_ctx = jnp.broadcast_to(modall[:, B].reshape(depth, 1, 1, 6, D), (depth, B, 1, 6, D))
    mods = jnp.pad(jnp.concatenate([mod_ctx, mod_lat], axis=2), ((0, 0), (0, 0), (0, 0), (0, 2), (0, 0)))

    fargs = (hy_w1, hy_b1, hy_w2, hy_b2, hy_w3, hy_b3, hy_w4, hy_freq)
    taps_lat = _conv_taps(_hy_filters(L, *fargs))
    taps_ctx = _conv_taps(_hy_filters(C, *fargs))

    cg, sg = _rope_tables(C, L, GQA_DH, 256)
    cd, sd = _rope_tables(C, L, DIFF_DH, 256)

    lam_init = jnp.asarray([0.8 - 0.6 * math.exp(-0.3 * l) for l in range(depth)], F32)
    lam = (jnp.exp(jnp.sum(diff_lq1.astype(F32) * diff_lk1.astype(F32), axis=-1))
           - jnp.exp(jnp.sum(diff_lq2.astype(F32) * diff_lk2.astype(F32), axis=-1)) + lam_init)
    diff_par = jnp.zeros((depth, 8, LANE), F32).at[:, 0, :].set(lam[:, None]).at[:, 1, :].set(1.0 - lam_init[:, None])

    wb = w_branch.astype(BF16)
    wb = wb.at[:, 2].set(wb[:, 2].reshape(depth, GQA_HEADS, GQA_DH, D)[:, (0, 2, 1, 3)].reshape(depth, BRANCH_W, D))

    xs = dict(
        mods=mods, n1=norm1_g[:, None, :], n2=norm2_g[:, None, :], w_in=_prep_w_in(w_in),
        bias=_na_bias_table(na_rpb, C), sw=hy_short_w, sb=hy_short_b[:, None, :], hbias=hy_bias[:, None, :],
        taps_lat=taps_lat, taps_ctx=taps_ctx,
        qn=jnp.tile(gqa_qn, (1, GQA_HEADS))[:, None, :], kn=jnp.tile(gqa_kn, (1, GQA_KV))[:, None, :],
        diff_par=diff_par, subln=jnp.tile(diff_subln, (1, DIFF_HEADS))[:, None, :],
        wb=wb, wo=w_out.astype(BF16), wq=peer_wq.astype(BF16), keys=_prep_peer_keys(peer_keys),
        u=peer_u, v=peer_v,
    )

    def layer(h, p):
        pb_na, pb_gqa, pb_diff, hy, gates = _in_proj(h, p["mods"], p["n1"], p["w_in"], cg, sg, cd, sd,
                                                     p["qn"], p["kn"], C)
        ya = _na_attention(pb_na, p["bias"], C)
        yc = _gqa_attention(pb_gqa, C)
        yd = _diff_attention(pb_diff, p["diff_par"], p["subln"], C)
        x0, z = _hy_pre(hy, p["sw"], p["sb"], C)
        yconv = jnp.concatenate([_hy_long_conv(z[:, :C], p["taps_ctx"]), _hy_long_conv(z[:, C:], p["taps_lat"])], axis=1)
        h, q, xb = _merge(h, p["mods"], ya, x0, z, yconv, p["hbias"], yc, yd, gates, p["wb"], p["wo"],
                          p["n2"], p["wq"], C)
        i_idx, j_idx, g = _peer_route(q.reshape(B * S, -1), p["keys"])
        slots = PEER_HEADS * PEER_TOPK
        to_rows = lambda a: jnp.transpose(a, (2, 0, 1)).reshape(B, S, slots)
        ut = jnp.transpose(_peer_chunk_order(p["u"].astype(BF16)).reshape(-1, PEER_TE, D), (0, 2, 1))
        h = _peer_experts(h, p["mods"], xb, to_rows(i_idx), to_rows(j_idx), to_rows(g), ut,
                          _peer_chunk_order(p["v"].astype(BF16)), C)
        return h, None

    h, _ = lax.scan(layer, h, xs)
    return _final_norm(h, final_g[None, :], C)
```

```python
import functools
import math

import numpy as np
import jax
import jax.numpy as jnp
from jax import lax
from jax.experimental import pallas as pl
from jax.experimental.pallas import tpu as pltpu

F32 = jnp.float32
BF16 = jnp.bfloat16
I32 = jnp.int32
HIGHEST = lax.Precision.HIGHEST

EPS = 1e-6
GRID_W = 64
ROPE_THETA = 10000.0
NA_HEADS, NA_DH, NA_KH, NA_KW = 4, 64, 8, 16
HY_W, HY_BANDS, HY_FFN = 256, 16, 64
HY_FAST_DECAY, HY_SLOW_DECAY, HY_TARGET = 0.3, 1.5, 1e-2
GQA_HEADS, GQA_KV, GQA_DH = 4, 2, 64
DIFF_HEADS, DIFF_DH, DIFF_DV = 4, 32, 64
N_BRANCH, BRANCH_W = 4, 256
PEER_HEADS, PEER_NKEYS, PEER_DK, PEER_TOPK = 8, 128, 128, 16

LANE = 128
ROW_TILE = 256
VMEM_LIMIT = 56 * 1024 * 1024
NEG = -1e30
LOG2E = math.log2(math.e)
MIX_W = 3072
GATE_W = N_BRANCH * 1024
COL_TILE = 1024


def _cparams(sem):
    return pltpu.CompilerParams(dimension_semantics=sem, vmem_limit_bytes=VMEM_LIMIT)


def _mod_kernel(c_ref, w_ref, b_ref, o_ref):
    cv = c_ref[...]
    a = cv * jax.nn.sigmoid(cv)
    o_ref[...] = jnp.dot(a, w_ref[...], preferred_element_type=F32, precision=HIGHEST) + b_ref[...]


def _modulation(cvec, w_mod, b_mod):
    depth, d, n = w_mod.shape
    r = cvec.shape[0]
    tn = 1024
    return pl.pallas_call(
        _mod_kernel,
        grid=(depth, n // tn),
        in_specs=[pl.BlockSpec((r, d), lambda l, j: (0, 0)),
                  pl.BlockSpec((None, d, tn), lambda l, j: (l, 0, j)),
                  pl.BlockSpec((None, 1, tn), lambda l, j: (l, 0, j))],
        out_specs=pl.BlockSpec((None, r, tn), lambda l, j: (l, 0, j)),
        out_shape=jax.ShapeDtypeStruct((depth, r, n), F32),
        compiler_params=_cparams(("arbitrary", "arbitrary")),
        name="modulation",
    )(cvec, w_mod, b_mod.reshape(depth, 1, n))


def _norm_modulate(x, mod_ref, g_ref, srow, row0, n_ctx):
    y = x * lax.rsqrt(jnp.mean(x * x, axis=-1, keepdims=True) + EPS) * g_ref[...]
    row = row0 + lax.broadcasted_iota(I32, x.shape, 0)
    is_ctx = row < n_ctx
    shift = jnp.where(is_ctx, mod_ref[0, srow:srow + 1, :], mod_ref[1, srow:srow + 1, :])
    scale = jnp.where(is_ctx, mod_ref[0, srow + 1:srow + 2, :], mod_ref[1, srow + 1:srow + 2, :])
    return y * (1.0 + scale) + shift


def _in_proj_kernel(h_ref, mod_ref, g_ref, w_ref, cg_ref, sg_ref, cd_ref, sd_ref, qn_ref, kn_ref, bd_ref,
                    na_ref, gqa_ref, diff_ref, hy_ref, gate_ref, nb_ref, *, tm, n_ctx, n_mix):
    i, j = pl.program_id(1), pl.program_id(2)

    @pl.when(j == 0)
    def _():
        nb_ref[...] = _norm_modulate(h_ref[...], mod_ref, g_ref, 0, i * tm, n_ctx).astype(BF16)

    def proj():
        return jnp.dot(nb_ref[...], w_ref[...], preferred_element_type=F32)

    @pl.when(j == 0)
    def _():
        res = proj()
        na_ref[:, 0:256] = (res[:, 0:256] * (LOG2E * NA_DH ** -0.5)).astype(BF16)
        na_ref[:, 256:768] = res[:, 256:768].astype(BF16)
        hy_ref[:, 0:HY_W] = res[:, 768:1024]

    @pl.when(j == 1)
    def _():
        res = jnp.dot(nb_ref[...], w_ref[:, 512:1024], preferred_element_type=F32)
        hy_ref[:, HY_W:3 * HY_W] = jnp.dot(nb_ref[...], w_ref[:, 0:512], preferred_element_type=F32)
        cg, sg = cg_ref[...], sg_ref[...]
        gq, gk = res[:, 0:256], res[:, 256:384]
        bd = bd_ref[...]
        gq = gq * lax.rsqrt(_head_mean_sq_mxu(gq, GQA_DH, bd) + EPS) * qn_ref[...]
        gk = gk * lax.rsqrt(_head_mean_sq_mxu(gk, GQA_DH, bd[0:128, 0:128]) + EPS) * kn_ref[...]
        gqa_ref[:, 0:256] = (_rope(gq, cg, sg, GQA_DH // 4) * (LOG2E * GQA_DH ** -0.5)).astype(BF16)
        gqa_ref[:, 256:384] = _rope(gk, cg[:, 0:128], sg[:, 0:128], GQA_DH // 4).astype(BF16)
        gqa_ref[:, 384:512] = res[:, 384:512].astype(BF16)

    @pl.when(j == 2)
    def _():
        res = proj()
        cd, sd = cd_ref[...], sd_ref[...]
        diff_ref[:, 0:256] = (_rope(res[:, 0:256], cd, sd, DIFF_DH // 4) * (LOG2E * DIFF_DH ** -0.5)).astype(BF16)
        diff_ref[:, 256:512] = _rope(res[:, 256:512], cd, sd, DIFF_DH // 4).astype(BF16)
        diff_ref[:, 512:768] = res[:, 512:768].astype(BF16)

    @pl.when(j >= n_mix)
    def _():
        gate_ref[...] = proj().astype(gate_ref.dtype)


def _in_proj(h, modl, g, w, cg, sg, cd, sd, qn, kn, n_ctx):
    b, s, d = h.shape
    tm, tn = 1152, COL_TILE
    n_mix = MIX_W // tn
    n_tot = (MIX_W + GATE_W) // tn
    assert n_mix == 3
    const = lambda shp: pl.BlockSpec(shp, lambda bi, i, j: (0,) * len(shp))
    tab = pl.BlockSpec((tm, 256), lambda bi, i, j: (i, 0))
    rows = lambda wd: pl.BlockSpec((None, tm, wd), lambda bi, i, j: (bi, i, 0))
    head_ones = jnp.asarray(np.kron(np.eye(GQA_HEADS), np.ones((GQA_DH, GQA_DH))), BF16)
    return pl.pallas_call(
        functools.partial(_in_proj_kernel, tm=tm, n_ctx=n_ctx, n_mix=n_mix),
        grid=(b, s // tm, n_tot),
        in_specs=[rows(d), pl.BlockSpec((None, 2, 8, d), lambda bi, i, j: (bi, 0, 0, 0)), const((1, d)),
                  pl.BlockSpec((d, tn), lambda bi, i, j: (0, j)),
                  tab, tab, tab, tab, const((1, 256)), const((1, 128)), const((256, 256))],
        out_specs=[rows(768), rows(512), rows(768), rows(3 * HY_W),
                   pl.BlockSpec((None, tm, tn), lambda bi, i, j: (bi, i, jnp.maximum(j - n_mix, 0)))],
        out_shape=[jax.ShapeDtypeStruct((b, s, 768), BF16), jax.ShapeDtypeStruct((b, s, 512), BF16),
                   jax.ShapeDtypeStruct((b, s, 768), BF16), jax.ShapeDtypeStruct((b, s, 3 * HY_W), F32),
                   jax.ShapeDtypeStruct((b, s, GATE_W), BF16)],
        scratch_shapes=[pltpu.VMEM((tm, d), BF16)],
        compiler_params=_cparams(("arbitrary", "arbitrary", "arbitrary")),
        name="in_proj",
    )(h, modl, g, w, cg, sg, cd, sd, qn, kn, head_ones)


def _lane_group(shape, width):
    return lax.broadcasted_iota(I32, shape, 1) >> int(math.log2(width))


def _head_mean_sq(x, n_heads, dh):
    hid = _lane_group(x.shape, dh)
    x2 = x * x
    ms = jnp.zeros_like(x)
    for hh in range(n_heads):
        m = hid == hh
        s = jnp.sum(jnp.where(m, x2, 0.0), axis=-1, keepdims=True)
        ms = jnp.where(m, s, ms)
    return ms * (1.0 / dh)


def _head_mean_sq_mxu(x, dh, head_ones):
    x2 = x * x
    hi = x2.astype(BF16)
    lo = (x2 - hi.astype(F32)).astype(BF16)
    s = jnp.dot(hi, head_ones, preferred_element_type=F32) + jnp.dot(lo, head_ones, preferred_element_type=F32)
    return s * (1.0 / dh)


def _rope(x, cos, sin_signed, qs):
    w = x.shape[-1]
    lane = lax.broadcasted_iota(I32, x.shape, 1)
    lo = (lane & (2 * qs - 1)) < qs
    partner = jnp.where(lo, pltpu.roll(x, w - qs, 1), pltpu.roll(x, qs, 1))
    return x * cos + partner * sin_signed


def _rope_tables(n_ctx, n_lat, dh, width):
    qs = dh // 4
    t = np.arange(n_lat)
    rows, cols = (t // GRID_W).astype(np.float64), (t % GRID_W).astype(np.float64)
    lane = np.arange(width) % dh
    part = lane // (2 * qs)
    u = lane % (2 * qs)
    f = u % qs
    lo = u < qs
    freqs = ROPE_THETA ** (-f.astype(np.float64) / qs)
    pos = np.where(part[None, :] == 0, rows[:, None], cols[:, None])
    ang = (pos.astype(np.float32) * freqs.astype(np.float32)[None, :]).astype(np.float32)
    cos = np.cos(ang.astype(np.float64))
    sin = np.sin(ang.astype(np.float64)) * np.where(lo, -1.0, 1.0)[None, :]
    cos = np.concatenate([np.ones((n_ctx, width)), cos], axis=0)
    sin = np.concatenate([np.zeros((n_ctx, width)), sin], axis=0)
    return jnp.asarray(cos, F32), jnp.asarray(sin, F32)


def _softmax_rows(s):
    m = jnp.max(s, axis=-1, keepdims=True)
    p = jnp.exp2(s - m)
    return p, jnp.sum(p, axis=-1, keepdims=True)


def _dot_nt(a, b):
    return lax.dot_general(a, b, (((1,), (1,)), ((), ())), preferred_element_type=F32)


def _gqa_kernel(q_ref, k_ref, v_ref, o_ref, *, n_ctx):
    i = pl.program_id(1)

    def run(nk):
        k = k_ref[0:nk, :]
        v = v_ref[0:nk, :]
        q = q_ref[...]
        tq = q.shape[0]
        grp = _lane_group((tq, LANE), GQA_DH)

        def scores(g):
            mask = grp == g
            ql = jnp.concatenate([jnp.where(mask, q[:, 0:128], 0), jnp.where(mask, q[:, 128:256], 0)], axis=0)
            return _dot_nt(ql, k)

        v1 = jnp.concatenate([v, jnp.ones_like(v)], axis=1)

        outs = []
        s_next = scores(0)
        for g in range(GQA_KV):
            s = s_next
            if g + 1 < GQA_KV:
                s_next = scores(g + 1)
            p = jnp.exp2(s - jnp.max(s, axis=-1, keepdims=True))
            ov = jnp.dot(p.astype(BF16), v1, preferred_element_type=F32)
            outs.append(ov[:, 0:LANE] / ov[:, LANE:2 * LANE])
        for half in range(2):
            sel = jnp.where(grp == 0, outs[0][half * tq:(half + 1) * tq], outs[1][half * tq:(half + 1) * tq])
            o_ref[:, half * 128:(half + 1) * 128] = sel.astype(o_ref.dtype)

    @pl.when(i == 0)
    def _():
        run(n_ctx)

    @pl.when(i > 0)
    def _():
        run(k_ref.shape[0])


def _gqa_attention(pb, n_ctx):
    b, s, _ = pb.shape
    tq = ROW_TILE
    return pl.pallas_call(
        functools.partial(_gqa_kernel, n_ctx=n_ctx),
        grid=(b, s // tq),
        in_specs=[pl.BlockSpec((None, tq, 256), lambda bi, i: (bi, i, 0)),
                  pl.BlockSpec((None, s, 128), lambda bi, i: (bi, 0, 2)),
                  pl.BlockSpec((None, s, 128), lambda bi, i: (bi, 0, 3))],
        out_specs=pl.BlockSpec((None, tq, 256), lambda bi, i: (bi, i, 0)),
        out_shape=jax.ShapeDtypeStruct((b, s, 256), BF16),
        compiler_params=_cparams(("arbitrary", "arbitrary")),
        name="gqa_attention",
    )(pb, pb, pb)


def _diff_kernel(q_ref, k_ref, v_ref, par_ref, g_ref, o_ref, *, n_ctx):
    i = pl.program_id(1)

    def run(nk):
        k = k_ref[0:nk, :]
        v = v_ref[0:nk, :]
        q = q_ref[...]
        tq = q.shape[0]
        lam = par_ref[0:1, 0:1]
        comp = _lane_group(q.shape, DIFF_DH)
        head = _lane_group(q.shape, DIFF_DV)
        acc = jnp.zeros(q.shape, F32)

        def scores(hh):
            ql = jnp.concatenate([jnp.where(comp == 2 * hh, q, 0), jnp.where(comp == 2 * hh + 1, q, 0)], axis=0)
            return _dot_nt(ql, k)

        s_next = scores(0)
        for hh in range(DIFF_HEADS):
            s = s_next
            if hh + 1 < DIFF_HEADS:
                s_next = scores(hh + 1)
            p, l = _softmax_rows(s)
            inv = 1.0 / l
            pc = (p[0:tq] * inv[0:tq] - p[tq:2 * tq] * (lam * inv[tq:2 * tq])).astype(BF16)
            o = jnp.dot(pc, v, preferred_element_type=F32)
            acc = jnp.where(head == hh, o, acc)
        y = acc * lax.rsqrt(_head_mean_sq(acc, DIFF_HEADS, DIFF_DV) + EPS) * g_ref[...]
        o_ref[...] = (y * par_ref[1:2, 0:1]).astype(o_ref.dtype)

    @pl.when(i == 0)
    def _():
        run(n_ctx)

    @pl.when(i > 0)
    def _():
        run(k_ref.shape[0])


def _diff_attention(pb, par, subln, n_ctx):
    b, s, _ = pb.shape
    tq = ROW_TILE
    return pl.pallas_call(
        functools.partial(_diff_kernel, n_ctx=n_ctx),
        grid=(b, s // tq),
        in_specs=[pl.BlockSpec((None, tq, 256), lambda bi, i: (bi, i, 0)),
                  pl.BlockSpec((None, s, 256), lambda bi, i: (bi, 0, 1)),
                  pl.BlockSpec((None, s, 256), lambda bi, i: (bi, 0, 2)),
                  pl.BlockSpec((8, 128), lambda bi, i: (0, 0)),
                  pl.BlockSpec((1, 256), lambda bi, i: (0, 0))],
        out_specs=pl.BlockSpec((None, tq, 256), lambda bi, i: (bi, i, 0)),
        out_shape=jax.ShapeDtypeStruct((b, s, 256), BF16),
        compiler_params=_cparams(("arbitrary", "arbitrary")),
        name="diff_attention",
    )(pb, pb, pb, par, subln)


def _na_kernel(q_ref, k_ref, v_ref, bias_ref, o_ref, *, n_ctx, n_rows):
    i = pl.program_id(1)
    tq = q_ref.shape[0]

    def stack_heads(q):
        head = _lane_group(q.shape, NA_DH)
        return jnp.concatenate([jnp.where(head == hh, q, 0) for hh in range(NA_HEADS)], axis=0)

    def unstack_heads(o, rows):
        head = _lane_group((rows, 256), NA_DH)
        out = jnp.zeros((rows, 256), F32)
        for hh in range(NA_HEADS):
            out = jnp.where(head == hh, o[hh * rows:(hh + 1) * rows], out)
        return out

    @pl.when(i == 0)
    def _():
        kc = k_ref[0:n_ctx, :]
        vc = v_ref[0:n_ctx, :]
        p, l = _softmax_rows(_dot_nt(stack_heads(q_ref[...]), kc))
        o = jnp.dot(p.astype(BF16), vc, preferred_element_type=F32) / l
        o_ref[...] = unstack_heads(o, tq).astype(o_ref.dtype)

    @pl.when(i > 0)
    def _():
        kc = k_ref[0:n_ctx, :]
        vc = v_ref[0:n_ctx, :]
        def scores(rr):
            r = (i - 1) * (tq // GRID_W) + rr
            rs = jnp.clip(r - NA_KH // 2, 0, n_rows - NA_KH)
            start = pl.multiple_of(n_ctx + rs * GRID_W, GRID_W)
            kcat = jnp.concatenate([kc, k_ref[pl.ds(start, NA_KH * GRID_W), :]], axis=0)
            q = q_ref[rr * GRID_W:(rr + 1) * GRID_W, :]
            return _dot_nt(stack_heads(q), kcat) + bias_ref[r - rs], start

        nxt = scores(0)
        for rr in range(tq // GRID_W):
            s, start = nxt
            if rr + 1 < tq // GRID_W:
                nxt = scores(rr + 1)
            vcat = jnp.concatenate([vc, v_ref[pl.ds(start, NA_KH * GRID_W), :]], axis=0)
            p, l = _softmax_rows(s)
            o = jnp.dot(p.astype(BF16), vcat, preferred_element_type=F32) / l
            o_ref[rr * GRID_W:(rr + 1) * GRID_W, :] = unstack_heads(o, GRID_W).astype(o_ref.dtype)


def _na_attention(pb, bias, n_ctx):
    b, s, _ = pb.shape
    tq = ROW_TILE
    n_rows = (s - n_ctx) // GRID_W
    return pl.pallas_call(
        functools.partial(_na_kernel, n_ctx=n_ctx, n_rows=n_rows),
        grid=(b, s // tq),
        in_specs=[pl.BlockSpec((None, tq, 256), lambda bi, i: (bi, i, 0)),
                  pl.BlockSpec((None, s, 256), lambda bi, i: (bi, 0, 1)),
                  pl.BlockSpec((None, s, 256), lambda bi, i: (bi, 0, 2)),
                  pl.BlockSpec(bias.shape, lambda bi, i: (0, 0, 0))],
        out_specs=pl.BlockSpec((None, tq, 256), lambda bi, i: (bi, i, 0)),
        out_shape=jax.ShapeDtypeStruct((b, s, 256), BF16),
        compiler_params=_cparams(("arbitrary", "arbitrary")),
        name="na_attention",
    )(pb, pb, pb, bias)


def _na_bias_table(rpb, n_ctx):
    depth = rpb.shape[0]
    pad = GRID_W - NA_KW
    padded = jnp.pad(rpb.astype(F32), ((0, 0), (0, 0), (0, 0), (pad, pad)))
    cols = jnp.stack([padded[..., GRID_W - 1 - w:2 * GRID_W - 1 - w] for w in range(GRID_W)], axis=3)
    vals = jnp.stack([cols[:, :, NA_KH - 1 - o:2 * NA_KH - 1 - o] for o in range(NA_KH)], axis=2)
    vals = jnp.transpose(vals, (0, 2, 1, 4, 3, 5))
    w = np.arange(GRID_W)[:, None, None]
    kc = np.arange(GRID_W)[None, None, :]
    cs = np.clip(w - NA_KW // 2, 0, GRID_W - NA_KW)
    inwin = np.broadcast_to((kc >= cs) & (kc < cs + NA_KW), (GRID_W, NA_KH, GRID_W))
    vals = jnp.where(jnp.asarray(inwin), vals * LOG2E, NEG).reshape(depth, NA_KH, NA_HEADS * GRID_W, NA_KH * GRID_W)
    return jnp.concatenate([jnp.zeros((depth, NA_KH, NA_HEADS * GRID_W, n_ctx), F32), vals], axis=-1)


def _hy_pre_kernel(u_ref, w_ref, b_ref, x0_ref, z_ref, *, bounds, tm):
    s = u_ref.shape[0]
    w0, w1, w2, bias = w_ref[0:1, :], w_ref[1:2, :], w_ref[2:3, :], b_ref[...]
    rid = lax.broadcasted_iota(I32, (tm, u_ref.shape[1]), 0)
    for c0 in range(0, s, tm):
        u = u_ref[c0:c0 + tm, :]
        prev = pltpu.roll(u, 1, 0)
        nxt = pltpu.roll(u, tm - 1, 0)
        first = jnp.zeros_like(w0) if c0 in bounds else u_ref[c0 - 1:c0, :]
        last = jnp.zeros_like(w0) if (c0 + tm) in bounds else u_ref[c0 + tm:c0 + tm + 1, :]
        prev = jnp.where(rid == 0, first, prev)
        nxt = jnp.where(rid == tm - 1, last, nxt)
        uc = prev * w0 + u * w1 + nxt * w2 + bias
        x0_ref[c0:c0 + tm, :] = uc[:, 0:HY_W]
        z_ref[c0:c0 + tm, :] = uc[:, HY_W:2 * HY_W] * uc[:, 2 * HY_W:3 * HY_W]


def _hy_pre(mix, w_short, b_short, n_ctx):
    b, s, _ = mix.shape
    return pl.pallas_call(
        functools.partial(_hy_pre_kernel, bounds=(0, n_ctx, s), tm=ROW_TILE),
        grid=(b,),
        in_specs=[pl.BlockSpec((None, s, 3 * HY_W), lambda bi: (bi, 0, 0)),
                  pl.BlockSpec((3, 3 * HY_W), lambda bi: (0, 0)),
                  pl.BlockSpec((1, 3 * HY_W), lambda bi: (0, 0))],
        out_specs=[pl.BlockSpec((None, s, HY_W), lambda bi: (bi, 0, 0)),
                   pl.BlockSpec((None, s, HY_W), lambda bi: (bi, 0, 0))],
        out_shape=[jax.ShapeDtypeStruct((b, s, HY_W), F32), jax.ShapeDtypeStruct((b, s, HY_W), F32)],
        compiler_params=_cparams(("arbitrary",)),
        name="hyena_pre",
    )(mix, w_short, b_short)


def _hy_filter_kernel(z_ref, w1_ref, b1_ref, w2_ref, b2_ref, w3_ref, b3_ref, w4_ref, f_ref, dec_ref, o_ref):
    def lin(a, w_ref, b_ref):
        return jnp.dot(a, w_ref[...], preferred_element_type=F32, precision=HIGHEST) + b_ref[...]

    hh = jnp.sin(f_ref[0:1, :] * lin(z_ref[...], w1_ref, b1_ref))
    hh = jnp.sin(f_ref[1:2, :] * lin(hh, w2_ref, b2_ref))
    hh = jnp.sin(f_ref[2:3, :] * lin(hh, w3_ref, b3_ref))
    hh = jnp.dot(hh, w4_ref[...], preferred_element_type=F32, precision=HIGHEST) * dec_ref[...]
    o_ref[...] = hh / jnp.sum(jnp.abs(hh), axis=0, keepdims=True)


def _pad_to(a, shape):
    return jnp.pad(a, [(0, t - s) for s, t in zip(a.shape, shape)])


def _hy_filters(n, w1, b1, w2, b2, w3, b3, w4, freq):
    depth = w1.shape[0]
    t = np.linspace(0.0, 1.0, n, dtype=np.float32)[:, None]
    w = np.float32(2.0 * math.pi / n) * np.arange(n, dtype=np.float32)[:, None]
    bands = np.linspace(1e-4, HY_BANDS - 1, HY_BANDS, dtype=np.float32)[None, :]
    z = np.concatenate([t, np.cos(w * bands), np.sin(w * bands)], axis=-1).astype(np.float32)
    z = np.pad(z, ((0, 0), (0, LANE - z.shape[1])))
    deltas = np.linspace(math.log(HY_TARGET) / HY_SLOW_DECAY, math.log(HY_TARGET) / HY_FAST_DECAY, HY_W, dtype=np.float32)
    deltas = np.tile(np.abs(deltas), 2)
    decay = np.exp(-t * deltas[None, :]).astype(np.float32)
    p = LANE
    args = (jnp.asarray(z), _pad_to(w1, (depth, p, p)), _pad_to(b1[:, None, :], (depth, 1, p)),
            _pad_to(w2, (depth, p, p)), _pad_to(b2[:, None, :], (depth, 1, p)),
            _pad_to(w3, (depth, p, p)), _pad_to(b3[:, None, :], (depth, 1, p)),
            _pad_to(w4, (depth, p, 2 * HY_W)), _pad_to(freq, (depth, 8, p)), jnp.asarray(decay))
    per_layer = lambda shp: pl.BlockSpec((None,) + shp, lambda l: (l,) + (0,) * len(shp))
    const = lambda shp: pl.BlockSpec(shp, lambda l: (0,) * len(shp))
    return pl.pallas_call(
        _hy_filter_kernel,
        grid=(depth,),
        in_specs=[const((n, p)), per_layer((p, p)), per_layer((1, p)), per_layer((p, p)), per_layer((1, p)),
                  per_layer((p, p)), per_layer((1, p)), per_layer((p, 2 * HY_W)), per_layer((8, p)),
                  const((n, 2 * HY_W))],
        out_specs=per_layer((n, 2 * HY_W)),
        out_shape=jax.ShapeDtypeStruct((depth, n, 2 * HY_W), F32),
        compiler_params=_cparams(("arbitrary",)),
        name="hyena_filters",
    )(*args)


def _conv_taps(filt):
    hf, hb = filt[..., :HY_W], filt[..., HY_W:]
    taps = jnp.concatenate([hb[:, :0:-1], (hf[:, 0:1] + hb[:, 0:1]), hf[:, 1:], jnp.zeros_like(hf[:, 0:1])], axis=1)
    return jnp.transpose(taps, (0, 2, 1))


def _hy_conv_kernel(z_ref, k_ref, o_ref, *, nblk, nb):
    cg = z_ref.shape[0]
    n2 = k_ref.shape[1]
    ncols = (2 * nblk - 1) * LANE

    def body(ci, _):
        krow = k_ref[pl.ds(ci, 1), :]
        kb = jnp.broadcast_to(krow, (LANE, n2))
        big = pltpu.roll(kb, n2 - (LANE - 1), 1, stride=1, stride_axis=0)[:, :ncols].astype(BF16)
        o_ref[ci] = jnp.zeros(o_ref.shape[1:], F32)
        for m in range(-(nblk - 1), nblk):
            km = big[:, (m + nblk - 1) * LANE:(m + nblk) * LANE]
            cnt = (nblk - abs(m)) * nb
            src = 0 if m >= 0 else -m * nb
            dst = m * nb if m >= 0 else 0
            zin = z_ref[ci, src:src + cnt, :]
            o_ref[ci, dst:dst + cnt, :] += jnp.dot(zin, km, preferred_element_type=F32)
        return 0

    lax.fori_loop(0, cg, body, 0)


def _hy_conv(zr, taps, nblk, nb):
    c, r, _ = zr.shape
    cg = 8
    return pl.pallas_call(
        functools.partial(_hy_conv_kernel, nblk=nblk, nb=nb),
        grid=(c // cg,),
        in_specs=[pl.BlockSpec((cg, r, LANE), lambda i: (i, 0, 0)),
                  pl.BlockSpec((cg, taps.shape[1]), lambda i: (i, 0))],
        out_specs=pl.BlockSpec((cg, r, LANE), lambda i: (i, 0, 0)),
        out_shape=jax.ShapeDtypeStruct((c, r, LANE), F32),
        compiler_params=_cparams(("arbitrary",)),
        name="hyena_conv",
    )(zr, taps)


def _hy_long_conv(z, taps):
    b, n, c = z.shape
    nblk = n // LANE
    zr = jnp.transpose(z.astype(BF16).reshape(b, nblk, LANE, c), (3, 1, 0, 2)).reshape(c, nblk * b, LANE)
    y = _hy_conv(zr, taps, nblk, b)
    return jnp.transpose(y.reshape(c, nblk, b, LANE), (2, 1, 3, 0)).reshape(b, n, c)


MERGE_TM = 768
MERGE_SUB = 256


def _merge_kernel(h_ref, mod_ref, ya_ref, x0_ref, z_ref, yconv_ref, hb_ref, yc_ref, yd_ref, gate_ref,
                  wb_ref, wo_ref, g2_ref, wq_ref, o_ref, q_ref, nb_ref, *, n_ctx, tm):
    i = pl.program_id(1)
    d = h_ref.shape[1]
    sub = MERGE_SUB

    def branches(r0):
        rs = slice(r0, r0 + sub)
        yb = (x0_ref[rs, :] * (yconv_ref[rs, :] + z_ref[rs, :] * hb_ref[...])).astype(BF16)
        ys = (ya_ref[rs, :], yb, yc_ref[rs, :], yd_ref[rs, :])
        acc = jnp.zeros((sub, d), F32)
        for k in range(N_BRANCH):
            proj = jnp.dot(ys[k], wb_ref[k], preferred_element_type=F32)
            acc = acc + jax.nn.sigmoid(gate_ref[rs, k * d:(k + 1) * d].astype(F32)) * proj
        return acc.astype(BF16)

    def residual(r0, acc):
        rs = slice(r0, r0 + sub)
        out = jnp.dot(acc, wo_ref[...], preferred_element_type=F32)
        row = i * tm + r0 + lax.broadcasted_iota(I32, out.shape, 0)
        gate = jnp.where(row < n_ctx, mod_ref[0, 2:3, :], mod_ref[1, 2:3, :])
        hn = h_ref[rs, :] + gate * out
        o_ref[rs, :] = hn
        nb = _norm_modulate(hn, mod_ref, g2_ref, 3, i * tm + r0, n_ctx).astype(BF16)
        nb_ref[rs, :] = nb
        return nb

    def query(r0, nb):
        q_ref[r0:r0 + sub, :] = jnp.dot(nb, wq_ref[...], preferred_element_type=F32).astype(q_ref.dtype)

    n_sub = tm // sub
    acc, nb = {}, {}
    for step in range(n_sub + 2):
        if step < n_sub:
            acc[step] = branches(step * sub)
        if 0 <= step - 1 < n_sub:
            nb[step - 1] = residual((step - 1) * sub, acc.pop(step - 1))
        if 0 <= step - 2 < n_sub:
            query((step - 2) * sub, nb.pop(step - 2))


def _merge(h, modl, ya, x0, z, yconv, hbias, yc, yd, gates, wb, wo, g2, wq, n_ctx):
    b, s, d = h.shape
    tm = MERGE_TM
    row = lambda w: pl.BlockSpec((None, tm, w), lambda bi, i: (bi, i, 0))
    return pl.pallas_call(
        functools.partial(_merge_kernel, n_ctx=n_ctx, tm=tm),
        grid=(b, s // tm),
        in_specs=[row(d), pl.BlockSpec((None, 2, 8, d), lambda bi, i: (bi, 0, 0, 0)),
                  row(256), row(256), row(256), row(256), pl.BlockSpec((1, 256), lambda bi, i: (0, 0)),
                  row(256), row(256), row(GATE_W),
                  pl.BlockSpec(wb.shape, lambda bi, i: (0, 0, 0)),
                  pl.BlockSpec(wo.shape, lambda bi, i: (0, 0)),
                  pl.BlockSpec((1, d), lambda bi, i: (0, 0)),
                  pl.BlockSpec(wq.shape, lambda bi, i: (0, 0))],
        out_specs=[row(d), row(wq.shape[1]), row(d)],
        out_shape=[jax.ShapeDtypeStruct((b, s, d), F32), jax.ShapeDtypeStruct((b, s, wq.shape[1]), BF16),
                   jax.ShapeDtypeStruct((b, s, d), BF16)],
        compiler_params=_cparams(("arbitrary", "arbitrary")),
        name="merge",
    )(h, modl, ya, x0, z, yconv, hbias, yc, yd, gates, wb, wo, g2, wq)


def _topk_rows(s, label, k):
    vals, labs = [], []
    for _ in range(k):
        m = jnp.max(s, axis=0, keepdims=True)
        lb = jnp.min(jnp.where(s == m, label, float(2 ** 20)), axis=0, keepdims=True)
        vals.append(m)
        labs.append(lb)
        s = jnp.where(label == lb, -jnp.inf, s)
    return jnp.concatenate(vals, axis=0), jnp.concatenate(labs, axis=0).astype(I32)


def _topk_sorted_columns(s, k):
    n = s.shape[0] // 8
    sub = lax.broadcasted_iota(I32, (8, s.shape[1]), 0)
    vals = [s[8 * v:8 * v + 8, :] for v in range(n)]
    rows = [(sub + 8 * v).astype(F32) for v in range(n)]
    for p in range(n):
        for v in range(p % 2, n - 1, 2):
            swap = vals[v + 1] > vals[v]
            vals[v], vals[v + 1] = jnp.where(swap, vals[v + 1], vals[v]), jnp.where(swap, vals[v], vals[v + 1])
            rows[v], rows[v + 1] = jnp.where(swap, rows[v + 1], rows[v]), jnp.where(swap, rows[v], rows[v + 1])
    out_v, out_r = [], []
    for r in range(k):
        m = jnp.max(vals[0], axis=0, keepdims=True)
        best = jnp.min(jnp.where(vals[0] == m, rows[0], float(2 ** 20)), axis=0, keepdims=True)
        out_v.append(m)
        out_r.append(best)
        if r + 1 < k:
            pop = rows[0] == best
            for v in range(min(n - 1, k - 1 - r)):
                vals[v] = jnp.where(pop, vals[v + 1], vals[v])
                rows[v] = jnp.where(pop, rows[v + 1], rows[v])
            if n - 1 < k - 1 - r:
                vals[n - 1] = jnp.where(pop, -jnp.inf, vals[n - 1])
    return jnp.concatenate(out_v, axis=0), jnp.concatenate(out_r, axis=0).astype(I32)


_CAND_FIXED_A = ((0, 0), (0, 8), (1, 0), (2, 0), (3, 0))
_CAND_FIXED_B = ((0, 0), (0, 8), (1, 0), (2, 0))


def _select_rows(table, pos):
    out = jnp.zeros(pos.shape, table.dtype)
    for a in range(table.shape[0]):
        out = jnp.where(pos == a, table[a:a + 1, :], out)
    return out


def _peer_route_kernel(q_ref, keys_ref, i_ref, j_ref, g_ref):
    st = _dot_nt(keys_ref[...], q_ref[...])
    t = st.shape[1]
    sv1, si1 = _topk_sorted_columns(st[0:PEER_NKEYS], PEER_TOPK)
    sv2, si2 = _topk_sorted_columns(st[PEER_NKEYS:2 * PEER_NKEYS], PEER_TOPK)
    r8 = lax.broadcasted_iota(I32, (8, t), 0)
    cands, labels = [], []
    for a, b0 in _CAND_FIXED_A:
        cands.append(sv1[a:a + 1, :] + sv2[b0:b0 + 8, :])
        labels.append((a * PEER_TOPK + b0 + r8).astype(F32))
    for b, a0 in _CAND_FIXED_B:
        c = sv1[a0:a0 + 8, :] + sv2[b:b + 1, :]
        cands.append(jnp.where(r8 < 4, -jnp.inf, c) if a0 == 0 else c)
        labels.append(((a0 + r8) * PEER_TOPK + b).astype(F32))
    best, pos = _topk_rows(jnp.concatenate(cands, axis=0), jnp.concatenate(labels, axis=0), PEER_TOPK)
    i_ref[...] = _select_rows(si1, pos >> int(math.log2(PEER_TOPK)))
    j_ref[...] = _select_rows(si2, pos & (PEER_TOPK - 1))
    e = jnp.exp(best - jnp.max(best, axis=0, keepdims=True))
    g_ref[...] = e / jnp.sum(e, axis=0, keepdims=True)


def _peer_route(q, keys_blk):
    t, _ = q.shape
    tt = 1024
    assert t % tt == 0
    out = pl.BlockSpec((None, PEER_TOPK, tt), lambda ti, hh: (hh, 0, ti))
    shp = (PEER_HEADS, PEER_TOPK, t)
    return pl.pallas_call(
        _peer_route_kernel,
        grid=(t // tt, PEER_HEADS),
        in_specs=[pl.BlockSpec((tt, PEER_DK), lambda ti, hh: (ti, hh)),
                  pl.BlockSpec((None, 2 * PEER_NKEYS, PEER_DK), lambda ti, hh: (hh, 0, 0))],
        out_specs=[out, out, out],
        out_shape=[jax.ShapeDtypeStruct(shp, I32), jax.ShapeDtypeStruct(shp, I32), jax.ShapeDtypeStruct(shp, F32)],
        compiler_params=_cparams(("arbitrary", "arbitrary")),
        name="peer_route",
    )(q, keys_blk)


GS_HALF = PEER_NKEYS // 2
GS_PITCH = GS_HALF + 1
GS_UNROLL = 64
PEER_TM = 576
PEER_TE = 2048
U32 = jnp.uint32


def _peer_expert_kernel(h_ref, mod_ref, x_ref, i_ref, j_ref, g_ref, ut_ref, v_ref, o_ref, gs_ref, acc_ref,
                        *, n_ctx, tm):
    ti, e = pl.program_id(1), pl.program_id(2)
    n_e = pl.num_programs(2)
    te = ut_ref.shape[1]
    nk = PEER_NKEYS

    @pl.when((pl.program_id(0) == 0) & (ti == 0) & (e == 0))
    def _():
        acc_ref[...] = jnp.zeros(acc_ref.shape, F32)

    @pl.when(e == 0)
    def _():
        sub = lax.broadcasted_iota(I32, (nk, nk), 0)

        def build(t):
            irow = i_ref[pl.ds(t, 1), :]
            jrow = j_ref[pl.ds(t, 1), :]
            grow = g_ref[pl.ds(t, 1), :]
            at = jnp.where(sub == irow, 1.0, 0.0).astype(BF16)
            bt = jnp.where(sub == jrow, grow, 0.0).astype(BF16)
            gt = _dot_nt(at, bt).astype(BF16).astype(F32)
            lo = pltpu.bitcast(gt[0:GS_HALF], U32) >> 16
            hi = pltpu.bitcast(gt[GS_HALF:nk], U32)
            gs_ref[pl.ds(t * GS_PITCH, GS_HALF), :] = hi | lo

        def body(tb, _):
            for u in range(GS_UNROLL):
                build(tb * GS_UNROLL + u)
            return 0

        lax.fori_loop(0, tm // GS_UNROLL, body, 0)

    nb = te // nk // 2

    def gelu(a):
        return 0.5 * a * (1.0 + lax.erf(a * (2.0 ** -0.5)))

    hid = jnp.dot(x_ref[...], ut_ref[...], preferred_element_type=F32)
    first, second = [], []
    for k in range(nb):
        word = gs_ref[pl.ds(e * nb + k, tm, stride=GS_PITCH), :]
        g_lo = pltpu.bitcast(word << 16, F32)
        g_hi = pltpu.bitcast(word & jnp.uint32(0xFFFF0000), F32)
        first.append((g_lo * gelu(hid[:, k * nk:(k + 1) * nk])).astype(BF16))
        second.append((g_hi * gelu(hid[:, (nb + k) * nk:(nb + k + 1) * nk])).astype(BF16))
    acc_ref[...] += jnp.dot(jnp.concatenate(first + second, axis=1), v_ref[...], preferred_element_type=F32)

    @pl.when(e == n_e - 1)
    def _():
        row = ti * tm + lax.broadcasted_iota(I32, acc_ref.shape, 0)
        gate = jnp.where(row < n_ctx, mod_ref[0, 5:6, :], mod_ref[1, 5:6, :])
        o_ref[...] = h_ref[...] + gate * acc_ref[...]
        acc_ref[...] = jnp.zeros(acc_ref.shape, F32)


def _peer_experts(h, modl, xb, i_idx, j_idx, g, ut, v, n_ctx):
    b, s, d = h.shape
    tm, te = PEER_TM, PEER_TE
    n_chunks = ut.shape[0]
    slots = PEER_HEADS * PEER_TOPK
    row = lambda w: pl.BlockSpec((None, tm, w), lambda bi, i, e: (bi, i, 0), pipeline_mode=pl.Buffered(1))
    return pl.pallas_call(
        functools.partial(_peer_expert_kernel, n_ctx=n_ctx, tm=tm),
        grid=(b, s // tm, n_chunks),
        in_specs=[row(d), pl.BlockSpec((None, 2, 8, d), lambda bi, i, e: (bi, 0, 0, 0)),
                  row(d), row(slots), row(slots), row(slots),
                  pl.BlockSpec((None, d, te), lambda bi, i, e: (e, 0, 0)),
                  pl.BlockSpec((te, d), lambda bi, i, e: (e, 0))],
        out_specs=pl.BlockSpec((None, tm, d), lambda bi, i, e: (bi, i, 0)),
        out_shape=jax.ShapeDtypeStruct((b, s, d), F32),
        scratch_shapes=[pltpu.VMEM((tm * GS_PITCH, PEER_NKEYS), U32), pltpu.VMEM((tm, d), F32)],
        compiler_params=_cparams(("arbitrary", "arbitrary", "arbitrary")),
        name="peer_experts",
    )(h, modl, xb, i_idx, j_idx, g, ut, v)


def _final_norm_kernel(x_ref, g_ref, o_ref):
    x = x_ref[...]
    o_ref[...] = x * lax.rsqrt(jnp.mean(x * x, axis=-1, keepdims=True) + EPS) * g_ref[...]


def _final_norm(h, g, n_ctx):
    b, s, d = h.shape
    tm = ROW_TILE
    n_lat = s - n_ctx
    off = n_ctx // tm
    return pl.pallas_call(
        _final_norm_kernel,
        grid=(b, n_lat // tm),
        in_specs=[pl.BlockSpec((None, tm, d), lambda bi, i: (bi, i + off, 0)),
                  pl.BlockSpec((1, d), lambda bi, i: (0, 0))],
        out_specs=pl.BlockSpec((None, tm, d), lambda bi, i: (bi, i, 0)),
        out_shape=jax.ShapeDtypeStruct((b, n_lat, d), F32),
        compiler_params=_cparams(("arbitrary", "arbitrary")),
        name="final_norm",
    )(h, g)


def _prep_w_in(w_in):
    depth, d, _ = w_in.shape
    mixw = w_in[:, :, :2816]
    gq = mixw[:, :, 1536:1792].reshape(depth, d, GQA_HEADS, GQA_DH)[:, :, (0, 2, 1, 3), :].reshape(depth, d, 256)
    mixw = jnp.concatenate([mixw[:, :, :1536], gq, mixw[:, :, 1792:], jnp.zeros((depth, d, MIX_W - 2816), w_in.dtype)], axis=-1)
    return jnp.concatenate([mixw, w_in[:, :, 2816:]], axis=-1).astype(BF16)


def _peer_chunk_order(w):
    n_exp, d = w.shape
    nb = PEER_TE // PEER_NKEYS // 2
    return w.reshape(2, GS_HALF // nb, nb, PEER_NKEYS, d).transpose(1, 0, 2, 3, 4).reshape(n_exp, d)


def _prep_peer_keys(keys):
    depth = keys.shape[0]
    half = PEER_DK // 2
    z = jnp.zeros((depth, PEER_HEADS, PEER_NKEYS, half), keys.dtype)
    top = jnp.concatenate([keys[:, :, 0], z], axis=-1)
    bot = jnp.concatenate([z, keys[:, :, 1]], axis=-1)
    return jnp.concatenate([top, bot], axis=2).astype(BF16)


def kernel(x, c, ctx, c_ctx, w_mod, b_mod, norm1_g, norm2_g, w_in, na_rpb, hy_short_w, hy_short_b, hy_w1, hy_b1, hy_w2, hy_b2, hy_w3, hy_b3, hy_w4, hy_freq, hy_bias, gqa_qn, gqa_kn, diff_lq1, diff_lk1, diff_lq2, diff_lk2, diff_subln, w_branch, w_out, peer_wq, peer_keys, peer_u, peer_v, final_g):
    B, L, D = x.shape
    C = ctx.shape[1]
    S = C + L
    depth = w_mod.shape[0]
    assert C == ROW_TILE and L % ROW_TILE == 0 and D == 1024 and L % GRID_W == 0

    h = jnp.concatenate([ctx, x], axis=1)

    r = -(-(B + 1) // 8) * 8
    cvec = jnp.zeros((r, D), F32).at[:B].set(c).at[B].set(c_ctx)
    modall = _modulation(cvec, w_mod, b_mod)
    mod_lat = modall[:, :B].reshape(depth, B, 1, 6, D)
    mod_ctx = jnp.broadcast_to(modall[:, B].reshape(depth, 1, 1, 6, D), (depth, B, 1, 6, D))
    mods = jnp.pad(jnp.concatenate([mod_ctx, mod_lat], axis=2), ((0, 0), (0, 0), (0, 0), (0, 2), (0, 0)))

    fargs = (hy_w1, hy_b1, hy_w2, hy_b2, hy_w3, hy_b3, hy_w4, hy_freq)
    taps_lat = _conv_taps(_hy_filters(L, *fargs))
    taps_ctx = _conv_taps(_hy_filters(C, *fargs))

    cg, sg = _rope_tables(C, L, GQA_DH, 256)
    cd, sd = _rope_tables(C, L, DIFF_DH, 256)

    lam_init = jnp.asarray([0.8 - 0.6 * math.exp(-0.3 * l) for l in range(depth)], F32)
    lam = (jnp.exp(jnp.sum(diff_lq1.astype(F32) * diff_lk1.astype(F32), axis=-1))
           - jnp.exp(jnp.sum(diff_lq2.astype(F32) * diff_lk2.astype(F32), axis=-1)) + lam_init)
    diff_par = jnp.zeros((depth, 8, LANE), F32).at[:, 0, :].set(lam[:, None]).at[:, 1, :].set(1.0 - lam_init[:, None])

    wb = w_branch.astype(BF16)
    wb = wb.at[:, 2].set(wb[:, 2].reshape(depth, GQA_HEADS, GQA_DH, D)[:, (0, 2, 1, 3)].reshape(depth, BRANCH_W, D))

    xs = dict(
        mods=mods, n1=norm1_g[:, None, :], n2=norm2_g[:, None, :], w_in=_prep_w_in(w_in),
        bias=_na_bias_table(na_rpb, C), sw=hy_short_w, sb=hy_short_b[:, None, :], hbias=hy_bias[:, None, :],
        taps_lat=taps_lat, taps_ctx=taps_ctx,
        qn=jnp.tile(gqa_qn, (1, GQA_HEADS))[:, None, :], kn=jnp.tile(gqa_kn, (1, GQA_KV))[:, None, :],
        diff_par=diff_par, subln=jnp.tile(diff_subln, (1, DIFF_HEADS))[:, None, :],
        wb=wb, wo=w_out.astype(BF16), wq=peer_wq.astype(BF16), keys=_prep_peer_keys(peer_keys),
        u=peer_u, v=peer_v,
    )

    def layer(h, p):
        pb_na, pb_gqa, pb_diff, hy, gates = _in_proj(h, p["mods"], p["n1"], p["w_in"], cg, sg, cd, sd,
                                                     p["qn"], p["kn"], C)
        ya = _na_attention(pb_na, p["bias"], C)
        yc = _gqa_attention(pb_gqa, C)
        yd = _diff_attention(pb_diff, p["diff_par"], p["subln"], C)
        x0, z = _hy_pre(hy, p["sw"], p["sb"], C)
        yconv = jnp.concatenate([_hy_long_conv(z[:, :C], p["taps_ctx"]), _hy_long_conv(z[:, C:], p["taps_lat"])], axis=1)
        h, q, xb = _merge(h, p["mods"], ya, x0, z, yconv, p["hbias"], yc, yd, gates, p["wb"], p["wo"],
                          p["n2"], p["wq"], C)
        i_idx, j_idx, g = _peer_route(q.reshape(B * S, -1), p["keys"])
        slots = PEER_HEADS * PEER_TOPK
        to_rows = lambda a: jnp.transpose(a, (2, 0, 1)).reshape(B, S, slots)
        ut = jnp.transpose(_peer_chunk_order(p["u"].astype(BF16)).reshape(-1, PEER_TE, D), (0, 2, 1))
        h = _peer_experts(h, p["mods"], xb, to_rows(i_idx), to_rows(j_idx), to_rows(g), ut,
                          _peer_chunk_order(p["v"].astype(BF16)), C)
        return h, None

    h, _ = lax.scan(layer, h, xs)
    return _final_norm(h, final_g[None, :], C)
```
